```python
import jax, jax.numpy as jnp
from jax import lax
import numpy as np

D_MODEL = 1024
BATCH = 4
SEQ = 4096
DEPTH = 2

N_MIXERS = 2
N_MEM = 256
MIX_WIDTH = 3 * D_MODEL // 4
MEM_HEADS = 4
MEM_HEAD_DIM = D_MODEL // 16
MEM_WIDTH = MEM_HEADS * MEM_HEAD_DIM
CHUNK = 64
GLA_HEADS = 4
GLA_KEY_WIDTH = D_MODEL // 2
GLA_DK = GLA_KEY_WIDTH // GLA_HEADS
GLA_DV = MIX_WIDTH // GLA_HEADS
GLA_GATE_RANK = 16
GLA_GATE_NORMALIZER = 16.0
GLA_IN_WIDTH = 2 * GLA_KEY_WIDTH + 2 * MIX_WIDTH + GLA_GATE_RANK + MEM_WIDTH
HG_EXPAND = 128
HG_HEADS = MIX_WIDTH // HG_EXPAND
HG_DK = HG_EXPAND
HG_DV = MIX_WIDTH // HG_HEADS
HG_IN_WIDTH = 4 * MIX_WIDTH + MEM_WIDTH
N_GROUPS = 4
EXPERTS_PER_GROUP = 4
N_EXPERTS = N_GROUPS * EXPERTS_PER_GROUP
EXPERT_FF = 256
TOP_K_IN_GROUP = 2
NORM_EPS = 1e-6

kernel_name = 'hybrid_gla_hgrn2_memxattn_hmoe'


def rms_norm(x, gain):
    xf = x.astype(jnp.float32)
    y = xf * lax.rsqrt(jnp.mean(xf * xf, axis=-1, keepdims=True) + NORM_EPS)
    return (y * gain.astype(jnp.float32)).astype(x.dtype)


def split_cols(a, sizes):
    offsets, acc = [], 0
    for s in sizes[:-1]:
        acc += s
        offsets.append(acc)
    return jnp.split(a, offsets, axis=-1)


def chunk_gated_linear_attention(q, k, v, log_decay):
    B, T, H, K = q.shape
    V = v.shape[-1]
    n_chunks = T // CHUNK

    def to_chunks(a):
        return a.astype(jnp.float32).reshape(B, n_chunks, CHUNK, H, a.shape[-1]).transpose(1, 0, 3, 2, 4)

    qc, kc, vc, gc = to_chunks(q), to_chunks(k), to_chunks(v), to_chunks(log_decay)
    bc = jnp.cumsum(gc, axis=-2)
    causal = jnp.tril(jnp.ones((CHUNK, CHUNK), dtype=bool))[:, :, None]

    def step(state, inp):
        q_, k_, v_, b_ = inp
        diff = b_[..., :, None, :] - b_[..., None, :, :]
        decay = jnp.exp(jnp.where(causal, diff, -jnp.inf))
        scores = jnp.sum(q_[..., :, None, :] * k_[..., None, :, :] * decay, axis=-1)
        out = scores @ v_ + (q_ * jnp.exp(b_)) @ state
        b_last = b_[..., -1:, :]
        state = state * jnp.exp(b_last).swapaxes(-1, -2) + (k_ * jnp.exp(b_last - b_)).swapaxes(-1, -2) @ v_
        return state, out

    state0 = jnp.zeros((B, H, K, V), jnp.float32)
    _, out = lax.scan(step, state0, (qc, kc, vc, bc))
    return out.transpose(1, 0, 3, 2, 4).reshape(B, T, H, V).astype(v.dtype)


def gla_mixer(u, w_in, w_gate_up, b_gate, out_gain):
    B, T, _ = u.shape
    proj = u @ w_in
    q, k, v, g, r, mq = split_cols(proj, [GLA_KEY_WIDTH, GLA_KEY_WIDTH, MIX_WIDTH, MIX_WIDTH, GLA_GATE_RANK, MEM_WIDTH])
    q = q.reshape(B, T, GLA_HEADS, GLA_DK) * (GLA_DK ** -0.5)
    k = k.reshape(B, T, GLA_HEADS, GLA_DK)
    v = v.reshape(B, T, GLA_HEADS, GLA_DV)
    gate_logit = (r @ w_gate_up + b_gate).astype(jnp.float32)
    log_decay = (jax.nn.log_sigmoid(gate_logit) / GLA_GATE_NORMALIZER).reshape(B, T, GLA_HEADS, GLA_DK)
    o = chunk_gated_linear_attention(q, k, v, log_decay)
    o = rms_norm(o, out_gain) * jax.nn.silu(g.reshape(B, T, GLA_HEADS, GLA_DV))
    return o.reshape(B, T, MIX_WIDTH), mq


def hgrn2_mixer(u, w_in, lower_bound, out_gain):
    B, T, _ = u.shape
    proj = u @ w_in
    q, f, i, g, mq = split_cols(proj, [MIX_WIDTH, MIX_WIDTH, MIX_WIDTH, MIX_WIDTH, MEM_WIDTH])
    q = jax.nn.silu(q).reshape(B, T, HG_HEADS, HG_DK)
    z = f.astype(jnp.float32).reshape(B, T, HG_HEADS, HG_DK)
    lb = lower_bound.reshape(HG_HEADS, HG_DK)
    log_f = jnp.logaddexp(jnp.log(lb), jnp.log1p(-lb) + jax.nn.log_sigmoid(z))
    one_minus_f = (1.0 - lb) * jax.nn.sigmoid(-z)
    i = i.reshape(B, T, HG_HEADS, HG_DV)
    o = chunk_gated_linear_attention(q, one_minus_f, i, log_f)
    o = rms_norm(o, out_gain) * jax.nn.silu(g.reshape(B, T, HG_HEADS, HG_DV))
    return o.reshape(B, T, MIX_WIDTH), mq


def memory_cross_attention(mq, mem_n, w_mem_kv):
    B, T, _ = mq.shape
    kv = mem_n @ w_mem_kv
    k, v = split_cols(kv, [MEM_WIDTH, MEM_WIDTH])
    k = k.reshape(B, N_MEM, MEM_HEADS, MEM_HEAD_DIM)
    v = v.reshape(B, N_MEM, MEM_HEADS, MEM_HEAD_DIM)
    q = mq.reshape(B, T, MEM_HEADS, MEM_HEAD_DIM)
    s = jnp.einsum('bthd,bmhd->bhtm', q, k).astype(jnp.float32) * (MEM_HEAD_DIM ** -0.5)
    p = jax.nn.softmax(s, axis=-1).astype(v.dtype)
    o = jnp.einsum('bhtm,bmhd->bthd', p, v)
    return o.reshape(B, T, MEM_WIDTH)


def hierarchical_moe(u, w_group, b_group, w_router, b_router, w_up, w_down):
    B, T, D = u.shape
    t = u.reshape(B * T, D)
    n_tok = t.shape[0]
    g_logits = (t @ w_group + b_group).astype(jnp.float32)
    g_prob = jax.nn.softmax(g_logits, axis=-1)
    g_idx = jnp.argmax(g_logits, axis=-1)
    g_w = jnp.take_along_axis(g_prob, g_idx[:, None], axis=-1)
    e_logits = (t @ w_router + b_router).astype(jnp.float32).reshape(n_tok, N_GROUPS, EXPERTS_PER_GROUP)
    e_in_group = jnp.take_along_axis(e_logits, g_idx[:, None, None], axis=1)[:, 0]
    top_v, top_i = lax.top_k(e_in_group, TOP_K_IN_GROUP)
    top_w = jax.nn.softmax(top_v, axis=-1) * g_w
    expert_id = g_idx[:, None] * EXPERTS_PER_GROUP + top_i
    gates = jnp.einsum('nk,nke->ne', top_w, jax.nn.one_hot(expert_id, N_EXPERTS, dtype=jnp.float32))
    h = jnp.einsum('nd,edf->nef', t, w_up)
    a, b = split_cols(h, [EXPERT_FF, EXPERT_FF])
    h = jax.nn.silu(a) * b * gates[..., None].astype(h.dtype)
    y = jnp.einsum('nef,efd->nd', h, w_down)
    return y.reshape(B, T, D)


def setup_inputs(seed: int = 0) -> dict:
    key = jax.random.key(seed)
    ks = jax.random.split(key, 24)
    n_gla = (DEPTH + N_MIXERS - 1) // N_MIXERS
    n_hg = DEPTH // N_MIXERS

    def nrm(k, shape, scale):
        return jax.random.normal(k, shape, jnp.float32) * scale

    def gain(k, shape):
        return 1.0 + 0.02 * jax.random.normal(k, shape, jnp.float32)

    return {
        'x': nrm(ks[0], (BATCH, SEQ, D_MODEL), 1.0),
        'mem': nrm(ks[1], (BATCH, N_MEM, D_MODEL), 1.0),
        'mix_norm': gain(ks[2], (DEPTH, D_MODEL)),
        'ffn_norm': gain(ks[3], (DEPTH, D_MODEL)),
        'mem_norm': gain(ks[4], (DEPTH, D_MODEL)),
        'final_norm': gain(ks[5], (D_MODEL,)),
        'gla_w_in': nrm(ks[6], (n_gla, D_MODEL, GLA_IN_WIDTH), D_MODEL ** -0.5),
        'gla_w_gate_up': nrm(ks[7], (n_gla, GLA_GATE_RANK, GLA_KEY_WIDTH), GLA_GATE_RANK ** -0.5),
        'gla_b_gate': nrm(ks[8], (n_gla, GLA_KEY_WIDTH), 0.1),
        'gla_out_norm': gain(ks[9], (n_gla, GLA_DV)),
        'hg_w_in': nrm(ks[10], (n_hg, D_MODEL, HG_IN_WIDTH), D_MODEL ** -0.5),
        'hg_lower_bounds': nrm(ks[11], (DEPTH, MIX_WIDTH), 0.1),
        'hg_out_norm': gain(ks[12], (n_hg, HG_DV)),
        'w_mem_kv': nrm(ks[13], (DEPTH, D_MODEL, 2 * MEM_WIDTH), D_MODEL ** -0.5),
        'w_out': nrm(ks[14], (DEPTH, MIX_WIDTH + MEM_WIDTH, D_MODEL), (MIX_WIDTH + MEM_WIDTH) ** -0.5),
        'w_group': nrm(ks[15], (DEPTH, D_MODEL, N_GROUPS), D_MODEL ** -0.5),
        'b_group': nrm(ks[16], (DEPTH, N_GROUPS), 0.01),
        'w_router': nrm(ks[17], (DEPTH, D_MODEL, N_EXPERTS), D_MODEL ** -0.5),
        'b_router': nrm(ks[18], (DEPTH, N_EXPERTS), 0.01),
        'w_up': nrm(ks[19], (DEPTH, N_EXPERTS, D_MODEL, 2 * EXPERT_FF), D_MODEL ** -0.5),
        'w_down': nrm(ks[20], (DEPTH, N_EXPERTS, EXPERT_FF, D_MODEL), EXPERT_FF ** -0.5),
    }


def reference(x, mem, mix_norm, ffn_norm, mem_norm, final_norm, gla_w_in, gla_w_gate_up, gla_b_gate,
              gla_out_norm, hg_w_in, hg_lower_bounds, hg_out_norm, w_mem_kv, w_out, w_group, b_group,
              w_router, b_router, w_up, w_down):
    lb_prob = jax.nn.softmax(hg_lower_bounds.astype(jnp.float32), axis=0)
    lower_bounds = jnp.cumsum(lb_prob, axis=0) - lb_prob[0]
    h = x
    for layer in range(DEPTH):
        u = rms_norm(h, mix_norm[layer])
        j = layer // N_MIXERS
        if layer % N_MIXERS == 0:
            mix, mq = gla_mixer(u, gla_w_in[j], gla_w_gate_up[j], gla_b_gate[j], gla_out_norm[j])
        else:
            mix, mq = hgrn2_mixer(u, hg_w_in[j], lower_bounds[layer], hg_out_norm[j])
        mem_o = memory_cross_attention(mq, rms_norm(mem, mem_norm[layer]), w_mem_kv[layer])
        h = h + jnp.concatenate([mix, mem_o], axis=-1) @ w_out[layer]
        h = h + hierarchical_moe(rms_norm(h, ffn_norm[layer]), w_group[layer], b_group[layer],
                                 w_router[layer], b_router[layer], w_up[layer], w_down[layer])
    return rms_norm(h, final_norm)
```

```python
import functools

import jax
import jax.numpy as jnp
from jax import lax
from jax.experimental import pallas as pl
from jax.experimental.pallas import tpu as pltpu

D_MODEL = 1024
N_MEM = 256
MIX_WIDTH = 768
MEM_HEADS = 4
MEM_HEAD_DIM = 64
MEM_WIDTH = 256
CHUNK = 64
GLA_HEADS = 4
GLA_KEY_WIDTH = 512
GLA_DK = 128
GLA_DV = 192
GLA_DV_PAD = 256
GLA_GATE_RANK = 16
GLA_GATE_NORMALIZER = 16.0
HG_HEADS = 6
HG_DK = 128
HG_DV = 128
N_GROUPS = 4
EXPERTS_PER_GROUP = 4
N_EXPERTS = 16
EXPERT_FF = 256
NORM_EPS = 1e-6
LANES = 128
SUBLANES = 8
SAFE_LOG_DECAY = -60.0
VMEM_LIMIT = 56 * 1024 * 1024

F32 = jnp.float32
BF16 = jnp.bfloat16


def _dot(a, b):
    return jnp.dot(a, b, preferred_element_type=F32)


def _dot_nt(a, b):
    return lax.dot_general(a, b, (((1,), (1,)), ((), ())), preferred_element_type=F32)


def _dot_tn(a, b):
    return lax.dot_general(a, b, (((0,), (0,)), ((), ())), preferred_element_type=F32)


def _split2(x):
    hi = x.astype(BF16)
    lo = (x - hi.astype(F32)).astype(BF16)
    return hi, lo


def _split3(x):
    hi = x.astype(BF16)
    r = x - hi.astype(F32)
    mid = r.astype(BF16)
    lo = (r - mid.astype(F32)).astype(BF16)
    return hi, mid, lo


def _rms(x, gain):
    ms = jnp.mean(x * x, axis=-1, keepdims=True)
    return x * lax.rsqrt(ms + NORM_EPS) * gain


def _log_sigmoid(x):
    return jnp.minimum(x, 0.0) - jnp.log1p(jnp.exp(-jnp.abs(x)))


def _sigmoid(x):
    return 1.0 / (1.0 + jnp.exp(-x))


def _silu(x):
    return x * _sigmoid(x)


def _chunk_cumsum(tri_ref, x):
    tri = tri_ref[...]
    hi, mid, lo = _split3(x)
    return _dot(tri, hi) + _dot(tri, mid) + _dot(tri, lo)


def _mem_kv_kernel(mem_ref, gain_ref, w_ref, k_ref, v_ref):
    m = _rms(mem_ref[0], gain_ref[...]).astype(BF16)
    kv = _dot(m, w_ref[...])
    k_ref[0] = kv[:, :MEM_WIDTH].astype(BF16)
    v_ref[0] = kv[:, MEM_WIDTH:].astype(BF16)


def _mem_kv(mem, gain, w_kv):
    B = mem.shape[0]
    return pl.pallas_call(
        _mem_kv_kernel,
        out_shape=(jax.ShapeDtypeStruct((B, N_MEM, MEM_WIDTH), BF16),
                   jax.ShapeDtypeStruct((B, N_MEM, MEM_WIDTH), BF16)),
        grid=(B,),
        in_specs=[pl.BlockSpec((1, N_MEM, D_MODEL), lambda b: (b, 0, 0)),
                  pl.BlockSpec((1, D_MODEL), lambda b: (0, 0)),
                  pl.BlockSpec((D_MODEL, 2 * MEM_WIDTH), lambda b: (0, 0))],
        out_specs=(pl.BlockSpec((1, N_MEM, MEM_WIDTH), lambda b: (b, 0, 0)),
                   pl.BlockSpec((1, N_MEM, MEM_WIDTH), lambda b: (b, 0, 0))),
        compiler_params=pltpu.CompilerParams(dimension_semantics=("arbitrary",)),
        name="mem_kv",
    )(mem, gain.reshape(1, D_MODEL), w_kv.astype(BF16))


_GQ, _GK = 0, GLA_KEY_WIDTH
_GV = 2 * GLA_KEY_WIDTH
_GG = _GV + GLA_HEADS * GLA_DV_PAD
_GR = _GG + GLA_HEADS * GLA_DV_PAD
_GM = _GR + LANES
_GW = _GM + MEM_WIDTH


def _gla_proj_kernel(h_ref, gain_ref, w_ref, wg_hi_ref, wg_lo_ref, bg_ref, tri_ref,
                     q_ref, k_ref, v_ref, sg_ref, b_ref, mq_ref):
    u = _rms(h_ref[...], gain_ref[...]).astype(BF16)
    q_ref[...] = (_dot(u, w_ref[:, _GQ:_GK]) * (GLA_DK ** -0.5)).astype(BF16)
    k_ref[...] = _dot(u, w_ref[:, _GK:_GV]).astype(BF16)
    v_ref[...] = _dot(u, w_ref[:, _GV:_GG]).astype(BF16)
    sg_ref[...] = _silu(_dot(u, w_ref[:, _GG:_GR])).astype(BF16)
    mq_ref[...] = _dot(u, w_ref[:, _GM:_GW]).astype(BF16)
    r_hi, r_lo = _split2(_dot(u, w_ref[:, _GR:_GM]))
    wg_hi = wg_hi_ref[...]
    logit = _dot(r_hi, wg_hi) + _dot(r_hi, wg_lo_ref[...]) + _dot(r_lo, wg_hi) + bg_ref[...]
    b_ref[...] = _chunk_cumsum(tri_ref, _log_sigmoid(logit) * (1.0 / GLA_GATE_NORMALIZER))


_HQ, _HF, _HI, _HGG, _HM = 0, MIX_WIDTH, 2 * MIX_WIDTH, 3 * MIX_WIDTH, 4 * MIX_WIDTH
_HW = _HM + MEM_WIDTH


def _hg_proj_kernel(layer, h_ref, gain_ref, w_ref, lbp_ref, tri_ref,
                    q_ref, k_ref, v_ref, sg_ref, b_ref, mq_ref):
    u = _rms(h_ref[...], gain_ref[...]).astype(BF16)
    p = lbp_ref[...]
    p = jnp.exp(p - jnp.max(p, axis=0, keepdims=True))
    p = p / jnp.sum(p, axis=0, keepdims=True)
    lb = jnp.sum(p[0:layer + 1], axis=0, keepdims=True) - p[0:1]
    q_ref[...] = _silu(_dot(u, w_ref[:, _HQ:_HF])).astype(BF16)
    z = _dot(u, w_ref[:, _HF:_HI])
    k_ref[...] = ((1.0 - lb) * _sigmoid(-z)).astype(BF16)
    v_ref[...] = _dot(u, w_ref[:, _HI:_HGG]).astype(BF16)
    sg_ref[...] = _silu(_dot(u, w_ref[:, _HGG:_HM])).astype(BF16)
    mq_ref[...] = _dot(u, w_ref[:, _HM:_HW]).astype(BF16)
    a = jnp.log(lb)
    c = jnp.log1p(-lb) + _log_sigmoid(z)
    log_f = jnp.maximum(a, c) + jnp.log1p(jnp.exp(-jnp.abs(a - c)))
    b_ref[...] = _chunk_cumsum(tri_ref, log_f)


def _proj_call(kernel, h, gain, w, extra, kw, vw, tm, name):
    N = h.shape[0]
    tri = (jnp.arange(tm)[:, None] >= jnp.arange(tm)[None, :]) & (
        jnp.arange(tm)[:, None] // CHUNK == jnp.arange(tm)[None, :] // CHUNK)
    tri = tri.astype(BF16)
    row = lambda width: pl.BlockSpec((tm, width), lambda i: (i, 0))
    full = lambda a: pl.BlockSpec(a.shape, lambda i: (0,) * a.ndim)
    ins = [h, gain.reshape(1, D_MODEL), w] + list(extra) + [tri]
    return pl.pallas_call(
        kernel,
        out_shape=(jax.ShapeDtypeStruct((N, kw), BF16), jax.ShapeDtypeStruct((N, kw), BF16),
                   jax.ShapeDtypeStruct((N, vw), BF16), jax.ShapeDtypeStruct((N, vw), BF16),
                   jax.ShapeDtypeStruct((N, kw), F32), jax.ShapeDtypeStruct((N, MEM_WIDTH), BF16)),
        grid=(N // tm,),
        in_specs=[row(D_MODEL)] + [full(a) for a in ins[1:]],
        out_specs=(row(kw), row(kw), row(vw), row(vw), row(kw), row(MEM_WIDTH)),
        compiler_params=pltpu.CompilerParams(dimension_semantics=("arbitrary",),
                                             vmem_limit_bytes=VMEM_LIMIT),
        name=name,
    )(*ins)


def _pad_heads(w, heads, dv, dv_pad):
    lead = w.shape[:-1]
    w = w.reshape(lead + (heads, dv))
    w = jnp.pad(w, [(0, 0)] * len(lead) + [(0, 0), (0, dv_pad - dv)])
    return w.reshape(lead + (heads * dv_pad,))


def _gla_proj(h, gain, w_in, w_gate_up, b_gate, tm):
    q, k, v, g, r, mq = jnp.split(
        w_in, [GLA_KEY_WIDTH, 2 * GLA_KEY_WIDTH, 2 * GLA_KEY_WIDTH + MIX_WIDTH,
               2 * GLA_KEY_WIDTH + 2 * MIX_WIDTH, 2 * GLA_KEY_WIDTH + 2 * MIX_WIDTH + GLA_GATE_RANK], axis=1)
    w = jnp.concatenate([q, k, _pad_heads(v, GLA_HEADS, GLA_DV, GLA_DV_PAD),
                         _pad_heads(g, GLA_HEADS, GLA_DV, GLA_DV_PAD),
                         jnp.pad(r, ((0, 0), (0, LANES - GLA_GATE_RANK))), mq], axis=1).astype(BF16)
    wg = jnp.pad(w_gate_up, ((0, LANES - GLA_GATE_RANK), (0, 0)))
    wg_hi, wg_lo = _split2(wg)
    return _proj_call(_gla_proj_kernel, h, gain, w, [wg_hi, wg_lo, b_gate.reshape(1, GLA_KEY_WIDTH)],
                      GLA_KEY_WIDTH, GLA_HEADS * GLA_DV_PAD, tm, "gla_proj")


def _hg_proj(h, gain, w_in, lower_bound_params, layer, tm):
    return _proj_call(functools.partial(_hg_proj_kernel, layer), h, gain, w_in.astype(BF16),
                      [lower_bound_params], MIX_WIDTH, MIX_WIDTH, tm, "hg_proj")


def _scan_kernel(heads, dk, dv_pad, dv, n_chunks,
                 q_ref, k_ref, b_ref, v_ref, sg_ref, gain_ref, o_ref, st_ref, kf_ref):
    @pl.when(pl.program_id(1) == 0)
    def _():
        st_ref[...] = jnp.zeros_like(st_ref)

    gain = gain_ref[...]
    row = lax.broadcasted_iota(jnp.int32, (CHUNK, CHUNK), 0)
    col = lax.broadcasted_iota(jnp.int32, (CHUNK, CHUNK), 1)
    causal = row >= col

    def finish(hd, rows, scores, q, k, b, eb):
        vc = slice(hd * dv_pad, (hd + 1) * dv_pad)
        v = v_ref[0, rows, vc]
        st = st_ref[hd]
        b_last = b[CHUNK - 1:CHUNK, :]
        qd = (q * eb).astype(BF16)
        out = _dot(scores.astype(BF16), v) + _dot_nt(qd, st.astype(BF16))
        kl = (k * jnp.exp(b_last - b)).astype(BF16)
        st_ref[hd] = st * jnp.exp(b_last) + _dot_tn(v, kl)
        ms = jnp.sum(out * out, axis=-1, keepdims=True) * (1.0 / dv)
        y = out * lax.rsqrt(ms + NORM_EPS) * gain * sg_ref[0, rows, vc].astype(F32)
        o_ref[0, rows, vc] = y.astype(BF16)

    def load(hd, rows):
        kc = slice(hd * dk, (hd + 1) * dk)
        return (q_ref[0, rows, kc].astype(F32), k_ref[0, rows, kc].astype(F32), b_ref[0, rows, kc])

    all_safe = jnp.min(b_ref[...]) >= SAFE_LOG_DECAY

    @pl.when(all_safe)
    def _():
        for c in range(n_chunks):
            rows = slice(c * CHUNK, (c + 1) * CHUNK)
            for hd in range(heads):
                q, k, b = load(hd, rows)
                eb = jnp.exp(b)
                kd = (k * jnp.exp(-b)).astype(BF16)
                scores = _dot_nt((q * eb).astype(BF16), kd)
                finish(hd, rows, jnp.where(causal, scores, 0.0), q, k, b, eb)

    @pl.when(jnp.logical_not(all_safe))
    def _():
        for hd in range(heads):
            kc = slice(hd * dk, (hd + 1) * dk)

            def chunk_body(c, carry):
                start = pl.multiple_of(c * CHUNK, CHUNK)
                rows = pl.ds(start, CHUNK)
                q, k, b = load(hd, rows)
                kf_ref[...] = k

                def col_body(j, sc):
                    base = pl.multiple_of((j >> 3) << 3, SUBLANES)
                    pick = lax.broadcasted_iota(jnp.int32, (SUBLANES, dk), 0) == (j & (SUBLANES - 1))
                    kj = jnp.sum(jnp.where(pick, kf_ref[pl.ds(base, SUBLANES), :], 0.0), axis=0, keepdims=True)
                    b8 = b_ref[0, pl.ds(pl.multiple_of(start + base, SUBLANES), SUBLANES), kc]
                    bj = jnp.sum(jnp.where(pick, b8, 0.0), axis=0, keepdims=True)
                    rid = lax.broadcasted_iota(jnp.int32, (CHUNK, dk), 0)
                    dec = jnp.exp(jnp.where(rid >= j, b - bj, -jnp.inf))
                    colv = jnp.sum(q * kj * dec, axis=-1, keepdims=True)
                    return jnp.where(col == j, colv, sc)

                scores = lax.fori_loop(0, CHUNK, col_body, jnp.zeros((CHUNK, CHUNK), F32))
                finish(hd, rows, scores, q, k, b, jnp.exp(b))
                return carry

            lax.fori_loop(0, n_chunks, chunk_body, 0)


def _scan(q, k, b, v, sg, gain, heads, dk, dv_pad, dv, tb):
    B, T, kw = q.shape
    vw = v.shape[-1]
    blk = lambda w: pl.BlockSpec((1, tb, w), lambda bi, ti: (bi, ti, 0))
    gain_row = jnp.pad(gain, (0, dv_pad - dv)).reshape(1, dv_pad)
    return pl.pallas_call(
        functools.partial(_scan_kernel, heads, dk, dv_pad, dv, tb // CHUNK),
        out_shape=jax.ShapeDtypeStruct((B, T, vw), BF16),
        grid=(B, T // tb),
        in_specs=[blk(kw), blk(kw), blk(kw), blk(vw), blk(vw),
                  pl.BlockSpec((1, dv_pad), lambda bi, ti: (0, 0))],
        out_specs=blk(vw),
        scratch_shapes=[pltpu.VMEM((heads, dv_pad, dk), F32), pltpu.VMEM((CHUNK, dk), F32)],
        compiler_params=pltpu.CompilerParams(dimension_semantics=("arbitrary", "arbitrary"),
                                             vmem_limit_bytes=VMEM_LIMIT),
        name="chunk_scan",
    )(q, k, b, v, sg, gain_row)


def _out_kernel(mix_ref, mq_ref, mk_ref, mv_ref, h_ref, wmix_ref, wmem_ref, fgain_ref,
                wr_hi_ref, wr_lo_ref, br_ref, h1_ref, un_ref, gates_ref):
    tm = mix_ref.shape[0]
    mq = mq_ref[...]
    mk = mk_ref[0]
    mv = mv_ref[0]
    lane_w = lax.broadcasted_iota(jnp.int32, (1, MEM_WIDTH), 1)
    mem_o = jnp.zeros((tm, MEM_WIDTH), F32)
    for hd in range(MEM_HEADS):
        in_head = (lane_w >= hd * MEM_HEAD_DIM) & (lane_w < (hd + 1) * MEM_HEAD_DIM)
        qh = jnp.where(in_head, mq, jnp.zeros_like(mq))
        s = _dot_nt(qh, mk) * (MEM_HEAD_DIM ** -0.5)
        e = jnp.exp(s - jnp.max(s, axis=-1, keepdims=True))
        p = e / jnp.sum(e, axis=-1, keepdims=True)
        mem_o = mem_o + _dot(p.astype(BF16), jnp.where(in_head, mv, jnp.zeros_like(mv)))
    h1 = h_ref[...] + _dot(mix_ref[...], wmix_ref[...]) + _dot(mem_o.astype(BF16), wmem_ref[...])
    h1_ref[...] = h1
    un = _rms(h1, fgain_ref[...])
    un_ref[...] = un.astype(BF16)

    u_hi, u_lo = _split2(un)
    wr_hi = wr_hi_ref[...]
    lg = _dot(u_hi, wr_hi) + _dot(u_hi, wr_lo_ref[...]) + _dot(u_lo, wr_hi) + br_ref[...]
    lane = lax.broadcasted_iota(jnp.int32, (tm, LANES), 1)
    neg = -jnp.inf
    gl = jnp.where(lane < N_GROUPS, lg[:, :LANES], neg)
    gmax = jnp.max(gl, axis=-1, keepdims=True)
    g_idx = jnp.min(jnp.where(gl == gmax, lane, LANES), axis=-1, keepdims=True)
    g_w = 1.0 / jnp.sum(jnp.exp(gl - gmax), axis=-1, keepdims=True)
    in_group = (lane >= g_idx * EXPERTS_PER_GROUP) & (lane < (g_idx + 1) * EXPERTS_PER_GROUP)
    el = jnp.where(in_group, lg[:, LANES:], neg)
    t1 = jnp.max(el, axis=-1, keepdims=True)
    i1 = jnp.min(jnp.where(el == t1, lane, LANES), axis=-1, keepdims=True)
    el2 = jnp.where(lane == i1, neg, el)
    t2 = jnp.max(el2, axis=-1, keepdims=True)
    i2 = jnp.min(jnp.where(el2 == t2, lane, LANES), axis=-1, keepdims=True)
    e2 = jnp.exp(t2 - t1)
    w1 = g_w / (1.0 + e2)
    w2 = g_w * e2 / (1.0 + e2)
    gates_ref[...] = jnp.where(lane == i1, w1, 0.0) + jnp.where(lane == i2, w2, 0.0)


def _out_call(mix, mq, mem_k, mem_v, h, w_mix, w_mem, ffn_gain, w_group, b_group, w_router, b_router, T, tm):
    N = h.shape[0]
    tiles_per_batch = T // tm
    wr = jnp.zeros((D_MODEL, 2 * LANES), F32)
    wr = wr.at[:, :N_GROUPS].set(w_group).at[:, LANES:LANES + N_EXPERTS].set(w_router)
    br = jnp.zeros((1, 2 * LANES), F32)
    br = br.at[0, :N_GROUPS].set(b_group).at[0, LANES:LANES + N_EXPERTS].set(b_router)
    wr_hi, wr_lo = _split2(wr)
    row = lambda width: pl.BlockSpec((tm, width), lambda i: (i, 0))
    full = lambda a: pl.BlockSpec(a.shape, lambda i: (0,) * a.ndim)
    memspec = pl.BlockSpec((1, N_MEM, MEM_WIDTH), lambda i: (i // tiles_per_batch, 0, 0))
    w_mix = w_mix.astype(BF16)
    w_mem = w_mem.astype(BF16)
    fg = ffn_gain.reshape(1, D_MODEL)
    return pl.pallas_call(
        _out_kernel,
        out_shape=(jax.ShapeDtypeStruct((N, D_MODEL), F32), jax.ShapeDtypeStruct((N, D_MODEL), BF16),
                   jax.ShapeDtypeStruct((N, LANES), F32)),
        grid=(N // tm,),
        in_specs=[row(mix.shape[1]), row(MEM_WIDTH), memspec, memspec, row(D_MODEL),
                  full(w_mix), full(w_mem), full(fg), full(wr_hi), full(wr_lo), full(br)],
        out_specs=(row(D_MODEL), row(D_MODEL), row(LANES)),
        compiler_params=pltpu.CompilerParams(dimension_semantics=("arbitrary",),
                                             vmem_limit_bytes=VMEM_LIMIT),
        name="out_proj_route",
    )(mix, mq, mem_k, mem_v, h, w_mix, w_mem, fg, wr_hi, wr_lo, br)


def _moe_kernel(final, un_ref, gates_ref, h1_ref, wup_ref, wdn_ref, fgain_ref, o_ref, acc_ref):
    e = pl.program_id(1)

    @pl.when(e == 0)
    def _():
        acc_ref[...] = h1_ref[...]

    lane = lax.broadcasted_iota(jnp.int32, gates_ref.shape, 1)
    gate = jnp.sum(jnp.where(lane == e, gates_ref[...], 0.0), axis=-1, keepdims=True)
    hh = _dot(un_ref[...], wup_ref[0])
    act = _silu(hh[:, :EXPERT_FF]) * hh[:, EXPERT_FF:] * gate
    acc_ref[...] += _dot(act.astype(BF16), wdn_ref[0])

    @pl.when(e == N_EXPERTS - 1)
    def _():
        y = acc_ref[...]
        o_ref[...] = _rms(y, fgain_ref[...]) if final else y


def _moe(un, gates, h1, w_up, w_down, final_gain, final, tm):
    N = un.shape[0]
    row = lambda width: pl.BlockSpec((tm, width), lambda i, e: (i, 0))
    return pl.pallas_call(
        functools.partial(_moe_kernel, final),
        out_shape=jax.ShapeDtypeStruct((N, D_MODEL), F32),
        grid=(N // tm, N_EXPERTS),
        in_specs=[row(D_MODEL), row(LANES), row(D_MODEL),
                  pl.BlockSpec((1, D_MODEL, 2 * EXPERT_FF), lambda i, e: (e, 0, 0)),
                  pl.BlockSpec((1, EXPERT_FF, D_MODEL), lambda i, e: (e, 0, 0)),
                  pl.BlockSpec((1, D_MODEL), lambda i, e: (0, 0))],
        out_specs=row(D_MODEL),
        scratch_shapes=[pltpu.VMEM((tm, D_MODEL), F32)],
        compiler_params=pltpu.CompilerParams(dimension_semantics=("arbitrary", "arbitrary"),
                                             vmem_limit_bytes=VMEM_LIMIT),
        name="moe",
    )(un, gates, h1, w_up.astype(BF16), w_down.astype(BF16), final_gain.reshape(1, D_MODEL))


def kernel(x, mem, mix_norm, ffn_norm, mem_norm, final_norm, gla_w_in, gla_w_gate_up, gla_b_gate, gla_out_norm,
           hg_w_in, hg_lower_bounds, hg_out_norm, w_mem_kv, w_out, w_group, b_group, w_router, b_router,
           w_up, w_down):
    B, T, _ = x.shape
    N = B * T
    depth = mix_norm.shape[0]
    tm_proj = min(256, T)
    tb_scan = min(256, T)
    tm_out = min(512, T)
    tm_moe = min(1024, N)
    h = x.reshape(N, D_MODEL)
    for layer in range(depth):
        j = layer // 2
        if layer % 2 == 0:
            q, k, v, sg, b, mq = _gla_proj(h, mix_norm[layer], gla_w_in[j], gla_w_gate_up[j], gla_b_gate[j], tm_proj)
            heads, dk, dv_pad, dv, out_gain = GLA_HEADS, GLA_DK, GLA_DV_PAD, GLA_DV, gla_out_norm[j]
        else:
            q, k, v, sg, b, mq = _hg_proj(h, mix_norm[layer], hg_w_in[j], hg_lower_bounds, layer, tm_proj)
            heads, dk, dv_pad, dv, out_gain = HG_HEADS, HG_DK, HG_DV, HG_DV, hg_out_norm[j]
        r3 = lambda a: a.reshape(B, T, a.shape[-1])
        mix = _scan(r3(q), r3(k), r3(b), r3(v), r3(sg), out_gain, heads, dk, dv_pad, dv, tb_scan)
        mix = mix.reshape(N, heads * dv_pad)
        mem_k, mem_v = _mem_kv(mem, mem_norm[layer], w_mem_kv[layer])
        w_mix = _pad_heads(w_out[layer, :MIX_WIDTH].T, heads, dv, dv_pad).T
        h1, un, gates = _out_call(mix, mq, mem_k, mem_v, h, w_mix, w_out[layer, MIX_WIDTH:], ffn_norm[layer],
                                  w_group[layer], b_group[layer], w_router[layer], b_router[layer], T, tm_out)
        h = _moe(un, gates, h1, w_up[layer], w_down[layer], final_norm, layer == depth - 1, tm_moe)
    return h.reshape(B, T, D_MODEL)
```

```python
import functools

import jax
import jax.numpy as jnp
from jax import lax
from jax.experimental import pallas as pl
from jax.experimental.pallas import tpu as pltpu

D_MODEL = 1024
N_MEM = 256
MIX_WIDTH = 768
MEM_HEADS = 4
MEM_HEAD_DIM = 64
MEM_WIDTH = 256
CHUNK = 64
GLA_HEADS = 4
GLA_KEY_WIDTH = 512
GLA_DK = 128
GLA_DV = 192
GLA_DV_PAD = 256
GLA_GATE_RANK = 16
GLA_GATE_NORMALIZER = 16.0
HG_HEADS = 6
HG_DK = 128
HG_DV = 128
N_GROUPS = 4
EXPERTS_PER_GROUP = 4
N_EXPERTS = 16
EXPERT_FF = 256
NORM_EPS = 1e-6
LANES = 128
SUBLANES = 8
MOE_BLOCK = 128
ROUTE_GROUP_LANE = 4
SAFE_LOG_DECAY = -60.0
VMEM_LIMIT = 56 * 1024 * 1024

F32 = jnp.float32
BF16 = jnp.bfloat16


def _dot(a, b):
    return jnp.dot(a, b, preferred_element_type=F32)


def _dot_nt(a, b):
    return lax.dot_general(a, b, (((1,), (1,)), ((), ())), preferred_element_type=F32)


def _dot_tn(a, b):
    return lax.dot_general(a, b, (((0,), (0,)), ((), ())), preferred_element_type=F32)


def _split2(x):
    hi = x.astype(BF16)
    lo = (x - hi.astype(F32)).astype(BF16)
    return hi, lo


def _split3(x):
    hi = x.astype(BF16)
    r = x - hi.astype(F32)
    mid = r.astype(BF16)
    lo = (r - mid.astype(F32)).astype(BF16)
    return hi, mid, lo


def _rms(x, gain):
    ms = jnp.mean(x * x, axis=-1, keepdims=True)
    return x * lax.rsqrt(ms + NORM_EPS) * gain


def _log_sigmoid(x):
    return jnp.minimum(x, 0.0) - jnp.log1p(jnp.exp(-jnp.abs(x)))


def _sigmoid(x):
    return 1.0 / (1.0 + jnp.exp(-x))


def _silu(x):
    return x * _sigmoid(x)


def _chunk_cumsum(tri_ref, x):
    tri = tri_ref[...]
    hi, mid, lo = _split3(x)
    return _dot(tri, hi) + _dot(tri, mid) + _dot(tri, lo)


def _mem_kv_kernel(mem_ref, gain_ref, w_ref, k_ref, v_ref):
    m = _rms(mem_ref[0], gain_ref[...]).astype(BF16)
    kv = _dot(m, w_ref[...])
    k_ref[0] = kv[:, :MEM_WIDTH].astype(BF16)
    v_ref[0] = kv[:, MEM_WIDTH:].astype(BF16)


def _mem_kv(mem, gain, w_kv):
    B = mem.shape[0]
    return pl.pallas_call(
        _mem_kv_kernel,
        out_shape=(jax.ShapeDtypeStruct((B, N_MEM, MEM_WIDTH), BF16),
                   jax.ShapeDtypeStruct((B, N_MEM, MEM_WIDTH), BF16)),
        grid=(B,),
        in_specs=[pl.BlockSpec((1, N_MEM, D_MODEL), lambda b: (b, 0, 0)),
                  pl.BlockSpec((1, D_MODEL), lambda b: (0, 0)),
                  pl.BlockSpec((D_MODEL, 2 * MEM_WIDTH), lambda b: (0, 0))],
        out_specs=(pl.BlockSpec((1, N_MEM, MEM_WIDTH), lambda b: (b, 0, 0)),
                   pl.BlockSpec((1, N_MEM, MEM_WIDTH), lambda b: (b, 0, 0))),
        compiler_params=pltpu.CompilerParams(dimension_semantics=("arbitrary",)),
        name="mem_kv",
    )(mem, gain.reshape(1, D_MODEL), w_kv.astype(BF16))


_GQ, _GK = 0, GLA_KEY_WIDTH
_GV = 2 * GLA_KEY_WIDTH
_GG = _GV + GLA_HEADS * GLA_DV_PAD
_GR = _GG + GLA_HEADS * GLA_DV_PAD
_GM = _GR + LANES
_GW = _GM + MEM_WIDTH


def _gla_proj_kernel(h_ref, gain_ref, w_ref, wg_hi_ref, wg_lo_ref, bg_ref, tri_ref,
                     q_ref, k_ref, v_ref, sg_ref, b_ref, mq_ref):
    u = _rms(h_ref[...], gain_ref[...]).astype(BF16)
    q_ref[...] = (_dot(u, w_ref[:, _GQ:_GK]) * (GLA_DK ** -0.5)).astype(BF16)
    k_ref[...] = _dot(u, w_ref[:, _GK:_GV]).astype(BF16)
    v_ref[...] = _dot(u, w_ref[:, _GV:_GG]).astype(BF16)
    sg_ref[...] = _silu(_dot(u, w_ref[:, _GG:_GR])).astype(BF16)
    mq_ref[...] = _dot(u, w_ref[:, _GM:_GW]).astype(BF16)
    r_hi, r_lo = _split2(_dot(u, w_ref[:, _GR:_GM]))
    wg_hi = wg_hi_ref[...]
    logit = _dot(r_hi, wg_hi) + _dot(r_hi, wg_lo_ref[...]) + _dot(r_lo, wg_hi) + bg_ref[...]
    b_ref[...] = _chunk_cumsum(tri_ref, _log_sigmoid(logit) * (1.0 / GLA_GATE_NORMALIZER))


_HQ, _HF, _HI, _HGG, _HM = 0, MIX_WIDTH, 2 * MIX_WIDTH, 3 * MIX_WIDTH, 4 * MIX_WIDTH
_HW = _HM + MEM_WIDTH


def _hg_proj_kernel(layer, h_ref, gain_ref, w_ref, lbp_ref, tri_ref,
                    q_ref, k_ref, v_ref, sg_ref, b_ref, mq_ref):
    u = _rms(h_ref[...], gain_ref[...]).astype(BF16)
    p = lbp_ref[...]
    p = jnp.exp(p - jnp.max(p, axis=0, keepdims=True))
    p = p / jnp.sum(p, axis=0, keepdims=True)
    lb = jnp.sum(p[0:layer + 1], axis=0, keepdims=True) - p[0:1]
    q_ref[...] = _silu(_dot(u, w_ref[:, _HQ:_HF])).astype(BF16)
    z = _dot(u, w_ref[:, _HF:_HI])
    k_ref[...] = ((1.0 - lb) * _sigmoid(-z)).astype(BF16)
    v_ref[...] = _dot(u, w_ref[:, _HI:_HGG]).astype(BF16)
    sg_ref[...] = _silu(_dot(u, w_ref[:, _HGG:_HM])).astype(BF16)
    mq_ref[...] = _dot(u, w_ref[:, _HM:_HW]).astype(BF16)
    a = jnp.log(lb)
    c = jnp.log1p(-lb) + _log_sigmoid(z)
    log_f = jnp.maximum(a, c) + jnp.log1p(jnp.exp(-jnp.abs(a - c)))
    b_ref[...] = _chunk_cumsum(tri_ref, log_f)


def _proj_call(kernel, h, gain, w, extra, kw, vw, tm, name):
    N = h.shape[0]
    tri = (jnp.arange(tm)[:, None] >= jnp.arange(tm)[None, :]) & (
        jnp.arange(tm)[:, None] // CHUNK == jnp.arange(tm)[None, :] // CHUNK)
    tri = tri.astype(BF16)
    row = lambda width: pl.BlockSpec((tm, width), lambda i: (i, 0))
    full = lambda a: pl.BlockSpec(a.shape, lambda i: (0,) * a.ndim)
    ins = [h, gain.reshape(1, D_MODEL), w] + list(extra) + [tri]
    return pl.pallas_call(
        kernel,
        out_shape=(jax.ShapeDtypeStruct((N, kw), BF16), jax.ShapeDtypeStruct((N, kw), BF16),
                   jax.ShapeDtypeStruct((N, vw), BF16), jax.ShapeDtypeStruct((N, vw), BF16),
                   jax.ShapeDtypeStruct((N, kw), F32), jax.ShapeDtypeStruct((N, MEM_WIDTH), BF16)),
        grid=(N // tm,),
        in_specs=[row(D_MODEL)] + [full(a) for a in ins[1:]],
        out_specs=(row(kw), row(kw), row(vw), row(vw), row(kw), row(MEM_WIDTH)),
        compiler_params=pltpu.CompilerParams(dimension_semantics=("arbitrary",),
                                             vmem_limit_bytes=VMEM_LIMIT),
        name=name,
    )(*ins)


def _pad_heads(w, heads, dv, dv_pad):
    lead = w.shape[:-1]
    w = w.reshape(lead + (heads, dv))
    w = jnp.pad(w, [(0, 0)] * len(lead) + [(0, 0), (0, dv_pad - dv)])
    return w.reshape(lead + (heads * dv_pad,))


def _gla_proj(h, gain, w_in, w_gate_up, b_gate, tm):
    q, k, v, g, r, mq = jnp.split(
        w_in, [GLA_KEY_WIDTH, 2 * GLA_KEY_WIDTH, 2 * GLA_KEY_WIDTH + MIX_WIDTH,
               2 * GLA_KEY_WIDTH + 2 * MIX_WIDTH, 2 * GLA_KEY_WIDTH + 2 * MIX_WIDTH + GLA_GATE_RANK], axis=1)
    w = jnp.concatenate([q, k, _pad_heads(v, GLA_HEADS, GLA_DV, GLA_DV_PAD),
                         _pad_heads(g, GLA_HEADS, GLA_DV, GLA_DV_PAD),
                         jnp.pad(r, ((0, 0), (0, LANES - GLA_GATE_RANK))), mq], axis=1).astype(BF16)
    wg = jnp.pad(w_gate_up, ((0, LANES - GLA_GATE_RANK), (0, 0)))
    wg_hi, wg_lo = _split2(wg)
    return _proj_call(_gla_proj_kernel, h, gain, w, [wg_hi, wg_lo, b_gate.reshape(1, GLA_KEY_WIDTH)],
                      GLA_KEY_WIDTH, GLA_HEADS * GLA_DV_PAD, tm, "gla_proj")


def _hg_proj(h, gain, w_in, lower_bound_params, layer, tm):
    return _proj_call(functools.partial(_hg_proj_kernel, layer), h, gain, w_in.astype(BF16),
                      [lower_bound_params], MIX_WIDTH, MIX_WIDTH, tm, "hg_proj")


def _scan_kernel(heads, dk, dv_pad, dv, n_chunks,
                 q_ref, k_ref, b_ref, v_ref, sg_ref, gain_ref, o_ref, st_ref, kf_ref):
    @pl.when(pl.program_id(1) == 0)
    def _():
        st_ref[...] = jnp.zeros_like(st_ref)

    gain = gain_ref[...]
    row = lax.broadcasted_iota(jnp.int32, (CHUNK, CHUNK), 0)
    col = lax.broadcasted_iota(jnp.int32, (CHUNK, CHUNK), 1)
    causal = row >= col

    def finish(hd, rows, scores, q, k, b, eb):
        vc = slice(hd * dv_pad, (hd + 1) * dv_pad)
        v = v_ref[0, rows, vc]
        st = st_ref[hd]
        b_last = b[CHUNK - 1:CHUNK, :]
        qd = (q * eb).astype(BF16)
        out = _dot(scores.astype(BF16), v) + _dot_nt(qd, st.astype(BF16))
        kl = (k * jnp.exp(b_last - b)).astype(BF16)
        st_ref[hd] = st * jnp.exp(b_last) + _dot_tn(v, kl)
        ms = jnp.sum(out * out, axis=-1, keepdims=True) * (1.0 / dv)
        y = out * lax.rsqrt(ms + NORM_EPS) * gain * sg_ref[0, rows, vc].astype(F32)
        o_ref[0, rows, vc] = y.astype(BF16)

    def load(hd, rows):
        kc = slice(hd * dk, (hd + 1) * dk)
        return (q_ref[0, rows, kc].astype(F32), k_ref[0, rows, kc].astype(F32), b_ref[0, rows, kc])

    all_safe = jnp.min(b_ref[...]) >= SAFE_LOG_DECAY

    @pl.when(all_safe)
    def _():
        for c in range(n_chunks):
            rows = slice(c * CHUNK, (c + 1) * CHUNK)
            for hd in range(heads):
                q, k, b = load(hd, rows)
                eb = jnp.exp(b)
                kd = (k * jnp.exp(-b)).astype(BF16)
                scores = _dot_nt((q * eb).astype(BF16), kd)
                finish(hd, rows, jnp.where(causal, scores, 0.0), q, k, b, eb)

    @pl.when(jnp.logical_not(all_safe))
    def _():
        for hd in range(heads):
            kc = slice(hd * dk, (hd + 1) * dk)

            def chunk_body(c, carry):
                start = pl.multiple_of(c * CHUNK, CHUNK)
                rows = pl.ds(start, CHUNK)
                q, k, b = load(hd, rows)
                kf_ref[...] = k

                def col_body(j, sc):
                    base = pl.multiple_of((j >> 3) << 3, SUBLANES)
                    pick = lax.broadcasted_iota(jnp.int32, (SUBLANES, dk), 0) == (j & (SUBLANES - 1))
                    kj = jnp.sum(jnp.where(pick, kf_ref[pl.ds(base, SUBLANES), :], 0.0), axis=0, keepdims=True)
                    b8 = b_ref[0, pl.ds(pl.multiple_of(start + base, SUBLANES), SUBLANES), kc]
                    bj = jnp.sum(jnp.where(pick, b8, 0.0), axis=0, keepdims=True)
                    rid = lax.broadcasted_iota(jnp.int32, (CHUNK, dk), 0)
                    dec = jnp.exp(jnp.where(rid >= j, b - bj, -jnp.inf))
                    colv = jnp.sum(q * kj * dec, axis=-1, keepdims=True)
                    return jnp.where(col == j, colv, sc)

                scores = lax.fori_loop(0, CHUNK, col_body, jnp.zeros((CHUNK, CHUNK), F32))
                finish(hd, rows, scores, q, k, b, jnp.exp(b))
                return carry

            lax.fori_loop(0, n_chunks, chunk_body, 0)


def _scan(q, k, b, v, sg, gain, heads, dk, dv_pad, dv, tb):
    B, T, kw = q.shape
    vw = v.shape[-1]
    blk = lambda w: pl.BlockSpec((1, tb, w), lambda bi, ti: (bi, ti, 0))
    gain_row = jnp.pad(gain, (0, dv_pad - dv)).reshape(1, dv_pad)
    return pl.pallas_call(
        functools.partial(_scan_kernel, heads, dk, dv_pad, dv, tb // CHUNK),
        out_shape=jax.ShapeDtypeStruct((B, T, vw), BF16),
        grid=(B, T // tb),
        in_specs=[blk(kw), blk(kw), blk(kw), blk(vw), blk(vw),
                  pl.BlockSpec((1, dv_pad), lambda bi, ti: (0, 0))],
        out_specs=blk(vw),
        scratch_shapes=[pltpu.VMEM((heads, dv_pad, dk), F32), pltpu.VMEM((CHUNK, dk), F32)],
        compiler_params=pltpu.CompilerParams(dimension_semantics=("arbitrary", "arbitrary"),
                                             vmem_limit_bytes=VMEM_LIMIT),
        name="chunk_scan",
    )(q, k, b, v, sg, gain_row)


def _out_kernel(mix_ref, mq_ref, mk_ref, mv_ref, h_ref, wmix_ref, wmem_ref, fgain_ref,
                wr_hi_ref, wr_lo_ref, br_ref, h1_ref, un_ref, gates_ref):
    tm = mix_ref.shape[0]
    mq = mq_ref[...]
    mk = mk_ref[0]
    mv = mv_ref[0]
    lane_w = lax.broadcasted_iota(jnp.int32, (1, MEM_WIDTH), 1)
    mem_o = jnp.zeros((tm, MEM_WIDTH), F32)
    for hd in range(MEM_HEADS):
        in_head = (lane_w >= hd * MEM_HEAD_DIM) & (lane_w < (hd + 1) * MEM_HEAD_DIM)
        qh = jnp.where(in_head, mq, jnp.zeros_like(mq))
        s = _dot_nt(qh, mk) * (MEM_HEAD_DIM ** -0.5)
        e = jnp.exp(s - jnp.max(s, axis=-1, keepdims=True))
        p = e / jnp.sum(e, axis=-1, keepdims=True)
        mem_o = mem_o + _dot(p.astype(BF16), jnp.where(in_head, mv, jnp.zeros_like(mv)))
    h1 = h_ref[...] + _dot(mix_ref[...], wmix_ref[...]) + _dot(mem_o.astype(BF16), wmem_ref[...])
    h1_ref[...] = h1
    un = _rms(h1, fgain_ref[...])
    un_ref[...] = un.astype(BF16)

    u_hi, u_lo = _split2(un)
    wr_hi = wr_hi_ref[...]
    lg = _dot(u_hi, wr_hi) + _dot(u_hi, wr_lo_ref[...]) + _dot(u_lo, wr_hi) + br_ref[...]
    lane = lax.broadcasted_iota(jnp.int32, (tm, LANES), 1)
    neg = -jnp.inf
    gl = jnp.where(lane < N_GROUPS, lg[:, :LANES], neg)
    gmax = jnp.max(gl, axis=-1, keepdims=True)
    g_idx = jnp.min(jnp.where(gl == gmax, lane, LANES), axis=-1, keepdims=True)
    g_w = 1.0 / jnp.sum(jnp.exp(gl - gmax), axis=-1, keepdims=True)
    in_group = (lane >= g_idx * EXPERTS_PER_GROUP) & (lane < (g_idx + 1) * EXPERTS_PER_GROUP)
    el = jnp.where(in_group, lg[:, LANES:], neg)
    t1 = jnp.max(el, axis=-1, keepdims=True)
    i1 = jnp.min(jnp.where(el == t1, lane, LANES), axis=-1, keepdims=True)
    el2 = jnp.where(lane == i1, neg, el)
    t2 = jnp.max(el2, axis=-1, keepdims=True)
    i2 = jnp.min(jnp.where(el2 == t2, lane, LANES), axis=-1, keepdims=True)
    e2 = jnp.exp(t2 - t1)
    w1 = g_w / (1.0 + e2)
    w2 = g_w * e2 / (1.0 + e2)
    first = g_idx * EXPERTS_PER_GROUP
    gates_ref[...] = (jnp.where(lane == i1 - first, w1, 0.0) + jnp.where(lane == i2 - first, w2, 0.0)
                      + jnp.where(lane == ROUTE_GROUP_LANE, g_idx.astype(F32), 0.0))


def _out_call(mix, mq, mem_k, mem_v, h, w_mix, w_mem, ffn_gain, w_group, b_group, w_router, b_router, T, tm):
    N = h.shape[0]
    tiles_per_batch = T // tm
    wr = jnp.zeros((D_MODEL, 2 * LANES), F32)
    wr = wr.at[:, :N_GROUPS].set(w_group).at[:, LANES:LANES + N_EXPERTS].set(w_router)
    br = jnp.zeros((1, 2 * LANES), F32)
    br = br.at[0, :N_GROUPS].set(b_group).at[0, LANES:LANES + N_EXPERTS].set(b_router)
    wr_hi, wr_lo = _split2(wr)
    row = lambda width: pl.BlockSpec((tm, width), lambda i: (i, 0))
    full = lambda a: pl.BlockSpec(a.shape, lambda i: (0,) * a.ndim)
    memspec = pl.BlockSpec((1, N_MEM, MEM_WIDTH), lambda i: (i // tiles_per_batch, 0, 0))
    w_mix = w_mix.astype(BF16)
    w_mem = w_mem.astype(BF16)
    fg = ffn_gain.reshape(1, D_MODEL)
    return pl.pallas_call(
        _out_kernel,
        out_shape=(jax.ShapeDtypeStruct((N, D_MODEL), F32), jax.ShapeDtypeStruct((N, D_MODEL), BF16),
                   jax.ShapeDtypeStruct((N, LANES), F32)),
        grid=(N // tm,),
        in_specs=[row(mix.shape[1]), row(MEM_WIDTH), memspec, memspec, row(D_MODEL),
                  full(w_mix), full(w_mem), full(fg), full(wr_hi), full(wr_lo), full(br)],
        out_specs=(row(D_MODEL), row(D_MODEL), row(LANES)),
        compiler_params=pltpu.CompilerParams(dimension_semantics=("arbitrary",),
                                             vmem_limit_bytes=VMEM_LIMIT),
        name="out_proj_route",
    )(mix, mq, mem_k, mem_v, h, w_mix, w_mem, fg, wr_hi, wr_lo, br)


def _moe_kernel(final, un_ref, route_ref, h1_ref, ltri_ref, wup_ref, wdn_ref, fgain_ref, o_ref,
                xs_ref, ys_ref, gs_ref):
    tm = un_ref.shape[0]
    slots = xs_ref.shape[0]
    route = route_ref[...]
    lane = lax.broadcasted_iota(jnp.int32, (tm, LANES), 1)
    lane1 = lax.broadcasted_iota(jnp.int32, (1, LANES), 1)
    g_idx = jnp.sum(jnp.where(lane == ROUTE_GROUP_LANE, route, 0.0), axis=-1, keepdims=True)
    member = jnp.where((lane < N_GROUPS) & (lane.astype(F32) == g_idx), 1.0, 0.0)
    earlier = _dot(ltri_ref[...], member.astype(BF16))
    rank = jnp.sum(member * earlier, axis=-1, keepdims=True)
    count = jnp.sum(member, axis=0, keepdims=True)
    n_blocks = jnp.floor((count + (MOE_BLOCK - 1.0)) * (1.0 / MOE_BLOCK))
    padded = n_blocks * MOE_BLOCK
    seg = jnp.zeros((1, LANES), F32)
    for g in range(1, N_GROUPS):
        seg = jnp.where(lane1 == g, jnp.sum(jnp.where(lane1 < g, padded, 0.0), axis=-1, keepdims=True), seg)
    dest = rank + jnp.sum(member * seg, axis=-1, keepdims=True)

    d_hi = jnp.floor(dest * (1.0 / 32.0))
    d_lo = dest - 32.0 * d_hi
    digits = jnp.where(lane == 0, d_hi, jnp.where(lane == 1, d_lo, 0.0)).astype(BF16)
    pick = (lax.broadcasted_iota(jnp.int32, (SUBLANES, LANES), 0)
            == lax.broadcasted_iota(jnp.int32, (SUBLANES, LANES), 1)).astype(BF16)
    dig_t = _dot_nt(pick, digits)
    dest_row = (32.0 * dig_t[0:1] + dig_t[1:2]).astype(jnp.int32)
    dest_col = dest.astype(jnp.int32)

    perm = jnp.where(lax.broadcasted_iota(jnp.int32, (slots, tm), 0) == dest_row, 1.0, 0.0).astype(BF16)
    xs_ref[...] = _dot(perm, un_ref[...]).astype(BF16)
    r_hi, r_lo = _split2(route)
    gs_ref[...] = _dot(perm, r_hi) + _dot(perm, r_lo)
    ys_ref[...] = jnp.zeros_like(ys_ref)

    lane_b = lax.broadcasted_iota(jnp.int32, (MOE_BLOCK, LANES), 1)
    for g in range(N_GROUPS):
        nb = jnp.sum(jnp.where(lane1 == g, n_blocks, 0.0)).astype(jnp.int32)
        s0 = jnp.sum(jnp.where(lane1 == g, seg, 0.0)).astype(jnp.int32)

        def block_body(i, carry, g=g, s0=s0):
            r0 = pl.multiple_of(s0 + i * MOE_BLOCK, MOE_BLOCK)
            xb = xs_ref[pl.ds(r0, MOE_BLOCK), :]
            gsb = gs_ref[pl.ds(r0, MOE_BLOCK), :]
            acc = jnp.zeros((MOE_BLOCK, D_MODEL), F32)
            for j in range(EXPERTS_PER_GROUP):
                e = g * EXPERTS_PER_GROUP + j
                hh = _dot(xb, wup_ref[e])
                gate = jnp.sum(jnp.where(lane_b == j, gsb, 0.0), axis=-1, keepdims=True)
                act = _silu(hh[:, :EXPERT_FF]) * hh[:, EXPERT_FF:] * gate
                acc = acc + _dot(act.astype(BF16), wdn_ref[e])
            ys_ref[pl.ds(r0, MOE_BLOCK), :] = acc.astype(BF16)
            return carry

        lax.fori_loop(0, nb, block_body, 0)

    perm_t = jnp.where(lax.broadcasted_iota(jnp.int32, (tm, slots), 1) == dest_col, 1.0, 0.0).astype(BF16)
    y = h1_ref[...] + _dot(perm_t, ys_ref[...])
    o_ref[...] = _rms(y, fgain_ref[...]) if final else y


def _moe(un, route, h1, w_up, w_down, final_gain, final, tm):
    N = un.shape[0]
    slots = tm + N_GROUPS * MOE_BLOCK
    ltri = (jnp.arange(tm)[:, None] > jnp.arange(tm)[None, :]).astype(BF16)
    row = lambda width: pl.BlockSpec((tm, width), lambda i: (i, 0))
    once = lambda a: pl.BlockSpec(a.shape, lambda i: (0,) * a.ndim, pipeline_mode=pl.Buffered(1))
    w_up = w_up.astype(BF16)
    w_down = w_down.astype(BF16)
    fg = final_gain.reshape(1, D_MODEL)
    return pl.pallas_call(
        functools.partial(_moe_kernel, final),
        out_shape=jax.ShapeDtypeStruct((N, D_MODEL), F32),
        grid=(N // tm,),
        in_specs=[row(D_MODEL), row(LANES), row(D_MODEL), once(ltri), once(w_up), once(w_down), once(fg)],
        out_specs=row(D_MODEL),
        scratch_shapes=[pltpu.VMEM((slots, D_MODEL), BF16), pltpu.VMEM((slots, D_MODEL), BF16),
                        pltpu.VMEM((slots, LANES), F32)],
        compiler_params=pltpu.CompilerParams(dimension_semantics=("arbitrary",),
                                             vmem_limit_bytes=VMEM_LIMIT),
        name="moe",
    )(un, route, h1, ltri, w_up, w_down, fg)


def kernel(x, mem, mix_norm, ffn_norm, mem_norm, final_norm, gla_w_in, gla_w_gate_up, gla_b_gate, gla_out_norm,
           hg_w_in, hg_lower_bounds, hg_out_norm, w_mem_kv, w_out, w_group, b_group, w_router, b_router,
           w_up, w_down):
    B, T, _ = x.shape
    N = B * T
    depth = mix_norm.shape[0]
    tm_proj = min(256, T)
    tb_scan = min(256, T)
    tm_out = min(512, T)
    tm_moe = min(512, N)
    h = x.reshape(N, D_MODEL)
    for layer in range(depth):
        j = layer // 2
        if layer % 2 == 0:
            q, k, v, sg, b, mq = _gla_proj(h, mix_norm[layer], gla_w_in[j], gla_w_gate_up[j], gla_b_gate[j], tm_proj)
            heads, dk, dv_pad, dv, out_gain = GLA_HEADS, GLA_DK, GLA_DV_PAD, GLA_DV, gla_out_norm[j]
        else:
            q, k, v, sg, b, mq = _hg_proj(h, mix_norm[layer], hg_w_in[j], hg_lower_bounds, layer, tm_proj)
            heads, dk, dv_pad, dv, out_gain = HG_HEADS, HG_DK, HG_DV, HG_DV, hg_out_norm[j]
        r3 = lambda a: a.reshape(B, T, a.shape[-1])
        mix = _scan(r3(q), r3(k), r3(b), r3(v), r3(sg), out_gain, heads, dk, dv_pad, dv, tb_scan)
        mix = mix.reshape(N, heads * dv_pad)
        mem_k, mem_v = _mem_kv(mem, mem_norm[layer], w_mem_kv[layer])
        w_mix = _pad_heads(w_out[layer, :MIX_WIDTH].T, heads, dv, dv_pad).T
        h1, un, gates = _out_call(mix, mq, mem_k, mem_v, h, w_mix, w_out[layer, MIX_WIDTH:], ffn_norm[layer],
                                  w_group[layer], b_group[layer], w_router[layer], b_router[layer], T, tm_out)
        h = _moe(un, gates, h1, w_up[layer], w_down[layer], final_norm, layer == depth - 1, tm_moe)
    return h.reshape(B, T, D_MODEL)
```

```python
import functools

import jax
import jax.numpy as jnp
from jax import lax
from jax.experimental import pallas as pl
from jax.experimental.pallas import tpu as pltpu

D_MODEL = 1024
N_MEM = 256
MIX_WIDTH = 768
MEM_HEADS = 4
MEM_HEAD_DIM = 64
MEM_WIDTH = 256
CHUNK = 64
GLA_HEADS = 4
GLA_KEY_WIDTH = 512
GLA_DK = 128
GLA_DV = 192
GLA_DV_PAD = 256
GLA_GATE_RANK = 16
GLA_GATE_NORMALIZER = 16.0
HG_HEADS = 6
HG_DK = 128
HG_DV = 128
N_GROUPS = 4
EXPERTS_PER_GROUP = 4
N_EXPERTS = 16
EXPERT_FF = 256
NORM_EPS = 1e-6
LANES = 128
SUBLANES = 8
MOE_BLOCK = 128
ROUTE_GROUP_ROW = 4
ROUTE_ROWS = 32
SAFE_LOG_DECAY = -60.0
VMEM_LIMIT = 56 * 1024 * 1024

F32 = jnp.float32
BF16 = jnp.bfloat16


def _dot(a, b):
    return jnp.dot(a, b, preferred_element_type=F32)


def _dot_nt(a, b):
    return lax.dot_general(a, b, (((1,), (1,)), ((), ())), preferred_element_type=F32)


def _dot_tn(a, b):
    return lax.dot_general(a, b, (((0,), (0,)), ((), ())), preferred_element_type=F32)


def _split2(x):
    hi = x.astype(BF16)
    lo = (x - hi.astype(F32)).astype(BF16)
    return hi, lo


def _split3(x):
    hi = x.astype(BF16)
    r = x - hi.astype(F32)
    mid = r.astype(BF16)
    lo = (r - mid.astype(F32)).astype(BF16)
    return hi, mid, lo


def _rms(x, gain):
    ms = jnp.mean(x * x, axis=-1, keepdims=True)
    return x * lax.rsqrt(ms + NORM_EPS) * gain


def _log_sigmoid(x):
    return jnp.minimum(x, 0.0) - jnp.log1p(jnp.exp(-jnp.abs(x)))


def _sigmoid(x):
    return 1.0 / (1.0 + jnp.exp(-x))


def _silu(x):
    return x * _sigmoid(x)


def _chunk_cumsum(tri_ref, x):
    tri = tri_ref[...]
    hi, mid, lo = _split3(x)
    return _dot(tri, hi) + _dot(tri, mid) + _dot(tri, lo)


def _mem_kv_kernel(mem_ref, gain_ref, w_ref, k_ref, v_ref):
    m = _rms(mem_ref[0], gain_ref[...]).astype(BF16)
    kv = _dot(m, w_ref[...])
    k_ref[0] = kv[:, :MEM_WIDTH].astype(BF16)
    v_ref[0] = kv[:, MEM_WIDTH:].astype(BF16)


def _mem_kv(mem, gain, w_kv):
    B = mem.shape[0]
    return pl.pallas_call(
        _mem_kv_kernel,
        out_shape=(jax.ShapeDtypeStruct((B, N_MEM, MEM_WIDTH), BF16),
                   jax.ShapeDtypeStruct((B, N_MEM, MEM_WIDTH), BF16)),
        grid=(B,),
        in_specs=[pl.BlockSpec((1, N_MEM, D_MODEL), lambda b: (b, 0, 0)),
                  pl.BlockSpec((1, D_MODEL), lambda b: (0, 0)),
                  pl.BlockSpec((D_MODEL, 2 * MEM_WIDTH), lambda b: (0, 0))],
        out_specs=(pl.BlockSpec((1, N_MEM, MEM_WIDTH), lambda b: (b, 0, 0)),
                   pl.BlockSpec((1, N_MEM, MEM_WIDTH), lambda b: (b, 0, 0))),
        compiler_params=pltpu.CompilerParams(dimension_semantics=("arbitrary",)),
        name="mem_kv",
    )(mem, gain.reshape(1, D_MODEL), w_kv.astype(BF16))


_GQ, _GK = 0, GLA_KEY_WIDTH
_GV = 2 * GLA_KEY_WIDTH
_GG = _GV + GLA_HEADS * GLA_DV_PAD
_GR = _GG + GLA_HEADS * GLA_DV_PAD
_GM = _GR + LANES
_GW = _GM + MEM_WIDTH


def _gla_proj_kernel(h_ref, gain_ref, w_ref, wg_hi_ref, wg_lo_ref, bg_ref, tri_ref,
                     q_ref, k_ref, v_ref, sg_ref, b_ref, mq_ref):
    u = _rms(h_ref[...], gain_ref[...]).astype(BF16)
    q_ref[...] = (_dot(u, w_ref[:, _GQ:_GK]) * (GLA_DK ** -0.5)).astype(BF16)
    k_ref[...] = _dot(u, w_ref[:, _GK:_GV]).astype(BF16)
    v_ref[...] = _dot(u, w_ref[:, _GV:_GG]).astype(BF16)
    sg_ref[...] = _silu(_dot(u, w_ref[:, _GG:_GR])).astype(BF16)
    mq_ref[...] = _dot(u, w_ref[:, _GM:_GW]).astype(BF16)
    r_hi, r_lo = _split2(_dot(u, w_ref[:, _GR:_GM]))
    wg_hi = wg_hi_ref[...]
    logit = _dot(r_hi, wg_hi) + _dot(r_hi, wg_lo_ref[...]) + _dot(r_lo, wg_hi) + bg_ref[...]
    b_ref[...] = _chunk_cumsum(tri_ref, _log_sigmoid(logit) * (1.0 / GLA_GATE_NORMALIZER))


_HQ, _HF, _HI, _HGG, _HM = 0, MIX_WIDTH, 2 * MIX_WIDTH, 3 * MIX_WIDTH, 4 * MIX_WIDTH
_HW = _HM + MEM_WIDTH


def _hg_proj_kernel(layer, h_ref, gain_ref, w_ref, lbp_ref, tri_ref,
                    q_ref, k_ref, v_ref, sg_ref, b_ref, mq_ref):
    u = _rms(h_ref[...], gain_ref[...]).astype(BF16)
    p = lbp_ref[...]
    p = jnp.exp(p - jnp.max(p, axis=0, keepdims=True))
    p = p / jnp.sum(p, axis=0, keepdims=True)
    lb = jnp.sum(p[0:layer + 1], axis=0, keepdims=True) - p[0:1]
    q_ref[...] = _silu(_dot(u, w_ref[:, _HQ:_HF])).astype(BF16)
    z = _dot(u, w_ref[:, _HF:_HI])
    k_ref[...] = ((1.0 - lb) * _sigmoid(-z)).astype(BF16)
    v_ref[...] = _dot(u, w_ref[:, _HI:_HGG]).astype(BF16)
    sg_ref[...] = _silu(_dot(u, w_ref[:, _HGG:_HM])).astype(BF16)
    mq_ref[...] = _dot(u, w_ref[:, _HM:_HW]).astype(BF16)
    a = jnp.log(lb)
    c = jnp.log1p(-lb) + _log_sigmoid(z)
    log_f = jnp.maximum(a, c) + jnp.log1p(jnp.exp(-jnp.abs(a - c)))
    b_ref[...] = _chunk_cumsum(tri_ref, log_f)


def _proj_call(kernel, h, gain, w, extra, kw, vw, tm, name):
    N = h.shape[0]
    tri = (jnp.arange(tm)[:, None] >= jnp.arange(tm)[None, :]) & (
        jnp.arange(tm)[:, None] // CHUNK == jnp.arange(tm)[None, :] // CHUNK)
    tri = tri.astype(BF16)
    row = lambda width: pl.BlockSpec((tm, width), lambda i: (i, 0))
    full = lambda a: pl.BlockSpec(a.shape, lambda i: (0,) * a.ndim)
    ins = [h, gain.reshape(1, D_MODEL), w] + list(extra) + [tri]
    return pl.pallas_call(
        kernel,
        out_shape=(jax.ShapeDtypeStruct((N, kw), BF16), jax.ShapeDtypeStruct((N, kw), BF16),
                   jax.ShapeDtypeStruct((N, vw), BF16), jax.ShapeDtypeStruct((N, vw), BF16),
                   jax.ShapeDtypeStruct((N, kw), F32), jax.ShapeDtypeStruct((N, MEM_WIDTH), BF16)),
        grid=(N // tm,),
        in_specs=[row(D_MODEL)] + [full(a) for a in ins[1:]],
        out_specs=(row(kw), row(kw), row(vw), row(vw), row(kw), row(MEM_WIDTH)),
        compiler_params=pltpu.CompilerParams(dimension_semantics=("arbitrary",),
                                             vmem_limit_bytes=VMEM_LIMIT),
        name=name,
    )(*ins)


def _pad_heads(w, heads, dv, dv_pad):
    lead = w.shape[:-1]
    w = w.reshape(lead + (heads, dv))
    w = jnp.pad(w, [(0, 0)] * len(lead) + [(0, 0), (0, dv_pad - dv)])
    return w.reshape(lead + (heads * dv_pad,))


def _gla_proj(h, gain, w_in, w_gate_up, b_gate, tm):
    q, k, v, g, r, mq = jnp.split(
        w_in, [GLA_KEY_WIDTH, 2 * GLA_KEY_WIDTH, 2 * GLA_KEY_WIDTH + MIX_WIDTH,
               2 * GLA_KEY_WIDTH + 2 * MIX_WIDTH, 2 * GLA_KEY_WIDTH + 2 * MIX_WIDTH + GLA_GATE_RANK], axis=1)
    w = jnp.concatenate([q, k, _pad_heads(v, GLA_HEADS, GLA_DV, GLA_DV_PAD),
                         _pad_heads(g, GLA_HEADS, GLA_DV, GLA_DV_PAD),
                         jnp.pad(r, ((0, 0), (0, LANES - GLA_GATE_RANK))), mq], axis=1).astype(BF16)
    wg = jnp.pad(w_gate_up, ((0, LANES - GLA_GATE_RANK), (0, 0)))
    wg_hi, wg_lo = _split2(wg)
    return _proj_call(_gla_proj_kernel, h, gain, w, [wg_hi, wg_lo, b_gate.reshape(1, GLA_KEY_WIDTH)],
                      GLA_KEY_WIDTH, GLA_HEADS * GLA_DV_PAD, tm, "gla_proj")


def _hg_proj(h, gain, w_in, lower_bound_params, layer, tm):
    return _proj_call(functools.partial(_hg_proj_kernel, layer), h, gain, w_in.astype(BF16),
                      [lower_bound_params], MIX_WIDTH, MIX_WIDTH, tm, "hg_proj")


def _scan_kernel(heads, dk, dv_pad, dv, n_chunks,
                 q_ref, k_ref, b_ref, v_ref, sg_ref, gain_ref, o_ref, st_ref, kf_ref):
    @pl.when(pl.program_id(1) == 0)
    def _():
        st_ref[...] = jnp.zeros_like(st_ref)

    gain = gain_ref[...]
    row = lax.broadcasted_iota(jnp.int32, (CHUNK, CHUNK), 0)
    col = lax.broadcasted_iota(jnp.int32, (CHUNK, CHUNK), 1)
    causal = row >= col

    def finish(hd, rows, scores, q, k, b, eb):
        vc = slice(hd * dv_pad, (hd + 1) * dv_pad)
        v = v_ref[0, rows, vc]
        st = st_ref[hd]
        b_last = b[CHUNK - 1:CHUNK, :]
        qd = (q * eb).astype(BF16)
        out = _dot(scores.astype(BF16), v) + _dot_nt(qd, st.astype(BF16))
        kl = (k * jnp.exp(b_last - b)).astype(BF16)
        st_ref[hd] = st * jnp.exp(b_last) + _dot_tn(v, kl)
        ms = jnp.sum(out * out, axis=-1, keepdims=True) * (1.0 / dv)
        y = out * lax.rsqrt(ms + NORM_EPS) * gain * sg_ref[0, rows, vc].astype(F32)
        o_ref[0, rows, vc] = y.astype(BF16)

    def load(hd, rows):
        kc = slice(hd * dk, (hd + 1) * dk)
        return (q_ref[0, rows, kc].astype(F32), k_ref[0, rows, kc].astype(F32), b_ref[0, rows, kc])

    all_safe = jnp.min(b_ref[...]) >= SAFE_LOG_DECAY

    @pl.when(all_safe)
    def _():
        for c in range(n_chunks):
            rows = slice(c * CHUNK, (c + 1) * CHUNK)
            for hd in range(heads):
                q, k, b = load(hd, rows)
                eb = jnp.exp(b)
                kd = (k * jnp.exp(-b)).astype(BF16)
                scores = _dot_nt((q * eb).astype(BF16), kd)
                finish(hd, rows, jnp.where(causal, scores, 0.0), q, k, b, eb)

    @pl.when(jnp.logical_not(all_safe))
    def _():
        for hd in range(heads):
            kc = slice(hd * dk, (hd + 1) * dk)

            def chunk_body(c, carry):
                start = pl.multiple_of(c * CHUNK, CHUNK)
                rows = pl.ds(start, CHUNK)
                q, k, b = load(hd, rows)
                kf_ref[...] = k

                def col_body(j, sc):
                    base = pl.multiple_of((j >> 3) << 3, SUBLANES)
                    pick = lax.broadcasted_iota(jnp.int32, (SUBLANES, dk), 0) == (j & (SUBLANES - 1))
                    kj = jnp.sum(jnp.where(pick, kf_ref[pl.ds(base, SUBLANES), :], 0.0), axis=0, keepdims=True)
                    b8 = b_ref[0, pl.ds(pl.multiple_of(start + base, SUBLANES), SUBLANES), kc]
                    bj = jnp.sum(jnp.where(pick, b8, 0.0), axis=0, keepdims=True)
                    rid = lax.broadcasted_iota(jnp.int32, (CHUNK, dk), 0)
                    dec = jnp.exp(jnp.where(rid >= j, b - bj, -jnp.inf))
                    colv = jnp.sum(q * kj * dec, axis=-1, keepdims=True)
                    return jnp.where(col == j, colv, sc)

                scores = lax.fori_loop(0, CHUNK, col_body, jnp.zeros((CHUNK, CHUNK), F32))
                finish(hd, rows, scores, q, k, b, jnp.exp(b))
                return carry

            lax.fori_loop(0, n_chunks, chunk_body, 0)


def _scan(q, k, b, v, sg, gain, heads, dk, dv_pad, dv, tb):
    B, T, kw = q.shape
    vw = v.shape[-1]
    blk = lambda w: pl.BlockSpec((1, tb, w), lambda bi, ti: (bi, ti, 0))
    gain_row = jnp.pad(gain, (0, dv_pad - dv)).reshape(1, dv_pad)
    return pl.pallas_call(
        functools.partial(_scan_kernel, heads, dk, dv_pad, dv, tb // CHUNK),
        out_shape=jax.ShapeDtypeStruct((B, T, vw), BF16),
        grid=(B, T // tb),
        in_specs=[blk(kw), blk(kw), blk(kw), blk(vw), blk(vw),
                  pl.BlockSpec((1, dv_pad), lambda bi, ti: (0, 0))],
        out_specs=blk(vw),
        scratch_shapes=[pltpu.VMEM((heads, dv_pad, dk), F32), pltpu.VMEM((CHUNK, dk), F32)],
        compiler_params=pltpu.CompilerParams(dimension_semantics=("arbitrary", "arbitrary"),
                                             vmem_limit_bytes=VMEM_LIMIT),
        name="chunk_scan",
    )(q, k, b, v, sg, gain_row)


def _out_kernel(mix_ref, mq_ref, mk_ref, mv_ref, h_ref, wmix_ref, wmem_ref, fgain_ref,
                wr_hi_ref, wr_lo_ref, br_ref, h1_ref, un_ref, route_ref):
    tm = mix_ref.shape[0]
    mk = mk_ref[0]
    mv = mv_ref[0]
    lane_w = lax.broadcasted_iota(jnp.int32, (1, MEM_WIDTH), 1)
    mq = mq_ref[...] * (MEM_HEAD_DIM ** -0.5)
    mem_o = jnp.zeros((tm, MEM_WIDTH), F32)
    for hd in range(MEM_HEADS):
        in_head = (lane_w >= hd * MEM_HEAD_DIM) & (lane_w < (hd + 1) * MEM_HEAD_DIM)
        s = _dot_nt(jnp.where(in_head, mq, jnp.zeros_like(mq)), mk)
        e = jnp.exp(s - jnp.max(s, axis=-1, keepdims=True))
        denom = jnp.sum(e, axis=-1, keepdims=True)
        mem_o = mem_o + _dot(e.astype(BF16), jnp.where(in_head, mv, jnp.zeros_like(mv))) * (1.0 / denom)
    h1 = h_ref[...] + _dot(mix_ref[...], wmix_ref[...]) + _dot(mem_o.astype(BF16), wmem_ref[...])
    h1_ref[...] = h1
    un = _rms(h1, fgain_ref[...])
    un_ref[...] = un.astype(BF16)

    u_hi, u_lo = _split2(un)
    wr_hi = wr_hi_ref[...]
    lg = _dot_nt(wr_hi, u_hi) + _dot_nt(wr_lo_ref[...], u_hi) + _dot_nt(wr_hi, u_lo) + br_ref[...]
    row = lax.broadcasted_iota(jnp.int32, (ROUTE_ROWS, tm), 0)
    neg = -jnp.inf
    gl = jnp.where(row < N_GROUPS, lg, neg)
    gmax = jnp.max(gl, axis=0, keepdims=True)
    g_idx = jnp.min(jnp.where(gl == gmax, row, ROUTE_ROWS), axis=0, keepdims=True)
    g_w = 1.0 / jnp.sum(jnp.exp(gl - gmax), axis=0, keepdims=True)
    first = N_GROUPS + g_idx * EXPERTS_PER_GROUP
    el = jnp.where((row >= first) & (row < first + EXPERTS_PER_GROUP), lg, neg)
    t1 = jnp.max(el, axis=0, keepdims=True)
    i1 = jnp.min(jnp.where(el == t1, row, ROUTE_ROWS), axis=0, keepdims=True)
    el2 = jnp.where(row == i1, neg, el)
    t2 = jnp.max(el2, axis=0, keepdims=True)
    i2 = jnp.min(jnp.where(el2 == t2, row, ROUTE_ROWS), axis=0, keepdims=True)
    e2 = jnp.exp(t2 - t1)
    w1 = g_w / (1.0 + e2)
    w2 = g_w * e2 / (1.0 + e2)
    row8 = lax.broadcasted_iota(jnp.int32, (SUBLANES, tm), 0)
    route_ref[...] = (jnp.where(row8 == i1 - first, w1, 0.0) + jnp.where(row8 == i2 - first, w2, 0.0)
                      + jnp.where(row8 == ROUTE_GROUP_ROW, g_idx.astype(F32), 0.0))


def _out_call(mix, mq, mem_k, mem_v, h, w_mix, w_mem, ffn_gain, w_group, b_group, w_router, b_router, T, tm):
    N = h.shape[0]
    tiles_per_batch = T // tm
    pad = ROUTE_ROWS - N_GROUPS - N_EXPERTS
    wr = jnp.pad(jnp.concatenate([w_group, w_router], axis=1).T, ((0, pad), (0, 0)))
    br = jnp.broadcast_to(jnp.pad(jnp.concatenate([b_group, b_router]), (0, pad))[:, None], (ROUTE_ROWS, tm))
    wr_hi, wr_lo = _split2(wr)
    row = lambda width: pl.BlockSpec((tm, width), lambda i: (i, 0))
    full = lambda a: pl.BlockSpec(a.shape, lambda i: (0,) * a.ndim)
    memspec = pl.BlockSpec((1, N_MEM, MEM_WIDTH), lambda i: (i // tiles_per_batch, 0, 0))
    w_mix = w_mix.astype(BF16)
    w_mem = w_mem.astype(BF16)
    fg = ffn_gain.reshape(1, D_MODEL)
    return pl.pallas_call(
        _out_kernel,
        out_shape=(jax.ShapeDtypeStruct((N, D_MODEL), F32), jax.ShapeDtypeStruct((N, D_MODEL), BF16),
                   jax.ShapeDtypeStruct((SUBLANES, N), F32)),
        grid=(N // tm,),
        in_specs=[row(mix.shape[1]), row(MEM_WIDTH), memspec, memspec, row(D_MODEL),
                  full(w_mix), full(w_mem), full(fg), full(wr_hi), full(wr_lo), full(br)],
        out_specs=(row(D_MODEL), row(D_MODEL), pl.BlockSpec((SUBLANES, tm), lambda i: (0, i))),
        compiler_params=pltpu.CompilerParams(dimension_semantics=("arbitrary",),
                                             vmem_limit_bytes=VMEM_LIMIT),
        name="out_proj_route",
    )(mix, mq, mem_k, mem_v, h, w_mix, w_mem, fg, wr_hi, wr_lo, br)


def _moe_kernel(final, un_ref, route_ref, h1_ref, utri_ref, wup_ref, wdn_ref, fgain_ref, o_ref,
                xs_ref, ys_ref, gs_ref):
    tm = un_ref.shape[0]
    slots = xs_ref.shape[0]
    route = route_ref[...]
    row8 = lax.broadcasted_iota(jnp.int32, (SUBLANES, tm), 0)
    g_idx = route[ROUTE_GROUP_ROW:ROUTE_GROUP_ROW + 1, :]
    member = jnp.where((row8 < N_GROUPS) & (row8.astype(F32) == g_idx), 1.0, 0.0)
    earlier = _dot(member.astype(BF16), utri_ref[...])
    rank = jnp.sum(member * earlier, axis=0, keepdims=True)
    count = jnp.sum(member, axis=1, keepdims=True)
    n_blocks = jnp.floor((count + (MOE_BLOCK - 1.0)) * (1.0 / MOE_BLOCK))
    padded = n_blocks * MOE_BLOCK
    starts = [jnp.zeros((1, 1), F32)]
    for g in range(1, N_GROUPS):
        starts.append(starts[-1] + padded[g - 1:g, :])
    seg = jnp.zeros((SUBLANES, 1), F32)
    row81 = lax.broadcasted_iota(jnp.int32, (SUBLANES, 1), 0)
    for g in range(1, N_GROUPS):
        seg = jnp.where(row81 == g, starts[g], seg)
    dest = rank + jnp.sum(member * seg, axis=0, keepdims=True)
    dest_row = dest.astype(jnp.int32)

    d_hi = jnp.floor(dest * (1.0 / 32.0))
    d_lo = dest - 32.0 * d_hi
    digits = jnp.where(row8 == 0, d_hi, jnp.where(row8 == 1, d_lo, 0.0)).astype(BF16)
    pick = (lax.broadcasted_iota(jnp.int32, (SUBLANES, LANES), 0)
            == lax.broadcasted_iota(jnp.int32, (SUBLANES, LANES), 1)).astype(BF16)
    dig_c = _dot_tn(digits, pick)
    dest_col = (32.0 * dig_c[:, 0:1] + dig_c[:, 1:2]).astype(jnp.int32)

    perm = jnp.where(lax.broadcasted_iota(jnp.int32, (slots, tm), 0) == dest_row, 1.0, 0.0).astype(BF16)
    xs_ref[...] = _dot(perm, un_ref[...]).astype(BF16)
    r_hi, r_lo = _split2(route)
    gs_ref[...] = _dot_nt(perm, jnp.concatenate([r_hi, r_lo], axis=0))
    ys_ref[...] = jnp.zeros_like(ys_ref)

    for g in range(N_GROUPS):
        nb = n_blocks[g, 0].astype(jnp.int32)
        s0 = starts[g][0, 0].astype(jnp.int32)

        def block_body(i, carry, g=g, s0=s0):
            r0 = pl.multiple_of(s0 + i * MOE_BLOCK, MOE_BLOCK)
            xb = xs_ref[pl.ds(r0, MOE_BLOCK), :]
            gsb = gs_ref[pl.ds(r0, MOE_BLOCK), :]
            acc = jnp.zeros((MOE_BLOCK, D_MODEL), F32)
            for j in range(EXPERTS_PER_GROUP):
                e = g * EXPERTS_PER_GROUP + j
                hh = _dot(xb, wup_ref[e])
                gate = gsb[:, j:j + 1] + gsb[:, SUBLANES + j:SUBLANES + j + 1]
                act = _silu(hh[:, :EXPERT_FF]) * hh[:, EXPERT_FF:] * gate
                acc = acc + _dot(act.astype(BF16), wdn_ref[e])
            ys_ref[pl.ds(r0, MOE_BLOCK), :] = acc.astype(BF16)
            return carry

        lax.fori_loop(0, nb, block_body, 0)

    perm_t = jnp.where(lax.broadcasted_iota(jnp.int32, (tm, slots), 1) == dest_col, 1.0, 0.0).astype(BF16)
    y = h1_ref[...] + _dot(perm_t, ys_ref[...])
    o_ref[...] = _rms(y, fgain_ref[...]) if final else y


def _moe(un, route, h1, w_up, w_down, final_gain, final, tm):
    N = un.shape[0]
    slots = tm + N_GROUPS * MOE_BLOCK
    utri = (jnp.arange(tm)[:, None] < jnp.arange(tm)[None, :]).astype(BF16)
    row = lambda width: pl.BlockSpec((tm, width), lambda i: (i, 0))
    once = lambda a: pl.BlockSpec(a.shape, lambda i: (0,) * a.ndim, pipeline_mode=pl.Buffered(1))
    w_up = w_up.astype(BF16)
    w_down = w_down.astype(BF16)
    fg = final_gain.reshape(1, D_MODEL)
    return pl.pallas_call(
        functools.partial(_moe_kernel, final),
        out_shape=jax.ShapeDtypeStruct((N, D_MODEL), F32),
        grid=(N // tm,),
        in_specs=[row(D_MODEL), pl.BlockSpec((SUBLANES, tm), lambda i: (0, i)), row(D_MODEL),
                  once(utri), once(w_up), once(w_down), once(fg)],
        out_specs=row(D_MODEL),
        scratch_shapes=[pltpu.VMEM((slots, D_MODEL), BF16), pltpu.VMEM((slots, D_MODEL), BF16),
                        pltpu.VMEM((slots, 2 * SUBLANES), F32)],
        compiler_params=pltpu.CompilerParams(dimension_semantics=("arbitrary",),
                                             vmem_limit_bytes=VMEM_LIMIT),
        name="moe",
    )(un, route, h1, utri, w_up, w_down, fg)


def kernel(x, mem, mix_norm, ffn_norm, mem_norm, final_norm, gla_w_in, gla_w_gate_up, gla_b_gate, gla_out_norm,
           hg_w_in, hg_lower_bounds, hg_out_norm, w_mem_kv, w_out, w_group, b_group, w_router, b_router,
           w_up, w_down):
    B, T, _ = x.shape
    N = B * T
    depth = mix_norm.shape[0]
    tm_proj = min(256, T)
    tb_scan = min(256, T)
    tm_out = min(512, T)
    tm_moe = min(512, N)
    h = x.reshape(N, D_MODEL)
    for layer in range(depth):
        j = layer // 2
        if layer % 2 == 0:
            q, k, v, sg, b, mq = _gla_proj(h, mix_norm[layer], gla_w_in[j], gla_w_gate_up[j], gla_b_gate[j], tm_proj)
            heads, dk, dv_pad, dv, out_gain = GLA_HEADS, GLA_DK, GLA_DV_PAD, GLA_DV, gla_out_norm[j]
        else:
            q, k, v, sg, b, mq = _hg_proj(h, mix_norm[layer], hg_w_in[j], hg_lower_bounds, layer, tm_proj)
            heads, dk, dv_pad, dv, out_gain = HG_HEADS, HG_DK, HG_DV, HG_DV, hg_out_norm[j]
        r3 = lambda a: a.reshape(B, T, a.shape[-1])
        mix = _scan(r3(q), r3(k), r3(b), r3(v), r3(sg), out_gain, heads, dk, dv_pad, dv, tb_scan)
        mix = mix.reshape(N, heads * dv_pad)
        mem_k, mem_v = _mem_kv(mem, mem_norm[layer], w_mem_kv[layer])
        w_mix = _pad_heads(w_out[layer, :MIX_WIDTH].T, heads, dv, dv_pad).T
        h1, un, gates = _out_call(mix, mq, mem_k, mem_v, h, w_mix, w_out[layer, MIX_WIDTH:], ffn_norm[layer],
                                  w_group[layer], b_group[layer], w_router[layer], b_router[layer], T, tm_out)
        h = _moe(un, gates, h1, w_up[layer], w_down[layer], final_norm, layer == depth - 1, tm_moe)
    return h.reshape(B, T, D_MODEL)
```

```python
import functools

import jax
import jax.numpy as jnp
from jax import lax
from jax.experimental import pallas as pl
from jax.experimental.pallas import tpu as pltpu

D_MODEL = 1024
N_MEM = 256
MIX_WIDTH = 768
MEM_HEADS = 4
MEM_HEAD_DIM = 64
MEM_WIDTH = 256
CHUNK = 64
GLA_HEADS = 4
GLA_KEY_WIDTH = 512
GLA_DK = 128
GLA_DV = 192
GLA_DV_PAD = 256
GLA_GATE_RANK = 16
GLA_GATE_NORMALIZER = 16.0
HG_HEADS = 6
HG_DK = 128
HG_DV = 128
N_GROUPS = 4
EXPERTS_PER_GROUP = 4
N_EXPERTS = 16
EXPERT_FF = 256
NORM_EPS = 1e-6
LANES = 128
SUBLANES = 8
MOE_BLOCK = 128
MOE_BIG_SHIFT = 8
MOE_BIG = 1 << MOE_BIG_SHIFT
MOE_ALIGN_SHIFT = 4
MOE_ALIGN = 1 << MOE_ALIGN_SHIFT
MOE_SUPER = 4
ROUTE_GROUP_ROW = 4
ROUTE_ROWS = 32
SAFE_LOG_DECAY = -60.0
VMEM_LIMIT = 56 * 1024 * 1024

F32 = jnp.float32
BF16 = jnp.bfloat16


def _dot(a, b):
    return jnp.dot(a, b, preferred_element_type=F32)


def _dot_nt(a, b):
    return lax.dot_general(a, b, (((1,), (1,)), ((), ())), preferred_element_type=F32)


def _dot_tn(a, b):
    return lax.dot_general(a, b, (((0,), (0,)), ((), ())), preferred_element_type=F32)


def _split2(x):
    hi = x.astype(BF16)
    lo = (x - hi.astype(F32)).astype(BF16)
    return hi, lo


def _split3(x):
    hi = x.astype(BF16)
    r = x - hi.astype(F32)
    mid = r.astype(BF16)
    lo = (r - mid.astype(F32)).astype(BF16)
    return hi, mid, lo


def _rms(x, gain):
    ms = jnp.mean(x * x, axis=-1, keepdims=True)
    return x * lax.rsqrt(ms + NORM_EPS) * gain


def _log_sigmoid(x):
    return jnp.minimum(x, 0.0) - jnp.log1p(jnp.exp(-jnp.abs(x)))


def _sigmoid(x):
    return 1.0 / (1.0 + jnp.exp(-x))


def _silu(x):
    return x * _sigmoid(x)


def _chunk_cumsum(tri_ref, x):
    tri = tri_ref[...]
    hi, mid, lo = _split3(x)
    return _dot(tri, hi) + _dot(tri, mid) + _dot(tri, lo)


def _mem_kv_kernel(mem_ref, gain_ref, w_ref, k_ref, v_ref):
    m = _rms(mem_ref[0], gain_ref[...]).astype(BF16)
    kv = _dot(m, w_ref[...])
    k_ref[0] = kv[:, :MEM_WIDTH].astype(BF16)
    v_ref[0] = kv[:, MEM_WIDTH:].astype(BF16)


def _mem_kv(mem, gain, w_kv):
    B = mem.shape[0]
    return pl.pallas_call(
        _mem_kv_kernel,
        out_shape=(jax.ShapeDtypeStruct((B, N_MEM, MEM_WIDTH), BF16),
                   jax.ShapeDtypeStruct((B, N_MEM, MEM_WIDTH), BF16)),
        grid=(B,),
        in_specs=[pl.BlockSpec((1, N_MEM, D_MODEL), lambda b: (b, 0, 0)),
                  pl.BlockSpec((1, D_MODEL), lambda b: (0, 0)),
                  pl.BlockSpec((D_MODEL, 2 * MEM_WIDTH), lambda b: (0, 0))],
        out_specs=(pl.BlockSpec((1, N_MEM, MEM_WIDTH), lambda b: (b, 0, 0)),
                   pl.BlockSpec((1, N_MEM, MEM_WIDTH), lambda b: (b, 0, 0))),
        compiler_params=pltpu.CompilerParams(dimension_semantics=("arbitrary",)),
        name="mem_kv",
    )(mem, gain.reshape(1, D_MODEL), w_kv.astype(BF16))


_GQ, _GK = 0, GLA_KEY_WIDTH
_GV = 2 * GLA_KEY_WIDTH
_GG = _GV + GLA_HEADS * GLA_DV_PAD
_GR = _GG + GLA_HEADS * GLA_DV_PAD
_GM = _GR + LANES
_GW = _GM + MEM_WIDTH


def _gla_proj_kernel(h_ref, gain_ref, w_ref, wg_hi_ref, wg_lo_ref, bg_ref, tri_ref,
                     q_ref, k_ref, v_ref, sg_ref, b_ref, mq_ref):
    u = _rms(h_ref[...], gain_ref[...]).astype(BF16)
    q_ref[...] = (_dot(u, w_ref[:, _GQ:_GK]) * (GLA_DK ** -0.5)).astype(BF16)
    k_ref[...] = _dot(u, w_ref[:, _GK:_GV]).astype(BF16)
    v_ref[...] = _dot(u, w_ref[:, _GV:_GG]).astype(BF16)
    sg_ref[...] = _silu(_dot(u, w_ref[:, _GG:_GR])).astype(BF16)
    mq_ref[...] = _dot(u, w_ref[:, _GM:_GW]).astype(BF16)
    r_hi, r_lo = _split2(_dot(u, w_ref[:, _GR:_GM]))
    wg_hi = wg_hi_ref[...]
    logit = _dot(r_hi, wg_hi) + _dot(r_hi, wg_lo_ref[...]) + _dot(r_lo, wg_hi) + bg_ref[...]
    b_ref[...] = _chunk_cumsum(tri_ref, _log_sigmoid(logit) * (1.0 / GLA_GATE_NORMALIZER))


_HQ, _HF, _HI, _HGG, _HM = 0, MIX_WIDTH, 2 * MIX_WIDTH, 3 * MIX_WIDTH, 4 * MIX_WIDTH
_HW = _HM + MEM_WIDTH


def _hg_proj_kernel(layer, h_ref, gain_ref, w_ref, lbp_ref, tri_ref,
                    q_ref, k_ref, v_ref, sg_ref, b_ref, mq_ref):
    u = _rms(h_ref[...], gain_ref[...]).astype(BF16)
    p = lbp_ref[...]
    p = jnp.exp(p - jnp.max(p, axis=0, keepdims=True))
    p = p / jnp.sum(p, axis=0, keepdims=True)
    lb = jnp.sum(p[0:layer + 1], axis=0, keepdims=True) - p[0:1]
    q_ref[...] = _silu(_dot(u, w_ref[:, _HQ:_HF])).astype(BF16)
    z = _dot(u, w_ref[:, _HF:_HI])
    k_ref[...] = ((1.0 - lb) * _sigmoid(-z)).astype(BF16)
    v_ref[...] = _dot(u, w_ref[:, _HI:_HGG]).astype(BF16)
    sg_ref[...] = _silu(_dot(u, w_ref[:, _HGG:_HM])).astype(BF16)
    mq_ref[...] = _dot(u, w_ref[:, _HM:_HW]).astype(BF16)
    a = jnp.log(lb)
    c = jnp.log1p(-lb) + _log_sigmoid(z)
    log_f = jnp.maximum(a, c) + jnp.log1p(jnp.exp(-jnp.abs(a - c)))
    b_ref[...] = _chunk_cumsum(tri_ref, log_f)


def _proj_call(kernel, h, gain, w, extra, kw, vw, tm, name):
    N = h.shape[0]
    tri = (jnp.arange(tm)[:, None] >= jnp.arange(tm)[None, :]) & (
        jnp.arange(tm)[:, None] // CHUNK == jnp.arange(tm)[None, :] // CHUNK)
    tri = tri.astype(BF16)
    row = lambda width: pl.BlockSpec((tm, width), lambda i: (i, 0))
    full = lambda a: pl.BlockSpec(a.shape, lambda i: (0,) * a.ndim)
    ins = [h, gain.reshape(1, D_MODEL), w] + list(extra) + [tri]
    return pl.pallas_call(
        kernel,
        out_shape=(jax.ShapeDtypeStruct((N, kw), BF16), jax.ShapeDtypeStruct((N, kw), BF16),
                   jax.ShapeDtypeStruct((N, vw), BF16), jax.ShapeDtypeStruct((N, vw), BF16),
                   jax.ShapeDtypeStruct((N, kw), F32), jax.ShapeDtypeStruct((N, MEM_WIDTH), BF16)),
        grid=(N // tm,),
        in_specs=[row(D_MODEL)] + [full(a) for a in ins[1:]],
        out_specs=(row(kw), row(kw), row(vw), row(vw), row(kw), row(MEM_WIDTH)),
        compiler_params=pltpu.CompilerParams(dimension_semantics=("arbitrary",),
                                             vmem_limit_bytes=VMEM_LIMIT),
        name=name,
    )(*ins)


def _pad_heads(w, heads, dv, dv_pad):
    lead = w.shape[:-1]
    w = w.reshape(lead + (heads, dv))
    w = jnp.pad(w, [(0, 0)] * len(lead) + [(0, 0), (0, dv_pad - dv)])
    return w.reshape(lead + (heads * dv_pad,))


def _gla_proj(h, gain, w_in, w_gate_up, b_gate, tm):
    q, k, v, g, r, mq = jnp.split(
        w_in, [GLA_KEY_WIDTH, 2 * GLA_KEY_WIDTH, 2 * GLA_KEY_WIDTH + MIX_WIDTH,
               2 * GLA_KEY_WIDTH + 2 * MIX_WIDTH, 2 * GLA_KEY_WIDTH + 2 * MIX_WIDTH + GLA_GATE_RANK], axis=1)
    w = jnp.concatenate([q, k, _pad_heads(v, GLA_HEADS, GLA_DV, GLA_DV_PAD),
                         _pad_heads(g, GLA_HEADS, GLA_DV, GLA_DV_PAD),
                         jnp.pad(r, ((0, 0), (0, LANES - GLA_GATE_RANK))), mq], axis=1).astype(BF16)
    wg = jnp.pad(w_gate_up, ((0, LANES - GLA_GATE_RANK), (0, 0)))
    wg_hi, wg_lo = _split2(wg)
    return _proj_call(_gla_proj_kernel, h, gain, w, [wg_hi, wg_lo, b_gate.reshape(1, GLA_KEY_WIDTH)],
                      GLA_KEY_WIDTH, GLA_HEADS * GLA_DV_PAD, tm, "gla_proj")


def _hg_proj(h, gain, w_in, lower_bound_params, layer, tm):
    return _proj_call(functools.partial(_hg_proj_kernel, layer), h, gain, w_in.astype(BF16),
                      [lower_bound_params], MIX_WIDTH, MIX_WIDTH, tm, "hg_proj")


def _scan_kernel(heads, dk, dv_pad, dv, n_chunks,
                 q_ref, k_ref, b_ref, v_ref, sg_ref, gain_ref, o_ref, st_ref, kf_ref):
    @pl.when(pl.program_id(1) == 0)
    def _():
        st_ref[...] = jnp.zeros_like(st_ref)

    gain = gain_ref[...]
    row = lax.broadcasted_iota(jnp.int32, (CHUNK, CHUNK), 0)
    col = lax.broadcasted_iota(jnp.int32, (CHUNK, CHUNK), 1)
    causal = row >= col

    def finish(hd, rows, scores, q, k, b, eb):
        vc = slice(hd * dv_pad, (hd + 1) * dv_pad)
        v = v_ref[0, rows, vc]
        st = st_ref[hd]
        b_last = b[CHUNK - 1:CHUNK, :]
        qd = (q * eb).astype(BF16)
        out = _dot(scores.astype(BF16), v) + _dot_nt(qd, st.astype(BF16))
        kl = (k * jnp.exp(b_last - b)).astype(BF16)
        st_ref[hd] = st * jnp.exp(b_last) + _dot_tn(v, kl)
        ms = jnp.sum(out * out, axis=-1, keepdims=True) * (1.0 / dv)
        y = out * lax.rsqrt(ms + NORM_EPS) * gain * sg_ref[0, rows, vc].astype(F32)
        o_ref[0, rows, vc] = y.astype(BF16)

    def load(hd, rows):
        kc = slice(hd * dk, (hd + 1) * dk)
        return (q_ref[0, rows, kc].astype(F32), k_ref[0, rows, kc].astype(F32), b_ref[0, rows, kc])

    all_safe = jnp.min(b_ref[...]) >= SAFE_LOG_DECAY

    @pl.when(all_safe)
    def _():
        for c in range(n_chunks):
            rows = slice(c * CHUNK, (c + 1) * CHUNK)
            for hd in range(heads):
                q, k, b = load(hd, rows)
                eb = jnp.exp(b)
                kd = (k * jnp.exp(-b)).astype(BF16)
                scores = _dot_nt((q * eb).astype(BF16), kd)
                finish(hd, rows, jnp.where(causal, scores, 0.0), q, k, b, eb)

    @pl.when(jnp.logical_not(all_safe))
    def _():
        for hd in range(heads):
            kc = slice(hd * dk, (hd + 1) * dk)

            def chunk_body(c, carry):
                start = pl.multiple_of(c * CHUNK, CHUNK)
                rows = pl.ds(start, CHUNK)
                q, k, b = load(hd, rows)
                kf_ref[...] = k

                def col_body(j, sc):
                    base = pl.multiple_of((j >> 3) << 3, SUBLANES)
                    pick = lax.broadcasted_iota(jnp.int32, (SUBLANES, dk), 0) == (j & (SUBLANES - 1))
                    kj = jnp.sum(jnp.where(pick, kf_ref[pl.ds(base, SUBLANES), :], 0.0), axis=0, keepdims=True)
                    b8 = b_ref[0, pl.ds(pl.multiple_of(start + base, SUBLANES), SUBLANES), kc]
                    bj = jnp.sum(jnp.where(pick, b8, 0.0), axis=0, keepdims=True)
                    rid = lax.broadcasted_iota(jnp.int32, (CHUNK, dk), 0)
                    dec = jnp.exp(jnp.where(rid >= j, b - bj, -jnp.inf))
                    colv = jnp.sum(q * kj * dec, axis=-1, keepdims=True)
                    return jnp.where(col == j, colv, sc)

                scores = lax.fori_loop(0, CHUNK, col_body, jnp.zeros((CHUNK, CHUNK), F32))
                finish(hd, rows, scores, q, k, b, jnp.exp(b))
                return carry

            lax.fori_loop(0, n_chunks, chunk_body, 0)


def _scan(q, k, b, v, sg, gain, heads, dk, dv_pad, dv, tb):
    B, T, kw = q.shape
    vw = v.shape[-1]
    blk = lambda w: pl.BlockSpec((1, tb, w), lambda bi, ti: (bi, ti, 0))
    gain_row = jnp.pad(gain, (0, dv_pad - dv)).reshape(1, dv_pad)
    return pl.pallas_call(
        functools.partial(_scan_kernel, heads, dk, dv_pad, dv, tb // CHUNK),
        out_shape=jax.ShapeDtypeStruct((B, T, vw), BF16),
        grid=(B, T // tb),
        in_specs=[blk(kw), blk(kw), blk(kw), blk(vw), blk(vw),
                  pl.BlockSpec((1, dv_pad), lambda bi, ti: (0, 0))],
        out_specs=blk(vw),
        scratch_shapes=[pltpu.VMEM((heads, dv_pad, dk), F32), pltpu.VMEM((CHUNK, dk), F32)],
        compiler_params=pltpu.CompilerParams(dimension_semantics=("arbitrary", "arbitrary"),
                                             vmem_limit_bytes=VMEM_LIMIT),
        name="chunk_scan",
    )(q, k, b, v, sg, gain_row)


def _out_kernel(mix_ref, mq_ref, mk_ref, mv_ref, h_ref, wmix_ref, wmem_ref, fgain_ref,
                wr_hi_ref, wr_lo_ref, br_ref, h1_ref, un_ref, route_ref):
    tm = mix_ref.shape[0]
    mk = mk_ref[0]
    mv = mv_ref[0]
    lane_w = lax.broadcasted_iota(jnp.int32, (1, MEM_WIDTH), 1)
    mq = mq_ref[...] * (MEM_HEAD_DIM ** -0.5)
    mem_o = jnp.zeros((tm, MEM_WIDTH), F32)
    for hd in range(MEM_HEADS):
        in_head = (lane_w >= hd * MEM_HEAD_DIM) & (lane_w < (hd + 1) * MEM_HEAD_DIM)
        s = _dot_nt(jnp.where(in_head, mq, jnp.zeros_like(mq)), mk)
        e = jnp.exp(s - jnp.max(s, axis=-1, keepdims=True))
        denom = jnp.sum(e, axis=-1, keepdims=True)
        mem_o = mem_o + _dot(e.astype(BF16), jnp.where(in_head, mv, jnp.zeros_like(mv))) * (1.0 / denom)
    h1 = h_ref[...] + _dot(mix_ref[...], wmix_ref[...]) + _dot(mem_o.astype(BF16), wmem_ref[...])
    h1_ref[...] = h1
    un = _rms(h1, fgain_ref[...])
    un_ref[...] = un.astype(BF16)

    u_hi, u_lo = _split2(un)
    wr_hi = wr_hi_ref[...]
    lg = _dot_nt(wr_hi, u_hi) + _dot_nt(wr_lo_ref[...], u_hi) + _dot_nt(wr_hi, u_lo) + br_ref[...]
    row = lax.broadcasted_iota(jnp.int32, (ROUTE_ROWS, tm), 0)
    neg = -jnp.inf
    gl = jnp.where(row < N_GROUPS, lg, neg)
    gmax = jnp.max(gl, axis=0, keepdims=True)
    g_idx = jnp.min(jnp.where(gl == gmax, row, ROUTE_ROWS), axis=0, keepdims=True)
    g_w = 1.0 / jnp.sum(jnp.exp(gl - gmax), axis=0, keepdims=True)
    first = N_GROUPS + g_idx * EXPERTS_PER_GROUP
    el = jnp.where((row >= first) & (row < first + EXPERTS_PER_GROUP), lg, neg)
    t1 = jnp.max(el, axis=0, keepdims=True)
    i1 = jnp.min(jnp.where(el == t1, row, ROUTE_ROWS), axis=0, keepdims=True)
    el2 = jnp.where(row == i1, neg, el)
    t2 = jnp.max(el2, axis=0, keepdims=True)
    i2 = jnp.min(jnp.where(el2 == t2, row, ROUTE_ROWS), axis=0, keepdims=True)
    e2 = jnp.exp(t2 - t1)
    w1 = g_w / (1.0 + e2)
    w2 = g_w * e2 / (1.0 + e2)
    row8 = lax.broadcasted_iota(jnp.int32, (SUBLANES, tm), 0)
    route_ref[...] = (jnp.where(row8 == i1 - first, w1, 0.0) + jnp.where(row8 == i2 - first, w2, 0.0)
                      + jnp.where(row8 == ROUTE_GROUP_ROW, g_idx.astype(F32), 0.0))


def _out_call(mix, mq, mem_k, mem_v, h, w_mix, w_mem, ffn_gain, w_group, b_group, w_router, b_router, T, tm):
    N = h.shape[0]
    tiles_per_batch = T // tm
    pad = ROUTE_ROWS - N_GROUPS - N_EXPERTS
    wr = jnp.pad(jnp.concatenate([w_group, w_router], axis=1).T, ((0, pad), (0, 0)))
    br = jnp.broadcast_to(jnp.pad(jnp.concatenate([b_group, b_router]), (0, pad))[:, None], (ROUTE_ROWS, tm))
    wr_hi, wr_lo = _split2(wr)
    row = lambda width: pl.BlockSpec((tm, width), lambda i: (i, 0))
    full = lambda a: pl.BlockSpec(a.shape, lambda i: (0,) * a.ndim)
    memspec = pl.BlockSpec((1, N_MEM, MEM_WIDTH), lambda i: (i // tiles_per_batch, 0, 0))
    w_mix = w_mix.astype(BF16)
    w_mem = w_mem.astype(BF16)
    fg = ffn_gain.reshape(1, D_MODEL)
    return pl.pallas_call(
        _out_kernel,
        out_shape=(jax.ShapeDtypeStruct((N, D_MODEL), F32), jax.ShapeDtypeStruct((N, D_MODEL), BF16),
                   jax.ShapeDtypeStruct((SUBLANES, N), F32)),
        grid=(N // tm,),
        in_specs=[row(mix.shape[1]), row(MEM_WIDTH), memspec, memspec, row(D_MODEL),
                  full(w_mix), full(w_mem), full(fg), full(wr_hi), full(wr_lo), full(br)],
        out_specs=(row(D_MODEL), row(D_MODEL), pl.BlockSpec((SUBLANES, tm), lambda i: (0, i))),
        compiler_params=pltpu.CompilerParams(dimension_semantics=("arbitrary",),
                                             vmem_limit_bytes=VMEM_LIMIT),
        name="out_proj_route",
    )(mix, mq, mem_k, mem_v, h, w_mix, w_mem, fg, wr_hi, wr_lo, br)


def _moe_kernel(final, n_super, un_ref, route_ref, h1_ref, utri_ref, wup_ref, wdn_ref, fgain_ref, o_ref,
                xs_ref, gs_ref, xg_ref, gg_ref, dest_ref, meta_ref):
    s = pl.program_id(0)
    k = pl.program_id(1)
    n_tiles, slots, _ = xs_ref.shape
    tm = un_ref.shape[0]
    row8 = lax.broadcasted_iota(jnp.int32, (SUBLANES, tm), 0)

    @pl.when((s == 0) & (k == 0))
    def _():
        xg_ref[...] = jnp.zeros_like(xg_ref)
        gg_ref[...] = jnp.zeros_like(gg_ref)

    @pl.when(s > 0)
    def _():
        dest = dest_ref[k][0:1, :]
        d_hi = jnp.floor(dest * (1.0 / 32.0))
        d_lo = dest - 32.0 * d_hi
        digits = jnp.where(row8 == 0, d_hi, jnp.where(row8 == 1, d_lo, 0.0)).astype(BF16)
        pick = (lax.broadcasted_iota(jnp.int32, (SUBLANES, LANES), 0)
                == lax.broadcasted_iota(jnp.int32, (SUBLANES, LANES), 1)).astype(BF16)
        dig_c = _dot_tn(digits, pick)
        dest_col = (32.0 * dig_c[:, 0:1] + dig_c[:, 1:2]).astype(jnp.int32)
        perm_t = jnp.where(lax.broadcasted_iota(jnp.int32, (tm, slots), 1) == dest_col, 1.0, 0.0).astype(BF16)
        y = h1_ref[...] + _dot(perm_t, xs_ref[k])
        o_ref[...] = _rms(y, fgain_ref[...]) if final else y

    @pl.when(s < n_super)
    def _():
        route = route_ref[...]
        g_idx = route[ROUTE_GROUP_ROW:ROUTE_GROUP_ROW + 1, :]
        member = jnp.where((row8 < N_GROUPS) & (row8.astype(F32) == g_idx), 1.0, 0.0)
        earlier = _dot(member.astype(BF16), utri_ref[...])
        rank = jnp.sum(member * earlier, axis=0, keepdims=True)
        count = jnp.sum(member, axis=1, keepdims=True)
        padded = jnp.floor((count + (MOE_ALIGN - 1.0)) * (1.0 / MOE_ALIGN)) * MOE_ALIGN
        starts = [jnp.zeros((1, 1), F32)]
        for g in range(1, N_GROUPS):
            starts.append(starts[-1] + padded[g - 1:g, :])
        seg = jnp.zeros((SUBLANES, 1), F32)
        row81 = lax.broadcasted_iota(jnp.int32, (SUBLANES, 1), 0)
        for g in range(1, N_GROUPS):
            seg = jnp.where(row81 == g, starts[g], seg)
        dest = rank + jnp.sum(member * seg, axis=0, keepdims=True)
        dest_ref[k] = jnp.broadcast_to(dest, (SUBLANES, tm))
        perm = jnp.where(lax.broadcasted_iota(jnp.int32, (slots, tm), 0) == dest.astype(jnp.int32),
                         1.0, 0.0).astype(BF16)
        xs_ref[k] = _dot(perm, un_ref[...]).astype(BF16)
        r_hi, r_lo = _split2(route)
        gs_ref[k] = _dot_nt(perm, jnp.concatenate([r_hi, r_lo], axis=0))
        for g in range(N_GROUPS):
            meta_ref[(k * N_GROUPS + g) * 2] = starts[g][0, 0].astype(jnp.int32)
            meta_ref[(k * N_GROUPS + g) * 2 + 1] = padded[g, 0].astype(jnp.int32)

    def expert_block(g, r0, rows):
        xb = xg_ref[pl.ds(r0, rows), :]
        gsb = gg_ref[pl.ds(r0, rows), :]
        acc = jnp.zeros((rows, D_MODEL), F32)
        for j in range(EXPERTS_PER_GROUP):
            e = g * EXPERTS_PER_GROUP + j
            hh = _dot(xb, wup_ref[e])
            gate = gsb[:, j:j + 1] + gsb[:, SUBLANES + j:SUBLANES + j + 1]
            act = _silu(hh[:, :EXPERT_FF]) * hh[:, EXPERT_FF:] * gate
            acc = acc + _dot(act.astype(BF16), wdn_ref[e])
        xg_ref[pl.ds(r0, rows), :] = acc.astype(BF16)

    def copy_segments(g, gather):
        off = jnp.int32(0)
        for t in range(n_tiles):
            st = meta_ref[(t * N_GROUPS + g) * 2]
            ln = meta_ref[(t * N_GROUPS + g) * 2 + 1]

            def copy_body(i, carry, t=t, st=st, off=off):
                src = pl.ds(pl.multiple_of(st + i * MOE_ALIGN, MOE_ALIGN), MOE_ALIGN)
                dst = pl.ds(pl.multiple_of(off + i * MOE_ALIGN, MOE_ALIGN), MOE_ALIGN)
                if gather:
                    xg_ref[dst, :] = xs_ref[t, src, :]
                    gg_ref[dst, :] = gs_ref[t, src, :]
                else:
                    xs_ref[t, src, :] = xg_ref[dst, :]
                return carry

            lax.fori_loop(0, lax.shift_right_logical(ln, MOE_ALIGN_SHIFT), copy_body, 0)
            off = off + ln
        return off

    @pl.when((s < n_super) & (k == n_tiles - 1))
    def _():
        for g in range(N_GROUPS):
            total = copy_segments(g, True)
            n_big = lax.shift_right_logical(total, MOE_BIG_SHIFT)
            rem = total - n_big * MOE_BIG
            n_big = n_big + (rem > MOE_BLOCK).astype(jnp.int32)

            def big_body(i, carry, g=g):
                expert_block(g, pl.multiple_of(i * MOE_BIG, MOE_BIG), MOE_BIG)
                return carry

            lax.fori_loop(0, n_big, big_body, 0)

            @pl.when((rem > 0) & (rem <= MOE_BLOCK))
            def _(g=g, n_big=n_big):
                expert_block(g, pl.multiple_of(n_big * MOE_BIG, MOE_BIG), MOE_BLOCK)

            copy_segments(g, False)


def _moe(un, route, h1, w_up, w_down, final_gain, final, tm):
    N = un.shape[0]
    n_tiles = min(MOE_SUPER, N // tm)
    n_super = N // (tm * n_tiles)
    slots = tm + N_GROUPS * MOE_ALIGN
    group_rows = n_tiles * slots + MOE_BIG
    utri = (jnp.arange(tm)[:, None] < jnp.arange(tm)[None, :]).astype(BF16)
    in_tile = lambda s, k: jnp.minimum(s, n_super - 1) * n_tiles + k
    out_tile = lambda s, k: jnp.maximum(s - 1, 0) * n_tiles + jnp.where(s > 0, k, 0)
    once = lambda a: pl.BlockSpec(a.shape, lambda s, k: (0,) * a.ndim, pipeline_mode=pl.Buffered(1))
    w_up = w_up.astype(BF16)
    w_down = w_down.astype(BF16)
    fg = final_gain.reshape(1, D_MODEL)
    return pl.pallas_call(
        functools.partial(_moe_kernel, final, n_super),
        out_shape=jax.ShapeDtypeStruct((N, D_MODEL), F32),
        grid=(n_super + 1, n_tiles),
        in_specs=[pl.BlockSpec((tm, D_MODEL), lambda s, k: (in_tile(s, k), 0)),
                  pl.BlockSpec((SUBLANES, tm), lambda s, k: (0, in_tile(s, k))),
                  pl.BlockSpec((tm, D_MODEL), lambda s, k: (out_tile(s, k), 0)),
                  once(utri), once(w_up), once(w_down), once(fg)],
        out_specs=pl.BlockSpec((tm, D_MODEL), lambda s, k: (out_tile(s, k), 0)),
        scratch_shapes=[pltpu.VMEM((n_tiles, slots, D_MODEL), BF16),
                        pltpu.VMEM((n_tiles, slots, 2 * SUBLANES), F32),
                        pltpu.VMEM((group_rows, D_MODEL), BF16),
                        pltpu.VMEM((group_rows, 2 * SUBLANES), F32),
                        pltpu.VMEM((n_tiles, SUBLANES, tm), F32),
                        pltpu.SMEM((n_tiles * N_GROUPS * 2,), jnp.int32)],
        compiler_params=pltpu.CompilerParams(dimension_semantics=("arbitrary", "arbitrary"),
                                             vmem_limit_bytes=VMEM_LIMIT),
        name="moe",
    )(un, route, h1, utri, w_up, w_down, fg)


def kernel(x, mem, mix_norm, ffn_norm, mem_norm, final_norm, gla_w_in, gla_w_gate_up, gla_b_gate, gla_out_norm,
           hg_w_in, hg_lower_bounds, hg_out_norm, w_mem_kv, w_out, w_group, b_group, w_router, b_router,
           w_up, w_down):
    B, T, _ = x.shape
    N = B * T
    depth = mix_norm.shape[0]
    tm_proj = min(256, T)
    tb_scan = min(256, T)
    tm_out = min(512, T)
    tm_moe = min(512, N)
    h = x.reshape(N, D_MODEL)
    for layer in range(depth):
        j = layer // 2
        if layer % 2 == 0:
            q, k, v, sg, b, mq = _gla_proj(h, mix_norm[layer], gla_w_in[j], gla_w_gate_up[j], gla_b_gate[j], tm_proj)
            heads, dk, dv_pad, dv, out_gain = GLA_HEADS, GLA_DK, GLA_DV_PAD, GLA_DV, gla_out_norm[j]
        else:
            q, k, v, sg, b, mq = _hg_proj(h, mix_norm[layer], hg_w_in[j], hg_lower_bounds, layer, tm_proj)
            heads, dk, dv_pad, dv, out_gain = HG_HEADS, HG_DK, HG_DV, HG_DV, hg_out_norm[j]
        r3 = lambda a: a.reshape(B, T, a.shape[-1])
        mix = _scan(r3(q), r3(k), r3(b), r3(v), r3(sg), out_gain, heads, dk, dv_pad, dv, tb_scan)
        mix = mix.reshape(N, heads * dv_pad)
        mem_k, mem_v = _mem_kv(mem, mem_norm[layer], w_mem_kv[layer])
        w_mix = _pad_heads(w_out[layer, :MIX_WIDTH].T, heads, dv, dv_pad).T
        h1, un, gates = _out_call(mix, mq, mem_k, mem_v, h, w_mix, w_out[layer, MIX_WIDTH:], ffn_norm[layer],
                                  w_group[layer], b_group[layer], w_router[layer], b_router[layer], T, tm_out)
        h = _moe(un, gates, h1, w_up[layer], w_down[layer], final_norm, layer == depth - 1, tm_moe)
    return h.reshape(B, T, D_MODEL)
```

```python
import functools

import jax
import jax.numpy as jnp
from jax import lax
from jax.experimental import pallas as pl
from jax.experimental.pallas import tpu as pltpu

D_MODEL = 1024
N_MEM = 256
MIX_WIDTH = 768
MEM_HEADS = 4
MEM_HEAD_DIM = 64
MEM_WIDTH = 256
CHUNK = 64
GLA_HEADS = 4
GLA_KEY_WIDTH = 512
GLA_DK = 128
GLA_DV = 192
GLA_DV_PAD = 256
GLA_GATE_RANK = 16
GLA_GATE_NORMALIZER = 16.0
HG_HEADS = 6
HG_DK = 128
HG_DV = 128
N_GROUPS = 4
EXPERTS_PER_GROUP = 4
N_EXPERTS = 16
EXPERT_FF = 256
NORM_EPS = 1e-6
LANES = 128
SUBLANES = 8
MOE_BLOCK = 128
MOE_ALIGN_SHIFT = 4
MOE_ALIGN = 1 << MOE_ALIGN_SHIFT
MOE_COPY_SHIFT = 6
MOE_COPY = 1 << MOE_COPY_SHIFT
MOE_BIG = 256 + 2 * MOE_ALIGN
MOE_SUPER = 4
ROUTE_GROUP_ROW = 4
ROUTE_ROWS = 32
SAFE_LOG_DECAY = -60.0
VMEM_LIMIT = 56 * 1024 * 1024

F32 = jnp.float32
BF16 = jnp.bfloat16


def _dot(a, b):
    return jnp.dot(a, b, preferred_element_type=F32)


def _dot_nt(a, b):
    return lax.dot_general(a, b, (((1,), (1,)), ((), ())), preferred_element_type=F32)


def _dot_tn(a, b):
    return lax.dot_general(a, b, (((0,), (0,)), ((), ())), preferred_element_type=F32)


def _split2(x):
    hi = x.astype(BF16)
    lo = (x - hi.astype(F32)).astype(BF16)
    return hi, lo


def _split3(x):
    hi = x.astype(BF16)
    r = x - hi.astype(F32)
    mid = r.astype(BF16)
    lo = (r - mid.astype(F32)).astype(BF16)
    return hi, mid, lo


def _rms(x, gain):
    ms = jnp.mean(x * x, axis=-1, keepdims=True)
    return x * lax.rsqrt(ms + NORM_EPS) * gain


def _log_sigmoid(x):
    return jnp.minimum(x, 0.0) - jnp.log1p(jnp.exp(-jnp.abs(x)))


def _sigmoid(x):
    return 1.0 / (1.0 + jnp.exp(-x))


def _silu(x):
    return x * _sigmoid(x)


def _chunk_cumsum(tri_ref, x):
    tri = tri_ref[...]
    hi, mid, lo = _split3(x)
    return _dot(tri, hi) + _dot(tri, mid) + _dot(tri, lo)


def _mem_kv_kernel(mem_ref, gain_ref, w_ref, k_ref, v_ref):
    m = _rms(mem_ref[0], gain_ref[...]).astype(BF16)
    kv = _dot(m, w_ref[...])
    k_ref[0] = kv[:, :MEM_WIDTH].astype(BF16)
    v_ref[0] = kv[:, MEM_WIDTH:].astype(BF16)


def _mem_kv(mem, gain, w_kv):
    B = mem.shape[0]
    return pl.pallas_call(
        _mem_kv_kernel,
        out_shape=(jax.ShapeDtypeStruct((B, N_MEM, MEM_WIDTH), BF16),
                   jax.ShapeDtypeStruct((B, N_MEM, MEM_WIDTH), BF16)),
        grid=(B,),
        in_specs=[pl.BlockSpec((1, N_MEM, D_MODEL), lambda b: (b, 0, 0)),
                  pl.BlockSpec((1, D_MODEL), lambda b: (0, 0)),
                  pl.BlockSpec((D_MODEL, 2 * MEM_WIDTH), lambda b: (0, 0))],
        out_specs=(pl.BlockSpec((1, N_MEM, MEM_WIDTH), lambda b: (b, 0, 0)),
                   pl.BlockSpec((1, N_MEM, MEM_WIDTH), lambda b: (b, 0, 0))),
        compiler_params=pltpu.CompilerParams(dimension_semantics=("arbitrary",)),
        name="mem_kv",
    )(mem, gain.reshape(1, D_MODEL), w_kv.astype(BF16))


_GQ, _GK = 0, GLA_KEY_WIDTH
_GV = 2 * GLA_KEY_WIDTH
_GG = _GV + GLA_HEADS * GLA_DV_PAD
_GR = _GG + GLA_HEADS * GLA_DV_PAD
_GM = _GR + LANES
_GW = _GM + MEM_WIDTH


def _gla_proj_kernel(h_ref, gain_ref, w_ref, wg_hi_ref, wg_lo_ref, bg_ref, tri_ref,
                     q_ref, k_ref, v_ref, sg_ref, b_ref, mq_ref):
    u = _rms(h_ref[...], gain_ref[...]).astype(BF16)
    q_ref[...] = (_dot(u, w_ref[:, _GQ:_GK]) * (GLA_DK ** -0.5)).astype(BF16)
    k_ref[...] = _dot(u, w_ref[:, _GK:_GV]).astype(BF16)
    v_ref[...] = _dot(u, w_ref[:, _GV:_GG]).astype(BF16)
    sg_ref[...] = _silu(_dot(u, w_ref[:, _GG:_GR])).astype(BF16)
    mq_ref[...] = _dot(u, w_ref[:, _GM:_GW]).astype(BF16)
    r_hi, r_lo = _split2(_dot(u, w_ref[:, _GR:_GM]))
    wg_hi = wg_hi_ref[...]
    logit = _dot(r_hi, wg_hi) + _dot(r_hi, wg_lo_ref[...]) + _dot(r_lo, wg_hi) + bg_ref[...]
    b_ref[...] = _chunk_cumsum(tri_ref, _log_sigmoid(logit) * (1.0 / GLA_GATE_NORMALIZER))


_HQ, _HF, _HI, _HGG, _HM = 0, MIX_WIDTH, 2 * MIX_WIDTH, 3 * MIX_WIDTH, 4 * MIX_WIDTH
_HW = _HM + MEM_WIDTH


def _hg_proj_kernel(layer, h_ref, gain_ref, w_ref, lbp_ref, tri_ref,
                    q_ref, k_ref, v_ref, sg_ref, b_ref, mq_ref):
    u = _rms(h_ref[...], gain_ref[...]).astype(BF16)
    p = lbp_ref[...]
    p = jnp.exp(p - jnp.max(p, axis=0, keepdims=True))
    p = p / jnp.sum(p, axis=0, keepdims=True)
    lb = jnp.sum(p[0:layer + 1], axis=0, keepdims=True) - p[0:1]
    q_ref[...] = _silu(_dot(u, w_ref[:, _HQ:_HF])).astype(BF16)
    z = _dot(u, w_ref[:, _HF:_HI])
    k_ref[...] = ((1.0 - lb) * _sigmoid(-z)).astype(BF16)
    v_ref[...] = _dot(u, w_ref[:, _HI:_HGG]).astype(BF16)
    sg_ref[...] = _silu(_dot(u, w_ref[:, _HGG:_HM])).astype(BF16)
    mq_ref[...] = _dot(u, w_ref[:, _HM:_HW]).astype(BF16)
    a = jnp.log(lb)
    c = jnp.log1p(-lb) + _log_sigmoid(z)
    log_f = jnp.maximum(a, c) + jnp.log1p(jnp.exp(-jnp.abs(a - c)))
    b_ref[...] = _chunk_cumsum(tri_ref, log_f)


def _proj_call(kernel, h, gain, w, extra, kw, vw, tm, name):
    N = h.shape[0]
    tri = (jnp.arange(tm)[:, None] >= jnp.arange(tm)[None, :]) & (
        jnp.arange(tm)[:, None] // CHUNK == jnp.arange(tm)[None, :] // CHUNK)
    tri = tri.astype(BF16)
    row = lambda width: pl.BlockSpec((tm, width), lambda i: (i, 0))
    full = lambda a: pl.BlockSpec(a.shape, lambda i: (0,) * a.ndim)
    ins = [h, gain.reshape(1, D_MODEL), w] + list(extra) + [tri]
    return pl.pallas_call(
        kernel,
        out_shape=(jax.ShapeDtypeStruct((N, kw), BF16), jax.ShapeDtypeStruct((N, kw), BF16),
                   jax.ShapeDtypeStruct((N, vw), BF16), jax.ShapeDtypeStruct((N, vw), BF16),
                   jax.ShapeDtypeStruct((N, kw), F32), jax.ShapeDtypeStruct((N, MEM_WIDTH), BF16)),
        grid=(N // tm,),
        in_specs=[row(D_MODEL)] + [full(a) for a in ins[1:]],
        out_specs=(row(kw), row(kw), row(vw), row(vw), row(kw), row(MEM_WIDTH)),
        compiler_params=pltpu.CompilerParams(dimension_semantics=("arbitrary",),
                                             vmem_limit_bytes=VMEM_LIMIT),
        name=name,
    )(*ins)


def _pad_heads(w, heads, dv, dv_pad):
    lead = w.shape[:-1]
    w = w.reshape(lead + (heads, dv))
    w = jnp.pad(w, [(0, 0)] * len(lead) + [(0, 0), (0, dv_pad - dv)])
    return w.reshape(lead + (heads * dv_pad,))


def _gla_proj(h, gain, w_in, w_gate_up, b_gate, tm):
    q, k, v, g, r, mq = jnp.split(
        w_in, [GLA_KEY_WIDTH, 2 * GLA_KEY_WIDTH, 2 * GLA_KEY_WIDTH + MIX_WIDTH,
               2 * GLA_KEY_WIDTH + 2 * MIX_WIDTH, 2 * GLA_KEY_WIDTH + 2 * MIX_WIDTH + GLA_GATE_RANK], axis=1)
    w = jnp.concatenate([q, k, _pad_heads(v, GLA_HEADS, GLA_DV, GLA_DV_PAD),
                         _pad_heads(g, GLA_HEADS, GLA_DV, GLA_DV_PAD),
                         jnp.pad(r, ((0, 0), (0, LANES - GLA_GATE_RANK))), mq], axis=1).astype(BF16)
    wg = jnp.pad(w_gate_up, ((0, LANES - GLA_GATE_RANK), (0, 0)))
    wg_hi, wg_lo = _split2(wg)
    return _proj_call(_gla_proj_kernel, h, gain, w, [wg_hi, wg_lo, b_gate.reshape(1, GLA_KEY_WIDTH)],
                      GLA_KEY_WIDTH, GLA_HEADS * GLA_DV_PAD, tm, "gla_proj")


def _hg_proj(h, gain, w_in, lower_bound_params, layer, tm):
    return _proj_call(functools.partial(_hg_proj_kernel, layer), h, gain, w_in.astype(BF16),
                      [lower_bound_params], MIX_WIDTH, MIX_WIDTH, tm, "hg_proj")


def _scan_kernel(heads, dk, dv_pad, dv, n_chunks,
                 q_ref, k_ref, b_ref, v_ref, sg_ref, gain_ref, o_ref, st_ref, kf_ref):
    @pl.when(pl.program_id(1) == 0)
    def _():
        st_ref[...] = jnp.zeros_like(st_ref)

    gain = gain_ref[...]
    row = lax.broadcasted_iota(jnp.int32, (CHUNK, CHUNK), 0)
    col = lax.broadcasted_iota(jnp.int32, (CHUNK, CHUNK), 1)
    causal = row >= col

    def finish(hd, rows, scores, q, k, b, eb):
        vc = slice(hd * dv_pad, (hd + 1) * dv_pad)
        v = v_ref[0, rows, vc]
        st = st_ref[hd]
        b_last = b[CHUNK - 1:CHUNK, :]
        qd = (q * eb).astype(BF16)
        out = _dot(scores.astype(BF16), v) + _dot_nt(qd, st.astype(BF16))
        kl = (k * jnp.exp(b_last - b)).astype(BF16)
        st_ref[hd] = st * jnp.exp(b_last) + _dot_tn(v, kl)
        ms = jnp.sum(out * out, axis=-1, keepdims=True) * (1.0 / dv)
        y = out * lax.rsqrt(ms + NORM_EPS) * gain * sg_ref[0, rows, vc].astype(F32)
        o_ref[0, rows, vc] = y.astype(BF16)

    def load(hd, rows):
        kc = slice(hd * dk, (hd + 1) * dk)
        return (q_ref[0, rows, kc].astype(F32), k_ref[0, rows, kc].astype(F32), b_ref[0, rows, kc])

    all_safe = jnp.min(b_ref[...]) >= SAFE_LOG_DECAY

    @pl.when(all_safe)
    def _():
        for c in range(n_chunks):
            rows = slice(c * CHUNK, (c + 1) * CHUNK)
            for hd in range(heads):
                q, k, b = load(hd, rows)
                eb = jnp.exp(b)
                kd = (k * jnp.exp(-b)).astype(BF16)
                scores = _dot_nt((q * eb).astype(BF16), kd)
                finish(hd, rows, jnp.where(causal, scores, 0.0), q, k, b, eb)

    @pl.when(jnp.logical_not(all_safe))
    def _():
        for hd in range(heads):
            kc = slice(hd * dk, (hd + 1) * dk)

            def chunk_body(c, carry):
                start = pl.multiple_of(c * CHUNK, CHUNK)
                rows = pl.ds(start, CHUNK)
                q, k, b = load(hd, rows)
                kf_ref[...] = k

                def col_body(j, sc):
                    base = pl.multiple_of((j >> 3) << 3, SUBLANES)
                    pick = lax.broadcasted_iota(jnp.int32, (SUBLANES, dk), 0) == (j & (SUBLANES - 1))
                    kj = jnp.sum(jnp.where(pick, kf_ref[pl.ds(base, SUBLANES), :], 0.0), axis=0, keepdims=True)
                    b8 = b_ref[0, pl.ds(pl.multiple_of(start + base, SUBLANES), SUBLANES), kc]
                    bj = jnp.sum(jnp.where(pick, b8, 0.0), axis=0, keepdims=True)
                    rid = lax.broadcasted_iota(jnp.int32, (CHUNK, dk), 0)
                    dec = jnp.exp(jnp.where(rid >= j, b - bj, -jnp.inf))
                    colv = jnp.sum(q * kj * dec, axis=-1, keepdims=True)
                    return jnp.where(col == j, colv, sc)

                scores = lax.fori_loop(0, CHUNK, col_body, jnp.zeros((CHUNK, CHUNK), F32))
                finish(hd, rows, scores, q, k, b, jnp.exp(b))
                return carry

            lax.fori_loop(0, n_chunks, chunk_body, 0)


def _scan(q, k, b, v, sg, gain, heads, dk, dv_pad, dv, tb):
    B, T, kw = q.shape
    vw = v.shape[-1]
    blk = lambda w: pl.BlockSpec((1, tb, w), lambda bi, ti: (bi, ti, 0))
    gain_row = jnp.pad(gain, (0, dv_pad - dv)).reshape(1, dv_pad)
    return pl.pallas_call(
        functools.partial(_scan_kernel, heads, dk, dv_pad, dv, tb // CHUNK),
        out_shape=jax.ShapeDtypeStruct((B, T, vw), BF16),
        grid=(B, T // tb),
        in_specs=[blk(kw), blk(kw), blk(kw), blk(vw), blk(vw),
                  pl.BlockSpec((1, dv_pad), lambda bi, ti: (0, 0))],
        out_specs=blk(vw),
        scratch_shapes=[pltpu.VMEM((heads, dv_pad, dk), F32), pltpu.VMEM((CHUNK, dk), F32)],
        compiler_params=pltpu.CompilerParams(dimension_semantics=("arbitrary", "arbitrary"),
                                             vmem_limit_bytes=VMEM_LIMIT),
        name="chunk_scan",
    )(q, k, b, v, sg, gain_row)


def _out_kernel(mix_ref, mq_ref, mk_ref, mv_ref, h_ref, wmix_ref, wmem_ref, fgain_ref,
                wr_hi_ref, wr_lo_ref, br_ref, h1_ref, un_ref, route_ref):
    tm = mix_ref.shape[0]
    mk = mk_ref[0]
    mv = mv_ref[0]
    lane_w = lax.broadcasted_iota(jnp.int32, (1, MEM_WIDTH), 1)
    mq = mq_ref[...] * (MEM_HEAD_DIM ** -0.5)
    mem_o = jnp.zeros((tm, MEM_WIDTH), F32)
    for hd in range(MEM_HEADS):
        in_head = (lane_w >= hd * MEM_HEAD_DIM) & (lane_w < (hd + 1) * MEM_HEAD_DIM)
        s = _dot_nt(jnp.where(in_head, mq, jnp.zeros_like(mq)), mk)
        e = jnp.exp(s - jnp.max(s, axis=-1, keepdims=True))
        denom = jnp.sum(e, axis=-1, keepdims=True)
        mem_o = mem_o + _dot(e.astype(BF16), jnp.where(in_head, mv, jnp.zeros_like(mv))) * (1.0 / denom)
    h1 = h_ref[...] + _dot(mix_ref[...], wmix_ref[...]) + _dot(mem_o.astype(BF16), wmem_ref[...])
    h1_ref[...] = h1
    un = _rms(h1, fgain_ref[...])
    un_ref[...] = un.astype(BF16)

    u_hi, u_lo = _split2(un)
    wr_hi = wr_hi_ref[...]
    lg = _dot_nt(wr_hi, u_hi) + _dot_nt(wr_lo_ref[...], u_hi) + _dot_nt(wr_hi, u_lo) + br_ref[...]
    row = lax.broadcasted_iota(jnp.int32, (ROUTE_ROWS, tm), 0)
    neg = -jnp.inf
    gl = jnp.where(row < N_GROUPS, lg, neg)
    gmax = jnp.max(gl, axis=0, keepdims=True)
    g_idx = jnp.min(jnp.where(gl == gmax, row, ROUTE_ROWS), axis=0, keepdims=True)
    g_w = 1.0 / jnp.sum(jnp.exp(gl - gmax), axis=0, keepdims=True)
    first = N_GROUPS + g_idx * EXPERTS_PER_GROUP
    el = jnp.where((row >= first) & (row < first + EXPERTS_PER_GROUP), lg, neg)
    t1 = jnp.max(el, axis=0, keepdims=True)
    i1 = jnp.min(jnp.where(el == t1, row, ROUTE_ROWS), axis=0, keepdims=True)
    el2 = jnp.where(row == i1, neg, el)
    t2 = jnp.max(el2, axis=0, keepdims=True)
    i2 = jnp.min(jnp.where(el2 == t2, row, ROUTE_ROWS), axis=0, keepdims=True)
    e2 = jnp.exp(t2 - t1)
    w1 = g_w / (1.0 + e2)
    w2 = g_w * e2 / (1.0 + e2)
    row8 = lax.broadcasted_iota(jnp.int32, (SUBLANES, tm), 0)
    route_ref[...] = (jnp.where(row8 == i1 - first, w1, 0.0) + jnp.where(row8 == i2 - first, w2, 0.0)
                      + jnp.where(row8 == ROUTE_GROUP_ROW, g_idx.astype(F32), 0.0))


def _out_call(mix, mq, mem_k, mem_v, h, w_mix, w_mem, ffn_gain, w_group, b_group, w_router, b_router, T, tm):
    N = h.shape[0]
    tiles_per_batch = T // tm
    pad = ROUTE_ROWS - N_GROUPS - N_EXPERTS
    wr = jnp.pad(jnp.concatenate([w_group, w_router], axis=1).T, ((0, pad), (0, 0)))
    br = jnp.broadcast_to(jnp.pad(jnp.concatenate([b_group, b_router]), (0, pad))[:, None], (ROUTE_ROWS, tm))
    wr_hi, wr_lo = _split2(wr)
    row = lambda width: pl.BlockSpec((tm, width), lambda i: (i, 0))
    full = lambda a: pl.BlockSpec(a.shape, lambda i: (0,) * a.ndim)
    memspec = pl.BlockSpec((1, N_MEM, MEM_WIDTH), lambda i: (i // tiles_per_batch, 0, 0))
    w_mix = w_mix.astype(BF16)
    w_mem = w_mem.astype(BF16)
    fg = ffn_gain.reshape(1, D_MODEL)
    return pl.pallas_call(
        _out_kernel,
        out_shape=(jax.ShapeDtypeStruct((N, D_MODEL), F32), jax.ShapeDtypeStruct((N, D_MODEL), BF16),
                   jax.ShapeDtypeStruct((SUBLANES, N), F32)),
        grid=(N // tm,),
        in_specs=[row(mix.shape[1]), row(MEM_WIDTH), memspec, memspec, row(D_MODEL),
                  full(w_mix), full(w_mem), full(fg), full(wr_hi), full(wr_lo), full(br)],
        out_specs=(row(D_MODEL), row(D_MODEL), pl.BlockSpec((SUBLANES, tm), lambda i: (0, i))),
        compiler_params=pltpu.CompilerParams(dimension_semantics=("arbitrary",),
                                             vmem_limit_bytes=VMEM_LIMIT),
        name="out_proj_route",
    )(mix, mq, mem_k, mem_v, h, w_mix, w_mem, fg, wr_hi, wr_lo, br)


def _moe_kernel(final, n_super, un_ref, route_ref, h1_ref, utri_ref, wup_ref, wdn_ref, fgain_ref, o_ref,
                xs_ref, gs_ref, xg_ref, gg_ref, dest_ref, meta_ref):
    s = pl.program_id(0)
    k = pl.program_id(1)
    n_tiles, slots, _ = xs_ref.shape
    tm = un_ref.shape[0]
    row8 = lax.broadcasted_iota(jnp.int32, (SUBLANES, tm), 0)

    @pl.when((s == 0) & (k == 0))
    def _():
        xg_ref[...] = jnp.zeros_like(xg_ref)
        gg_ref[...] = jnp.zeros_like(gg_ref)

    @pl.when(s > 0)
    def _():
        dest = dest_ref[k][0:1, :]
        d_hi = jnp.floor(dest * (1.0 / 32.0))
        d_lo = dest - 32.0 * d_hi
        digits = jnp.where(row8 == 0, d_hi, jnp.where(row8 == 1, d_lo, 0.0)).astype(BF16)
        pick = (lax.broadcasted_iota(jnp.int32, (SUBLANES, LANES), 0)
                == lax.broadcasted_iota(jnp.int32, (SUBLANES, LANES), 1)).astype(BF16)
        dig_c = _dot_tn(digits, pick)
        dest_col = (32.0 * dig_c[:, 0:1] + dig_c[:, 1:2]).astype(jnp.int32)
        perm_t = jnp.where(lax.broadcasted_iota(jnp.int32, (tm, slots), 1) == dest_col, 1.0, 0.0).astype(BF16)
        y = h1_ref[...] + _dot(perm_t, xs_ref[k])
        o_ref[...] = _rms(y, fgain_ref[...]) if final else y

    @pl.when(s < n_super)
    def _():
        route = route_ref[...]
        g_idx = route[ROUTE_GROUP_ROW:ROUTE_GROUP_ROW + 1, :]
        member = jnp.where((row8 < N_GROUPS) & (row8.astype(F32) == g_idx), 1.0, 0.0)
        earlier = _dot(member.astype(BF16), utri_ref[...])
        rank = jnp.sum(member * earlier, axis=0, keepdims=True)
        count = jnp.sum(member, axis=1, keepdims=True)
        padded = jnp.floor((count + (MOE_ALIGN - 1.0)) * (1.0 / MOE_ALIGN)) * MOE_ALIGN
        starts = [jnp.zeros((1, 1), F32)]
        for g in range(1, N_GROUPS):
            starts.append(starts[-1] + padded[g - 1:g, :])
        seg = jnp.zeros((SUBLANES, 1), F32)
        row81 = lax.broadcasted_iota(jnp.int32, (SUBLANES, 1), 0)
        for g in range(1, N_GROUPS):
            seg = jnp.where(row81 == g, starts[g], seg)
        dest = rank + jnp.sum(member * seg, axis=0, keepdims=True)
        dest_ref[k] = jnp.broadcast_to(dest, (SUBLANES, tm))
        perm = jnp.where(lax.broadcasted_iota(jnp.int32, (slots, tm), 0) == dest.astype(jnp.int32),
                         1.0, 0.0).astype(BF16)
        xs_ref[k] = _dot(perm, un_ref[...]).astype(BF16)
        r_hi, r_lo = _split2(route)
        gs_ref[k] = _dot_nt(perm, jnp.concatenate([r_hi, r_lo], axis=0))
        for g in range(N_GROUPS):
            meta_ref[(k * N_GROUPS + g) * 2] = starts[g][0, 0].astype(jnp.int32)
            meta_ref[(k * N_GROUPS + g) * 2 + 1] = padded[g, 0].astype(jnp.int32)

    def expert_block(g, r0, rows):
        xb = xg_ref[pl.ds(r0, rows), :]
        gsb = gg_ref[pl.ds(r0, rows), :]
        acc = jnp.zeros((rows, D_MODEL), F32)
        for j in range(EXPERTS_PER_GROUP):
            e = g * EXPERTS_PER_GROUP + j
            hh = _dot(xb, wup_ref[e])
            gate = gsb[:, j:j + 1] + gsb[:, SUBLANES + j:SUBLANES + j + 1]
            act = _silu(hh[:, :EXPERT_FF]) * hh[:, EXPERT_FF:] * gate
            acc = acc + _dot(act.astype(BF16), wdn_ref[e])
        xg_ref[pl.ds(r0, rows), :] = acc.astype(BF16)

    def copy_segments(g, gather):
        off = jnp.int32(0)
        for t in range(n_tiles):
            st = meta_ref[(t * N_GROUPS + g) * 2]
            ln = meta_ref[(t * N_GROUPS + g) * 2 + 1]

            def copy_rows(src0, dst0, rows, t=t):
                src = pl.ds(pl.multiple_of(src0, MOE_ALIGN), rows)
                dst = pl.ds(pl.multiple_of(dst0, MOE_ALIGN), rows)
                if gather:
                    xg_ref[dst, :] = xs_ref[t, src, :]
                    gg_ref[dst, :] = gs_ref[t, src, :]
                else:
                    xs_ref[t, src, :] = xg_ref[dst, :]

            def copy_body(i, carry, st=st, off=off, copy_rows=copy_rows):
                copy_rows(st + i * MOE_COPY, off + i * MOE_COPY, MOE_COPY)
                return carry

            n_copy = lax.shift_right_logical(ln, MOE_COPY_SHIFT)
            lax.fori_loop(0, n_copy, copy_body, 0)
            done = n_copy * MOE_COPY

            def tail_body(i, carry, st=st, off=off, done=done, copy_rows=copy_rows):
                copy_rows(st + done + i * MOE_ALIGN, off + done + i * MOE_ALIGN, MOE_ALIGN)
                return carry

            lax.fori_loop(0, lax.shift_right_logical(ln - done, MOE_ALIGN_SHIFT), tail_body, 0)
            off = off + ln
        return off

    @pl.when((s < n_super) & (k == n_tiles - 1))
    def _():
        for g in range(N_GROUPS):
            total = copy_segments(g, True)
            n_big = total // MOE_BIG
            rem = total - n_big * MOE_BIG
            n_big = n_big + (rem > MOE_BLOCK).astype(jnp.int32)

            def big_body(i, carry, g=g):
                expert_block(g, pl.multiple_of(i * MOE_BIG, MOE_BIG), MOE_BIG)
                return carry

            lax.fori_loop(0, n_big, big_body, 0)

            @pl.when((rem > 0) & (rem <= MOE_BLOCK))
            def _(g=g, n_big=n_big):
                expert_block(g, pl.multiple_of(n_big * MOE_BIG, MOE_BIG), MOE_BLOCK)

            copy_segments(g, False)


def _moe(un, route, h1, w_up, w_down, final_gain, final, tm):
    N = un.shape[0]
    n_tiles = min(MOE_SUPER, N // tm)
    n_super = N // (tm * n_tiles)
    slots = tm + N_GROUPS * MOE_ALIGN
    group_rows = n_tiles * slots + MOE_BIG
    utri = (jnp.arange(tm)[:, None] < jnp.arange(tm)[None, :]).astype(BF16)
    in_tile = lambda s, k: jnp.minimum(s, n_super - 1) * n_tiles + k
    out_tile = lambda s, k: jnp.maximum(s - 1, 0) * n_tiles + jnp.where(s > 0, k, 0)
    once = lambda a: pl.BlockSpec(a.shape, lambda s, k: (0,) * a.ndim, pipeline_mode=pl.Buffered(1))
    w_up = w_up.astype(BF16)
    w_down = w_down.astype(BF16)
    fg = final_gain.reshape(1, D_MODEL)
    return pl.pallas_call(
        functools.partial(_moe_kernel, final, n_super),
        out_shape=jax.ShapeDtypeStruct((N, D_MODEL), F32),
        grid=(n_super + 1, n_tiles),
        in_specs=[pl.BlockSpec((tm, D_MODEL), lambda s, k: (in_tile(s, k), 0)),
                  pl.BlockSpec((SUBLANES, tm), lambda s, k: (0, in_tile(s, k))),
                  pl.BlockSpec((tm, D_MODEL), lambda s, k: (out_tile(s, k), 0)),
                  once(utri), once(w_up), once(w_down), once(fg)],
        out_specs=pl.BlockSpec((tm, D_MODEL), lambda s, k: (out_tile(s, k), 0)),
        scratch_shapes=[pltpu.VMEM((n_tiles, slots, D_MODEL), BF16),
                        pltpu.VMEM((n_tiles, slots, 2 * SUBLANES), F32),
                        pltpu.VMEM((group_rows, D_MODEL), BF16),
                        pltpu.VMEM((group_rows, 2 * SUBLANES), F32),
                        pltpu.VMEM((n_tiles, SUBLANES, tm), F32),
                        pltpu.SMEM((n_tiles * N_GROUPS * 2,), jnp.int32)],
        compiler_params=pltpu.CompilerParams(dimension_semantics=("arbitrary", "arbitrary"),
                                             vmem_limit_bytes=VMEM_LIMIT),
        name="moe",
    )(un, route, h1, utri, w_up, w_down, fg)


def kernel(x, mem, mix_norm, ffn_norm, mem_norm, final_norm, gla_w_in, gla_w_gate_up, gla_b_gate, gla_out_norm,
           hg_w_in, hg_lower_bounds, hg_out_norm, w_mem_kv, w_out, w_group, b_group, w_router, b_router,
           w_up, w_down):
    B, T, _ = x.shape
    N = B * T
    depth = mix_norm.shape[0]
    tm_proj = min(256, T)
    tb_scan = min(512, T)
    tm_out = min(512, T)
    tm_moe = min(512, N)
    h = x.reshape(N, D_MODEL)
    for layer in range(depth):
        j = layer // 2
        if layer % 2 == 0:
            q, k, v, sg, b, mq = _gla_proj(h, mix_norm[layer], gla_w_in[j], gla_w_gate_up[j], gla_b_gate[j], tm_proj)
            heads, dk, dv_pad, dv, out_gain = GLA_HEADS, GLA_DK, GLA_DV_PAD, GLA_DV, gla_out_norm[j]
        else:
            q, k, v, sg, b, mq = _hg_proj(h, mix_norm[layer], hg_w_in[j], hg_lower_bounds, layer, tm_proj)
            heads, dk, dv_pad, dv, out_gain = HG_HEADS, HG_DK, HG_DV, HG_DV, hg_out_norm[j]
        r3 = lambda a: a.reshape(B, T, a.shape[-1])
        mix = _scan(r3(q), r3(k), r3(b), r3(v), r3(sg), out_gain, heads, dk, dv_pad, dv, tb_scan)
        mix = mix.reshape(N, heads * dv_pad)
        mem_k, mem_v = _mem_kv(mem, mem_norm[layer], w_mem_kv[layer])
        w_mix = _pad_heads(w_out[layer, :MIX_WIDTH].T, heads, dv, dv_pad).T
        h1, un, gates = _out_call(mix, mq, mem_k, mem_v, h, w_mix, w_out[layer, MIX_WIDTH:], ffn_norm[layer],
                                  w_group[layer], b_group[layer], w_router[layer], b_router[layer], T, tm_out)
        h = _moe(un, gates, h1, w_up[layer], w_down[layer], final_norm, layer == depth - 1, tm_moe)
    return h.reshape(B, T, D_MODEL)
```

```python
import functools

import jax
import jax.numpy as jnp
from jax import lax
from jax.experimental import pallas as pl
from jax.experimental.pallas import tpu as pltpu

D_MODEL = 1024
N_MEM = 256
MIX_WIDTH = 768
MEM_HEADS = 4
MEM_HEAD_DIM = 64
MEM_WIDTH = 256
CHUNK = 64
GLA_HEADS = 4
GLA_KEY_WIDTH = 512
GLA_DK = 128
GLA_DV = 192
GLA_DV_PAD = 256
GLA_GATE_RANK = 16
GLA_GATE_NORMALIZER = 16.0
HG_HEADS = 6
HG_DK = 128
HG_DV = 128
N_GROUPS = 4
EXPERTS_PER_GROUP = 4
N_EXPERTS = 16
EXPERT_FF = 256
NORM_EPS = 1e-6
LANES = 128
SUBLANES = 8
MOE_BLOCK = 128
CUMSUM_SPAN = 256
MOE_ALIGN_SHIFT = 4
MOE_ALIGN = 1 << MOE_ALIGN_SHIFT
MOE_COPY_SHIFT = 6
MOE_COPY = 1 << MOE_COPY_SHIFT
MOE_BIG = 256 + 2 * MOE_ALIGN
MOE_SUPER = 4
ROUTE_GROUP_ROW = 4
ROUTE_ROWS = 32
SAFE_LOG_DECAY = -60.0
VMEM_LIMIT = 56 * 1024 * 1024

F32 = jnp.float32
BF16 = jnp.bfloat16


def _dot(a, b):
    return jnp.dot(a, b, preferred_element_type=F32)


def _dot_nt(a, b):
    return lax.dot_general(a, b, (((1,), (1,)), ((), ())), preferred_element_type=F32)


def _dot_tn(a, b):
    return lax.dot_general(a, b, (((0,), (0,)), ((), ())), preferred_element_type=F32)


def _split2(x):
    hi = x.astype(BF16)
    lo = (x - hi.astype(F32)).astype(BF16)
    return hi, lo


def _rms(x, gain):
    ms = jnp.mean(x * x, axis=-1, keepdims=True)
    return x * lax.rsqrt(ms + NORM_EPS) * gain


def _log_sigmoid(x):
    return jnp.minimum(x, 0.0) - jnp.log1p(jnp.exp(-jnp.abs(x)))


def _sigmoid(x):
    return 1.0 / (1.0 + jnp.exp(-x))


def _silu(x):
    return x * _sigmoid(x)


def _chunk_cumsum(tri_ref, x):
    tri = tri_ref[...]
    span = tri.shape[0]
    parts = []
    for r in range(x.shape[0] // span):
        hi, lo = _split2(x[r * span:(r + 1) * span])
        parts.append(_dot(tri, hi) + _dot(tri, lo))
    return jnp.concatenate(parts, axis=0) if len(parts) > 1 else parts[0]


def _mem_kv_kernel(mem_ref, gain_ref, w_ref, k_ref, v_ref):
    m = _rms(mem_ref[0], gain_ref[...]).astype(BF16)
    kv = _dot(m, w_ref[...])
    k_ref[0] = kv[:, :MEM_WIDTH].astype(BF16)
    v_ref[0] = kv[:, MEM_WIDTH:].astype(BF16)


def _mem_kv(mem, gain, w_kv):
    B = mem.shape[0]
    return pl.pallas_call(
        _mem_kv_kernel,
        out_shape=(jax.ShapeDtypeStruct((B, N_MEM, MEM_WIDTH), BF16),
                   jax.ShapeDtypeStruct((B, N_MEM, MEM_WIDTH), BF16)),
        grid=(B,),
        in_specs=[pl.BlockSpec((1, N_MEM, D_MODEL), lambda b: (b, 0, 0)),
                  pl.BlockSpec((1, D_MODEL), lambda b: (0, 0)),
                  pl.BlockSpec((D_MODEL, 2 * MEM_WIDTH), lambda b: (0, 0))],
        out_specs=(pl.BlockSpec((1, N_MEM, MEM_WIDTH), lambda b: (b, 0, 0)),
                   pl.BlockSpec((1, N_MEM, MEM_WIDTH), lambda b: (b, 0, 0))),
        compiler_params=pltpu.CompilerParams(dimension_semantics=("arbitrary",)),
        name="mem_kv",
    )(mem, gain.reshape(1, D_MODEL), w_kv.astype(BF16))


_GQ, _GK = 0, GLA_KEY_WIDTH
_GV = 2 * GLA_KEY_WIDTH
_GG = _GV + GLA_HEADS * GLA_DV_PAD
_GR = _GG + GLA_HEADS * GLA_DV_PAD
_GM = _GR + LANES
_GW = _GM + MEM_WIDTH


def _gla_proj_kernel(h_ref, gain_ref, w_ref, wg_hi_ref, wg_lo_ref, bg_ref, tri_ref,
                     q_ref, k_ref, v_ref, sg_ref, b_ref, mq_ref):
    u = _rms(h_ref[...], gain_ref[...]).astype(BF16)
    q_ref[...] = (_dot(u, w_ref[:, _GQ:_GK]) * (GLA_DK ** -0.5)).astype(BF16)
    k_ref[...] = _dot(u, w_ref[:, _GK:_GV]).astype(BF16)
    v_ref[...] = _dot(u, w_ref[:, _GV:_GG]).astype(BF16)
    sg_ref[...] = _silu(_dot(u, w_ref[:, _GG:_GR])).astype(BF16)
    mq_ref[...] = _dot(u, w_ref[:, _GM:_GW]).astype(BF16)
    r_hi, r_lo = _split2(_dot(u, w_ref[:, _GR:_GM]))
    wg_hi = wg_hi_ref[...]
    logit = _dot(r_hi, wg_hi) + _dot(r_hi, wg_lo_ref[...]) + _dot(r_lo, wg_hi) + bg_ref[...]
    b_ref[...] = _chunk_cumsum(tri_ref, _log_sigmoid(logit) * (1.0 / GLA_GATE_NORMALIZER))


_HQ, _HF, _HI, _HGG, _HM = 0, MIX_WIDTH, 2 * MIX_WIDTH, 3 * MIX_WIDTH, 4 * MIX_WIDTH
_HW = _HM + MEM_WIDTH


def _hg_proj_kernel(layer, h_ref, gain_ref, w_ref, lbp_ref, tri_ref,
                    q_ref, k_ref, v_ref, sg_ref, b_ref, mq_ref):
    u = _rms(h_ref[...], gain_ref[...]).astype(BF16)
    p = lbp_ref[...]
    p = jnp.exp(p - jnp.max(p, axis=0, keepdims=True))
    p = p / jnp.sum(p, axis=0, keepdims=True)
    lb = jnp.sum(p[0:layer + 1], axis=0, keepdims=True) - p[0:1]
    q_ref[...] = _silu(_dot(u, w_ref[:, _HQ:_HF])).astype(BF16)
    z = _dot(u, w_ref[:, _HF:_HI])
    k_ref[...] = ((1.0 - lb) * _sigmoid(-z)).astype(BF16)
    v_ref[...] = _dot(u, w_ref[:, _HI:_HGG]).astype(BF16)
    sg_ref[...] = _silu(_dot(u, w_ref[:, _HGG:_HM])).astype(BF16)
    mq_ref[...] = _dot(u, w_ref[:, _HM:_HW]).astype(BF16)
    a = jnp.log(lb)
    c = jnp.log1p(-lb) + _log_sigmoid(z)
    log_f = jnp.maximum(a, c) + jnp.log1p(jnp.exp(-jnp.abs(a - c)))
    b_ref[...] = _chunk_cumsum(tri_ref, log_f)


def _proj_call(kernel, h, gain, w, extra, kw, vw, tm, name):
    N = h.shape[0]
    span = min(CUMSUM_SPAN, tm)
    tri = (jnp.arange(span)[:, None] >= jnp.arange(span)[None, :]) & (
        jnp.arange(span)[:, None] // CHUNK == jnp.arange(span)[None, :] // CHUNK)
    tri = tri.astype(BF16)
    row = lambda width: pl.BlockSpec((tm, width), lambda i: (i, 0))
    full = lambda a: pl.BlockSpec(a.shape, lambda i: (0,) * a.ndim)
    ins = [h, gain.reshape(1, D_MODEL), w] + list(extra) + [tri]
    return pl.pallas_call(
        kernel,
        out_shape=(jax.ShapeDtypeStruct((N, kw), BF16), jax.ShapeDtypeStruct((N, kw), BF16),
                   jax.ShapeDtypeStruct((N, vw), BF16), jax.ShapeDtypeStruct((N, vw), BF16),
                   jax.ShapeDtypeStruct((N, kw), F32), jax.ShapeDtypeStruct((N, MEM_WIDTH), BF16)),
        grid=(N // tm,),
        in_specs=[row(D_MODEL)] + [full(a) for a in ins[1:]],
        out_specs=(row(kw), row(kw), row(vw), row(vw), row(kw), row(MEM_WIDTH)),
        compiler_params=pltpu.CompilerParams(dimension_semantics=("arbitrary",),
                                             vmem_limit_bytes=VMEM_LIMIT),
        name=name,
    )(*ins)


def _pad_heads(w, heads, dv, dv_pad):
    lead = w.shape[:-1]
    w = w.reshape(lead + (heads, dv))
    w = jnp.pad(w, [(0, 0)] * len(lead) + [(0, 0), (0, dv_pad - dv)])
    return w.reshape(lead + (heads * dv_pad,))


def _gla_proj(h, gain, w_in, w_gate_up, b_gate, tm):
    q, k, v, g, r, mq = jnp.split(
        w_in, [GLA_KEY_WIDTH, 2 * GLA_KEY_WIDTH, 2 * GLA_KEY_WIDTH + MIX_WIDTH,
               2 * GLA_KEY_WIDTH + 2 * MIX_WIDTH, 2 * GLA_KEY_WIDTH + 2 * MIX_WIDTH + GLA_GATE_RANK], axis=1)
    w = jnp.concatenate([q, k, _pad_heads(v, GLA_HEADS, GLA_DV, GLA_DV_PAD),
                         _pad_heads(g, GLA_HEADS, GLA_DV, GLA_DV_PAD),
                         jnp.pad(r, ((0, 0), (0, LANES - GLA_GATE_RANK))), mq], axis=1).astype(BF16)
    wg = jnp.pad(w_gate_up, ((0, LANES - GLA_GATE_RANK), (0, 0)))
    wg_hi, wg_lo = _split2(wg)
    return _proj_call(_gla_proj_kernel, h, gain, w, [wg_hi, wg_lo, b_gate.reshape(1, GLA_KEY_WIDTH)],
                      GLA_KEY_WIDTH, GLA_HEADS * GLA_DV_PAD, tm, "gla_proj")


def _hg_proj(h, gain, w_in, lower_bound_params, layer, tm):
    return _proj_call(functools.partial(_hg_proj_kernel, layer), h, gain, w_in.astype(BF16),
                      [lower_bound_params], MIX_WIDTH, MIX_WIDTH, tm, "hg_proj")


def _scan_kernel(heads, dk, dv_pad, dv, n_chunks,
                 q_ref, k_ref, b_ref, v_ref, sg_ref, gain_ref, o_ref, st_ref, kf_ref):
    @pl.when(pl.program_id(1) == 0)
    def _():
        st_ref[...] = jnp.zeros_like(st_ref)

    gain = gain_ref[...]
    row = lax.broadcasted_iota(jnp.int32, (CHUNK, CHUNK), 0)
    col = lax.broadcasted_iota(jnp.int32, (CHUNK, CHUNK), 1)
    causal = row >= col

    def finish(hd, rows, scores, qd, kl, eb_last):
        vc = slice(hd * dv_pad, (hd + 1) * dv_pad)
        v = v_ref[0, rows, vc]
        st = st_ref[hd]
        out = _dot(scores.astype(BF16), v) + _dot_nt(qd, st.astype(BF16))
        st_ref[hd] = st * eb_last + _dot_tn(v, kl)
        ms = jnp.sum(out * out, axis=-1, keepdims=True) * (1.0 / dv)
        y = out * lax.rsqrt(ms + NORM_EPS) * gain * sg_ref[0, rows, vc].astype(F32)
        o_ref[0, rows, vc] = y.astype(BF16)

    def load(hd, rows):
        kc = slice(hd * dk, (hd + 1) * dk)
        return (q_ref[0, rows, kc].astype(F32), k_ref[0, rows, kc].astype(F32), b_ref[0, rows, kc])

    all_safe = jnp.min(b_ref[...]) >= SAFE_LOG_DECAY

    @pl.when(all_safe)
    def _():
        for c in range(n_chunks):
            rows = slice(c * CHUNK, (c + 1) * CHUNK)
            for hd in range(heads):
                q, k, b = load(hd, rows)
                eb = jnp.exp(b)
                eb_last = eb[CHUNK - 1:CHUNK, :]
                qd = (q * eb).astype(BF16)
                kd = k * jnp.exp(-b)
                scores = _dot_nt(qd, kd.astype(BF16))
                finish(hd, rows, jnp.where(causal, scores, 0.0), qd, (kd * eb_last).astype(BF16), eb_last)

    @pl.when(jnp.logical_not(all_safe))
    def _():
        for hd in range(heads):
            kc = slice(hd * dk, (hd + 1) * dk)

            def chunk_body(c, carry):
                start = pl.multiple_of(c * CHUNK, CHUNK)
                rows = pl.ds(start, CHUNK)
                q, k, b = load(hd, rows)
                kf_ref[...] = k

                def col_body(j, sc):
                    base = pl.multiple_of((j >> 3) << 3, SUBLANES)
                    pick = lax.broadcasted_iota(jnp.int32, (SUBLANES, dk), 0) == (j & (SUBLANES - 1))
                    kj = jnp.sum(jnp.where(pick, kf_ref[pl.ds(base, SUBLANES), :], 0.0), axis=0, keepdims=True)
                    b8 = b_ref[0, pl.ds(pl.multiple_of(start + base, SUBLANES), SUBLANES), kc]
                    bj = jnp.sum(jnp.where(pick, b8, 0.0), axis=0, keepdims=True)
                    rid = lax.broadcasted_iota(jnp.int32, (CHUNK, dk), 0)
                    dec = jnp.exp(jnp.where(rid >= j, b - bj, -jnp.inf))
                    colv = jnp.sum(q * kj * dec, axis=-1, keepdims=True)
                    return jnp.where(col == j, colv, sc)

                scores = lax.fori_loop(0, CHUNK, col_body, jnp.zeros((CHUNK, CHUNK), F32))
                b_last = b[CHUNK - 1:CHUNK, :]
                finish(hd, rows, scores, (q * jnp.exp(b)).astype(BF16),
                       (k * jnp.exp(b_last - b)).astype(BF16), jnp.exp(b_last))
                return carry

            lax.fori_loop(0, n_chunks, chunk_body, 0)


def _scan(q, k, b, v, sg, gain, heads, dk, dv_pad, dv, tb):
    B, T, kw = q.shape
    vw = v.shape[-1]
    blk = lambda w: pl.BlockSpec((1, tb, w), lambda bi, ti: (bi, ti, 0))
    gain_row = jnp.pad(gain, (0, dv_pad - dv)).reshape(1, dv_pad)
    return pl.pallas_call(
        functools.partial(_scan_kernel, heads, dk, dv_pad, dv, tb // CHUNK),
        out_shape=jax.ShapeDtypeStruct((B, T, vw), BF16),
        grid=(B, T // tb),
        in_specs=[blk(kw), blk(kw), blk(kw), blk(vw), blk(vw),
                  pl.BlockSpec((1, dv_pad), lambda bi, ti: (0, 0))],
        out_specs=blk(vw),
        scratch_shapes=[pltpu.VMEM((heads, dv_pad, dk), F32), pltpu.VMEM((CHUNK, dk), F32)],
        compiler_params=pltpu.CompilerParams(dimension_semantics=("arbitrary", "arbitrary"),
                                             vmem_limit_bytes=VMEM_LIMIT),
        name="chunk_scan",
    )(q, k, b, v, sg, gain_row)


def _out_kernel(mix_ref, mq_ref, mk_ref, mv_ref, h_ref, wmix_ref, wmem_ref, fgain_ref,
                wr_hi_ref, wr_lo_ref, br_ref, h1_ref, un_ref, route_ref):
    tm = mix_ref.shape[0]
    mk = mk_ref[0]
    mv = mv_ref[0]
    lane_w = lax.broadcasted_iota(jnp.int32, (1, MEM_WIDTH), 1)
    mq = mq_ref[...] * (MEM_HEAD_DIM ** -0.5)
    mem_o = jnp.zeros((tm, MEM_WIDTH), F32)
    for hd in range(MEM_HEADS):
        in_head = (lane_w >= hd * MEM_HEAD_DIM) & (lane_w < (hd + 1) * MEM_HEAD_DIM)
        s = _dot_nt(jnp.where(in_head, mq, jnp.zeros_like(mq)), mk)
        e = jnp.exp(s - jnp.max(s, axis=-1, keepdims=True))
        denom = jnp.sum(e, axis=-1, keepdims=True)
        mem_o = mem_o + _dot(e.astype(BF16), jnp.where(in_head, mv, jnp.zeros_like(mv))) * (1.0 / denom)
    h1 = h_ref[...] + _dot(mix_ref[...], wmix_ref[...]) + _dot(mem_o.astype(BF16), wmem_ref[...])
    h1_ref[...] = h1
    un = _rms(h1, fgain_ref[...])
    un_ref[...] = un.astype(BF16)

    u_hi, u_lo = _split2(un)
    wr_hi = wr_hi_ref[...]
    lg = _dot_nt(wr_hi, u_hi) + _dot_nt(wr_lo_ref[...], u_hi) + _dot_nt(wr_hi, u_lo) + br_ref[...]
    row = lax.broadcasted_iota(jnp.int32, (ROUTE_ROWS, tm), 0)
    neg = -jnp.inf
    gl = jnp.where(row < N_GROUPS, lg, neg)
    gmax = jnp.max(gl, axis=0, keepdims=True)
    g_idx = jnp.min(jnp.where(gl == gmax, row, ROUTE_ROWS), axis=0, keepdims=True)
    g_w = 1.0 / jnp.sum(jnp.exp(gl - gmax), axis=0, keepdims=True)
    first = N_GROUPS + g_idx * EXPERTS_PER_GROUP
    el = jnp.where((row >= first) & (row < first + EXPERTS_PER_GROUP), lg, neg)
    t1 = jnp.max(el, axis=0, keepdims=True)
    i1 = jnp.min(jnp.where(el == t1, row, ROUTE_ROWS), axis=0, keepdims=True)
    el2 = jnp.where(row == i1, neg, el)
    t2 = jnp.max(el2, axis=0, keepdims=True)
    i2 = jnp.min(jnp.where(el2 == t2, row, ROUTE_ROWS), axis=0, keepdims=True)
    e2 = jnp.exp(t2 - t1)
    w1 = g_w / (1.0 + e2)
    w2 = g_w * e2 / (1.0 + e2)
    row8 = lax.broadcasted_iota(jnp.int32, (SUBLANES, tm), 0)
    route_ref[...] = (jnp.where(row8 == i1 - first, w1, 0.0) + jnp.where(row8 == i2 - first, w2, 0.0)
                      + jnp.where(row8 == ROUTE_GROUP_ROW, g_idx.astype(F32), 0.0))


def _out_call(mix, mq, mem_k, mem_v, h, w_mix, w_mem, ffn_gain, w_group, b_group, w_router, b_router, T, tm):
    N = h.shape[0]
    tiles_per_batch = T // tm
    pad = ROUTE_ROWS - N_GROUPS - N_EXPERTS
    wr = jnp.pad(jnp.concatenate([w_group, w_router], axis=1).T, ((0, pad), (0, 0)))
    br = jnp.broadcast_to(jnp.pad(jnp.concatenate([b_group, b_router]), (0, pad))[:, None], (ROUTE_ROWS, tm))
    wr_hi, wr_lo = _split2(wr)
    row = lambda width: pl.BlockSpec((tm, width), lambda i: (i, 0))
    full = lambda a: pl.BlockSpec(a.shape, lambda i: (0,) * a.ndim)
    memspec = pl.BlockSpec((1, N_MEM, MEM_WIDTH), lambda i: (i // tiles_per_batch, 0, 0))
    w_mix = w_mix.astype(BF16)
    w_mem = w_mem.astype(BF16)
    fg = ffn_gain.reshape(1, D_MODEL)
    return pl.pallas_call(
        _out_kernel,
        out_shape=(jax.ShapeDtypeStruct((N, D_MODEL), F32), jax.ShapeDtypeStruct((N, D_MODEL), BF16),
                   jax.ShapeDtypeStruct((SUBLANES, N), F32)),
        grid=(N // tm,),
        in_specs=[row(mix.shape[1]), row(MEM_WIDTH), memspec, memspec, row(D_MODEL),
                  full(w_mix), full(w_mem), full(fg), full(wr_hi), full(wr_lo), full(br)],
        out_specs=(row(D_MODEL), row(D_MODEL), pl.BlockSpec((SUBLANES, tm), lambda i: (0, i))),
        compiler_params=pltpu.CompilerParams(dimension_semantics=("arbitrary",),
                                             vmem_limit_bytes=VMEM_LIMIT),
        name="out_proj_route",
    )(mix, mq, mem_k, mem_v, h, w_mix, w_mem, fg, wr_hi, wr_lo, br)


def _moe_kernel(final, n_super, un_ref, route_ref, h1_ref, utri_ref, wup_ref, wdn_ref, fgain_ref, o_ref,
                xs_ref, gs_ref, xg_ref, gg_ref, dest_ref, meta_ref):
    s = pl.program_id(0)
    k = pl.program_id(1)
    n_tiles, slots, _ = xs_ref.shape
    tm = un_ref.shape[0]
    row8 = lax.broadcasted_iota(jnp.int32, (SUBLANES, tm), 0)

    @pl.when((s == 0) & (k == 0))
    def _():
        xg_ref[...] = jnp.zeros_like(xg_ref)
        gg_ref[...] = jnp.zeros_like(gg_ref)

    @pl.when(s > 0)
    def _():
        dest = dest_ref[k][0:1, :]
        d_hi = jnp.floor(dest * (1.0 / 32.0))
        d_lo = dest - 32.0 * d_hi
        digits = jnp.where(row8 == 0, d_hi, jnp.where(row8 == 1, d_lo, 0.0)).astype(BF16)
        pick = (lax.broadcasted_iota(jnp.int32, (SUBLANES, LANES), 0)
                == lax.broadcasted_iota(jnp.int32, (SUBLANES, LANES), 1)).astype(BF16)
        dig_c = _dot_tn(digits, pick)
        dest_col = (32.0 * dig_c[:, 0:1] + dig_c[:, 1:2]).astype(jnp.int32)
        perm_t = jnp.where(lax.broadcasted_iota(jnp.int32, (tm, slots), 1) == dest_col, 1.0, 0.0).astype(BF16)
        y = h1_ref[...] + _dot(perm_t, xs_ref[k])
        o_ref[...] = _rms(y, fgain_ref[...]) if final else y

    @pl.when(s < n_super)
    def _():
        route = route_ref[...]
        g_idx = route[ROUTE_GROUP_ROW:ROUTE_GROUP_ROW + 1, :]
        member = jnp.where((row8 < N_GROUPS) & (row8.astype(F32) == g_idx), 1.0, 0.0)
        earlier = _dot(member.astype(BF16), utri_ref[...])
        rank = jnp.sum(member * earlier, axis=0, keepdims=True)
        count = jnp.sum(member, axis=1, keepdims=True)
        padded = jnp.floor((count + (MOE_ALIGN - 1.0)) * (1.0 / MOE_ALIGN)) * MOE_ALIGN
        starts = [jnp.zeros((1, 1), F32)]
        for g in range(1, N_GROUPS):
            starts.append(starts[-1] + padded[g - 1:g, :])
        seg = jnp.zeros((SUBLANES, 1), F32)
        row81 = lax.broadcasted_iota(jnp.int32, (SUBLANES, 1), 0)
        for g in range(1, N_GROUPS):
            seg = jnp.where(row81 == g, starts[g], seg)
        dest = rank + jnp.sum(member * seg, axis=0, keepdims=True)
        dest_ref[k] = jnp.broadcast_to(dest, (SUBLANES, tm))
        perm = jnp.where(lax.broadcasted_iota(jnp.int32, (slots, tm), 0) == dest.astype(jnp.int32),
                         1.0, 0.0).astype(BF16)
        xs_ref[k] = _dot(perm, un_ref[...]).astype(BF16)
        r_hi, r_lo = _split2(route)
        gs_ref[k] = _dot_nt(perm, jnp.concatenate([r_hi, r_lo], axis=0))
        for g in range(N_GROUPS):
            meta_ref[(k * N_GROUPS + g) * 2] = starts[g][0, 0].astype(jnp.int32)
            meta_ref[(k * N_GROUPS + g) * 2 + 1] = padded[g, 0].astype(jnp.int32)

    def expert_block(g, r0, rows):
        xb = xg_ref[pl.ds(r0, rows), :]
        gsb = gg_ref[pl.ds(r0, rows), :]
        acc = jnp.zeros((rows, D_MODEL), F32)
        for j in range(EXPERTS_PER_GROUP):
            e = g * EXPERTS_PER_GROUP + j
            hh = _dot(xb, wup_ref[e])
            gate = gsb[:, j:j + 1] + gsb[:, SUBLANES + j:SUBLANES + j + 1]
            act = _silu(hh[:, :EXPERT_FF]) * hh[:, EXPERT_FF:] * gate
            acc = acc + _dot(act.astype(BF16), wdn_ref[e])
        xg_ref[pl.ds(r0, rows), :] = acc.astype(BF16)

    def copy_segments(g, gather):
        off = jnp.int32(0)
        for t in range(n_tiles):
            st = meta_ref[(t * N_GROUPS + g) * 2]
            ln = meta_ref[(t * N_GROUPS + g) * 2 + 1]

            def copy_rows(src0, dst0, rows, t=t):
                src = pl.ds(pl.multiple_of(src0, MOE_ALIGN), rows)
                dst = pl.ds(pl.multiple_of(dst0, MOE_ALIGN), rows)
                if gather:
                    xg_ref[dst, :] = xs_ref[t, src, :]
                    gg_ref[dst, :] = gs_ref[t, src, :]
                else:
                    xs_ref[t, src, :] = xg_ref[dst, :]

            def copy_body(i, carry, st=st, off=off, copy_rows=copy_rows):
                copy_rows(st + i * MOE_COPY, off + i * MOE_COPY, MOE_COPY)
                return carry

            n_copy = lax.shift_right_logical(ln, MOE_COPY_SHIFT)
            lax.fori_loop(0, n_copy, copy_body, 0)
            done = n_copy * MOE_COPY

            def tail_body(i, carry, st=st, off=off, done=done, copy_rows=copy_rows):
                copy_rows(st + done + i * MOE_ALIGN, off + done + i * MOE_ALIGN, MOE_ALIGN)
                return carry

            lax.fori_loop(0, lax.shift_right_logical(ln - done, MOE_ALIGN_SHIFT), tail_body, 0)
            off = off + ln
        return off

    @pl.when((s < n_super) & (k == n_tiles - 1))
    def _():
        for g in range(N_GROUPS):
            total = copy_segments(g, True)
            n_big = total // MOE_BIG
            rem = total - n_big * MOE_BIG
            n_big = n_big + (rem > MOE_BLOCK).astype(jnp.int32)

            def big_body(i, carry, g=g):
                expert_block(g, pl.multiple_of(i * MOE_BIG, MOE_BIG), MOE_BIG)
                return carry

            lax.fori_loop(0, n_big, big_body, 0)

            @pl.when((rem > 0) & (rem <= MOE_BLOCK))
            def _(g=g, n_big=n_big):
                expert_block(g, pl.multiple_of(n_big * MOE_BIG, MOE_BIG), MOE_BLOCK)

            copy_segments(g, False)


def _moe(un, route, h1, w_up, w_down, layer, final_gain, final, tm):
    N = un.shape[0]
    n_tiles = min(MOE_SUPER, N // tm)
    n_super = N // (tm * n_tiles)
    slots = tm + N_GROUPS * MOE_ALIGN
    group_rows = n_tiles * slots + MOE_BIG
    utri = (jnp.arange(tm)[:, None] < jnp.arange(tm)[None, :]).astype(BF16)
    in_tile = lambda s, k: jnp.minimum(s, n_super - 1) * n_tiles + k
    out_tile = lambda s, k: jnp.maximum(s - 1, 0) * n_tiles + jnp.where(s > 0, k, 0)
    once = lambda a: pl.BlockSpec(a.shape, lambda s, k: (0,) * a.ndim, pipeline_mode=pl.Buffered(1))
    of_layer = lambda a: pl.BlockSpec((None,) + a.shape[1:], lambda s, k: (layer,) + (0,) * (a.ndim - 1),
                                      pipeline_mode=pl.Buffered(1))
    fg = final_gain.reshape(1, D_MODEL)
    return pl.pallas_call(
        functools.partial(_moe_kernel, final, n_super),
        out_shape=jax.ShapeDtypeStruct((N, D_MODEL), F32),
        grid=(n_super + 1, n_tiles),
        in_specs=[pl.BlockSpec((tm, D_MODEL), lambda s, k: (in_tile(s, k), 0)),
                  pl.BlockSpec((SUBLANES, tm), lambda s, k: (0, in_tile(s, k))),
                  pl.BlockSpec((tm, D_MODEL), lambda s, k: (out_tile(s, k), 0)),
                  once(utri), of_layer(w_up), of_layer(w_down), once(fg)],
        out_specs=pl.BlockSpec((tm, D_MODEL), lambda s, k: (out_tile(s, k), 0)),
        scratch_shapes=[pltpu.VMEM((n_tiles, slots, D_MODEL), BF16),
                        pltpu.VMEM((n_tiles, slots, 2 * SUBLANES), F32),
                        pltpu.VMEM((group_rows, D_MODEL), BF16),
                        pltpu.VMEM((group_rows, 2 * SUBLANES), F32),
                        pltpu.VMEM((n_tiles, SUBLANES, tm), F32),
                        pltpu.SMEM((n_tiles * N_GROUPS * 2,), jnp.int32)],
        compiler_params=pltpu.CompilerParams(dimension_semantics=("arbitrary", "arbitrary"),
                                             vmem_limit_bytes=VMEM_LIMIT),
        name="moe",
    )(un, route, h1, utri, w_up, w_down, fg)


def kernel(x, mem, mix_norm, ffn_norm, mem_norm, final_norm, gla_w_in, gla_w_gate_up, gla_b_gate, gla_out_norm,
           hg_w_in, hg_lower_bounds, hg_out_norm, w_mem_kv, w_out, w_group, b_group, w_router, b_router,
           w_up, w_down):
    B, T, _ = x.shape
    N = B * T
    depth = mix_norm.shape[0]
    tm_proj = min(512, T)
    tb_scan = min(512, T)
    tm_out = min(512, T)
    tm_moe = min(512, N)
    h = x.reshape(N, D_MODEL)
    w_up_bf16 = w_up.astype(BF16)
    w_down_bf16 = w_down.astype(BF16)
    for layer in range(depth):
        j = layer // 2
        if layer % 2 == 0:
            q, k, v, sg, b, mq = _gla_proj(h, mix_norm[layer], gla_w_in[j], gla_w_gate_up[j], gla_b_gate[j], tm_proj)
            heads, dk, dv_pad, dv, out_gain = GLA_HEADS, GLA_DK, GLA_DV_PAD, GLA_DV, gla_out_norm[j]
        else:
            q, k, v, sg, b, mq = _hg_proj(h, mix_norm[layer], hg_w_in[j], hg_lower_bounds, layer, tm_proj)
            heads, dk, dv_pad, dv, out_gain = HG_HEADS, HG_DK, HG_DV, HG_DV, hg_out_norm[j]
        r3 = lambda a: a.reshape(B, T, a.shape[-1])
        mix = _scan(r3(q), r3(k), r3(b), r3(v), r3(sg), out_gain, heads, dk, dv_pad, dv, tb_scan)
        mix = mix.reshape(N, heads * dv_pad)
        mem_k, mem_v = _mem_kv(mem, mem_norm[layer], w_mem_kv[layer])
        w_mix = _pad_heads(w_out[layer, :MIX_WIDTH].T, heads, dv, dv_pad).T
        h1, un, gates = _out_call(mix, mq, mem_k, mem_v, h, w_mix, w_out[layer, MIX_WIDTH:], ffn_norm[layer],
                                  w_group[layer], b_group[layer], w_router[layer], b_router[layer], T, tm_out)
        h = _moe(un, gates, h1, w_up_bf16, w_down_bf16, layer, final_norm, layer == depth - 1, tm_moe)
    return h.reshape(B, T, D_MODEL)
```

```python
import functools

import jax
import jax.numpy as jnp
from jax import lax
from jax.experimental import pallas as pl
from jax.experimental.pallas import tpu as pltpu

D_MODEL = 1024
N_MEM = 256
MIX_WIDTH = 768
MEM_HEADS = 4
MEM_HEAD_DIM = 64
MEM_WIDTH = 256
CHUNK = 64
GLA_HEADS = 4
GLA_KEY_WIDTH = 512
GLA_DK = 128
GLA_DV = 192
GLA_DV_PAD = 256
GLA_GATE_RANK = 16
GLA_GATE_NORMALIZER = 16.0
HG_HEADS = 6
HG_DK = 128
HG_DV = 128
N_GROUPS = 4
EXPERTS_PER_GROUP = 4
N_EXPERTS = 16
EXPERT_FF = 256
NORM_EPS = 1e-6
LANES = 128
SUBLANES = 8
MOE_BLOCK = 128
CUMSUM_SPAN = 256
MOE_ALIGN_SHIFT = 4
MOE_ALIGN = 1 << MOE_ALIGN_SHIFT
MOE_COPY_SHIFT = 6
MOE_COPY = 1 << MOE_COPY_SHIFT
MOE_BIG = 256 + 2 * MOE_ALIGN
MOE_SUPER = 4
ROUTE_GROUP_ROW = 4
ROUTE_ROWS = 32
SAFE_LOG_DECAY = -60.0
VMEM_LIMIT = 56 * 1024 * 1024

F32 = jnp.float32
BF16 = jnp.bfloat16


def _dot(a, b):
    return jnp.dot(a, b, preferred_element_type=F32)


def _dot_nt(a, b):
    return lax.dot_general(a, b, (((1,), (1,)), ((), ())), preferred_element_type=F32)


def _dot_tn(a, b):
    return lax.dot_general(a, b, (((0,), (0,)), ((), ())), preferred_element_type=F32)


def _split2(x):
    hi = x.astype(BF16)
    lo = (x - hi.astype(F32)).astype(BF16)
    return hi, lo


def _rms(x, gain):
    ms = jnp.mean(x * x, axis=-1, keepdims=True)
    return x * lax.rsqrt(ms + NORM_EPS) * gain


def _log_sigmoid(x):
    return jnp.minimum(x, 0.0) - jnp.log1p(jnp.exp(-jnp.abs(x)))


def _sigmoid(x):
    return 1.0 / (1.0 + jnp.exp(-x))


def _silu(x):
    return x * _sigmoid(x)


def _chunk_cumsum(tri_ref, x):
    tri = tri_ref[...]
    span = tri.shape[0]
    parts = []
    for r in range(x.shape[0] // span):
        hi, lo = _split2(x[r * span:(r + 1) * span])
        parts.append(_dot(tri, hi) + _dot(tri, lo))
    return jnp.concatenate(parts, axis=0) if len(parts) > 1 else parts[0]


def _mem_kv_kernel(mem_ref, gain_ref, w_ref, k_ref, v_ref):
    m = _rms(mem_ref[0], gain_ref[...]).astype(BF16)
    kv = _dot(m, w_ref[...])
    k_ref[0] = kv[:, :MEM_WIDTH].astype(BF16)
    v_ref[0] = kv[:, MEM_WIDTH:].astype(BF16)


def _mem_kv(mem, gain, w_kv):
    B = mem.shape[0]
    return pl.pallas_call(
        _mem_kv_kernel,
        out_shape=(jax.ShapeDtypeStruct((B, N_MEM, MEM_WIDTH), BF16),
                   jax.ShapeDtypeStruct((B, N_MEM, MEM_WIDTH), BF16)),
        grid=(B,),
        in_specs=[pl.BlockSpec((1, N_MEM, D_MODEL), lambda b: (b, 0, 0)),
                  pl.BlockSpec((1, D_MODEL), lambda b: (0, 0)),
                  pl.BlockSpec((D_MODEL, 2 * MEM_WIDTH), lambda b: (0, 0))],
        out_specs=(pl.BlockSpec((1, N_MEM, MEM_WIDTH), lambda b: (b, 0, 0)),
                   pl.BlockSpec((1, N_MEM, MEM_WIDTH), lambda b: (b, 0, 0))),
        compiler_params=pltpu.CompilerParams(dimension_semantics=("arbitrary",)),
        name="mem_kv",
    )(mem, gain.reshape(1, D_MODEL), w_kv.astype(BF16))


_GQ, _GK = 0, GLA_KEY_WIDTH
_GV = 2 * GLA_KEY_WIDTH
_GG = _GV + MIX_WIDTH
_GR = _GG + MIX_WIDTH
_GM = _GR + LANES
_GW = _GM + MEM_WIDTH


def _store_padded_heads(ref, x):
    zeros = jnp.zeros((x.shape[0], GLA_DV_PAD - GLA_DV), ref.dtype)
    for hd in range(GLA_HEADS):
        ref[:, hd * GLA_DV_PAD:hd * GLA_DV_PAD + GLA_DV] = x[:, hd * GLA_DV:(hd + 1) * GLA_DV].astype(ref.dtype)
        ref[:, hd * GLA_DV_PAD + GLA_DV:(hd + 1) * GLA_DV_PAD] = zeros


def _gla_proj_kernel(h_ref, gain_ref, w_ref, wg_hi_ref, wg_lo_ref, bg_ref, tri_ref,
                     q_ref, k_ref, v_ref, sg_ref, b_ref, mq_ref):
    u = _rms(h_ref[...], gain_ref[...]).astype(BF16)
    q_ref[...] = (_dot(u, w_ref[:, _GQ:_GK]) * (GLA_DK ** -0.5)).astype(BF16)
    k_ref[...] = _dot(u, w_ref[:, _GK:_GV]).astype(BF16)
    _store_padded_heads(v_ref, _dot(u, w_ref[:, _GV:_GG]))
    _store_padded_heads(sg_ref, _silu(_dot(u, w_ref[:, _GG:_GR])))
    mq_ref[...] = _dot(u, w_ref[:, _GM:_GW]).astype(BF16)
    r_hi, r_lo = _split2(_dot(u, w_ref[:, _GR:_GM]))
    wg_hi = wg_hi_ref[...]
    logit = _dot(r_hi, wg_hi) + _dot(r_hi, wg_lo_ref[...]) + _dot(r_lo, wg_hi) + bg_ref[...]
    b_ref[...] = _chunk_cumsum(tri_ref, _log_sigmoid(logit) * (1.0 / GLA_GATE_NORMALIZER))


_HQ, _HF, _HI, _HGG, _HM = 0, MIX_WIDTH, 2 * MIX_WIDTH, 3 * MIX_WIDTH, 4 * MIX_WIDTH
_HW = _HM + MEM_WIDTH


def _hg_proj_kernel(layer, h_ref, gain_ref, w_ref, lbp_ref, tri_ref,
                    q_ref, k_ref, v_ref, sg_ref, b_ref, mq_ref):
    u = _rms(h_ref[...], gain_ref[...]).astype(BF16)
    p = lbp_ref[...]
    p = jnp.exp(p - jnp.max(p, axis=0, keepdims=True))
    p = p / jnp.sum(p, axis=0, keepdims=True)
    lb = jnp.sum(p[0:layer + 1], axis=0, keepdims=True) - p[0:1]
    q_ref[...] = _silu(_dot(u, w_ref[:, _HQ:_HF])).astype(BF16)
    z = _dot(u, w_ref[:, _HF:_HI])
    k_ref[...] = ((1.0 - lb) * _sigmoid(-z)).astype(BF16)
    v_ref[...] = _dot(u, w_ref[:, _HI:_HGG]).astype(BF16)
    sg_ref[...] = _silu(_dot(u, w_ref[:, _HGG:_HM])).astype(BF16)
    mq_ref[...] = _dot(u, w_ref[:, _HM:_HW]).astype(BF16)
    a = jnp.log(lb)
    c = jnp.log1p(-lb) + _log_sigmoid(z)
    log_f = jnp.maximum(a, c) + jnp.log1p(jnp.exp(-jnp.abs(a - c)))
    b_ref[...] = _chunk_cumsum(tri_ref, log_f)


def _proj_call(kernel, h, gain, w, extra, kw, vw, tm, name):
    N = h.shape[0]
    span = min(CUMSUM_SPAN, tm)
    tri = (jnp.arange(span)[:, None] >= jnp.arange(span)[None, :]) & (
        jnp.arange(span)[:, None] // CHUNK == jnp.arange(span)[None, :] // CHUNK)
    tri = tri.astype(BF16)
    row = lambda width: pl.BlockSpec((tm, width), lambda i: (i, 0))
    full = lambda a: pl.BlockSpec(a.shape, lambda i: (0,) * a.ndim)
    ins = [h, gain.reshape(1, D_MODEL), w] + list(extra) + [tri]
    return pl.pallas_call(
        kernel,
        out_shape=(jax.ShapeDtypeStruct((N, kw), BF16), jax.ShapeDtypeStruct((N, kw), BF16),
                   jax.ShapeDtypeStruct((N, vw), BF16), jax.ShapeDtypeStruct((N, vw), BF16),
                   jax.ShapeDtypeStruct((N, kw), F32), jax.ShapeDtypeStruct((N, MEM_WIDTH), BF16)),
        grid=(N // tm,),
        in_specs=[row(D_MODEL)] + [full(a) for a in ins[1:]],
        out_specs=(row(kw), row(kw), row(vw), row(vw), row(kw), row(MEM_WIDTH)),
        compiler_params=pltpu.CompilerParams(dimension_semantics=("arbitrary",),
                                             vmem_limit_bytes=VMEM_LIMIT),
        name=name,
    )(*ins)


def _pad_heads(w, heads, dv, dv_pad):
    lead = w.shape[:-1]
    w = w.reshape(lead + (heads, dv))
    w = jnp.pad(w, [(0, 0)] * len(lead) + [(0, 0), (0, dv_pad - dv)])
    return w.reshape(lead + (heads * dv_pad,))


def _gla_proj(h, gain, w_in, w_gate_up, b_gate, tm):
    q, k, v, g, r, mq = jnp.split(
        w_in, [GLA_KEY_WIDTH, 2 * GLA_KEY_WIDTH, 2 * GLA_KEY_WIDTH + MIX_WIDTH,
               2 * GLA_KEY_WIDTH + 2 * MIX_WIDTH, 2 * GLA_KEY_WIDTH + 2 * MIX_WIDTH + GLA_GATE_RANK], axis=1)
    w = jnp.concatenate([q, k, v, g, jnp.pad(r, ((0, 0), (0, LANES - GLA_GATE_RANK))), mq],
                        axis=1).astype(BF16)
    wg = jnp.pad(w_gate_up, ((0, LANES - GLA_GATE_RANK), (0, 0)))
    wg_hi, wg_lo = _split2(wg)
    return _proj_call(_gla_proj_kernel, h, gain, w, [wg_hi, wg_lo, b_gate.reshape(1, GLA_KEY_WIDTH)],
                      GLA_KEY_WIDTH, GLA_HEADS * GLA_DV_PAD, tm, "gla_proj")


def _hg_proj(h, gain, w_in, lower_bound_params, layer, tm):
    return _proj_call(functools.partial(_hg_proj_kernel, layer), h, gain, w_in.astype(BF16),
                      [lower_bound_params], MIX_WIDTH, MIX_WIDTH, tm, "hg_proj")


def _scan_kernel(heads, dk, dv_pad, dv, n_chunks,
                 q_ref, k_ref, b_ref, v_ref, sg_ref, gain_ref, o_ref, st_ref, kf_ref):
    @pl.when(pl.program_id(1) == 0)
    def _():
        st_ref[...] = jnp.zeros_like(st_ref)

    gain = gain_ref[...]
    row = lax.broadcasted_iota(jnp.int32, (CHUNK, CHUNK), 0)
    col = lax.broadcasted_iota(jnp.int32, (CHUNK, CHUNK), 1)
    causal = row >= col

    n_batch = q_ref.shape[0]

    def finish(bi, hd, rows, scores, qd, kl, eb_last):
        vc = slice(hd * dv_pad, (hd + 1) * dv_pad)
        v = v_ref[bi, rows, vc]
        st = st_ref[bi * heads + hd]
        out = _dot(scores.astype(BF16), v) + _dot_nt(qd, st.astype(BF16))
        st_ref[bi * heads + hd] = st * eb_last + _dot_tn(v, kl)
        ms = jnp.sum(out * out, axis=-1, keepdims=True) * (1.0 / dv)
        y = out * lax.rsqrt(ms + NORM_EPS) * gain * sg_ref[bi, rows, vc].astype(F32)
        o_ref[bi, rows, vc] = y.astype(BF16)

    def load(bi, hd, rows):
        kc = slice(hd * dk, (hd + 1) * dk)
        return (q_ref[bi, rows, kc].astype(F32), k_ref[bi, rows, kc].astype(F32), b_ref[bi, rows, kc])

    all_safe = jnp.min(b_ref[...]) >= SAFE_LOG_DECAY

    @pl.when(all_safe)
    def _():
        for c in range(n_chunks):
            rows = slice(c * CHUNK, (c + 1) * CHUNK)
            for bi in range(n_batch):
                for hd in range(heads):
                    q, k, b = load(bi, hd, rows)
                    eb = jnp.exp(b)
                    eb_last = eb[CHUNK - 1:CHUNK, :]
                    qd = (q * eb).astype(BF16)
                    kd = k * jnp.exp(-b)
                    scores = _dot_nt(qd, kd.astype(BF16))
                    finish(bi, hd, rows, jnp.where(causal, scores, 0.0), qd, (kd * eb_last).astype(BF16),
                           eb_last)

    @pl.when(jnp.logical_not(all_safe))
    def _():
        for bi, hd in [(bi, hd) for bi in range(n_batch) for hd in range(heads)]:
            kc = slice(hd * dk, (hd + 1) * dk)

            def chunk_body(c, carry, bi=bi, hd=hd, kc=kc):
                start = pl.multiple_of(c * CHUNK, CHUNK)
                rows = pl.ds(start, CHUNK)
                q, k, b = load(bi, hd, rows)
                kf_ref[...] = k

                def col_body(j, sc):
                    base = pl.multiple_of((j >> 3) << 3, SUBLANES)
                    pick = lax.broadcasted_iota(jnp.int32, (SUBLANES, dk), 0) == (j & (SUBLANES - 1))
                    kj = jnp.sum(jnp.where(pick, kf_ref[pl.ds(base, SUBLANES), :], 0.0), axis=0, keepdims=True)
                    b8 = b_ref[bi, pl.ds(pl.multiple_of(start + base, SUBLANES), SUBLANES), kc]
                    bj = jnp.sum(jnp.where(pick, b8, 0.0), axis=0, keepdims=True)
                    rid = lax.broadcasted_iota(jnp.int32, (CHUNK, dk), 0)
                    dec = jnp.exp(jnp.where(rid >= j, b - bj, -jnp.inf))
                    colv = jnp.sum(q * kj * dec, axis=-1, keepdims=True)
                    return jnp.where(col == j, colv, sc)

                scores = lax.fori_loop(0, CHUNK, col_body, jnp.zeros((CHUNK, CHUNK), F32))
                b_last = b[CHUNK - 1:CHUNK, :]
                finish(bi, hd, rows, scores, (q * jnp.exp(b)).astype(BF16),
                       (k * jnp.exp(b_last - b)).astype(BF16), jnp.exp(b_last))
                return carry

            lax.fori_loop(0, n_chunks, chunk_body, 0)


def _scan(q, k, b, v, sg, gain, heads, dk, dv_pad, dv, tb, nb):
    B, T, kw = q.shape
    vw = v.shape[-1]
    blk = lambda w: pl.BlockSpec((nb, tb, w), lambda bi, ti: (bi, ti, 0))
    gain_row = jnp.pad(gain, (0, dv_pad - dv)).reshape(1, dv_pad)
    return pl.pallas_call(
        functools.partial(_scan_kernel, heads, dk, dv_pad, dv, tb // CHUNK),
        out_shape=jax.ShapeDtypeStruct((B, T, vw), BF16),
        grid=(B // nb, T // tb),
        in_specs=[blk(kw), blk(kw), blk(kw), blk(vw), blk(vw),
                  pl.BlockSpec((1, dv_pad), lambda bi, ti: (0, 0))],
        out_specs=blk(vw),
        scratch_shapes=[pltpu.VMEM((nb * heads, dv_pad, dk), F32), pltpu.VMEM((CHUNK, dk), F32)],
        compiler_params=pltpu.CompilerParams(dimension_semantics=("arbitrary", "arbitrary"),
                                             vmem_limit_bytes=VMEM_LIMIT),
        name="chunk_scan",
    )(q, k, b, v, sg, gain_row)


def _out_kernel(mix_ref, mq_ref, mk_ref, mv_ref, h_ref, wmix_ref, wmem_ref, fgain_ref,
                wr_ref, br_ref, h1_ref, un_ref, route_ref):
    tm = mix_ref.shape[0]
    mk = mk_ref[0]
    mv = mv_ref[0]
    lane_w = lax.broadcasted_iota(jnp.int32, (1, MEM_WIDTH), 1)
    mq = mq_ref[...] * (MEM_HEAD_DIM ** -0.5)
    mem_o = jnp.zeros((tm, MEM_WIDTH), F32)
    for hd in range(MEM_HEADS):
        in_head = (lane_w >= hd * MEM_HEAD_DIM) & (lane_w < (hd + 1) * MEM_HEAD_DIM)
        s = _dot_nt(jnp.where(in_head, mq, jnp.zeros_like(mq)), mk)
        e = jnp.exp(s - jnp.max(s, axis=-1, keepdims=True))
        denom = jnp.sum(e, axis=-1, keepdims=True)
        mem_o = mem_o + _dot(e.astype(BF16), jnp.where(in_head, mv, jnp.zeros_like(mv))) * (1.0 / denom)
    h1 = h_ref[...] + _dot(mix_ref[...], wmix_ref[...]) + _dot(mem_o.astype(BF16), wmem_ref[...])
    h1_ref[...] = h1
    un = _rms(h1, fgain_ref[...])
    un_ref[...] = un.astype(BF16)

    u_hi, u_lo = _split2(un)
    both = _dot_nt(wr_ref[...], u_hi)
    lg = (both[:ROUTE_ROWS] + both[ROUTE_ROWS:] + _dot_nt(wr_ref[:ROUTE_ROWS, :], u_lo) + br_ref[...])
    row = lax.broadcasted_iota(jnp.int32, (ROUTE_ROWS, tm), 0)
    neg = -jnp.inf
    gl = jnp.where(row < N_GROUPS, lg, neg)
    gmax = jnp.max(gl, axis=0, keepdims=True)
    g_idx = jnp.min(jnp.where(gl == gmax, row, ROUTE_ROWS), axis=0, keepdims=True)
    g_w = 1.0 / jnp.sum(jnp.exp(gl - gmax), axis=0, keepdims=True)
    first = N_GROUPS + g_idx * EXPERTS_PER_GROUP
    el = jnp.where((row >= first) & (row < first + EXPERTS_PER_GROUP), lg, neg)
    t1 = jnp.max(el, axis=0, keepdims=True)
    i1 = jnp.min(jnp.where(el == t1, row, ROUTE_ROWS), axis=0, keepdims=True)
    el2 = jnp.where(row == i1, neg, el)
    t2 = jnp.max(el2, axis=0, keepdims=True)
    i2 = jnp.min(jnp.where(el2 == t2, row, ROUTE_ROWS), axis=0, keepdims=True)
    e2 = jnp.exp(t2 - t1)
    w1 = g_w / (1.0 + e2)
    w2 = g_w * e2 / (1.0 + e2)
    row8 = lax.broadcasted_iota(jnp.int32, (SUBLANES, tm), 0)
    route_ref[...] = (jnp.where(row8 == i1 - first, w1, 0.0) + jnp.where(row8 == i2 - first, w2, 0.0)
                      + jnp.where(row8 == ROUTE_GROUP_ROW, g_idx.astype(F32), 0.0))


def _out_call(mix, mq, mem_k, mem_v, h, w_mix, w_mem, ffn_gain, w_group, b_group, w_router, b_router, T, tm):
    N = h.shape[0]
    tiles_per_batch = T // tm
    pad = ROUTE_ROWS - N_GROUPS - N_EXPERTS
    wr = jnp.pad(jnp.concatenate([w_group, w_router], axis=1).T, ((0, pad), (0, 0)))
    br = jnp.broadcast_to(jnp.pad(jnp.concatenate([b_group, b_router]), (0, pad))[:, None], (ROUTE_ROWS, tm))
    wr_both = jnp.concatenate(_split2(wr), axis=0)
    row = lambda width: pl.BlockSpec((tm, width), lambda i: (i, 0))
    full = lambda a: pl.BlockSpec(a.shape, lambda i: (0,) * a.ndim)
    memspec = pl.BlockSpec((1, N_MEM, MEM_WIDTH), lambda i: (i // tiles_per_batch, 0, 0))
    w_mix = w_mix.astype(BF16)
    w_mem = w_mem.astype(BF16)
    fg = ffn_gain.reshape(1, D_MODEL)
    return pl.pallas_call(
        _out_kernel,
        out_shape=(jax.ShapeDtypeStruct((N, D_MODEL), F32), jax.ShapeDtypeStruct((N, D_MODEL), BF16),
                   jax.ShapeDtypeStruct((SUBLANES, N), F32)),
        grid=(N // tm,),
        in_specs=[row(mix.shape[1]), row(MEM_WIDTH), memspec, memspec, row(D_MODEL),
                  full(w_mix), full(w_mem), full(fg), full(wr_both), full(br)],
        out_specs=(row(D_MODEL), row(D_MODEL), pl.BlockSpec((SUBLANES, tm), lambda i: (0, i))),
        compiler_params=pltpu.CompilerParams(dimension_semantics=("arbitrary",),
                                             vmem_limit_bytes=VMEM_LIMIT),
        name="out_proj_route",
    )(mix, mq, mem_k, mem_v, h, w_mix, w_mem, fg, wr_both, br)


def _moe_kernel(final, n_super, un_ref, route_ref, h1_ref, utri_ref, wup_ref, wdn_ref, fgain_ref, o_ref,
                xs_ref, gs_ref, xg_ref, gg_ref, dest_ref, meta_ref):
    s = pl.program_id(0)
    k = pl.program_id(1)
    n_tiles, slots, _ = xs_ref.shape
    tm = un_ref.shape[0]
    row8 = lax.broadcasted_iota(jnp.int32, (SUBLANES, tm), 0)

    @pl.when((s == 0) & (k == 0))
    def _():
        xg_ref[...] = jnp.zeros_like(xg_ref)
        gg_ref[...] = jnp.zeros_like(gg_ref)

    @pl.when(s > 0)
    def _():
        dest = dest_ref[k][0:1, :]
        d_hi = jnp.floor(dest * (1.0 / 32.0))
        d_lo = dest - 32.0 * d_hi
        digits = jnp.where(row8 == 0, d_hi, jnp.where(row8 == 1, d_lo, 0.0)).astype(BF16)
        pick = (lax.broadcasted_iota(jnp.int32, (SUBLANES, LANES), 0)
                == lax.broadcasted_iota(jnp.int32, (SUBLANES, LANES), 1)).astype(BF16)
        dig_c = _dot_tn(digits, pick)
        dest_col = (32.0 * dig_c[:, 0:1] + dig_c[:, 1:2]).astype(jnp.int32)
        perm_t = jnp.where(lax.broadcasted_iota(jnp.int32, (tm, slots), 1) == dest_col, 1.0, 0.0).astype(BF16)
        y = h1_ref[...] + _dot(perm_t, xs_ref[k])
        o_ref[...] = _rms(y, fgain_ref[...]) if final else y

    @pl.when(s < n_super)
    def _():
        route = route_ref[...]
        g_idx = route[ROUTE_GROUP_ROW:ROUTE_GROUP_ROW + 1, :]
        member = jnp.where((row8 < N_GROUPS) & (row8.astype(F32) == g_idx), 1.0, 0.0)
        earlier = _dot(member.astype(BF16), utri_ref[...])
        rank = jnp.sum(member * earlier, axis=0, keepdims=True)
        count = jnp.sum(member, axis=1, keepdims=True)
        padded = jnp.floor((count + (MOE_ALIGN - 1.0)) * (1.0 / MOE_ALIGN)) * MOE_ALIGN
        starts = [jnp.zeros((1, 1), F32)]
        for g in range(1, N_GROUPS):
            starts.append(starts[-1] + padded[g - 1:g, :])
        seg = jnp.zeros((SUBLANES, 1), F32)
        row81 = lax.broadcasted_iota(jnp.int32, (SUBLANES, 1), 0)
        for g in range(1, N_GROUPS):
            seg = jnp.where(row81 == g, starts[g], seg)
        dest = rank + jnp.sum(member * seg, axis=0, keepdims=True)
        dest_ref[k] = jnp.broadcast_to(dest, (SUBLANES, tm))
        perm = jnp.where(lax.broadcasted_iota(jnp.int32, (slots, tm), 0) == dest.astype(jnp.int32),
                         1.0, 0.0).astype(BF16)
        xs_ref[k] = _dot(perm, un_ref[...]).astype(BF16)
        r_hi, r_lo = _split2(route)
        gs_ref[k] = _dot_nt(perm, jnp.concatenate([r_hi, r_lo], axis=0))
        for g in range(N_GROUPS):
            meta_ref[(k * N_GROUPS + g) * 2] = starts[g][0, 0].astype(jnp.int32)
            meta_ref[(k * N_GROUPS + g) * 2 + 1] = padded[g, 0].astype(jnp.int32)

    def expert_block(g, r0, rows):
        xb = xg_ref[pl.ds(r0, rows), :]
        gsb = gg_ref[pl.ds(r0, rows), :]
        acc = jnp.zeros((rows, D_MODEL), F32)
        for j in range(EXPERTS_PER_GROUP):
            e = g * EXPERTS_PER_GROUP + j
            hh = _dot(xb, wup_ref[e])
            gate = gsb[:, j:j + 1] + gsb[:, SUBLANES + j:SUBLANES + j + 1]
            act = _silu(hh[:, :EXPERT_FF]) * hh[:, EXPERT_FF:] * gate
            acc = acc + _dot(act.astype(BF16), wdn_ref[e])
        xg_ref[pl.ds(r0, rows), :] = acc.astype(BF16)

    def copy_segments(g, gather):
        off = jnp.int32(0)
        for t in range(n_tiles):
            st = meta_ref[(t * N_GROUPS + g) * 2]
            ln = meta_ref[(t * N_GROUPS + g) * 2 + 1]

            def copy_rows(src0, dst0, rows, t=t):
                src = pl.ds(pl.multiple_of(src0, MOE_ALIGN), rows)
                dst = pl.ds(pl.multiple_of(dst0, MOE_ALIGN), rows)
                if gather:
                    xg_ref[dst, :] = xs_ref[t, src, :]
                    gg_ref[dst, :] = gs_ref[t, src, :]
                else:
                    xs_ref[t, src, :] = xg_ref[dst, :]

            def copy_body(i, carry, st=st, off=off, copy_rows=copy_rows):
                copy_rows(st + i * MOE_COPY, off + i * MOE_COPY, MOE_COPY)
                return carry

            n_copy = lax.shift_right_logical(ln, MOE_COPY_SHIFT)
            lax.fori_loop(0, n_copy, copy_body, 0)
            done = n_copy * MOE_COPY

            def tail_body(i, carry, st=st, off=off, done=done, copy_rows=copy_rows):
                copy_rows(st + done + i * MOE_ALIGN, off + done + i * MOE_ALIGN, MOE_ALIGN)
                return carry

            lax.fori_loop(0, lax.shift_right_logical(ln - done, MOE_ALIGN_SHIFT), tail_body, 0)
            off = off + ln
        return off

    @pl.when((s < n_super) & (k == n_tiles - 1))
    def _():
        for g in range(N_GROUPS):
            total = copy_segments(g, True)
            n_big = total // MOE_BIG
            rem = total - n_big * MOE_BIG
            n_big = n_big + (rem > MOE_BLOCK).astype(jnp.int32)

            def big_body(i, carry, g=g):
                expert_block(g, pl.multiple_of(i * MOE_BIG, MOE_BIG), MOE_BIG)
                return carry

            lax.fori_loop(0, n_big, big_body, 0)

            @pl.when((rem > 0) & (rem <= MOE_BLOCK))
            def _(g=g, n_big=n_big):
                expert_block(g, pl.multiple_of(n_big * MOE_BIG, MOE_BIG), MOE_BLOCK)

            copy_segments(g, False)


def _moe(un, route, h1, w_up, w_down, layer, final_gain, final, tm):
    N = un.shape[0]
    n_tiles = min(MOE_SUPER, N // tm)
    n_super = N // (tm * n_tiles)
    slots = tm + N_GROUPS * MOE_ALIGN
    group_rows = n_tiles * slots + MOE_BIG
    utri = (jnp.arange(tm)[:, None] < jnp.arange(tm)[None, :]).astype(BF16)
    in_tile = lambda s, k: jnp.minimum(s, n_super - 1) * n_tiles + k
    out_tile = lambda s, k: jnp.maximum(s - 1, 0) * n_tiles + jnp.where(s > 0, k, 0)
    once = lambda a: pl.BlockSpec(a.shape, lambda s, k: (0,) * a.ndim, pipeline_mode=pl.Buffered(1))
    of_layer = lambda a: pl.BlockSpec((None,) + a.shape[1:], lambda s, k: (layer,) + (0,) * (a.ndim - 1),
                                      pipeline_mode=pl.Buffered(1))
    fg = final_gain.reshape(1, D_MODEL)
    return pl.pallas_call(
        functools.partial(_moe_kernel, final, n_super),
        out_shape=jax.ShapeDtypeStruct((N, D_MODEL), F32),
        grid=(n_super + 1, n_tiles),
        in_specs=[pl.BlockSpec((tm, D_MODEL), lambda s, k: (in_tile(s, k), 0)),
                  pl.BlockSpec((SUBLANES, tm), lambda s, k: (0, in_tile(s, k))),
                  pl.BlockSpec((tm, D_MODEL), lambda s, k: (out_tile(s, k), 0)),
                  once(utri), of_layer(w_up), of_layer(w_down), once(fg)],
        out_specs=pl.BlockSpec((tm, D_MODEL), lambda s, k: (out_tile(s, k), 0)),
        scratch_shapes=[pltpu.VMEM((n_tiles, slots, D_MODEL), BF16),
                        pltpu.VMEM((n_tiles, slots, 2 * SUBLANES), F32),
                        pltpu.VMEM((group_rows, D_MODEL), BF16),
                        pltpu.VMEM((group_rows, 2 * SUBLANES), F32),
                        pltpu.VMEM((n_tiles, SUBLANES, tm), F32),
                        pltpu.SMEM((n_tiles * N_GROUPS * 2,), jnp.int32)],
        compiler_params=pltpu.CompilerParams(dimension_semantics=("arbitrary", "arbitrary"),
                                             vmem_limit_bytes=VMEM_LIMIT),
        name="moe",
    )(un, route, h1, utri, w_up, w_down, fg)


def kernel(x, mem, mix_norm, ffn_norm, mem_norm, final_norm, gla_w_in, gla_w_gate_up, gla_b_gate, gla_out_norm,
           hg_w_in, hg_lower_bounds, hg_out_norm, w_mem_kv, w_out, w_group, b_group, w_router, b_router,
           w_up, w_down):
    B, T, _ = x.shape
    N = B * T
    depth = mix_norm.shape[0]
    tm_proj = min(512, T)
    nb_scan = 1
    tb_scan = min(512 // nb_scan, T)
    tm_out = min(1024, T)
    tm_moe = min(512, N)
    h = x.reshape(N, D_MODEL)
    w_up_bf16 = w_up.astype(BF16)
    w_down_bf16 = w_down.astype(BF16)
    for layer in range(depth):
        j = layer // 2
        if layer % 2 == 0:
            q, k, v, sg, b, mq = _gla_proj(h, mix_norm[layer], gla_w_in[j], gla_w_gate_up[j], gla_b_gate[j], tm_proj)
            heads, dk, dv_pad, dv, out_gain = GLA_HEADS, GLA_DK, GLA_DV_PAD, GLA_DV, gla_out_norm[j]
        else:
            q, k, v, sg, b, mq = _hg_proj(h, mix_norm[layer], hg_w_in[j], hg_lower_bounds, layer, tm_proj)
            heads, dk, dv_pad, dv, out_gain = HG_HEADS, HG_DK, HG_DV, HG_DV, hg_out_norm[j]
        r3 = lambda a: a.reshape(B, T, a.shape[-1])
        mix = _scan(r3(q), r3(k), r3(b), r3(v), r3(sg), out_gain, heads, dk, dv_pad, dv, tb_scan, nb_scan)
        mix = mix.reshape(N, heads * dv_pad)
        mem_k, mem_v = _mem_kv(mem, mem_norm[layer], w_mem_kv[layer])
        w_mix = _pad_heads(w_out[layer, :MIX_WIDTH].T, heads, dv, dv_pad).T
        h1, un, gates = _out_call(mix, mq, mem_k, mem_v, h, w_mix, w_out[layer, MIX_WIDTH:], ffn_norm[layer],
                                  w_group[layer], b_group[layer], w_router[layer], b_router[layer], T, tm_out)
        h = _moe(un, gates, h1, w_up_bf16, w_down_bf16, layer, final_norm, layer == depth - 1, tm_moe)
    return h.reshape(B, T, D_MODEL)
```

```python
import functools

import jax
import jax.numpy as jnp
from jax import lax
from jax.experimental import pallas as pl
from jax.experimental.pallas import tpu as pltpu

D_MODEL = 1024
N_MEM = 256
MIX_WIDTH = 768
MEM_HEADS = 4
MEM_HEAD_DIM = 64
MEM_WIDTH = 256
CHUNK = 64
GLA_HEADS = 4
GLA_KEY_WIDTH = 512
GLA_DK = 128
GLA_DV = 192
GLA_DV_PAD = 256
GLA_GATE_RANK = 16
GLA_GATE_NORMALIZER = 16.0
HG_HEADS = 6
HG_DK = 128
HG_DV = 128
N_GROUPS = 4
EXPERTS_PER_GROUP = 4
N_EXPERTS = 16
EXPERT_FF = 256
NORM_EPS = 1e-6
LANES = 128
SUBLANES = 8
MOE_BLOCK = 128
CUMSUM_SPAN = 256
MOE_ALIGN_SHIFT = 4
MOE_ALIGN = 1 << MOE_ALIGN_SHIFT
MOE_COPY_SHIFT = 6
MOE_COPY = 1 << MOE_COPY_SHIFT
MOE_BIG = 256 + 2 * MOE_ALIGN
MOE_SUPER = 4
ROUTE_GROUP_ROW = 4
ROUTE_ROWS = 32
SAFE_LOG_DECAY = -60.0
VMEM_LIMIT = 56 * 1024 * 1024

F32 = jnp.float32
BF16 = jnp.bfloat16


def _dot(a, b):
    return jnp.dot(a, b, preferred_element_type=F32)


def _dot_nt(a, b):
    return lax.dot_general(a, b, (((1,), (1,)), ((), ())), preferred_element_type=F32)


def _dot_tn(a, b):
    return lax.dot_general(a, b, (((0,), (0,)), ((), ())), preferred_element_type=F32)


def _split2(x):
    hi = x.astype(BF16)
    lo = (x - hi.astype(F32)).astype(BF16)
    return hi, lo


def _rms(x, gain):
    ms = jnp.mean(x * x, axis=-1, keepdims=True)
    return x * lax.rsqrt(ms + NORM_EPS) * gain


def _log_sigmoid(x):
    return jnp.minimum(x, 0.0) - jnp.log1p(jnp.exp(-jnp.abs(x)))


def _sigmoid(x):
    return 1.0 / (1.0 + jnp.exp(-x))


def _silu(x):
    return x * _sigmoid(x)


def _chunk_cumsum(tri_ref, x):
    tri = tri_ref[...]
    span = tri.shape[0]
    parts = []
    for r in range(x.shape[0] // span):
        hi, lo = _split2(x[r * span:(r + 1) * span])
        parts.append(_dot(tri, hi) + _dot(tri, lo))
    return jnp.concatenate(parts, axis=0) if len(parts) > 1 else parts[0]


def _mem_kv_kernel(mem_ref, gain_ref, w_ref, k_ref, v_ref):
    m = _rms(mem_ref[0], gain_ref[...]).astype(BF16)
    kv = _dot(m, w_ref[...])
    k_ref[0] = kv[:, :MEM_WIDTH].astype(BF16)
    v_ref[0] = kv[:, MEM_WIDTH:].astype(BF16)


def _mem_kv(mem, gain, w_kv):
    B = mem.shape[0]
    return pl.pallas_call(
        _mem_kv_kernel,
        out_shape=(jax.ShapeDtypeStruct((B, N_MEM, MEM_WIDTH), BF16),
                   jax.ShapeDtypeStruct((B, N_MEM, MEM_WIDTH), BF16)),
        grid=(B,),
        in_specs=[pl.BlockSpec((1, N_MEM, D_MODEL), lambda b: (b, 0, 0)),
                  pl.BlockSpec((1, D_MODEL), lambda b: (0, 0)),
                  pl.BlockSpec((D_MODEL, 2 * MEM_WIDTH), lambda b: (0, 0))],
        out_specs=(pl.BlockSpec((1, N_MEM, MEM_WIDTH), lambda b: (b, 0, 0)),
                   pl.BlockSpec((1, N_MEM, MEM_WIDTH), lambda b: (b, 0, 0))),
        compiler_params=pltpu.CompilerParams(dimension_semantics=("arbitrary",)),
        name="mem_kv",
    )(mem, gain.reshape(1, D_MODEL), w_kv.astype(BF16))


_GQ, _GK = 0, GLA_KEY_WIDTH
_GV = 2 * GLA_KEY_WIDTH
_GG = _GV + MIX_WIDTH
_GR = _GG + MIX_WIDTH
_GM = _GR + LANES
_GW = _GM + MEM_WIDTH


def _store_padded_heads(ref, x):
    zeros = jnp.zeros((x.shape[0], GLA_DV_PAD - GLA_DV), ref.dtype)
    for hd in range(GLA_HEADS):
        ref[:, hd * GLA_DV_PAD:hd * GLA_DV_PAD + GLA_DV] = x[:, hd * GLA_DV:(hd + 1) * GLA_DV].astype(ref.dtype)
        ref[:, hd * GLA_DV_PAD + GLA_DV:(hd + 1) * GLA_DV_PAD] = zeros


def _store_decay(b_ref, bmin_ref, b):
    b_ref[...] = b
    lowest = jnp.min(jnp.min(b, axis=0, keepdims=True), axis=1, keepdims=True)
    bmin_ref[0] = jnp.broadcast_to(lowest, (SUBLANES, LANES))


def _gla_proj_kernel(h_ref, gain_ref, w_ref, wg_hi_ref, wg_lo_ref, bg_ref, tri_ref,
                     q_ref, k_ref, v_ref, sg_ref, b_ref, mq_ref, bmin_ref):
    u = _rms(h_ref[...], gain_ref[...]).astype(BF16)
    q_ref[...] = (_dot(u, w_ref[:, _GQ:_GK]) * (GLA_DK ** -0.5)).astype(BF16)
    k_ref[...] = _dot(u, w_ref[:, _GK:_GV]).astype(BF16)
    _store_padded_heads(v_ref, _dot(u, w_ref[:, _GV:_GG]))
    _store_padded_heads(sg_ref, _silu(_dot(u, w_ref[:, _GG:_GR])))
    mq_ref[...] = _dot(u, w_ref[:, _GM:_GW]).astype(BF16)
    r_hi, r_lo = _split2(_dot(u, w_ref[:, _GR:_GM]))
    wg_hi = wg_hi_ref[...]
    logit = _dot(r_hi, wg_hi) + _dot(r_hi, wg_lo_ref[...]) + _dot(r_lo, wg_hi) + bg_ref[...]
    _store_decay(b_ref, bmin_ref, _chunk_cumsum(tri_ref, _log_sigmoid(logit) * (1.0 / GLA_GATE_NORMALIZER)))


_HQ, _HF, _HI, _HGG, _HM = 0, MIX_WIDTH, 2 * MIX_WIDTH, 3 * MIX_WIDTH, 4 * MIX_WIDTH
_HW = _HM + MEM_WIDTH


def _hg_proj_kernel(layer, h_ref, gain_ref, w_ref, lbp_ref, tri_ref,
                    q_ref, k_ref, v_ref, sg_ref, b_ref, mq_ref, bmin_ref):
    u = _rms(h_ref[...], gain_ref[...]).astype(BF16)
    p = lbp_ref[...]
    p = jnp.exp(p - jnp.max(p, axis=0, keepdims=True))
    p = p / jnp.sum(p, axis=0, keepdims=True)
    lb = jnp.sum(p[0:layer + 1], axis=0, keepdims=True) - p[0:1]
    q_ref[...] = _silu(_dot(u, w_ref[:, _HQ:_HF])).astype(BF16)
    z = _dot(u, w_ref[:, _HF:_HI])
    k_ref[...] = ((1.0 - lb) * _sigmoid(-z)).astype(BF16)
    v_ref[...] = _dot(u, w_ref[:, _HI:_HGG]).astype(BF16)
    sg_ref[...] = _silu(_dot(u, w_ref[:, _HGG:_HM])).astype(BF16)
    mq_ref[...] = _dot(u, w_ref[:, _HM:_HW]).astype(BF16)
    a = jnp.log(lb)
    c = jnp.log1p(-lb) + _log_sigmoid(z)
    log_f = jnp.maximum(a, c) + jnp.log1p(jnp.exp(-jnp.abs(a - c)))
    _store_decay(b_ref, bmin_ref, _chunk_cumsum(tri_ref, log_f))


def _proj_call(kernel, h, gain, w, extra, kw, vw, tm, name):
    N = h.shape[0]
    span = min(CUMSUM_SPAN, tm)
    tri = (jnp.arange(span)[:, None] >= jnp.arange(span)[None, :]) & (
        jnp.arange(span)[:, None] // CHUNK == jnp.arange(span)[None, :] // CHUNK)
    tri = tri.astype(BF16)
    row = lambda width: pl.BlockSpec((tm, width), lambda i: (i, 0))
    full = lambda a: pl.BlockSpec(a.shape, lambda i: (0,) * a.ndim)
    ins = [h, gain.reshape(1, D_MODEL), w] + list(extra) + [tri]
    return pl.pallas_call(
        kernel,
        out_shape=(jax.ShapeDtypeStruct((N, kw), BF16), jax.ShapeDtypeStruct((N, kw), BF16),
                   jax.ShapeDtypeStruct((N, vw), BF16), jax.ShapeDtypeStruct((N, vw), BF16),
                   jax.ShapeDtypeStruct((N, kw), F32), jax.ShapeDtypeStruct((N, MEM_WIDTH), BF16),
                   jax.ShapeDtypeStruct((N // tm, SUBLANES, LANES), F32)),
        grid=(N // tm,),
        in_specs=[row(D_MODEL)] + [full(a) for a in ins[1:]],
        out_specs=(row(kw), row(kw), row(vw), row(vw), row(kw), row(MEM_WIDTH),
                   pl.BlockSpec((1, SUBLANES, LANES), lambda i: (i, 0, 0))),
        compiler_params=pltpu.CompilerParams(dimension_semantics=("arbitrary",),
                                             vmem_limit_bytes=VMEM_LIMIT),
        name=name,
    )(*ins)


def _pad_heads(w, heads, dv, dv_pad):
    lead = w.shape[:-1]
    w = w.reshape(lead + (heads, dv))
    w = jnp.pad(w, [(0, 0)] * len(lead) + [(0, 0), (0, dv_pad - dv)])
    return w.reshape(lead + (heads * dv_pad,))


def _gla_proj(h, gain, w_in, w_gate_up, b_gate, tm):
    q, k, v, g, r, mq = jnp.split(
        w_in, [GLA_KEY_WIDTH, 2 * GLA_KEY_WIDTH, 2 * GLA_KEY_WIDTH + MIX_WIDTH,
               2 * GLA_KEY_WIDTH + 2 * MIX_WIDTH, 2 * GLA_KEY_WIDTH + 2 * MIX_WIDTH + GLA_GATE_RANK], axis=1)
    w = jnp.concatenate([q, k, v, g, jnp.pad(r, ((0, 0), (0, LANES - GLA_GATE_RANK))), mq],
                        axis=1).astype(BF16)
    wg = jnp.pad(w_gate_up, ((0, LANES - GLA_GATE_RANK), (0, 0)))
    wg_hi, wg_lo = _split2(wg)
    return _proj_call(_gla_proj_kernel, h, gain, w, [wg_hi, wg_lo, b_gate.reshape(1, GLA_KEY_WIDTH)],
                      GLA_KEY_WIDTH, GLA_HEADS * GLA_DV_PAD, tm, "gla_proj")


def _hg_proj(h, gain, w_in, lower_bound_params, layer, tm):
    return _proj_call(functools.partial(_hg_proj_kernel, layer), h, gain, w_in.astype(BF16),
                      [lower_bound_params], MIX_WIDTH, MIX_WIDTH, tm, "hg_proj")


def _scan_kernel(heads, dk, dv_pad, dv, n_chunks,
                 q_ref, k_ref, b_ref, bmin_ref, v_ref, sg_ref, gain_ref, o_ref, st_ref, kf_ref):
    @pl.when(pl.program_id(1) == 0)
    def _():
        st_ref[...] = jnp.zeros_like(st_ref)

    gain = gain_ref[...]
    row = lax.broadcasted_iota(jnp.int32, (CHUNK, CHUNK), 0)
    col = lax.broadcasted_iota(jnp.int32, (CHUNK, CHUNK), 1)
    causal = row >= col

    n_batch = q_ref.shape[0]

    def finish(bi, hd, rows, scores, qd, kl, eb_last):
        vc = slice(hd * dv_pad, (hd + 1) * dv_pad)
        v = v_ref[bi, rows, vc]
        st = st_ref[bi * heads + hd]
        out = _dot(scores.astype(BF16), v) + _dot_nt(qd, st.astype(BF16))
        st_ref[bi * heads + hd] = _dot_tn(v, kl) + st * eb_last
        ms = jnp.sum(out * out, axis=-1, keepdims=True) * (1.0 / dv)
        y = out * lax.rsqrt(ms + NORM_EPS) * gain * sg_ref[bi, rows, vc].astype(F32)
        o_ref[bi, rows, vc] = y.astype(BF16)

    def load(bi, hd, rows):
        kc = slice(hd * dk, (hd + 1) * dk)
        return (q_ref[bi, rows, kc].astype(F32), k_ref[bi, rows, kc].astype(F32), b_ref[bi, rows, kc])

    all_safe = jnp.min(bmin_ref[...]) >= SAFE_LOG_DECAY

    @pl.when(all_safe)
    def _():
        for c in range(n_chunks):
            rows = slice(c * CHUNK, (c + 1) * CHUNK)
            for bi in range(n_batch):
                for hd in range(heads):
                    q, k, b = load(bi, hd, rows)
                    eb = jnp.exp(b)
                    eb_last = eb[CHUNK - 1:CHUNK, :]
                    qd = (q * eb).astype(BF16)
                    kd = k * jnp.exp(-b)
                    scores = _dot_nt(qd, kd.astype(BF16))
                    finish(bi, hd, rows, jnp.where(causal, scores, 0.0), qd, (kd * eb_last).astype(BF16),
                           eb_last)

    @pl.when(jnp.logical_not(all_safe))
    def _():
        for bi, hd in [(bi, hd) for bi in range(n_batch) for hd in range(heads)]:
            kc = slice(hd * dk, (hd + 1) * dk)

            def chunk_body(c, carry, bi=bi, hd=hd, kc=kc):
                start = pl.multiple_of(c * CHUNK, CHUNK)
                rows = pl.ds(start, CHUNK)
                q, k, b = load(bi, hd, rows)
                kf_ref[...] = k

                def col_body(j, sc):
                    base = pl.multiple_of((j >> 3) << 3, SUBLANES)
                    pick = lax.broadcasted_iota(jnp.int32, (SUBLANES, dk), 0) == (j & (SUBLANES - 1))
                    kj = jnp.sum(jnp.where(pick, kf_ref[pl.ds(base, SUBLANES), :], 0.0), axis=0, keepdims=True)
                    b8 = b_ref[bi, pl.ds(pl.multiple_of(start + base, SUBLANES), SUBLANES), kc]
                    bj = jnp.sum(jnp.where(pick, b8, 0.0), axis=0, keepdims=True)
                    rid = lax.broadcasted_iota(jnp.int32, (CHUNK, dk), 0)
                    dec = jnp.exp(jnp.where(rid >= j, b - bj, -jnp.inf))
                    colv = jnp.sum(q * kj * dec, axis=-1, keepdims=True)
                    return jnp.where(col == j, colv, sc)

                scores = lax.fori_loop(0, CHUNK, col_body, jnp.zeros((CHUNK, CHUNK), F32))
                b_last = b[CHUNK - 1:CHUNK, :]
                finish(bi, hd, rows, scores, (q * jnp.exp(b)).astype(BF16),
                       (k * jnp.exp(b_last - b)).astype(BF16), jnp.exp(b_last))
                return carry

            lax.fori_loop(0, n_chunks, chunk_body, 0)


def _scan(q, k, b, bmin, v, sg, gain, heads, dk, dv_pad, dv, tb, nb):
    B, T, kw = q.shape
    vw = v.shape[-1]
    blk = lambda w: pl.BlockSpec((nb, tb, w), lambda bi, ti: (bi, ti, 0))
    gain_row = jnp.pad(gain, (0, dv_pad - dv)).reshape(1, dv_pad)
    return pl.pallas_call(
        functools.partial(_scan_kernel, heads, dk, dv_pad, dv, tb // CHUNK),
        out_shape=jax.ShapeDtypeStruct((B, T, vw), BF16),
        grid=(B // nb, T // tb),
        in_specs=[blk(kw), blk(kw), blk(kw),
                  pl.BlockSpec((nb, 1, SUBLANES, LANES), lambda bi, ti: (bi, ti, 0, 0)),
                  blk(vw), blk(vw), pl.BlockSpec((1, dv_pad), lambda bi, ti: (0, 0))],
        out_specs=blk(vw),
        scratch_shapes=[pltpu.VMEM((nb * heads, dv_pad, dk), F32), pltpu.VMEM((CHUNK, dk), F32)],
        compiler_params=pltpu.CompilerParams(dimension_semantics=("arbitrary", "arbitrary"),
                                             vmem_limit_bytes=VMEM_LIMIT),
        name="chunk_scan",
    )(q, k, b, bmin, v, sg, gain_row)


def _out_kernel(mix_ref, mq_ref, mk_ref, mv_ref, h_ref, wmix_ref, wmem_ref, fgain_ref,
                wr_ref, br_ref, h1_ref, un_ref, route_ref):
    tm = mix_ref.shape[0]
    mk = mk_ref[0]
    mv = mv_ref[0]
    lane_w = lax.broadcasted_iota(jnp.int32, (1, MEM_WIDTH), 1)
    mq = mq_ref[...] * (MEM_HEAD_DIM ** -0.5)
    mem_o = jnp.zeros((tm, MEM_WIDTH), F32)
    for hd in range(MEM_HEADS):
        in_head = (lane_w >= hd * MEM_HEAD_DIM) & (lane_w < (hd + 1) * MEM_HEAD_DIM)
        s = _dot_nt(jnp.where(in_head, mq, jnp.zeros_like(mq)), mk)
        e = jnp.exp(s - jnp.max(s, axis=-1, keepdims=True))
        denom = jnp.sum(e, axis=-1, keepdims=True)
        mem_o = mem_o + _dot(e.astype(BF16), jnp.where(in_head, mv, jnp.zeros_like(mv))) * (1.0 / denom)
    h1 = h_ref[...] + _dot(mix_ref[...], wmix_ref[...]) + _dot(mem_o.astype(BF16), wmem_ref[...])
    h1_ref[...] = h1
    un = _rms(h1, fgain_ref[...])
    un_ref[...] = un.astype(BF16)

    u_hi, u_lo = _split2(un)
    both = _dot_nt(wr_ref[...], u_hi)
    lg = (both[:ROUTE_ROWS] + both[ROUTE_ROWS:] + _dot_nt(wr_ref[:ROUTE_ROWS, :], u_lo) + br_ref[...])
    row = lax.broadcasted_iota(jnp.int32, (ROUTE_ROWS, tm), 0)
    neg = -jnp.inf
    gl = jnp.where(row < N_GROUPS, lg, neg)
    gmax = jnp.max(gl, axis=0, keepdims=True)
    g_idx = jnp.min(jnp.where(gl == gmax, row, ROUTE_ROWS), axis=0, keepdims=True)
    g_w = 1.0 / jnp.sum(jnp.exp(gl - gmax), axis=0, keepdims=True)
    first = N_GROUPS + g_idx * EXPERTS_PER_GROUP
    el = jnp.where((row >= first) & (row < first + EXPERTS_PER_GROUP), lg, neg)
    t1 = jnp.max(el, axis=0, keepdims=True)
    i1 = jnp.min(jnp.where(el == t1, row, ROUTE_ROWS), axis=0, keepdims=True)
    el2 = jnp.where(row == i1, neg, el)
    t2 = jnp.max(el2, axis=0, keepdims=True)
    i2 = jnp.min(jnp.where(el2 == t2, row, ROUTE_ROWS), axis=0, keepdims=True)
    e2 = jnp.exp(t2 - t1)
    w1 = g_w / (1.0 + e2)
    w2 = g_w * e2 / (1.0 + e2)
    row8 = lax.broadcasted_iota(jnp.int32, (SUBLANES, tm), 0)
    route_ref[...] = (jnp.where(row8 == i1 - first, w1, 0.0) + jnp.where(row8 == i2 - first, w2, 0.0)
                      + jnp.where(row8 == ROUTE_GROUP_ROW, g_idx.astype(F32), 0.0))


def _out_call(mix, mq, mem_k, mem_v, h, w_mix, w_mem, ffn_gain, w_group, b_group, w_router, b_router, T, tm):
    N = h.shape[0]
    tiles_per_batch = T // tm
    pad = ROUTE_ROWS - N_GROUPS - N_EXPERTS
    wr = jnp.pad(jnp.concatenate([w_group, w_router], axis=1).T, ((0, pad), (0, 0)))
    br = jnp.broadcast_to(jnp.pad(jnp.concatenate([b_group, b_router]), (0, pad))[:, None], (ROUTE_ROWS, tm))
    wr_both = jnp.concatenate(_split2(wr), axis=0)
    row = lambda width: pl.BlockSpec((tm, width), lambda i: (i, 0))
    full = lambda a: pl.BlockSpec(a.shape, lambda i: (0,) * a.ndim)
    memspec = pl.BlockSpec((1, N_MEM, MEM_WIDTH), lambda i: (i // tiles_per_batch, 0, 0))
    w_mix = w_mix.astype(BF16)
    w_mem = w_mem.astype(BF16)
    fg = ffn_gain.reshape(1, D_MODEL)
    return pl.pallas_call(
        _out_kernel,
        out_shape=(jax.ShapeDtypeStruct((N, D_MODEL), F32), jax.ShapeDtypeStruct((N, D_MODEL), BF16),
                   jax.ShapeDtypeStruct((SUBLANES, N), F32)),
        grid=(N // tm,),
        in_specs=[row(mix.shape[1]), row(MEM_WIDTH), memspec, memspec, row(D_MODEL),
                  full(w_mix), full(w_mem), full(fg), full(wr_both), full(br)],
        out_specs=(row(D_MODEL), row(D_MODEL), pl.BlockSpec((SUBLANES, tm), lambda i: (0, i))),
        compiler_params=pltpu.CompilerParams(dimension_semantics=("arbitrary",),
                                             vmem_limit_bytes=VMEM_LIMIT),
        name="out_proj_route",
    )(mix, mq, mem_k, mem_v, h, w_mix, w_mem, fg, wr_both, br)


def _moe_kernel(final, n_super, un_ref, route_ref, h1_ref, utri_ref, wup_ref, wdn_ref, fgain_ref, o_ref,
                xs_ref, gs_ref, xg_ref, gg_ref, dest_ref, meta_ref):
    s = pl.program_id(0)
    k = pl.program_id(1)
    n_tiles, slots, _ = xs_ref.shape
    tm = un_ref.shape[0]
    row8 = lax.broadcasted_iota(jnp.int32, (SUBLANES, tm), 0)

    @pl.when((s == 0) & (k == 0))
    def _():
        xg_ref[...] = jnp.zeros_like(xg_ref)
        gg_ref[...] = jnp.zeros_like(gg_ref)

    @pl.when(s > 0)
    def _():
        dest = dest_ref[k][0:1, :]
        d_hi = jnp.floor(dest * (1.0 / 32.0))
        d_lo = dest - 32.0 * d_hi
        digits = jnp.where(row8 == 0, d_hi, jnp.where(row8 == 1, d_lo, 0.0)).astype(BF16)
        pick = (lax.broadcasted_iota(jnp.int32, (SUBLANES, LANES), 0)
                == lax.broadcasted_iota(jnp.int32, (SUBLANES, LANES), 1)).astype(BF16)
        dig_c = _dot_tn(digits, pick)
        dest_col = (32.0 * dig_c[:, 0:1] + dig_c[:, 1:2]).astype(jnp.int32)
        perm_t = jnp.where(lax.broadcasted_iota(jnp.int32, (tm, slots), 1) == dest_col, 1.0, 0.0).astype(BF16)
        y = h1_ref[...] + _dot(perm_t, xs_ref[k])
        o_ref[...] = _rms(y, fgain_ref[...]) if final else y

    @pl.when(s < n_super)
    def _():
        route = route_ref[...]
        g_idx = route[ROUTE_GROUP_ROW:ROUTE_GROUP_ROW + 1, :]
        member = jnp.where((row8 < N_GROUPS) & (row8.astype(F32) == g_idx), 1.0, 0.0)
        earlier = _dot(member.astype(BF16), utri_ref[...])
        rank = jnp.sum(member * earlier, axis=0, keepdims=True)
        count = jnp.sum(member, axis=1, keepdims=True)
        padded = jnp.floor((count + (MOE_ALIGN - 1.0)) * (1.0 / MOE_ALIGN)) * MOE_ALIGN
        starts = [jnp.zeros((1, 1), F32)]
        for g in range(1, N_GROUPS):
            starts.append(starts[-1] + padded[g - 1:g, :])
        seg = jnp.zeros((SUBLANES, 1), F32)
        row81 = lax.broadcasted_iota(jnp.int32, (SUBLANES, 1), 0)
        for g in range(1, N_GROUPS):
            seg = jnp.where(row81 == g, starts[g], seg)
        dest = rank + jnp.sum(member * seg, axis=0, keepdims=True)
        dest_ref[k] = jnp.broadcast_to(dest, (SUBLANES, tm))
        perm = jnp.where(lax.broadcasted_iota(jnp.int32, (slots, tm), 0) == dest.astype(jnp.int32),
                         1.0, 0.0).astype(BF16)
        xs_ref[k] = _dot(perm, un_ref[...]).astype(BF16)
        r_hi, r_lo = _split2(route)
        gs_ref[k] = _dot_nt(perm, jnp.concatenate([r_hi, r_lo], axis=0))
        for g in range(N_GROUPS):
            meta_ref[(k * N_GROUPS + g) * 2] = starts[g][0, 0].astype(jnp.int32)
            meta_ref[(k * N_GROUPS + g) * 2 + 1] = padded[g, 0].astype(jnp.int32)

    def expert_block(g, r0, rows):
        xb = xg_ref[pl.ds(r0, rows), :]
        gsb = gg_ref[pl.ds(r0, rows), :]
        acc = jnp.zeros((rows, D_MODEL), F32)
        for j in range(EXPERTS_PER_GROUP):
            e = g * EXPERTS_PER_GROUP + j
            hh = _dot(xb, wup_ref[e])
            gate = gsb[:, j:j + 1] + gsb[:, SUBLANES + j:SUBLANES + j + 1]
            act = _silu(hh[:, :EXPERT_FF]) * hh[:, EXPERT_FF:] * gate
            acc = acc + _dot(act.astype(BF16), wdn_ref[e])
        xg_ref[pl.ds(r0, rows), :] = acc.astype(BF16)

    def copy_segments(g, gather):
        off = jnp.int32(0)
        for t in range(n_tiles):
            st = meta_ref[(t * N_GROUPS + g) * 2]
            ln = meta_ref[(t * N_GROUPS + g) * 2 + 1]

            def copy_rows(src0, dst0, rows, t=t):
                src = pl.ds(pl.multiple_of(src0, MOE_ALIGN), rows)
                dst = pl.ds(pl.multiple_of(dst0, MOE_ALIGN), rows)
                if gather:
                    xg_ref[dst, :] = xs_ref[t, src, :]
                    gg_ref[dst, :] = gs_ref[t, src, :]
                else:
                    xs_ref[t, src, :] = xg_ref[dst, :]

            def copy_body(i, carry, st=st, off=off, copy_rows=copy_rows):
                copy_rows(st + i * MOE_COPY, off + i * MOE_COPY, MOE_COPY)
                return carry

            n_copy = lax.shift_right_logical(ln, MOE_COPY_SHIFT)
            lax.fori_loop(0, n_copy, copy_body, 0)
            done = n_copy * MOE_COPY

            def tail_body(i, carry, st=st, off=off, done=done, copy_rows=copy_rows):
                copy_rows(st + done + i * MOE_ALIGN, off + done + i * MOE_ALIGN, MOE_ALIGN)
                return carry

            lax.fori_loop(0, lax.shift_right_logical(ln - done, MOE_ALIGN_SHIFT), tail_body, 0)
            off = off + ln
        return off

    @pl.when((s < n_super) & (k == n_tiles - 1))
    def _():
        for g in range(N_GROUPS):
            total = copy_segments(g, True)
            n_big = total // MOE_BIG
            rem = total - n_big * MOE_BIG
            n_big = n_big + (rem > MOE_BLOCK).astype(jnp.int32)

            def big_body(i, carry, g=g):
                expert_block(g, pl.multiple_of(i * MOE_BIG, MOE_BIG), MOE_BIG)
                return carry

            lax.fori_loop(0, n_big, big_body, 0)

            @pl.when((rem > 0) & (rem <= MOE_BLOCK))
            def _(g=g, n_big=n_big):
                expert_block(g, pl.multiple_of(n_big * MOE_BIG, MOE_BIG), MOE_BLOCK)

            copy_segments(g, False)


def _moe(un, route, h1, w_up, w_down, layer, final_gain, final, tm):
    N = un.shape[0]
    n_tiles = min(MOE_SUPER, N // tm)
    n_super = N // (tm * n_tiles)
    slots = tm + N_GROUPS * MOE_ALIGN
    group_rows = n_tiles * slots + MOE_BIG
    utri = (jnp.arange(tm)[:, None] < jnp.arange(tm)[None, :]).astype(BF16)
    in_tile = lambda s, k: jnp.minimum(s, n_super - 1) * n_tiles + k
    out_tile = lambda s, k: jnp.maximum(s - 1, 0) * n_tiles + jnp.where(s > 0, k, 0)
    once = lambda a: pl.BlockSpec(a.shape, lambda s, k: (0,) * a.ndim, pipeline_mode=pl.Buffered(1))
    of_layer = lambda a: pl.BlockSpec((None,) + a.shape[1:], lambda s, k: (layer,) + (0,) * (a.ndim - 1),
                                      pipeline_mode=pl.Buffered(1))
    fg = final_gain.reshape(1, D_MODEL)
    return pl.pallas_call(
        functools.partial(_moe_kernel, final, n_super),
        out_shape=jax.ShapeDtypeStruct((N, D_MODEL), F32),
        grid=(n_super + 1, n_tiles),
        in_specs=[pl.BlockSpec((tm, D_MODEL), lambda s, k: (in_tile(s, k), 0)),
                  pl.BlockSpec((SUBLANES, tm), lambda s, k: (0, in_tile(s, k))),
                  pl.BlockSpec((tm, D_MODEL), lambda s, k: (out_tile(s, k), 0)),
                  once(utri), of_layer(w_up), of_layer(w_down), once(fg)],
        out_specs=pl.BlockSpec((tm, D_MODEL), lambda s, k: (out_tile(s, k), 0)),
        scratch_shapes=[pltpu.VMEM((n_tiles, slots, D_MODEL), BF16),
                        pltpu.VMEM((n_tiles, slots, 2 * SUBLANES), F32),
                        pltpu.VMEM((group_rows, D_MODEL), BF16),
                        pltpu.VMEM((group_rows, 2 * SUBLANES), F32),
                        pltpu.VMEM((n_tiles, SUBLANES, tm), F32),
                        pltpu.SMEM((n_tiles * N_GROUPS * 2,), jnp.int32)],
        compiler_params=pltpu.CompilerParams(dimension_semantics=("arbitrary", "arbitrary"),
                                             vmem_limit_bytes=VMEM_LIMIT),
        name="moe",
    )(un, route, h1, utri, w_up, w_down, fg)


def kernel(x, mem, mix_norm, ffn_norm, mem_norm, final_norm, gla_w_in, gla_w_gate_up, gla_b_gate, gla_out_norm,
           hg_w_in, hg_lower_bounds, hg_out_norm, w_mem_kv, w_out, w_group, b_group, w_router, b_router,
           w_up, w_down):
    B, T, _ = x.shape
    N = B * T
    depth = mix_norm.shape[0]
    tm_proj = min(512, T)
    nb_scan = 1
    tb_scan = tm_proj
    tm_out = min(1024, T)
    tm_moe = min(512, N)
    h = x.reshape(N, D_MODEL)
    w_up_bf16 = w_up.astype(BF16)
    w_down_bf16 = w_down.astype(BF16)
    for layer in range(depth):
        j = layer // 2
        if layer % 2 == 0:
            q, k, v, sg, b, mq, bmin = _gla_proj(h, mix_norm[layer], gla_w_in[j], gla_w_gate_up[j], gla_b_gate[j],
                                                 tm_proj)
            heads, dk, dv_pad, dv, out_gain = GLA_HEADS, GLA_DK, GLA_DV_PAD, GLA_DV, gla_out_norm[j]
        else:
            q, k, v, sg, b, mq, bmin = _hg_proj(h, mix_norm[layer], hg_w_in[j], hg_lower_bounds, layer, tm_proj)
            heads, dk, dv_pad, dv, out_gain = HG_HEADS, HG_DK, HG_DV, HG_DV, hg_out_norm[j]
        r3 = lambda a: a.reshape(B, T, a.shape[-1])
        bmin = bmin.reshape(B, T // tb_scan, SUBLANES, LANES)
        mix = _scan(r3(q), r3(k), r3(b), bmin, r3(v), r3(sg), out_gain, heads, dk, dv_pad, dv, tb_scan, nb_scan)
        mix = mix.reshape(N, heads * dv_pad)
        mem_k, mem_v = _mem_kv(mem, mem_norm[layer], w_mem_kv[layer])
        w_mix = _pad_heads(w_out[layer, :MIX_WIDTH].T, heads, dv, dv_pad).T
        h1, un, gates = _out_call(mix, mq, mem_k, mem_v, h, w_mix, w_out[layer, MIX_WIDTH:], ffn_norm[layer],
                                  w_group[layer], b_group[layer], w_router[layer], b_router[layer], T, tm_out)
        h = _moe(un, gates, h1, w_up_bf16, w_down_bf16, layer, final_norm, layer == depth - 1, tm_moe)
    return h.reshape(B, T, D_MODEL)
```

```python
import functools

import jax
import jax.numpy as jnp
from jax import lax
from jax.experimental import pallas as pl
from jax.experimental.pallas import tpu as pltpu

D_MODEL = 1024
N_MEM = 256
MIX_WIDTH = 768
MEM_HEADS = 4
MEM_HEAD_DIM = 64
MEM_WIDTH = 256
CHUNK = 64
GLA_HEADS = 4
GLA_KEY_WIDTH = 512
GLA_DK = 128
GLA_DV = 192
GLA_DV_PAD = 256
GLA_GATE_RANK = 16
GLA_GATE_NORMALIZER = 16.0
HG_HEADS = 6
HG_DK = 128
HG_DV = 128
N_GROUPS = 4
EXPERTS_PER_GROUP = 4
N_EXPERTS = 16
EXPERT_FF = 256
NORM_EPS = 1e-6
LANES = 128
SUBLANES = 8
MOE_BLOCK = 128
CUMSUM_SPAN = 256
MOE_ALIGN_SHIFT = 4
MOE_ALIGN = 1 << MOE_ALIGN_SHIFT
MOE_COPY_SHIFT = 6
MOE_COPY = 1 << MOE_COPY_SHIFT
MOE_BIG = 256 + 2 * MOE_ALIGN
MOE_SUPER = 4
ROUTE_GROUP_ROW = 4
ROUTE_ROWS = 32
SAFE_LOG_DECAY = -60.0
VMEM_LIMIT = 56 * 1024 * 1024

F32 = jnp.float32
BF16 = jnp.bfloat16


def _dot(a, b):
    return jnp.dot(a, b, preferred_element_type=F32)


def _dot_nt(a, b):
    return lax.dot_general(a, b, (((1,), (1,)), ((), ())), preferred_element_type=F32)


def _dot_tn(a, b):
    return lax.dot_general(a, b, (((0,), (0,)), ((), ())), preferred_element_type=F32)


def _split2(x):
    hi = x.astype(BF16)
    lo = (x - hi.astype(F32)).astype(BF16)
    return hi, lo


def _rms(x, gain):
    ms = jnp.mean(x * x, axis=-1, keepdims=True)
    return x * lax.rsqrt(ms + NORM_EPS) * gain


def _log_sigmoid(x):
    return jnp.minimum(x, 0.0) - jnp.log1p(jnp.exp(-jnp.abs(x)))


def _sigmoid(x):
    return 1.0 / (1.0 + jnp.exp(-x))


def _silu(x):
    return x * _sigmoid(x)


def _chunk_cumsum(tri_ref, x):
    tri = tri_ref[...]
    span = tri.shape[0]
    parts = []
    for r in range(x.shape[0] // span):
        hi, lo = _split2(x[r * span:(r + 1) * span])
        parts.append(_dot(tri, hi) + _dot(tri, lo))
    return jnp.concatenate(parts, axis=0) if len(parts) > 1 else parts[0]


def _mem_kv_kernel(mem_ref, gain_ref, w_ref, k_ref, v_ref):
    m = _rms(mem_ref[0], gain_ref[...]).astype(BF16)
    kv = _dot(m, w_ref[...])
    k_ref[0] = kv[:, :MEM_WIDTH].astype(BF16)
    v_ref[0] = kv[:, MEM_WIDTH:].astype(BF16)


def _mem_kv(mem, gain, w_kv):
    B = mem.shape[0]
    return pl.pallas_call(
        _mem_kv_kernel,
        out_shape=(jax.ShapeDtypeStruct((B, N_MEM, MEM_WIDTH), BF16),
                   jax.ShapeDtypeStruct((B, N_MEM, MEM_WIDTH), BF16)),
        grid=(B,),
        in_specs=[pl.BlockSpec((1, N_MEM, D_MODEL), lambda b: (b, 0, 0)),
                  pl.BlockSpec((1, D_MODEL), lambda b: (0, 0)),
                  pl.BlockSpec((D_MODEL, 2 * MEM_WIDTH), lambda b: (0, 0))],
        out_specs=(pl.BlockSpec((1, N_MEM, MEM_WIDTH), lambda b: (b, 0, 0)),
                   pl.BlockSpec((1, N_MEM, MEM_WIDTH), lambda b: (b, 0, 0))),
        compiler_params=pltpu.CompilerParams(dimension_semantics=("arbitrary",)),
        name="mem_kv",
    )(mem, gain.reshape(1, D_MODEL), w_kv.astype(BF16))


_GQ, _GK = 0, GLA_KEY_WIDTH
_GV = 2 * GLA_KEY_WIDTH
_GG = _GV + MIX_WIDTH
_GR = _GG + MIX_WIDTH
_GM = _GR + LANES
_GW = _GM + MEM_WIDTH


def _store_padded_heads(ref, x):
    zeros = jnp.zeros((x.shape[0], GLA_DV_PAD - GLA_DV), ref.dtype)
    for hd in range(GLA_HEADS):
        ref[:, hd * GLA_DV_PAD:hd * GLA_DV_PAD + GLA_DV] = x[:, hd * GLA_DV:(hd + 1) * GLA_DV].astype(ref.dtype)
        ref[:, hd * GLA_DV_PAD + GLA_DV:(hd + 1) * GLA_DV_PAD] = zeros


def _store_decay(b_ref, bmin_ref, b):
    b_ref[...] = b
    lowest = jnp.min(jnp.min(b, axis=0, keepdims=True), axis=1, keepdims=True)
    bmin_ref[0] = jnp.broadcast_to(lowest, (SUBLANES, LANES))


def _gla_proj_kernel(h_ref, gain_ref, w_ref, wg_ref, bg_ref, tri_ref,
                     q_ref, k_ref, v_ref, sg_ref, b_ref, mq_ref, bmin_ref):
    u = _rms(h_ref[...], gain_ref[...]).astype(BF16)
    q_ref[...] = (_dot(u, w_ref[:, _GQ:_GK]) * (GLA_DK ** -0.5)).astype(BF16)
    k_ref[...] = _dot(u, w_ref[:, _GK:_GV]).astype(BF16)
    _store_padded_heads(v_ref, _dot(u, w_ref[:, _GV:_GG]))
    _store_padded_heads(sg_ref, _silu(_dot(u, w_ref[:, _GG:_GR])))
    mq_ref[...] = _dot(u, w_ref[:, _GM:_GW]).astype(BF16)
    r = _dot(u, w_ref[:, _GR:_GM])
    r_hi = r.astype(BF16)
    lane = lax.broadcasted_iota(jnp.int32, r.shape, 1)
    is_lo = (lane >= GLA_GATE_RANK) & (lane < 2 * GLA_GATE_RANK)
    r_parts = jnp.where(is_lo, r - r_hi.astype(F32), r_hi.astype(F32)).astype(BF16)
    logit = _dot(r_parts, wg_ref[...]) + bg_ref[...]
    _store_decay(b_ref, bmin_ref, _chunk_cumsum(tri_ref, _log_sigmoid(logit) * (1.0 / GLA_GATE_NORMALIZER)))


_HQ, _HF, _HI, _HGG, _HM = 0, MIX_WIDTH, 2 * MIX_WIDTH, 3 * MIX_WIDTH, 4 * MIX_WIDTH
_HW = _HM + MEM_WIDTH


def _hg_proj_kernel(layer, h_ref, gain_ref, w_ref, lbp_ref, tri_ref,
                    q_ref, k_ref, v_ref, sg_ref, b_ref, mq_ref, bmin_ref):
    u = _rms(h_ref[...], gain_ref[...]).astype(BF16)
    p = lbp_ref[...]
    p = jnp.exp(p - jnp.max(p, axis=0, keepdims=True))
    p = p / jnp.sum(p, axis=0, keepdims=True)
    lb = jnp.sum(p[0:layer + 1], axis=0, keepdims=True) - p[0:1]
    q_ref[...] = _silu(_dot(u, w_ref[:, _HQ:_HF])).astype(BF16)
    z = _dot(u, w_ref[:, _HF:_HI])
    k_ref[...] = ((1.0 - lb) * _sigmoid(-z)).astype(BF16)
    v_ref[...] = _dot(u, w_ref[:, _HI:_HGG]).astype(BF16)
    sg_ref[...] = _silu(_dot(u, w_ref[:, _HGG:_HM])).astype(BF16)
    mq_ref[...] = _dot(u, w_ref[:, _HM:_HW]).astype(BF16)
    a = jnp.log(lb)
    c = jnp.log1p(-lb) + _log_sigmoid(z)
    log_f = jnp.maximum(a, c) + jnp.log1p(jnp.exp(-jnp.abs(a - c)))
    _store_decay(b_ref, bmin_ref, _chunk_cumsum(tri_ref, log_f))


def _proj_call(kernel, h, gain, w, extra, kw, vw, tm, name):
    N = h.shape[0]
    span = min(CUMSUM_SPAN, tm)
    tri = (jnp.arange(span)[:, None] >= jnp.arange(span)[None, :]) & (
        jnp.arange(span)[:, None] // CHUNK == jnp.arange(span)[None, :] // CHUNK)
    tri = tri.astype(BF16)
    row = lambda width: pl.BlockSpec((tm, width), lambda i: (i, 0))
    full = lambda a: pl.BlockSpec(a.shape, lambda i: (0,) * a.ndim)
    ins = [h, gain.reshape(1, D_MODEL), w] + list(extra) + [tri]
    return pl.pallas_call(
        kernel,
        out_shape=(jax.ShapeDtypeStruct((N, kw), BF16), jax.ShapeDtypeStruct((N, kw), BF16),
                   jax.ShapeDtypeStruct((N, vw), BF16), jax.ShapeDtypeStruct((N, vw), BF16),
                   jax.ShapeDtypeStruct((N, kw), F32), jax.ShapeDtypeStruct((N, MEM_WIDTH), BF16),
                   jax.ShapeDtypeStruct((N // tm, SUBLANES, LANES), F32)),
        grid=(N // tm,),
        in_specs=[row(D_MODEL)] + [full(a) for a in ins[1:]],
        out_specs=(row(kw), row(kw), row(vw), row(vw), row(kw), row(MEM_WIDTH),
                   pl.BlockSpec((1, SUBLANES, LANES), lambda i: (i, 0, 0))),
        compiler_params=pltpu.CompilerParams(dimension_semantics=("arbitrary",),
                                             vmem_limit_bytes=VMEM_LIMIT),
        name=name,
    )(*ins)


def _pad_heads(w, heads, dv, dv_pad):
    lead = w.shape[:-1]
    w = w.reshape(lead + (heads, dv))
    w = jnp.pad(w, [(0, 0)] * len(lead) + [(0, 0), (0, dv_pad - dv)])
    return w.reshape(lead + (heads * dv_pad,))


def _gla_proj(h, gain, w_in, w_gate_up, b_gate, tm):
    q, k, v, g, r, mq = jnp.split(
        w_in, [GLA_KEY_WIDTH, 2 * GLA_KEY_WIDTH, 2 * GLA_KEY_WIDTH + MIX_WIDTH,
               2 * GLA_KEY_WIDTH + 2 * MIX_WIDTH, 2 * GLA_KEY_WIDTH + 2 * MIX_WIDTH + GLA_GATE_RANK], axis=1)
    r3 = jnp.pad(jnp.concatenate([r, r, r], axis=1), ((0, 0), (0, LANES - 3 * GLA_GATE_RANK)))
    w = jnp.concatenate([q, k, v, g, r3, mq], axis=1).astype(BF16)
    wg_hi, wg_lo = _split2(w_gate_up)
    wg = jnp.pad(jnp.concatenate([wg_hi, wg_hi, wg_lo], axis=0), ((0, LANES - 3 * GLA_GATE_RANK), (0, 0)))
    return _proj_call(_gla_proj_kernel, h, gain, w, [wg, b_gate.reshape(1, GLA_KEY_WIDTH)],
                      GLA_KEY_WIDTH, GLA_HEADS * GLA_DV_PAD, tm, "gla_proj")


def _hg_proj(h, gain, w_in, lower_bound_params, layer, tm):
    return _proj_call(functools.partial(_hg_proj_kernel, layer), h, gain, w_in.astype(BF16),
                      [lower_bound_params], MIX_WIDTH, MIX_WIDTH, tm, "hg_proj")


def _scan_kernel(heads, dk, dv_pad, dv, n_chunks,
                 q_ref, k_ref, b_ref, bmin_ref, v_ref, sg_ref, gain_ref, o_ref, st_ref, kf_ref):
    @pl.when(pl.program_id(1) == 0)
    def _():
        st_ref[...] = jnp.zeros_like(st_ref)

    gain = gain_ref[...]
    row = lax.broadcasted_iota(jnp.int32, (CHUNK, CHUNK), 0)
    col = lax.broadcasted_iota(jnp.int32, (CHUNK, CHUNK), 1)
    causal = row >= col

    n_batch = q_ref.shape[0]

    def finish(bi, hd, rows, scores, qd, kl, eb_last):
        vc = slice(hd * dv_pad, (hd + 1) * dv_pad)
        v = v_ref[bi, rows, vc]
        st = st_ref[bi * heads + hd]
        out = _dot(scores.astype(BF16), v) + _dot_nt(qd, st.astype(BF16))
        st_ref[bi * heads + hd] = _dot_tn(v, kl) + st * eb_last
        ms = jnp.sum(out * out, axis=-1, keepdims=True) * (1.0 / dv)
        y = out * lax.rsqrt(ms + NORM_EPS) * gain * sg_ref[bi, rows, vc].astype(F32)
        o_ref[bi, rows, vc] = y.astype(BF16)

    def load(bi, hd, rows):
        kc = slice(hd * dk, (hd + 1) * dk)
        return (q_ref[bi, rows, kc].astype(F32), k_ref[bi, rows, kc].astype(F32), b_ref[bi, rows, kc])

    all_safe = jnp.min(bmin_ref[...]) >= SAFE_LOG_DECAY

    @pl.when(all_safe)
    def _():
        for c in range(n_chunks):
            rows = slice(c * CHUNK, (c + 1) * CHUNK)
            for bi in range(n_batch):
                for hd in range(heads):
                    q, k, b = load(bi, hd, rows)
                    eb = jnp.exp(b)
                    eb_last = eb[CHUNK - 1:CHUNK, :]
                    qd = (q * eb).astype(BF16)
                    kd = k * jnp.exp(-b)
                    scores = _dot_nt(qd, kd.astype(BF16))
                    finish(bi, hd, rows, jnp.where(causal, scores, 0.0), qd, (kd * eb_last).astype(BF16),
                           eb_last)

    @pl.when(jnp.logical_not(all_safe))
    def _():
        for bi, hd in [(bi, hd) for bi in range(n_batch) for hd in range(heads)]:
            kc = slice(hd * dk, (hd + 1) * dk)

            def chunk_body(c, carry, bi=bi, hd=hd, kc=kc):
                start = pl.multiple_of(c * CHUNK, CHUNK)
                rows = pl.ds(start, CHUNK)
                q, k, b = load(bi, hd, rows)
                kf_ref[...] = k

                def col_body(j, sc):
                    base = pl.multiple_of((j >> 3) << 3, SUBLANES)
                    pick = lax.broadcasted_iota(jnp.int32, (SUBLANES, dk), 0) == (j & (SUBLANES - 1))
                    kj = jnp.sum(jnp.where(pick, kf_ref[pl.ds(base, SUBLANES), :], 0.0), axis=0, keepdims=True)
                    b8 = b_ref[bi, pl.ds(pl.multiple_of(start + base, SUBLANES), SUBLANES), kc]
                    bj = jnp.sum(jnp.where(pick, b8, 0.0), axis=0, keepdims=True)
                    rid = lax.broadcasted_iota(jnp.int32, (CHUNK, dk), 0)
                    dec = jnp.exp(jnp.where(rid >= j, b - bj, -jnp.inf))
                    colv = jnp.sum(q * kj * dec, axis=-1, keepdims=True)
                    return jnp.where(col == j, colv, sc)

                scores = lax.fori_loop(0, CHUNK, col_body, jnp.zeros((CHUNK, CHUNK), F32))
                b_last = b[CHUNK - 1:CHUNK, :]
                finish(bi, hd, rows, scores, (q * jnp.exp(b)).astype(BF16),
                       (k * jnp.exp(b_last - b)).astype(BF16), jnp.exp(b_last))
                return carry

            lax.fori_loop(0, n_chunks, chunk_body, 0)


def _scan(q, k, b, bmin, v, sg, gain, heads, dk, dv_pad, dv, tb, nb):
    B, T, kw = q.shape
    vw = v.shape[-1]
    blk = lambda w: pl.BlockSpec((nb, tb, w), lambda bi, ti: (bi, ti, 0))
    gain_row = jnp.pad(gain, (0, dv_pad - dv)).reshape(1, dv_pad)
    return pl.pallas_call(
        functools.partial(_scan_kernel, heads, dk, dv_pad, dv, tb // CHUNK),
        out_shape=jax.ShapeDtypeStruct((B, T, vw), BF16),
        grid=(B // nb, T // tb),
        in_specs=[blk(kw), blk(kw), blk(kw),
                  pl.BlockSpec((nb, 1, SUBLANES, LANES), lambda bi, ti: (bi, ti, 0, 0)),
                  blk(vw), blk(vw), pl.BlockSpec((1, dv_pad), lambda bi, ti: (0, 0))],
        out_specs=blk(vw),
        scratch_shapes=[pltpu.VMEM((nb * heads, dv_pad, dk), F32), pltpu.VMEM((CHUNK, dk), F32)],
        compiler_params=pltpu.CompilerParams(dimension_semantics=("arbitrary", "arbitrary"),
                                             vmem_limit_bytes=VMEM_LIMIT),
        name="chunk_scan",
    )(q, k, b, bmin, v, sg, gain_row)


def _out_kernel(mix_ref, mq_ref, mk_ref, mv_ref, h_ref, wmix_ref, wmem_ref, fgain_ref,
                wr_ref, br_ref, h1_ref, un_ref, route_ref):
    tm = mix_ref.shape[0]
    mk = mk_ref[0]
    mv = mv_ref[0]
    lane_w = lax.broadcasted_iota(jnp.int32, (1, MEM_WIDTH), 1)
    mq = mq_ref[...] * (MEM_HEAD_DIM ** -0.5)
    mem_o = jnp.zeros((tm, MEM_WIDTH), F32)
    for hd in range(MEM_HEADS):
        in_head = (lane_w >= hd * MEM_HEAD_DIM) & (lane_w < (hd + 1) * MEM_HEAD_DIM)
        s = _dot_nt(jnp.where(in_head, mq, jnp.zeros_like(mq)), mk)
        e = jnp.exp(s - jnp.max(s, axis=-1, keepdims=True))
        denom = jnp.sum(e, axis=-1, keepdims=True)
        mem_o = mem_o + _dot(e.astype(BF16), jnp.where(in_head, mv, jnp.zeros_like(mv))) * (1.0 / denom)
    h1 = h_ref[...] + _dot(mix_ref[...], wmix_ref[...]) + _dot(mem_o.astype(BF16), wmem_ref[...])
    h1_ref[...] = h1
    un = _rms(h1, fgain_ref[...])
    un_ref[...] = un.astype(BF16)

    both = _dot_nt(wr_ref[...], un.astype(BF16))
    lg = both[:ROUTE_ROWS] + both[ROUTE_ROWS:] + br_ref[...]
    row = lax.broadcasted_iota(jnp.int32, (ROUTE_ROWS, tm), 0)
    neg = -jnp.inf
    gl = jnp.where(row < N_GROUPS, lg, neg)
    gmax = jnp.max(gl, axis=0, keepdims=True)
    g_idx = jnp.min(jnp.where(gl == gmax, row, ROUTE_ROWS), axis=0, keepdims=True)
    g_w = 1.0 / jnp.sum(jnp.exp(gl - gmax), axis=0, keepdims=True)
    first = N_GROUPS + g_idx * EXPERTS_PER_GROUP
    el = jnp.where((row >= first) & (row < first + EXPERTS_PER_GROUP), lg, neg)
    t1 = jnp.max(el, axis=0, keepdims=True)
    i1 = jnp.min(jnp.where(el == t1, row, ROUTE_ROWS), axis=0, keepdims=True)
    el2 = jnp.where(row == i1, neg, el)
    t2 = jnp.max(el2, axis=0, keepdims=True)
    i2 = jnp.min(jnp.where(el2 == t2, row, ROUTE_ROWS), axis=0, keepdims=True)
    e2 = jnp.exp(t2 - t1)
    w1 = g_w / (1.0 + e2)
    w2 = g_w * e2 / (1.0 + e2)
    row8 = lax.broadcasted_iota(jnp.int32, (SUBLANES, tm), 0)
    route_ref[...] = (jnp.where(row8 == i1 - first, w1, 0.0) + jnp.where(row8 == i2 - first, w2, 0.0)
                      + jnp.where(row8 == ROUTE_GROUP_ROW, g_idx.astype(F32), 0.0))


def _out_call(mix, mq, mem_k, mem_v, h, w_mix, w_mem, ffn_gain, w_group, b_group, w_router, b_router, T, tm):
    N = h.shape[0]
    tiles_per_batch = T // tm
    pad = ROUTE_ROWS - N_GROUPS - N_EXPERTS
    wr = jnp.pad(jnp.concatenate([w_group, w_router], axis=1).T, ((0, pad), (0, 0)))
    br = jnp.broadcast_to(jnp.pad(jnp.concatenate([b_group, b_router]), (0, pad))[:, None], (ROUTE_ROWS, tm))
    wr_both = jnp.concatenate(_split2(wr), axis=0)
    row = lambda width: pl.BlockSpec((tm, width), lambda i: (i, 0))
    full = lambda a: pl.BlockSpec(a.shape, lambda i: (0,) * a.ndim)
    memspec = pl.BlockSpec((1, N_MEM, MEM_WIDTH), lambda i: (i // tiles_per_batch, 0, 0))
    w_mix = w_mix.astype(BF16)
    w_mem = w_mem.astype(BF16)
    fg = ffn_gain.reshape(1, D_MODEL)
    return pl.pallas_call(
        _out_kernel,
        out_shape=(jax.ShapeDtypeStruct((N, D_MODEL), F32), jax.ShapeDtypeStruct((N, D_MODEL), BF16),
                   jax.ShapeDtypeStruct((SUBLANES, N), F32)),
        grid=(N // tm,),
        in_specs=[row(mix.shape[1]), row(MEM_WIDTH), memspec, memspec, row(D_MODEL),
                  full(w_mix), full(w_mem), full(fg), full(wr_both), full(br)],
        out_specs=(row(D_MODEL), row(D_MODEL), pl.BlockSpec((SUBLANES, tm), lambda i: (0, i))),
        compiler_params=pltpu.CompilerParams(dimension_semantics=("arbitrary",),
                                             vmem_limit_bytes=VMEM_LIMIT),
        name="out_proj_route",
    )(mix, mq, mem_k, mem_v, h, w_mix, w_mem, fg, wr_both, br)


def _moe_kernel(final, n_super, un_ref, route_ref, h1_ref, utri_ref, wup_ref, wdn_ref, fgain_ref, o_ref,
                xs_ref, gs_ref, xg_ref, gg_ref, dest_ref, meta_ref):
    s = pl.program_id(0)
    k = pl.program_id(1)
    n_tiles, slots, _ = xs_ref.shape
    tm = un_ref.shape[0]
    row8 = lax.broadcasted_iota(jnp.int32, (SUBLANES, tm), 0)

    @pl.when((s == 0) & (k == 0))
    def _():
        xg_ref[...] = jnp.zeros_like(xg_ref)
        gg_ref[...] = jnp.zeros_like(gg_ref)

    @pl.when(s > 0)
    def _():
        dest = dest_ref[k][0:1, :]
        d_hi = jnp.floor(dest * (1.0 / 32.0))
        d_lo = dest - 32.0 * d_hi
        digits = jnp.where(row8 == 0, d_hi, jnp.where(row8 == 1, d_lo, 0.0)).astype(BF16)
        pick = (lax.broadcasted_iota(jnp.int32, (SUBLANES, LANES), 0)
                == lax.broadcasted_iota(jnp.int32, (SUBLANES, LANES), 1)).astype(BF16)
        dig_c = _dot_tn(digits, pick)
        dest_col = (32.0 * dig_c[:, 0:1] + dig_c[:, 1:2]).astype(jnp.int32)
        perm_t = jnp.where(lax.broadcasted_iota(jnp.int32, (tm, slots), 1) == dest_col, 1.0, 0.0).astype(BF16)
        y = h1_ref[...] + _dot(perm_t, xs_ref[k])
        o_ref[...] = _rms(y, fgain_ref[...]) if final else y

    @pl.when(s < n_super)
    def _():
        route = route_ref[...]
        g_idx = route[ROUTE_GROUP_ROW:ROUTE_GROUP_ROW + 1, :]
        member = jnp.where((row8 < N_GROUPS) & (row8.astype(F32) == g_idx), 1.0, 0.0)
        earlier = _dot(member.astype(BF16), utri_ref[...])
        rank = jnp.sum(member * earlier, axis=0, keepdims=True)
        count = jnp.sum(member, axis=1, keepdims=True)
        padded = jnp.floor((count + (MOE_ALIGN - 1.0)) * (1.0 / MOE_ALIGN)) * MOE_ALIGN
        starts = [jnp.zeros((1, 1), F32)]
        for g in range(1, N_GROUPS):
            starts.append(starts[-1] + padded[g - 1:g, :])
        seg = jnp.zeros((SUBLANES, 1), F32)
        row81 = lax.broadcasted_iota(jnp.int32, (SUBLANES, 1), 0)
        for g in range(1, N_GROUPS):
            seg = jnp.where(row81 == g, starts[g], seg)
        dest = rank + jnp.sum(member * seg, axis=0, keepdims=True)
        dest_ref[k] = jnp.broadcast_to(dest, (SUBLANES, tm))
        perm = jnp.where(lax.broadcasted_iota(jnp.int32, (slots, tm), 0) == dest.astype(jnp.int32),
                         1.0, 0.0).astype(BF16)
        xs_ref[k] = _dot(perm, un_ref[...]).astype(BF16)
        r_hi, r_lo = _split2(route)
        gs_ref[k] = _dot_nt(perm, jnp.concatenate([r_hi, r_lo], axis=0))
        for g in range(N_GROUPS):
            meta_ref[(k * N_GROUPS + g) * 2] = starts[g][0, 0].astype(jnp.int32)
            meta_ref[(k * N_GROUPS + g) * 2 + 1] = padded[g, 0].astype(jnp.int32)

    def expert_block(g, r0, rows):
        xb = xg_ref[pl.ds(r0, rows), :]
        gsb = gg_ref[pl.ds(r0, rows), :]
        acc = jnp.zeros((rows, D_MODEL), F32)
        for j in range(EXPERTS_PER_GROUP):
            e = g * EXPERTS_PER_GROUP + j
            hh = _dot(xb, wup_ref[e])
            gate = gsb[:, j:j + 1] + gsb[:, SUBLANES + j:SUBLANES + j + 1]
            act = _silu(hh[:, :EXPERT_FF]) * hh[:, EXPERT_FF:] * gate
            acc = acc + _dot(act.astype(BF16), wdn_ref[e])
        xg_ref[pl.ds(r0, rows), :] = acc.astype(BF16)

    def copy_segments(g, gather):
        off = jnp.int32(0)
        for t in range(n_tiles):
            st = meta_ref[(t * N_GROUPS + g) * 2]
            ln = meta_ref[(t * N_GROUPS + g) * 2 + 1]

            def copy_rows(src0, dst0, rows, t=t):
                src = pl.ds(pl.multiple_of(src0, MOE_ALIGN), rows)
                dst = pl.ds(pl.multiple_of(dst0, MOE_ALIGN), rows)
                if gather:
                    xg_ref[dst, :] = xs_ref[t, src, :]
                    gg_ref[dst, :] = gs_ref[t, src, :]
                else:
                    xs_ref[t, src, :] = xg_ref[dst, :]

            def copy_body(i, carry, st=st, off=off, copy_rows=copy_rows):
                copy_rows(st + i * MOE_COPY, off + i * MOE_COPY, MOE_COPY)
                return carry

            n_copy = lax.shift_right_logical(ln, MOE_COPY_SHIFT)
            lax.fori_loop(0, n_copy, copy_body, 0)
            done = n_copy * MOE_COPY

            def tail_body(i, carry, st=st, off=off, done=done, copy_rows=copy_rows):
                copy_rows(st + done + i * MOE_ALIGN, off + done + i * MOE_ALIGN, MOE_ALIGN)
                return carry

            lax.fori_loop(0, lax.shift_right_logical(ln - done, MOE_ALIGN_SHIFT), tail_body, 0)
            off = off + ln
        return off

    @pl.when((s < n_super) & (k == n_tiles - 1))
    def _():
        for g in range(N_GROUPS):
            total = copy_segments(g, True)
            n_big = total // MOE_BIG
            rem = total - n_big * MOE_BIG
            n_big = n_big + (rem > MOE_BLOCK).astype(jnp.int32)

            def big_body(i, carry, g=g):
                expert_block(g, pl.multiple_of(i * MOE_BIG, MOE_BIG), MOE_BIG)
                return carry

            lax.fori_loop(0, n_big, big_body, 0)

            @pl.when((rem > 0) & (rem <= MOE_BLOCK))
            def _(g=g, n_big=n_big):
                expert_block(g, pl.multiple_of(n_big * MOE_BIG, MOE_BIG), MOE_BLOCK)

            copy_segments(g, False)


def _moe(un, route, h1, w_up, w_down, layer, final_gain, final, tm):
    N = un.shape[0]
    n_tiles = min(MOE_SUPER, N // tm)
    n_super = N // (tm * n_tiles)
    slots = tm + N_GROUPS * MOE_ALIGN
    group_rows = n_tiles * slots + MOE_BIG
    utri = (jnp.arange(tm)[:, None] < jnp.arange(tm)[None, :]).astype(BF16)
    in_tile = lambda s, k: jnp.minimum(s, n_super - 1) * n_tiles + k
    out_tile = lambda s, k: jnp.maximum(s - 1, 0) * n_tiles + jnp.where(s > 0, k, 0)
    once = lambda a: pl.BlockSpec(a.shape, lambda s, k: (0,) * a.ndim, pipeline_mode=pl.Buffered(1))
    of_layer = lambda a: pl.BlockSpec((None,) + a.shape[1:], lambda s, k: (layer,) + (0,) * (a.ndim - 1),
                                      pipeline_mode=pl.Buffered(1))
    fg = final_gain.reshape(1, D_MODEL)
    return pl.pallas_call(
        functools.partial(_moe_kernel, final, n_super),
        out_shape=jax.ShapeDtypeStruct((N, D_MODEL), F32),
        grid=(n_super + 1, n_tiles),
        in_specs=[pl.BlockSpec((tm, D_MODEL), lambda s, k: (in_tile(s, k), 0)),
                  pl.BlockSpec((SUBLANES, tm), lambda s, k: (0, in_tile(s, k))),
                  pl.BlockSpec((tm, D_MODEL), lambda s, k: (out_tile(s, k), 0)),
                  once(utri), of_layer(w_up), of_layer(w_down), once(fg)],
        out_specs=pl.BlockSpec((tm, D_MODEL), lambda s, k: (out_tile(s, k), 0)),
        scratch_shapes=[pltpu.VMEM((n_tiles, slots, D_MODEL), BF16),
                        pltpu.VMEM((n_tiles, slots, 2 * SUBLANES), F32),
                        pltpu.VMEM((group_rows, D_MODEL), BF16),
                        pltpu.VMEM((group_rows, 2 * SUBLANES), F32),
                        pltpu.VMEM((n_tiles, SUBLANES, tm), F32),
                        pltpu.SMEM((n_tiles * N_GROUPS * 2,), jnp.int32)],
        compiler_params=pltpu.CompilerParams(dimension_semantics=("arbitrary", "arbitrary"),
                                             vmem_limit_bytes=VMEM_LIMIT),
        name="moe",
    )(un, route, h1, utri, w_up, w_down, fg)


def kernel(x, mem, mix_norm, ffn_norm, mem_norm, final_norm, gla_w_in, gla_w_gate_up, gla_b_gate, gla_out_norm,
           hg_w_in, hg_lower_bounds, hg_out_norm, w_mem_kv, w_out, w_group, b_group, w_router, b_router,
           w_up, w_down):
    B, T, _ = x.shape
    N = B * T
    depth = mix_norm.shape[0]
    tm_proj = min(512, T)
    nb_scan = 1
    tb_scan = tm_proj
    tm_out = min(1024, T)
    tm_moe = min(512, N)
    h = x.reshape(N, D_MODEL)
    w_up_bf16 = w_up.astype(BF16)
    w_down_bf16 = w_down.astype(BF16)
    for layer in range(depth):
        j = layer // 2
        if layer % 2 == 0:
            q, k, v, sg, b, mq, bmin = _gla_proj(h, mix_norm[layer], gla_w_in[j], gla_w_gate_up[j], gla_b_gate[j],
                                                 tm_proj)
            heads, dk, dv_pad, dv, out_gain = GLA_HEADS, GLA_DK, GLA_DV_PAD, GLA_DV, gla_out_norm[j]
        else:
            q, k, v, sg, b, mq, bmin = _hg_proj(h, mix_norm[layer], hg_w_in[j], hg_lower_bounds, layer, tm_proj)
            heads, dk, dv_pad, dv, out_gain = HG_HEADS, HG_DK, HG_DV, HG_DV, hg_out_norm[j]
        r3 = lambda a: a.reshape(B, T, a.shape[-1])
        bmin = bmin.reshape(B, T // tb_scan, SUBLANES, LANES)
        mix = _scan(r3(q), r3(k), r3(b), bmin, r3(v), r3(sg), out_gain, heads, dk, dv_pad, dv, tb_scan, nb_scan)
        mix = mix.reshape(N, heads * dv_pad)
        mem_k, mem_v = _mem_kv(mem, mem_norm[layer], w_mem_kv[layer])
        w_mix = _pad_heads(w_out[layer, :MIX_WIDTH].T, heads, dv, dv_pad).T
        h1, un, gates = _out_call(mix, mq, mem_k, mem_v, h, w_mix, w_out[layer, MIX_WIDTH:], ffn_norm[layer],
                                  w_group[layer], b_group[layer], w_router[layer], b_router[layer], T, tm_out)
        h = _moe(un, gates, h1, w_up_bf16, w_down_bf16, layer, final_norm, layer == depth - 1, tm_moe)
    return h.reshape(B, T, D_MODEL)
```

```python
import functools

import jax
import jax.numpy as jnp
from jax import lax
from jax.experimental import pallas as pl
from jax.experimental.pallas import tpu as pltpu

D_MODEL = 1024
N_MEM = 256
MIX_WIDTH = 768
MEM_HEADS = 4
MEM_HEAD_DIM = 64
MEM_WIDTH = 256
CHUNK = 64
GLA_HEADS = 4
GLA_KEY_WIDTH = 512
GLA_DK = 128
GLA_DV = 192
GLA_DV_PAD = 256
GLA_GATE_RANK = 16
GLA_GATE_NORMALIZER = 16.0
HG_HEADS = 6
HG_DK = 128
HG_DV = 128
N_GROUPS = 4
EXPERTS_PER_GROUP = 4
N_EXPERTS = 16
EXPERT_FF = 256
NORM_EPS = 1e-6
LANES = 128
SUBLANES = 8
MOE_BLOCK = 128
CUMSUM_SPAN = 256
MOE_ALIGN_SHIFT = 4
MOE_ALIGN = 1 << MOE_ALIGN_SHIFT
MOE_COPY_SHIFT = 6
MOE_COPY = 1 << MOE_COPY_SHIFT
MOE_BIG = 256 + 2 * MOE_ALIGN
MOE_SUPER = 4
ROUTE_GROUP_ROW = 4
ROUTE_ROWS = 32
SAFE_LOG_DECAY = -60.0
VMEM_LIMIT = 56 * 1024 * 1024

F32 = jnp.float32
BF16 = jnp.bfloat16


def _dot(a, b):
    return jnp.dot(a, b, preferred_element_type=F32)


def _dot_nt(a, b):
    return lax.dot_general(a, b, (((1,), (1,)), ((), ())), preferred_element_type=F32)


def _dot_tn(a, b):
    return lax.dot_general(a, b, (((0,), (0,)), ((), ())), preferred_element_type=F32)


def _split2(x):
    hi = x.astype(BF16)
    lo = (x - hi.astype(F32)).astype(BF16)
    return hi, lo


def _rms(x, gain):
    ms = jnp.mean(x * x, axis=-1, keepdims=True)
    return x * lax.rsqrt(ms + NORM_EPS) * gain


def _log_sigmoid(x):
    return jnp.minimum(x, 0.0) - jnp.log1p(jnp.exp(-jnp.abs(x)))


def _sigmoid(x):
    return 1.0 / (1.0 + jnp.exp(-x))


def _silu(x):
    return x * _sigmoid(x)


def _chunk_cumsum(tri_ref, x):
    tri = tri_ref[...]
    span = tri.shape[0]
    parts = []
    for r in range(x.shape[0] // span):
        hi, lo = _split2(x[r * span:(r + 1) * span])
        parts.append(_dot(tri, hi) + _dot(tri, lo))
    return jnp.concatenate(parts, axis=0) if len(parts) > 1 else parts[0]


def _mem_kv_kernel(mem_ref, gain_ref, w_ref, k_ref, v_ref):
    m = _rms(mem_ref[0], gain_ref[...]).astype(BF16)
    kv = _dot(m, w_ref[...])
    k_ref[0] = kv[:, :MEM_WIDTH].astype(BF16)
    v_ref[0] = kv[:, MEM_WIDTH:].astype(BF16)


def _mem_kv(mem, gain, w_kv):
    B = mem.shape[0]
    return pl.pallas_call(
        _mem_kv_kernel,
        out_shape=(jax.ShapeDtypeStruct((B, N_MEM, MEM_WIDTH), BF16),
                   jax.ShapeDtypeStruct((B, N_MEM, MEM_WIDTH), BF16)),
        grid=(B,),
        in_specs=[pl.BlockSpec((1, N_MEM, D_MODEL), lambda b: (b, 0, 0)),
                  pl.BlockSpec((1, D_MODEL), lambda b: (0, 0)),
                  pl.BlockSpec((D_MODEL, 2 * MEM_WIDTH), lambda b: (0, 0))],
        out_specs=(pl.BlockSpec((1, N_MEM, MEM_WIDTH), lambda b: (b, 0, 0)),
                   pl.BlockSpec((1, N_MEM, MEM_WIDTH), lambda b: (b, 0, 0))),
        compiler_params=pltpu.CompilerParams(dimension_semantics=("arbitrary",)),
        name="mem_kv",
    )(mem, gain.reshape(1, D_MODEL), w_kv.astype(BF16))


_GQ, _GK = 0, GLA_KEY_WIDTH
_GV = 2 * GLA_KEY_WIDTH
_GG = _GV + MIX_WIDTH
_GR = _GG + MIX_WIDTH
_GM = _GR + LANES
_GW = _GM + MEM_WIDTH


def _store_padded_heads(ref, x):
    zeros = jnp.zeros((x.shape[0], GLA_DV_PAD - GLA_DV), ref.dtype)
    for hd in range(GLA_HEADS):
        ref[:, hd * GLA_DV_PAD:hd * GLA_DV_PAD + GLA_DV] = x[:, hd * GLA_DV:(hd + 1) * GLA_DV].astype(ref.dtype)
        ref[:, hd * GLA_DV_PAD + GLA_DV:(hd + 1) * GLA_DV_PAD] = zeros


def _store_decay(b_ref, bmin_ref, b):
    b_ref[...] = b
    lowest = jnp.min(jnp.min(b, axis=0, keepdims=True), axis=1, keepdims=True)
    bmin_ref[0] = jnp.broadcast_to(lowest, (SUBLANES, LANES))


def _gla_proj_kernel(h_ref, gain_ref, w_ref, wg_ref, bg_ref, tri_ref,
                     q_ref, k_ref, v_ref, sg_ref, b_ref, mq_ref, bmin_ref):
    u = _rms(h_ref[...], gain_ref[...]).astype(BF16)
    q_ref[...] = (_dot(u, w_ref[:, _GQ:_GK]) * (GLA_DK ** -0.5)).astype(BF16)
    k_ref[...] = _dot(u, w_ref[:, _GK:_GV]).astype(BF16)
    _store_padded_heads(v_ref, _dot(u, w_ref[:, _GV:_GG]))
    _store_padded_heads(sg_ref, _silu(_dot(u, w_ref[:, _GG:_GR])))
    mq_ref[...] = _dot(u, w_ref[:, _GM:_GW]).astype(BF16)
    r = _dot(u, w_ref[:, _GR:_GM])
    r_hi = r.astype(BF16)
    lane = lax.broadcasted_iota(jnp.int32, r.shape, 1)
    is_lo = (lane >= GLA_GATE_RANK) & (lane < 2 * GLA_GATE_RANK)
    r_parts = jnp.where(is_lo, r - r_hi.astype(F32), r_hi.astype(F32)).astype(BF16)
    logit = _dot(r_parts, wg_ref[...]) + bg_ref[...]
    _store_decay(b_ref, bmin_ref, _chunk_cumsum(tri_ref, _log_sigmoid(logit) * (1.0 / GLA_GATE_NORMALIZER)))


_HQ, _HF, _HI, _HGG, _HM = 0, MIX_WIDTH, 2 * MIX_WIDTH, 3 * MIX_WIDTH, 4 * MIX_WIDTH
_HW = _HM + MEM_WIDTH


def _hg_proj_kernel(layer, h_ref, gain_ref, w_ref, lbp_ref, tri_ref,
                    q_ref, k_ref, v_ref, sg_ref, b_ref, mq_ref, bmin_ref):
    u = _rms(h_ref[...], gain_ref[...]).astype(BF16)
    p = lbp_ref[...]
    p = jnp.exp(p - jnp.max(p, axis=0, keepdims=True))
    p = p / jnp.sum(p, axis=0, keepdims=True)
    lb = jnp.sum(p[0:layer + 1], axis=0, keepdims=True) - p[0:1]
    q_ref[...] = _silu(_dot(u, w_ref[:, _HQ:_HF])).astype(BF16)
    z = _dot(u, w_ref[:, _HF:_HI])
    k_ref[...] = ((1.0 - lb) * _sigmoid(-z)).astype(BF16)
    v_ref[...] = _dot(u, w_ref[:, _HI:_HGG]).astype(BF16)
    sg_ref[...] = _silu(_dot(u, w_ref[:, _HGG:_HM])).astype(BF16)
    mq_ref[...] = _dot(u, w_ref[:, _HM:_HW]).astype(BF16)
    a = jnp.log(lb)
    c = jnp.log1p(-lb) + _log_sigmoid(z)
    log_f = jnp.maximum(a, c) + jnp.log1p(jnp.exp(-jnp.abs(a - c)))
    _store_decay(b_ref, bmin_ref, _chunk_cumsum(tri_ref, log_f))


def _proj_call(kernel, h, gain, w, extra, kw, vw, tm, name):
    N = h.shape[0]
    span = min(CUMSUM_SPAN, tm)
    tri = (jnp.arange(span)[:, None] >= jnp.arange(span)[None, :]) & (
        jnp.arange(span)[:, None] // CHUNK == jnp.arange(span)[None, :] // CHUNK)
    tri = tri.astype(BF16)
    row = lambda width: pl.BlockSpec((tm, width), lambda i: (i, 0))
    full = lambda a: pl.BlockSpec(a.shape, lambda i: (0,) * a.ndim)
    ins = [h, gain.reshape(1, D_MODEL), w] + list(extra) + [tri]
    return pl.pallas_call(
        kernel,
        out_shape=(jax.ShapeDtypeStruct((N, kw), BF16), jax.ShapeDtypeStruct((N, kw), BF16),
                   jax.ShapeDtypeStruct((N, vw), BF16), jax.ShapeDtypeStruct((N, vw), BF16),
                   jax.ShapeDtypeStruct((N, kw), F32), jax.ShapeDtypeStruct((N, MEM_WIDTH), BF16),
                   jax.ShapeDtypeStruct((N // tm, SUBLANES, LANES), F32)),
        grid=(N // tm,),
        in_specs=[row(D_MODEL)] + [full(a) for a in ins[1:]],
        out_specs=(row(kw), row(kw), row(vw), row(vw), row(kw), row(MEM_WIDTH),
                   pl.BlockSpec((1, SUBLANES, LANES), lambda i: (i, 0, 0))),
        compiler_params=pltpu.CompilerParams(dimension_semantics=("arbitrary",),
                                             vmem_limit_bytes=VMEM_LIMIT),
        name=name,
    )(*ins)


def _pad_heads(w, heads, dv, dv_pad):
    lead = w.shape[:-1]
    w = w.reshape(lead + (heads, dv))
    w = jnp.pad(w, [(0, 0)] * len(lead) + [(0, 0), (0, dv_pad - dv)])
    return w.reshape(lead + (heads * dv_pad,))


def _gla_proj(h, gain, w_in, w_gate_up, b_gate, tm):
    q, k, v, g, r, mq = jnp.split(
        w_in, [GLA_KEY_WIDTH, 2 * GLA_KEY_WIDTH, 2 * GLA_KEY_WIDTH + MIX_WIDTH,
               2 * GLA_KEY_WIDTH + 2 * MIX_WIDTH, 2 * GLA_KEY_WIDTH + 2 * MIX_WIDTH + GLA_GATE_RANK], axis=1)
    r3 = jnp.pad(jnp.concatenate([r, r, r], axis=1), ((0, 0), (0, LANES - 3 * GLA_GATE_RANK)))
    w = jnp.concatenate([q, k, v, g, r3, mq], axis=1).astype(BF16)
    wg_hi, wg_lo = _split2(w_gate_up)
    wg = jnp.pad(jnp.concatenate([wg_hi, wg_hi, wg_lo], axis=0), ((0, LANES - 3 * GLA_GATE_RANK), (0, 0)))
    return _proj_call(_gla_proj_kernel, h, gain, w, [wg, b_gate.reshape(1, GLA_KEY_WIDTH)],
                      GLA_KEY_WIDTH, GLA_HEADS * GLA_DV_PAD, tm, "gla_proj")


def _hg_proj(h, gain, w_in, lower_bound_params, layer, tm):
    return _proj_call(functools.partial(_hg_proj_kernel, layer), h, gain, w_in.astype(BF16),
                      [lower_bound_params], MIX_WIDTH, MIX_WIDTH, tm, "hg_proj")


def _scan_kernel(heads, dk, dv_pad, dv, n_chunks,
                 q_ref, k_ref, b_ref, bmin_ref, v_ref, sg_ref, gain_ref, o_ref, st_ref, kf_ref):
    @pl.when(pl.program_id(1) == 0)
    def _():
        st_ref[...] = jnp.zeros_like(st_ref)

    gain = gain_ref[...]
    row = lax.broadcasted_iota(jnp.int32, (CHUNK, CHUNK), 0)
    col = lax.broadcasted_iota(jnp.int32, (CHUNK, CHUNK), 1)
    causal = row >= col

    n_batch = q_ref.shape[0]

    def finish(bi, hd, rows, scores, qd, kl, eb_last):
        vc = slice(hd * dv_pad, (hd + 1) * dv_pad)
        v = v_ref[bi, rows, vc]
        st = st_ref[bi * heads + hd]
        out = _dot(scores.astype(BF16), v) + _dot_nt(qd, st.astype(BF16))
        st_ref[bi * heads + hd] = _dot_tn(v, kl) + st * eb_last
        ms = jnp.sum(out * out, axis=-1, keepdims=True) * (1.0 / dv)
        y = out * lax.rsqrt(ms + NORM_EPS) * gain * sg_ref[bi, rows, vc].astype(F32)
        o_ref[bi, rows, vc] = y.astype(BF16)

    def load(bi, hd, rows):
        kc = slice(hd * dk, (hd + 1) * dk)
        return (q_ref[bi, rows, kc].astype(F32), k_ref[bi, rows, kc].astype(F32), b_ref[bi, rows, kc])

    all_safe = jnp.min(bmin_ref[...]) >= SAFE_LOG_DECAY

    @pl.when(all_safe)
    def _():
        for c in range(n_chunks):
            rows = slice(c * CHUNK, (c + 1) * CHUNK)
            for bi in range(n_batch):
                for hd in range(heads):
                    q, k, b = load(bi, hd, rows)
                    eb = jnp.exp(b)
                    eb_last = eb[CHUNK - 1:CHUNK, :]
                    qd = (q * eb).astype(BF16)
                    kd = k * jnp.exp(-b)
                    scores = _dot_nt(qd, kd.astype(BF16))
                    finish(bi, hd, rows, jnp.where(causal, scores, 0.0), qd, (kd * eb_last).astype(BF16),
                           eb_last)

    @pl.when(jnp.logical_not(all_safe))
    def _():
        for bi, hd in [(bi, hd) for bi in range(n_batch) for hd in range(heads)]:
            kc = slice(hd * dk, (hd + 1) * dk)

            def chunk_body(c, carry, bi=bi, hd=hd, kc=kc):
                start = pl.multiple_of(c * CHUNK, CHUNK)
                rows = pl.ds(start, CHUNK)
                q, k, b = load(bi, hd, rows)
                kf_ref[...] = k

                def col_body(j, sc):
                    base = pl.multiple_of((j >> 3) << 3, SUBLANES)
                    pick = lax.broadcasted_iota(jnp.int32, (SUBLANES, dk), 0) == (j & (SUBLANES - 1))
                    kj = jnp.sum(jnp.where(pick, kf_ref[pl.ds(base, SUBLANES), :], 0.0), axis=0, keepdims=True)
                    b8 = b_ref[bi, pl.ds(pl.multiple_of(start + base, SUBLANES), SUBLANES), kc]
                    bj = jnp.sum(jnp.where(pick, b8, 0.0), axis=0, keepdims=True)
                    rid = lax.broadcasted_iota(jnp.int32, (CHUNK, dk), 0)
                    dec = jnp.exp(jnp.where(rid >= j, b - bj, -jnp.inf))
                    colv = jnp.sum(q * kj * dec, axis=-1, keepdims=True)
                    return jnp.where(col == j, colv, sc)

                scores = lax.fori_loop(0, CHUNK, col_body, jnp.zeros((CHUNK, CHUNK), F32))
                b_last = b[CHUNK - 1:CHUNK, :]
                finish(bi, hd, rows, scores, (q * jnp.exp(b)).astype(BF16),
                       (k * jnp.exp(b_last - b)).astype(BF16), jnp.exp(b_last))
                return carry

            lax.fori_loop(0, n_chunks, chunk_body, 0)


def _scan(q, k, b, bmin, v, sg, gain, heads, dk, dv_pad, dv, tb, nb):
    B, T, kw = q.shape
    vw = v.shape[-1]
    blk = lambda w: pl.BlockSpec((nb, tb, w), lambda bi, ti: (bi, ti, 0))
    gain_row = jnp.pad(gain, (0, dv_pad - dv)).reshape(1, dv_pad)
    return pl.pallas_call(
        functools.partial(_scan_kernel, heads, dk, dv_pad, dv, tb // CHUNK),
        out_shape=jax.ShapeDtypeStruct((B, T, vw), BF16),
        grid=(B // nb, T // tb),
        in_specs=[blk(kw), blk(kw), blk(kw),
                  pl.BlockSpec((nb, 1, SUBLANES, LANES), lambda bi, ti: (bi, ti, 0, 0)),
                  blk(vw), blk(vw), pl.BlockSpec((1, dv_pad), lambda bi, ti: (0, 0))],
        out_specs=blk(vw),
        scratch_shapes=[pltpu.VMEM((nb * heads, dv_pad, dk), F32), pltpu.VMEM((CHUNK, dk), F32)],
        compiler_params=pltpu.CompilerParams(dimension_semantics=("arbitrary", "arbitrary"),
                                             vmem_limit_bytes=VMEM_LIMIT),
        name="chunk_scan",
    )(q, k, b, bmin, v, sg, gain_row)


def _out_kernel(mix_ref, mq_ref, mk_ref, mv_ref, h_ref, wmix_ref, wmem_ref, fgain_ref,
                wr_ref, br_ref, h1_ref, un_ref, route_ref):
    tm = mix_ref.shape[0]
    mk = mk_ref[0]
    mv = mv_ref[0]
    lane_w = lax.broadcasted_iota(jnp.int32, (1, MEM_WIDTH), 1)
    mq = mq_ref[...] * (MEM_HEAD_DIM ** -0.5)
    mem_o = jnp.zeros((tm, MEM_WIDTH), F32)
    for hd in range(MEM_HEADS):
        in_head = (lane_w >= hd * MEM_HEAD_DIM) & (lane_w < (hd + 1) * MEM_HEAD_DIM)
        s = _dot_nt(jnp.where(in_head, mq, jnp.zeros_like(mq)), mk)
        e = jnp.exp(s - jnp.max(s, axis=-1, keepdims=True))
        denom = jnp.sum(e, axis=-1, keepdims=True)
        mem_o = mem_o + _dot(e.astype(BF16), jnp.where(in_head, mv, jnp.zeros_like(mv))) * (1.0 / denom)
    h1 = h_ref[...] + _dot(mix_ref[...], wmix_ref[...]) + _dot(mem_o.astype(BF16), wmem_ref[...])
    h1_ref[...] = h1
    un = _rms(h1, fgain_ref[...])
    un_ref[...] = un.astype(BF16)

    both = _dot_nt(wr_ref[...], un.astype(BF16))
    lg = both[:ROUTE_ROWS] + both[ROUTE_ROWS:] + br_ref[...]
    row = lax.broadcasted_iota(jnp.int32, (ROUTE_ROWS, tm), 0)
    neg = -jnp.inf
    gl = jnp.where(row < N_GROUPS, lg, neg)
    gmax = jnp.max(gl, axis=0, keepdims=True)
    g_idx = jnp.min(jnp.where(gl == gmax, row, ROUTE_ROWS), axis=0, keepdims=True)
    g_w = 1.0 / jnp.sum(jnp.exp(gl - gmax), axis=0, keepdims=True)
    first = N_GROUPS + g_idx * EXPERTS_PER_GROUP
    el = jnp.where((row >= first) & (row < first + EXPERTS_PER_GROUP), lg, neg)
    t1 = jnp.max(el, axis=0, keepdims=True)
    i1 = jnp.min(jnp.where(el == t1, row, ROUTE_ROWS), axis=0, keepdims=True)
    el2 = jnp.where(row == i1, neg, el)
    t2 = jnp.max(el2, axis=0, keepdims=True)
    i2 = jnp.min(jnp.where(el2 == t2, row, ROUTE_ROWS), axis=0, keepdims=True)
    e2 = jnp.exp(t2 - t1)
    w1 = g_w / (1.0 + e2)
    w2 = g_w * e2 / (1.0 + e2)
    row8 = lax.broadcasted_iota(jnp.int32, (SUBLANES, tm), 0)
    route_ref[...] = (jnp.where(row8 == i1 - first, w1, 0.0) + jnp.where(row8 == i2 - first, w2, 0.0)
                      + jnp.where(row8 == ROUTE_GROUP_ROW, g_idx.astype(F32), 0.0))


def _out_call(mix, mq, mem_k, mem_v, h, w_mix, w_mem, ffn_gain, w_group, b_group, w_router, b_router, T, tm):
    N = h.shape[0]
    tiles_per_batch = T // tm
    pad = ROUTE_ROWS - N_GROUPS - N_EXPERTS
    wr = jnp.pad(jnp.concatenate([w_group, w_router], axis=1).T, ((0, pad), (0, 0)))
    br = jnp.broadcast_to(jnp.pad(jnp.concatenate([b_group, b_router]), (0, pad))[:, None], (ROUTE_ROWS, tm))
    wr_both = jnp.concatenate(_split2(wr), axis=0)
    row = lambda width: pl.BlockSpec((tm, width), lambda i: (i, 0))
    full = lambda a: pl.BlockSpec(a.shape, lambda i: (0,) * a.ndim)
    memspec = pl.BlockSpec((1, N_MEM, MEM_WIDTH), lambda i: (i // tiles_per_batch, 0, 0))
    w_mix = w_mix.astype(BF16)
    w_mem = w_mem.astype(BF16)
    fg = ffn_gain.reshape(1, D_MODEL)
    return pl.pallas_call(
        _out_kernel,
        out_shape=(jax.ShapeDtypeStruct((N, D_MODEL), F32), jax.ShapeDtypeStruct((N, D_MODEL), BF16),
                   jax.ShapeDtypeStruct((SUBLANES, N), F32)),
        grid=(N // tm,),
        in_specs=[row(mix.shape[1]), row(MEM_WIDTH), memspec, memspec, row(D_MODEL),
                  full(w_mix), full(w_mem), full(fg), full(wr_both), full(br)],
        out_specs=(row(D_MODEL), row(D_MODEL), pl.BlockSpec((SUBLANES, tm), lambda i: (0, i))),
        compiler_params=pltpu.CompilerParams(dimension_semantics=("arbitrary",),
                                             vmem_limit_bytes=VMEM_LIMIT),
        name="out_proj_route",
    )(mix, mq, mem_k, mem_v, h, w_mix, w_mem, fg, wr_both, br)


def _moe_kernel(final, n_super, un_ref, route_ref, h1_ref, utri_ref, wup_ref, wdn_ref, fgain_ref, o_ref,
                xs_ref, gs_ref, xg_ref, gg_ref, permt_ref, meta_ref):
    s = pl.program_id(0)
    k = pl.program_id(1)
    n_tiles, slots, _ = xs_ref.shape
    tm = un_ref.shape[0]
    row8 = lax.broadcasted_iota(jnp.int32, (SUBLANES, tm), 0)

    @pl.when((s == 0) & (k == 0))
    def _():
        xg_ref[...] = jnp.zeros_like(xg_ref)
        gg_ref[...] = jnp.zeros_like(gg_ref)

    def unpartition():
        y = h1_ref[...] + _dot(permt_ref[k], xs_ref[k])
        o_ref[...] = _rms(y, fgain_ref[...]) if final else y

    def partition():
        route = route_ref[...]
        g_idx = route[ROUTE_GROUP_ROW:ROUTE_GROUP_ROW + 1, :]
        member = jnp.where((row8 < N_GROUPS) & (row8.astype(F32) == g_idx), 1.0, 0.0)
        earlier = _dot(member.astype(BF16), utri_ref[...])
        rank = jnp.sum(member * earlier, axis=0, keepdims=True)
        count = jnp.sum(member, axis=1, keepdims=True)
        padded = jnp.floor((count + (MOE_ALIGN - 1.0)) * (1.0 / MOE_ALIGN)) * MOE_ALIGN
        starts = [jnp.zeros((1, 1), F32)]
        for g in range(1, N_GROUPS):
            starts.append(starts[-1] + padded[g - 1:g, :])
        seg = jnp.zeros((SUBLANES, 1), F32)
        row81 = lax.broadcasted_iota(jnp.int32, (SUBLANES, 1), 0)
        for g in range(1, N_GROUPS):
            seg = jnp.where(row81 == g, starts[g], seg)
        dest = rank + jnp.sum(member * seg, axis=0, keepdims=True)
        perm = jnp.where(lax.broadcasted_iota(jnp.int32, (slots, tm), 0) == dest.astype(jnp.int32),
                         1.0, 0.0).astype(BF16)
        d_hi = jnp.floor(dest * (1.0 / 32.0))
        d_lo = dest - 32.0 * d_hi
        digits = jnp.where(row8 == 0, d_hi, jnp.where(row8 == 1, d_lo, 0.0)).astype(BF16)
        pick = (lax.broadcasted_iota(jnp.int32, (SUBLANES, LANES), 0)
                == lax.broadcasted_iota(jnp.int32, (SUBLANES, LANES), 1)).astype(BF16)
        dig_c = _dot_tn(digits, pick)
        dest_col = (32.0 * dig_c[:, 0:1] + dig_c[:, 1:2]).astype(jnp.int32)
        permt_ref[k] = jnp.where(lax.broadcasted_iota(jnp.int32, (tm, slots), 1) == dest_col,
                                 1.0, 0.0).astype(BF16)
        xs_ref[k] = _dot(perm, un_ref[...]).astype(BF16)
        r_hi, r_lo = _split2(route)
        gs_ref[k] = _dot_nt(perm, jnp.concatenate([r_hi, r_lo], axis=0))
        for g in range(N_GROUPS):
            meta_ref[(k * N_GROUPS + g) * 2] = starts[g][0, 0].astype(jnp.int32)
            meta_ref[(k * N_GROUPS + g) * 2 + 1] = padded[g, 0].astype(jnp.int32)

    pl.when(s > 0)(unpartition)
    pl.when(s < n_super)(partition)

    def expert_block(g, r0, rows):
        xb = xg_ref[pl.ds(r0, rows), :]
        gsb = gg_ref[pl.ds(r0, rows), :]
        acc = jnp.zeros((rows, D_MODEL), F32)
        for j in range(EXPERTS_PER_GROUP):
            e = g * EXPERTS_PER_GROUP + j
            hh = _dot(xb, wup_ref[e])
            gate = gsb[:, j:j + 1] + gsb[:, SUBLANES + j:SUBLANES + j + 1]
            act = _silu(hh[:, :EXPERT_FF]) * hh[:, EXPERT_FF:] * gate
            acc = acc + _dot(act.astype(BF16), wdn_ref[e])
        xg_ref[pl.ds(r0, rows), :] = acc.astype(BF16)

    def copy_segments(g, gather):
        off = jnp.int32(0)
        for t in range(n_tiles):
            st = meta_ref[(t * N_GROUPS + g) * 2]
            ln = meta_ref[(t * N_GROUPS + g) * 2 + 1]

            def copy_rows(src0, dst0, rows, t=t):
                src = pl.ds(pl.multiple_of(src0, MOE_ALIGN), rows)
                dst = pl.ds(pl.multiple_of(dst0, MOE_ALIGN), rows)
                if gather:
                    xg_ref[dst, :] = xs_ref[t, src, :]
                    gg_ref[dst, :] = gs_ref[t, src, :]
                else:
                    xs_ref[t, src, :] = xg_ref[dst, :]

            def copy_body(i, carry, st=st, off=off, copy_rows=copy_rows):
                copy_rows(st + i * MOE_COPY, off + i * MOE_COPY, MOE_COPY)
                return carry

            n_copy = lax.shift_right_logical(ln, MOE_COPY_SHIFT)
            lax.fori_loop(0, n_copy, copy_body, 0)
            done = n_copy * MOE_COPY

            def tail_body(i, carry, st=st, off=off, done=done, copy_rows=copy_rows):
                copy_rows(st + done + i * MOE_ALIGN, off + done + i * MOE_ALIGN, MOE_ALIGN)
                return carry

            lax.fori_loop(0, lax.shift_right_logical(ln - done, MOE_ALIGN_SHIFT), tail_body, 0)
            off = off + ln
        return off

    @pl.when((s < n_super) & (k == n_tiles - 1))
    def _():
        for g in range(N_GROUPS):
            total = copy_segments(g, True)
            n_big = total // MOE_BIG
            rem = total - n_big * MOE_BIG
            n_big = n_big + (rem > MOE_BLOCK).astype(jnp.int32)

            def big_body(i, carry, g=g):
                expert_block(g, pl.multiple_of(i * MOE_BIG, MOE_BIG), MOE_BIG)
                return carry

            lax.fori_loop(0, n_big, big_body, 0)

            @pl.when((rem > 0) & (rem <= MOE_BLOCK))
            def _(g=g, n_big=n_big):
                expert_block(g, pl.multiple_of(n_big * MOE_BIG, MOE_BIG), MOE_BLOCK)

            copy_segments(g, False)


def _moe(un, route, h1, w_up, w_down, layer, final_gain, final, tm):
    N = un.shape[0]
    n_tiles = min(MOE_SUPER, N // tm)
    n_super = N // (tm * n_tiles)
    slots = tm + N_GROUPS * MOE_ALIGN
    group_rows = n_tiles * slots + MOE_BIG
    utri = (jnp.arange(tm)[:, None] < jnp.arange(tm)[None, :]).astype(BF16)
    in_tile = lambda s, k: jnp.minimum(s, n_super - 1) * n_tiles + k
    out_tile = lambda s, k: jnp.maximum(s - 1, 0) * n_tiles + jnp.where(s > 0, k, 0)
    once = lambda a: pl.BlockSpec(a.shape, lambda s, k: (0,) * a.ndim, pipeline_mode=pl.Buffered(1))
    of_layer = lambda a: pl.BlockSpec((None,) + a.shape[1:], lambda s, k: (layer,) + (0,) * (a.ndim - 1),
                                      pipeline_mode=pl.Buffered(1))
    fg = final_gain.reshape(1, D_MODEL)
    return pl.pallas_call(
        functools.partial(_moe_kernel, final, n_super),
        out_shape=jax.ShapeDtypeStruct((N, D_MODEL), F32),
        grid=(n_super + 1, n_tiles),
        in_specs=[pl.BlockSpec((tm, D_MODEL), lambda s, k: (in_tile(s, k), 0)),
                  pl.BlockSpec((SUBLANES, tm), lambda s, k: (0, in_tile(s, k))),
                  pl.BlockSpec((tm, D_MODEL), lambda s, k: (out_tile(s, k), 0)),
                  once(utri), of_layer(w_up), of_layer(w_down), once(fg)],
        out_specs=pl.BlockSpec((tm, D_MODEL), lambda s, k: (out_tile(s, k), 0)),
        scratch_shapes=[pltpu.VMEM((n_tiles, slots, D_MODEL), BF16),
                        pltpu.VMEM((n_tiles, slots, 2 * SUBLANES), F32),
                        pltpu.VMEM((group_rows, D_MODEL), BF16),
                        pltpu.VMEM((group_rows, 2 * SUBLANES), F32),
                        pltpu.VMEM((n_tiles, tm, slots), BF16),
                        pltpu.SMEM((n_tiles * N_GROUPS * 2,), jnp.int32)],
        compiler_params=pltpu.CompilerParams(dimension_semantics=("arbitrary", "arbitrary"),
                                             vmem_limit_bytes=VMEM_LIMIT),
        name="moe",
    )(un, route, h1, utri, w_up, w_down, fg)


def kernel(x, mem, mix_norm, ffn_norm, mem_norm, final_norm, gla_w_in, gla_w_gate_up, gla_b_gate, gla_out_norm,
           hg_w_in, hg_lower_bounds, hg_out_norm, w_mem_kv, w_out, w_group, b_group, w_router, b_router,
           w_up, w_down):
    B, T, _ = x.shape
    N = B * T
    depth = mix_norm.shape[0]
    tm_proj = min(512, T)
    nb_scan = 1
    tb_scan = tm_proj
    tm_out = min(1024, T)
    tm_moe = min(512, N)
    h = x.reshape(N, D_MODEL)
    w_up_bf16 = w_up.astype(BF16)
    w_down_bf16 = w_down.astype(BF16)
    for layer in range(depth):
        j = layer // 2
        if layer % 2 == 0:
            q, k, v, sg, b, mq, bmin = _gla_proj(h, mix_norm[layer], gla_w_in[j], gla_w_gate_up[j], gla_b_gate[j],
                                                 tm_proj)
            heads, dk, dv_pad, dv, out_gain = GLA_HEADS, GLA_DK, GLA_DV_PAD, GLA_DV, gla_out_norm[j]
        else:
            q, k, v, sg, b, mq, bmin = _hg_proj(h, mix_norm[layer], hg_w_in[j], hg_lower_bounds, layer, tm_proj)
            heads, dk, dv_pad, dv, out_gain = HG_HEADS, HG_DK, HG_DV, HG_DV, hg_out_norm[j]
        r3 = lambda a: a.reshape(B, T, a.shape[-1])
        bmin = bmin.reshape(B, T // tb_scan, SUBLANES, LANES)
        mix = _scan(r3(q), r3(k), r3(b), bmin, r3(v), r3(sg), out_gain, heads, dk, dv_pad, dv, tb_scan, nb_scan)
        mix = mix.reshape(N, heads * dv_pad)
        mem_k, mem_v = _mem_kv(mem, mem_norm[layer], w_mem_kv[layer])
        w_mix = _pad_heads(w_out[layer, :MIX_WIDTH].T, heads, dv, dv_pad).T
        h1, un, gates = _out_call(mix, mq, mem_k, mem_v, h, w_mix, w_out[layer, MIX_WIDTH:], ffn_norm[layer],
                                  w_group[layer], b_group[layer], w_router[layer], b_router[layer], T, tm_out)
        h = _moe(un, gates, h1, w_up_bf16, w_down_bf16, layer, final_norm, layer == depth - 1, tm_moe)
    return h.reshape(B, T, D_MODEL)
```

```python
import functools

import jax
import jax.numpy as jnp
from jax import lax
from jax.experimental import pallas as pl
from jax.experimental.pallas import tpu as pltpu

D_MODEL = 1024
N_MEM = 256
MIX_WIDTH = 768
MEM_HEADS = 4
MEM_HEAD_DIM = 64
MEM_WIDTH = 256
CHUNK = 64
GLA_HEADS = 4
GLA_KEY_WIDTH = 512
GLA_DK = 128
GLA_DV = 192
GLA_DV_PAD = 256
GLA_GATE_RANK = 16
GLA_GATE_NORMALIZER = 16.0
HG_HEADS = 6
HG_DK = 128
HG_DV = 128
N_GROUPS = 4
EXPERTS_PER_GROUP = 4
N_EXPERTS = 16
EXPERT_FF = 256
NORM_EPS = 1e-6
LANES = 128
SUBLANES = 8
MOE_BLOCK = 128
CUMSUM_SPAN = 256
MOE_ALIGN_SHIFT = 4
MOE_ALIGN = 1 << MOE_ALIGN_SHIFT
MOE_COPY_SHIFT = 6
MOE_COPY = 1 << MOE_COPY_SHIFT
MOE_BIG = 256 + 2 * MOE_ALIGN
MOE_SUPER = 4
ROUTE_GROUP_ROW = 4
ROUTE_ROWS = 32
SAFE_LOG_DECAY = -60.0
SAFE_JOIN_LOG_DECAY = -70.0
SCAN_JOINS = (4, 2, 1)
VMEM_LIMIT = 56 * 1024 * 1024

F32 = jnp.float32
BF16 = jnp.bfloat16


def _dot(a, b):
    return jnp.dot(a, b, preferred_element_type=F32)


def _dot_nt(a, b):
    return lax.dot_general(a, b, (((1,), (1,)), ((), ())), preferred_element_type=F32)


def _dot_tn(a, b):
    return lax.dot_general(a, b, (((0,), (0,)), ((), ())), preferred_element_type=F32)


def _split2(x):
    hi = x.astype(BF16)
    lo = (x - hi.astype(F32)).astype(BF16)
    return hi, lo


def _rms(x, gain):
    ms = jnp.mean(x * x, axis=-1, keepdims=True)
    return x * lax.rsqrt(ms + NORM_EPS) * gain


def _log_sigmoid(x):
    return jnp.minimum(x, 0.0) - jnp.log1p(jnp.exp(-jnp.abs(x)))


def _sigmoid(x):
    return 1.0 / (1.0 + jnp.exp(-x))


def _silu(x):
    return x * _sigmoid(x)


def _chunk_cumsum(tri_ref, x):
    tri = tri_ref[...]
    span = tri.shape[0]
    parts = []
    for r in range(x.shape[0] // span):
        hi, lo = _split2(x[r * span:(r + 1) * span])
        parts.append(_dot(tri, hi) + _dot(tri, lo))
    return jnp.concatenate(parts, axis=0) if len(parts) > 1 else parts[0]


def _mem_kv_kernel(mem_ref, gain_ref, w_ref, k_ref, v_ref):
    m = _rms(mem_ref[0], gain_ref[...]).astype(BF16)
    kv = _dot(m, w_ref[...])
    k_ref[0] = kv[:, :MEM_WIDTH].astype(BF16)
    v_ref[0] = kv[:, MEM_WIDTH:].astype(BF16)


def _mem_kv(mem, gain, w_kv):
    B = mem.shape[0]
    return pl.pallas_call(
        _mem_kv_kernel,
        out_shape=(jax.ShapeDtypeStruct((B, N_MEM, MEM_WIDTH), BF16),
                   jax.ShapeDtypeStruct((B, N_MEM, MEM_WIDTH), BF16)),
        grid=(B,),
        in_specs=[pl.BlockSpec((1, N_MEM, D_MODEL), lambda b: (b, 0, 0)),
                  pl.BlockSpec((1, D_MODEL), lambda b: (0, 0)),
                  pl.BlockSpec((D_MODEL, 2 * MEM_WIDTH), lambda b: (0, 0))],
        out_specs=(pl.BlockSpec((1, N_MEM, MEM_WIDTH), lambda b: (b, 0, 0)),
                   pl.BlockSpec((1, N_MEM, MEM_WIDTH), lambda b: (b, 0, 0))),
        compiler_params=pltpu.CompilerParams(dimension_semantics=("arbitrary",)),
        name="mem_kv",
    )(mem, gain.reshape(1, D_MODEL), w_kv.astype(BF16))


_GQ, _GK = 0, GLA_KEY_WIDTH
_GV = 2 * GLA_KEY_WIDTH
_GG = _GV + MIX_WIDTH
_GR = _GG + MIX_WIDTH
_GM = _GR + LANES
_GW = _GM + MEM_WIDTH


def _store_padded_heads(ref, x):
    zeros = jnp.zeros((x.shape[0], GLA_DV_PAD - GLA_DV), ref.dtype)
    for hd in range(GLA_HEADS):
        ref[:, hd * GLA_DV_PAD:hd * GLA_DV_PAD + GLA_DV] = x[:, hd * GLA_DV:(hd + 1) * GLA_DV].astype(ref.dtype)
        ref[:, hd * GLA_DV_PAD + GLA_DV:(hd + 1) * GLA_DV_PAD] = zeros


def _store_decay(b_ref, bmin_ref, b):
    b_ref[...] = b
    lowest = jnp.min(jnp.min(b, axis=0, keepdims=True), axis=1, keepdims=True)
    bmin_ref[0] = jnp.broadcast_to(lowest, (SUBLANES, LANES))


def _gla_proj_kernel(h_ref, gain_ref, w_ref, wg_ref, bg_ref, tri_ref,
                     q_ref, k_ref, v_ref, sg_ref, b_ref, mq_ref, bmin_ref):
    u = _rms(h_ref[...], gain_ref[...]).astype(BF16)
    q_ref[...] = (_dot(u, w_ref[:, _GQ:_GK]) * (GLA_DK ** -0.5)).astype(BF16)
    k_ref[...] = _dot(u, w_ref[:, _GK:_GV]).astype(BF16)
    _store_padded_heads(v_ref, _dot(u, w_ref[:, _GV:_GG]))
    _store_padded_heads(sg_ref, _silu(_dot(u, w_ref[:, _GG:_GR])))
    mq_ref[...] = _dot(u, w_ref[:, _GM:_GW]).astype(BF16)
    r = _dot(u, w_ref[:, _GR:_GM])
    r_hi = r.astype(BF16)
    lane = lax.broadcasted_iota(jnp.int32, r.shape, 1)
    is_lo = (lane >= GLA_GATE_RANK) & (lane < 2 * GLA_GATE_RANK)
    r_parts = jnp.where(is_lo, r - r_hi.astype(F32), r_hi.astype(F32)).astype(BF16)
    logit = _dot(r_parts, wg_ref[...]) + bg_ref[...]
    _store_decay(b_ref, bmin_ref, _chunk_cumsum(tri_ref, _log_sigmoid(logit) * (1.0 / GLA_GATE_NORMALIZER)))


_HQ, _HF, _HI, _HGG, _HM = 0, MIX_WIDTH, 2 * MIX_WIDTH, 3 * MIX_WIDTH, 4 * MIX_WIDTH
_HW = _HM + MEM_WIDTH


def _hg_proj_kernel(layer, h_ref, gain_ref, w_ref, lbp_ref, tri_ref,
                    q_ref, k_ref, v_ref, sg_ref, b_ref, mq_ref, bmin_ref):
    u = _rms(h_ref[...], gain_ref[...]).astype(BF16)
    p = lbp_ref[...]
    p = jnp.exp(p - jnp.max(p, axis=0, keepdims=True))
    p = p / jnp.sum(p, axis=0, keepdims=True)
    lb = jnp.sum(p[0:layer + 1], axis=0, keepdims=True) - p[0:1]
    q_ref[...] = _silu(_dot(u, w_ref[:, _HQ:_HF])).astype(BF16)
    z = _dot(u, w_ref[:, _HF:_HI])
    k_ref[...] = ((1.0 - lb) * _sigmoid(-z)).astype(BF16)
    v_ref[...] = _dot(u, w_ref[:, _HI:_HGG]).astype(BF16)
    sg_ref[...] = _silu(_dot(u, w_ref[:, _HGG:_HM])).astype(BF16)
    mq_ref[...] = _dot(u, w_ref[:, _HM:_HW]).astype(BF16)
    a = jnp.log(lb)
    c = jnp.log1p(-lb) + _log_sigmoid(z)
    log_f = jnp.maximum(a, c) + jnp.log1p(jnp.exp(-jnp.abs(a - c)))
    _store_decay(b_ref, bmin_ref, _chunk_cumsum(tri_ref, log_f))


def _proj_call(kernel, h, gain, w, extra, kw, vw, tm, name):
    N = h.shape[0]
    span = min(CUMSUM_SPAN, tm)
    tri = (jnp.arange(span)[:, None] >= jnp.arange(span)[None, :]) & (
        jnp.arange(span)[:, None] // CHUNK == jnp.arange(span)[None, :] // CHUNK)
    tri = tri.astype(BF16)
    row = lambda width: pl.BlockSpec((tm, width), lambda i: (i, 0))
    full = lambda a: pl.BlockSpec(a.shape, lambda i: (0,) * a.ndim)
    ins = [h, gain.reshape(1, D_MODEL), w] + list(extra) + [tri]
    return pl.pallas_call(
        kernel,
        out_shape=(jax.ShapeDtypeStruct((N, kw), BF16), jax.ShapeDtypeStruct((N, kw), BF16),
                   jax.ShapeDtypeStruct((N, vw), BF16), jax.ShapeDtypeStruct((N, vw), BF16),
                   jax.ShapeDtypeStruct((N, kw), F32), jax.ShapeDtypeStruct((N, MEM_WIDTH), BF16),
                   jax.ShapeDtypeStruct((N // tm, SUBLANES, LANES), F32)),
        grid=(N // tm,),
        in_specs=[row(D_MODEL)] + [full(a) for a in ins[1:]],
        out_specs=(row(kw), row(kw), row(vw), row(vw), row(kw), row(MEM_WIDTH),
                   pl.BlockSpec((1, SUBLANES, LANES), lambda i: (i, 0, 0))),
        compiler_params=pltpu.CompilerParams(dimension_semantics=("arbitrary",),
                                             vmem_limit_bytes=VMEM_LIMIT),
        name=name,
    )(*ins)


def _pad_heads(w, heads, dv, dv_pad):
    lead = w.shape[:-1]
    w = w.reshape(lead + (heads, dv))
    w = jnp.pad(w, [(0, 0)] * len(lead) + [(0, 0), (0, dv_pad - dv)])
    return w.reshape(lead + (heads * dv_pad,))


def _gla_proj(h, gain, w_in, w_gate_up, b_gate, tm):
    q, k, v, g, r, mq = jnp.split(
        w_in, [GLA_KEY_WIDTH, 2 * GLA_KEY_WIDTH, 2 * GLA_KEY_WIDTH + MIX_WIDTH,
               2 * GLA_KEY_WIDTH + 2 * MIX_WIDTH, 2 * GLA_KEY_WIDTH + 2 * MIX_WIDTH + GLA_GATE_RANK], axis=1)
    r3 = jnp.pad(jnp.concatenate([r, r, r], axis=1), ((0, 0), (0, LANES - 3 * GLA_GATE_RANK)))
    w = jnp.concatenate([q, k, v, g, r3, mq], axis=1).astype(BF16)
    wg_hi, wg_lo = _split2(w_gate_up)
    wg = jnp.pad(jnp.concatenate([wg_hi, wg_hi, wg_lo], axis=0), ((0, LANES - 3 * GLA_GATE_RANK), (0, 0)))
    return _proj_call(_gla_proj_kernel, h, gain, w, [wg, b_gate.reshape(1, GLA_KEY_WIDTH)],
                      GLA_KEY_WIDTH, GLA_HEADS * GLA_DV_PAD, tm, "gla_proj")


def _hg_proj(h, gain, w_in, lower_bound_params, layer, tm):
    return _proj_call(functools.partial(_hg_proj_kernel, layer), h, gain, w_in.astype(BF16),
                      [lower_bound_params], MIX_WIDTH, MIX_WIDTH, tm, "hg_proj")


def _scan_kernel(heads, dk, dv_pad, dv, n_chunks,
                 q_ref, k_ref, b_ref, bmin_ref, v_ref, sg_ref, gain_ref, o_ref, st_ref, kf_ref):
    @pl.when(pl.program_id(1) == 0)
    def _():
        st_ref[...] = jnp.zeros_like(st_ref)

    gain = gain_ref[...]
    col = lax.broadcasted_iota(jnp.int32, (CHUNK, CHUNK), 1)

    n_batch = q_ref.shape[0]

    def finish(bi, hd, rows, scores, qd, kl, eb_last):
        vc = slice(hd * dv_pad, (hd + 1) * dv_pad)
        v = v_ref[bi, rows, vc]
        st = st_ref[bi * heads + hd]
        out = _dot(scores.astype(BF16), v) + _dot_nt(qd, st.astype(BF16))
        st_ref[bi * heads + hd] = _dot_tn(v, kl) + st * eb_last
        ms = jnp.sum(out * out, axis=-1, keepdims=True) * (1.0 / dv)
        y = out * lax.rsqrt(ms + NORM_EPS) * gain * sg_ref[bi, rows, vc].astype(F32)
        o_ref[bi, rows, vc] = y.astype(BF16)

    def load(bi, hd, rows):
        kc = slice(hd * dk, (hd + 1) * dk)
        return (q_ref[bi, rows, kc].astype(F32), k_ref[bi, rows, kc].astype(F32), b_ref[bi, rows, kc])

    lowest = jnp.min(bmin_ref[...])

    def factored(n_join):
        span = n_join * CHUNK
        ri = lax.broadcasted_iota(jnp.int32, (span, span), 0)
        ci = lax.broadcasted_iota(jnp.int32, (span, span), 1)
        for c in range(n_chunks // n_join):
            rows = slice(c * span, (c + 1) * span)
            for bi in range(n_batch):
                for hd in range(heads):
                    q, k, b = load(bi, hd, rows)
                    parts = [b[:CHUNK]]
                    for i in range(1, n_join):
                        parts.append(b[i * CHUNK:(i + 1) * CHUNK] + parts[-1][CHUNK - 1:CHUNK])
                    b = jnp.concatenate(parts, axis=0) if n_join > 1 else b
                    eb = jnp.exp(b)
                    eb_last = eb[span - 1:span, :]
                    qd = (q * eb).astype(BF16)
                    kd = k * jnp.exp(-b)
                    scores = _dot_nt(qd, kd.astype(BF16))
                    finish(bi, hd, rows, jnp.where(ri >= ci, scores, 0.0), qd, (kd * eb_last).astype(BF16),
                           eb_last)

    joins = [j for j in SCAN_JOINS if n_chunks % j == 0]
    bounds = [SAFE_JOIN_LOG_DECAY / j if j > 1 else SAFE_LOG_DECAY for j in joins]
    taken = False
    for j, bound in zip(joins, bounds):
        ok = lowest >= bound
        pl.when(ok if taken is False else ok & jnp.logical_not(taken))(functools.partial(factored, j))
        taken = ok if taken is False else taken | ok
    all_safe = taken

    @pl.when(jnp.logical_not(all_safe))
    def _():
        for bi, hd in [(bi, hd) for bi in range(n_batch) for hd in range(heads)]:
            kc = slice(hd * dk, (hd + 1) * dk)

            def chunk_body(c, carry, bi=bi, hd=hd, kc=kc):
                start = pl.multiple_of(c * CHUNK, CHUNK)
                rows = pl.ds(start, CHUNK)
                q, k, b = load(bi, hd, rows)
                kf_ref[...] = k

                def col_body(j, sc):
                    base = pl.multiple_of((j >> 3) << 3, SUBLANES)
                    pick = lax.broadcasted_iota(jnp.int32, (SUBLANES, dk), 0) == (j & (SUBLANES - 1))
                    kj = jnp.sum(jnp.where(pick, kf_ref[pl.ds(base, SUBLANES), :], 0.0), axis=0, keepdims=True)
                    b8 = b_ref[bi, pl.ds(pl.multiple_of(start + base, SUBLANES), SUBLANES), kc]
                    bj = jnp.sum(jnp.where(pick, b8, 0.0), axis=0, keepdims=True)
                    rid = lax.broadcasted_iota(jnp.int32, (CHUNK, dk), 0)
                    dec = jnp.exp(jnp.where(rid >= j, b - bj, -jnp.inf))
                    colv = jnp.sum(q * kj * dec, axis=-1, keepdims=True)
                    return jnp.where(col == j, colv, sc)

                scores = lax.fori_loop(0, CHUNK, col_body, jnp.zeros((CHUNK, CHUNK), F32))
                b_last = b[CHUNK - 1:CHUNK, :]
                finish(bi, hd, rows, scores, (q * jnp.exp(b)).astype(BF16),
                       (k * jnp.exp(b_last - b)).astype(BF16), jnp.exp(b_last))
                return carry

            lax.fori_loop(0, n_chunks, chunk_body, 0)


def _scan(q, k, b, bmin, v, sg, gain, heads, dk, dv_pad, dv, tb, nb):
    B, T, kw = q.shape
    vw = v.shape[-1]
    blk = lambda w: pl.BlockSpec((nb, tb, w), lambda bi, ti: (bi, ti, 0))
    gain_row = jnp.pad(gain, (0, dv_pad - dv)).reshape(1, dv_pad)
    return pl.pallas_call(
        functools.partial(_scan_kernel, heads, dk, dv_pad, dv, tb // CHUNK),
        out_shape=jax.ShapeDtypeStruct((B, T, vw), BF16),
        grid=(B // nb, T // tb),
        in_specs=[blk(kw), blk(kw), blk(kw),
                  pl.BlockSpec((nb, 1, SUBLANES, LANES), lambda bi, ti: (bi, ti, 0, 0)),
                  blk(vw), blk(vw), pl.BlockSpec((1, dv_pad), lambda bi, ti: (0, 0))],
        out_specs=blk(vw),
        scratch_shapes=[pltpu.VMEM((nb * heads, dv_pad, dk), F32), pltpu.VMEM((CHUNK, dk), F32)],
        compiler_params=pltpu.CompilerParams(dimension_semantics=("arbitrary", "arbitrary"),
                                             vmem_limit_bytes=VMEM_LIMIT),
        name="chunk_scan",
    )(q, k, b, bmin, v, sg, gain_row)


def _out_kernel(mix_ref, mq_ref, mk_ref, mv_ref, h_ref, wmix_ref, wmem_ref, fgain_ref,
                wr_ref, br_ref, h1_ref, un_ref, route_ref):
    tm = mix_ref.shape[0]
    mk = mk_ref[0]
    mv = mv_ref[0]
    lane_w = lax.broadcasted_iota(jnp.int32, (1, MEM_WIDTH), 1)
    mq = mq_ref[...] * (MEM_HEAD_DIM ** -0.5)
    mem_o = jnp.zeros((tm, MEM_WIDTH), F32)
    for hd in range(MEM_HEADS):
        in_head = (lane_w >= hd * MEM_HEAD_DIM) & (lane_w < (hd + 1) * MEM_HEAD_DIM)
        s = _dot_nt(jnp.where(in_head, mq, jnp.zeros_like(mq)), mk)
        e = jnp.exp(s - jnp.max(s, axis=-1, keepdims=True))
        denom = jnp.sum(e, axis=-1, keepdims=True)
        mem_o = mem_o + _dot(e.astype(BF16), jnp.where(in_head, mv, jnp.zeros_like(mv))) * (1.0 / denom)
    h1 = h_ref[...] + _dot(mix_ref[...], wmix_ref[...]) + _dot(mem_o.astype(BF16), wmem_ref[...])
    h1_ref[...] = h1
    un = _rms(h1, fgain_ref[...])
    un_ref[...] = un.astype(BF16)

    both = _dot_nt(wr_ref[...], un.astype(BF16))
    lg = both[:ROUTE_ROWS] + both[ROUTE_ROWS:] + br_ref[...]
    row = lax.broadcasted_iota(jnp.int32, (ROUTE_ROWS, tm), 0)
    neg = -jnp.inf
    gl = jnp.where(row < N_GROUPS, lg, neg)
    gmax = jnp.max(gl, axis=0, keepdims=True)
    g_idx = jnp.min(jnp.where(gl == gmax, row, ROUTE_ROWS), axis=0, keepdims=True)
    g_w = 1.0 / jnp.sum(jnp.exp(gl - gmax), axis=0, keepdims=True)
    first = N_GROUPS + g_idx * EXPERTS_PER_GROUP
    el = jnp.where((row >= first) & (row < first + EXPERTS_PER_GROUP), lg, neg)
    t1 = jnp.max(el, axis=0, keepdims=True)
    i1 = jnp.min(jnp.where(el == t1, row, ROUTE_ROWS), axis=0, keepdims=True)
    el2 = jnp.where(row == i1, neg, el)
    t2 = jnp.max(el2, axis=0, keepdims=True)
    i2 = jnp.min(jnp.where(el2 == t2, row, ROUTE_ROWS), axis=0, keepdims=True)
    e2 = jnp.exp(t2 - t1)
    w1 = g_w / (1.0 + e2)
    w2 = g_w * e2 / (1.0 + e2)
    row8 = lax.broadcasted_iota(jnp.int32, (SUBLANES, tm), 0)
    route_ref[...] = (jnp.where(row8 == i1 - first, w1, 0.0) + jnp.where(row8 == i2 - first, w2, 0.0)
                      + jnp.where(row8 == ROUTE_GROUP_ROW, g_idx.astype(F32), 0.0))


def _out_call(mix, mq, mem_k, mem_v, h, w_mix, w_mem, ffn_gain, w_group, b_group, w_router, b_router, T, tm):
    N = h.shape[0]
    tiles_per_batch = T // tm
    pad = ROUTE_ROWS - N_GROUPS - N_EXPERTS
    wr = jnp.pad(jnp.concatenate([w_group, w_router], axis=1).T, ((0, pad), (0, 0)))
    br = jnp.broadcast_to(jnp.pad(jnp.concatenate([b_group, b_router]), (0, pad))[:, None], (ROUTE_ROWS, tm))
    wr_both = jnp.concatenate(_split2(wr), axis=0)
    row = lambda width: pl.BlockSpec((tm, width), lambda i: (i, 0))
    full = lambda a: pl.BlockSpec(a.shape, lambda i: (0,) * a.ndim)
    memspec = pl.BlockSpec((1, N_MEM, MEM_WIDTH), lambda i: (i // tiles_per_batch, 0, 0))
    w_mix = w_mix.astype(BF16)
    w_mem = w_mem.astype(BF16)
    fg = ffn_gain.reshape(1, D_MODEL)
    return pl.pallas_call(
        _out_kernel,
        out_shape=(jax.ShapeDtypeStruct((N, D_MODEL), F32), jax.ShapeDtypeStruct((N, D_MODEL), BF16),
                   jax.ShapeDtypeStruct((SUBLANES, N), F32)),
        grid=(N // tm,),
        in_specs=[row(mix.shape[1]), row(MEM_WIDTH), memspec, memspec, row(D_MODEL),
                  full(w_mix), full(w_mem), full(fg), full(wr_both), full(br)],
        out_specs=(row(D_MODEL), row(D_MODEL), pl.BlockSpec((SUBLANES, tm), lambda i: (0, i))),
        compiler_params=pltpu.CompilerParams(dimension_semantics=("arbitrary",),
                                             vmem_limit_bytes=VMEM_LIMIT),
        name="out_proj_route",
    )(mix, mq, mem_k, mem_v, h, w_mix, w_mem, fg, wr_both, br)


def _moe_kernel(final, n_super, un_ref, route_ref, h1_ref, utri_ref, wup_ref, wdn_ref, fgain_ref, o_ref,
                xs_ref, gs_ref, xg_ref, gg_ref, permt_ref, meta_ref):
    s = pl.program_id(0)
    k = pl.program_id(1)
    n_tiles, slots, _ = xs_ref.shape
    tm = un_ref.shape[0]
    row8 = lax.broadcasted_iota(jnp.int32, (SUBLANES, tm), 0)

    @pl.when((s == 0) & (k == 0))
    def _():
        xg_ref[...] = jnp.zeros_like(xg_ref)
        gg_ref[...] = jnp.zeros_like(gg_ref)

    def unpartition():
        y = h1_ref[...] + _dot(permt_ref[k], xs_ref[k])
        o_ref[...] = _rms(y, fgain_ref[...]) if final else y

    def partition():
        route = route_ref[...]
        g_idx = route[ROUTE_GROUP_ROW:ROUTE_GROUP_ROW + 1, :]
        member = jnp.where((row8 < N_GROUPS) & (row8.astype(F32) == g_idx), 1.0, 0.0)
        earlier = _dot(member.astype(BF16), utri_ref[...])
        rank = jnp.sum(member * earlier, axis=0, keepdims=True)
        count = jnp.sum(member, axis=1, keepdims=True)
        padded = jnp.floor((count + (MOE_ALIGN - 1.0)) * (1.0 / MOE_ALIGN)) * MOE_ALIGN
        starts = [jnp.zeros((1, 1), F32)]
        for g in range(1, N_GROUPS):
            starts.append(starts[-1] + padded[g - 1:g, :])
        seg = jnp.zeros((SUBLANES, 1), F32)
        row81 = lax.broadcasted_iota(jnp.int32, (SUBLANES, 1), 0)
        for g in range(1, N_GROUPS):
            seg = jnp.where(row81 == g, starts[g], seg)
        dest = rank + jnp.sum(member * seg, axis=0, keepdims=True)
        perm = jnp.where(lax.broadcasted_iota(jnp.int32, (slots, tm), 0) == dest.astype(jnp.int32),
                         1.0, 0.0).astype(BF16)
        d_hi = jnp.floor(dest * (1.0 / 32.0))
        d_lo = dest - 32.0 * d_hi
        digits = jnp.where(row8 == 0, d_hi, jnp.where(row8 == 1, d_lo, 0.0)).astype(BF16)
        pick = (lax.broadcasted_iota(jnp.int32, (SUBLANES, LANES), 0)
                == lax.broadcasted_iota(jnp.int32, (SUBLANES, LANES), 1)).astype(BF16)
        dig_c = _dot_tn(digits, pick)
        dest_col = (32.0 * dig_c[:, 0:1] + dig_c[:, 1:2]).astype(jnp.int32)
        permt_ref[k] = jnp.where(lax.broadcasted_iota(jnp.int32, (tm, slots), 1) == dest_col,
                                 1.0, 0.0).astype(BF16)
        xs_ref[k] = _dot(perm, un_ref[...]).astype(BF16)
        r_hi, r_lo = _split2(route)
        gs_ref[k] = _dot_nt(perm, jnp.concatenate([r_hi, r_lo], axis=0))
        for g in range(N_GROUPS):
            meta_ref[(k * N_GROUPS + g) * 2] = starts[g][0, 0].astype(jnp.int32)
            meta_ref[(k * N_GROUPS + g) * 2 + 1] = padded[g, 0].astype(jnp.int32)

    pl.when(s > 0)(unpartition)
    pl.when(s < n_super)(partition)

    def expert_block(g, r0, rows):
        xb = xg_ref[pl.ds(r0, rows), :]
        gsb = gg_ref[pl.ds(r0, rows), :]
        acc = jnp.zeros((rows, D_MODEL), F32)
        for j in range(EXPERTS_PER_GROUP):
            e = g * EXPERTS_PER_GROUP + j
            hh = _dot(xb, wup_ref[e])
            gate = gsb[:, j:j + 1] + gsb[:, SUBLANES + j:SUBLANES + j + 1]
            act = _silu(hh[:, :EXPERT_FF]) * hh[:, EXPERT_FF:] * gate
            acc = acc + _dot(act.astype(BF16), wdn_ref[e])
        xg_ref[pl.ds(r0, rows), :] = acc.astype(BF16)

    def copy_segments(g, gather):
        off = jnp.int32(0)
        for t in range(n_tiles):
            st = meta_ref[(t * N_GROUPS + g) * 2]
            ln = meta_ref[(t * N_GROUPS + g) * 2 + 1]

            def copy_rows(src0, dst0, rows, t=t):
                src = pl.ds(pl.multiple_of(src0, MOE_ALIGN), rows)
                dst = pl.ds(pl.multiple_of(dst0, MOE_ALIGN), rows)
                if gather:
                    xg_ref[dst, :] = xs_ref[t, src, :]
                    gg_ref[dst, :] = gs_ref[t, src, :]
                else:
                    xs_ref[t, src, :] = xg_ref[dst, :]

            def copy_body(i, carry, st=st, off=off, copy_rows=copy_rows):
                copy_rows(st + i * MOE_COPY, off + i * MOE_COPY, MOE_COPY)
                return carry

            n_copy = lax.shift_right_logical(ln, MOE_COPY_SHIFT)
            lax.fori_loop(0, n_copy, copy_body, 0)
            done = n_copy * MOE_COPY

            def tail_body(i, carry, st=st, off=off, done=done, copy_rows=copy_rows):
                copy_rows(st + done + i * MOE_ALIGN, off + done + i * MOE_ALIGN, MOE_ALIGN)
                return carry

            lax.fori_loop(0, lax.shift_right_logical(ln - done, MOE_ALIGN_SHIFT), tail_body, 0)
            off = off + ln
        return off

    @pl.when((s < n_super) & (k == n_tiles - 1))
    def _():
        for g in range(N_GROUPS):
            total = copy_segments(g, True)
            n_big = total // MOE_BIG
            rem = total - n_big * MOE_BIG
            n_big = n_big + (rem > MOE_BLOCK).astype(jnp.int32)

            def big_body(i, carry, g=g):
                expert_block(g, pl.multiple_of(i * MOE_BIG, MOE_BIG), MOE_BIG)
                return carry

            lax.fori_loop(0, n_big, big_body, 0)

            @pl.when((rem > 0) & (rem <= MOE_BLOCK))
            def _(g=g, n_big=n_big):
                expert_block(g, pl.multiple_of(n_big * MOE_BIG, MOE_BIG), MOE_BLOCK)

            copy_segments(g, False)


def _moe(un, route, h1, w_up, w_down, layer, final_gain, final, tm):
    N = un.shape[0]
    n_tiles = min(MOE_SUPER, N // tm)
    n_super = N // (tm * n_tiles)
    slots = tm + N_GROUPS * MOE_ALIGN
    group_rows = n_tiles * slots + MOE_BIG
    utri = (jnp.arange(tm)[:, None] < jnp.arange(tm)[None, :]).astype(BF16)
    in_tile = lambda s, k: jnp.minimum(s, n_super - 1) * n_tiles + k
    out_tile = lambda s, k: jnp.maximum(s - 1, 0) * n_tiles + jnp.where(s > 0, k, 0)
    once = lambda a: pl.BlockSpec(a.shape, lambda s, k: (0,) * a.ndim, pipeline_mode=pl.Buffered(1))
    of_layer = lambda a: pl.BlockSpec((None,) + a.shape[1:], lambda s, k: (layer,) + (0,) * (a.ndim - 1),
                                      pipeline_mode=pl.Buffered(1))
    fg = final_gain.reshape(1, D_MODEL)
    return pl.pallas_call(
        functools.partial(_moe_kernel, final, n_super),
        out_shape=jax.ShapeDtypeStruct((N, D_MODEL), F32),
        grid=(n_super + 1, n_tiles),
        in_specs=[pl.BlockSpec((tm, D_MODEL), lambda s, k: (in_tile(s, k), 0)),
                  pl.BlockSpec((SUBLANES, tm), lambda s, k: (0, in_tile(s, k))),
                  pl.BlockSpec((tm, D_MODEL), lambda s, k: (out_tile(s, k), 0)),
                  once(utri), of_layer(w_up), of_layer(w_down), once(fg)],
        out_specs=pl.BlockSpec((tm, D_MODEL), lambda s, k: (out_tile(s, k), 0)),
        scratch_shapes=[pltpu.VMEM((n_tiles, slots, D_MODEL), BF16),
                        pltpu.VMEM((n_tiles, slots, 2 * SUBLANES), F32),
                        pltpu.VMEM((group_rows, D_MODEL), BF16),
                        pltpu.VMEM((group_rows, 2 * SUBLANES), F32),
                        pltpu.VMEM((n_tiles, tm, slots), BF16),
                        pltpu.SMEM((n_tiles * N_GROUPS * 2,), jnp.int32)],
        compiler_params=pltpu.CompilerParams(dimension_semantics=("arbitrary", "arbitrary"),
                                             vmem_limit_bytes=VMEM_LIMIT),
        name="moe",
    )(un, route, h1, utri, w_up, w_down, fg)


def kernel(x, mem, mix_norm, ffn_norm, mem_norm, final_norm, gla_w_in, gla_w_gate_up, gla_b_gate, gla_out_norm,
           hg_w_in, hg_lower_bounds, hg_out_norm, w_mem_kv, w_out, w_group, b_group, w_router, b_router,
           w_up, w_down):
    B, T, _ = x.shape
    N = B * T
    depth = mix_norm.shape[0]
    tm_proj = min(512, T)
    nb_scan = 1
    tb_scan = tm_proj
    tm_out = min(1024, T)
    tm_moe = min(512, N)
    h = x.reshape(N, D_MODEL)
    w_up_bf16 = w_up.astype(BF16)
    w_down_bf16 = w_down.astype(BF16)
    for layer in range(depth):
        j = layer // 2
        if layer % 2 == 0:
            q, k, v, sg, b, mq, bmin = _gla_proj(h, mix_norm[layer], gla_w_in[j], gla_w_gate_up[j], gla_b_gate[j],
                                                 tm_proj)
            heads, dk, dv_pad, dv, out_gain = GLA_HEADS, GLA_DK, GLA_DV_PAD, GLA_DV, gla_out_norm[j]
        else:
            q, k, v, sg, b, mq, bmin = _hg_proj(h, mix_norm[layer], hg_w_in[j], hg_lower_bounds, layer, tm_proj)
            heads, dk, dv_pad, dv, out_gain = HG_HEADS, HG_DK, HG_DV, HG_DV, hg_out_norm[j]
        r3 = lambda a: a.reshape(B, T, a.shape[-1])
        bmin = bmin.reshape(B, T // tb_scan, SUBLANES, LANES)
        mix = _scan(r3(q), r3(k), r3(b), bmin, r3(v), r3(sg), out_gain, heads, dk, dv_pad, dv, tb_scan, nb_scan)
        mix = mix.reshape(N, heads * dv_pad)
        mem_k, mem_v = _mem_kv(mem, mem_norm[layer], w_mem_kv[layer])
        w_mix = _pad_heads(w_out[layer, :MIX_WIDTH].T, heads, dv, dv_pad).T
        h1, un, gates = _out_call(mix, mq, mem_k, mem_v, h, w_mix, w_out[layer, MIX_WIDTH:], ffn_norm[layer],
                                  w_group[layer], b_group[layer], w_router[layer], b_router[layer], T, tm_out)
        h = _moe(un, gates, h1, w_up_bf16, w_down_bf16, layer, final_norm, layer == depth - 1, tm_moe)
    return h.reshape(B, T, D_MODEL)
```

```python
import functools

import jax
import jax.numpy as jnp
from jax import lax
from jax.experimental import pallas as pl
from jax.experimental.pallas import tpu as pltpu

D_MODEL = 1024
N_MEM = 256
MIX_WIDTH = 768
MEM_HEADS = 4
MEM_HEAD_DIM = 64
MEM_WIDTH = 256
CHUNK = 64
GLA_HEADS = 4
GLA_KEY_WIDTH = 512
GLA_DK = 128
GLA_DV = 192
GLA_DV_PAD = 256
GLA_GATE_RANK = 16
GLA_GATE_NORMALIZER = 16.0
HG_HEADS = 6
HG_DK = 128
HG_DV = 128
N_GROUPS = 4
EXPERTS_PER_GROUP = 4
N_EXPERTS = 16
EXPERT_FF = 256
NORM_EPS = 1e-6
LANES = 128
SUBLANES = 8
MOE_BLOCK = 128
CUMSUM_SPAN = 256
MOE_ALIGN_SHIFT = 4
MOE_ALIGN = 1 << MOE_ALIGN_SHIFT
MOE_COPY_SHIFT = 6
MOE_COPY = 1 << MOE_COPY_SHIFT
MOE_BIG = 256 + 2 * MOE_ALIGN
MOE_SUPER = 4
ROUTE_GROUP_ROW = 4
ROUTE_ROWS = 32
SAFE_LOG_DECAY = -60.0
SAFE_JOIN_LOG_DECAY = -70.0
SCAN_JOINS = (4, 2, 1)
VMEM_LIMIT = 56 * 1024 * 1024

F32 = jnp.float32
BF16 = jnp.bfloat16


def _dot(a, b):
    return jnp.dot(a, b, preferred_element_type=F32)


def _dot_nt(a, b):
    return lax.dot_general(a, b, (((1,), (1,)), ((), ())), preferred_element_type=F32)


def _dot_tn(a, b):
    return lax.dot_general(a, b, (((0,), (0,)), ((), ())), preferred_element_type=F32)


def _split2(x):
    hi = x.astype(BF16)
    lo = (x - hi.astype(F32)).astype(BF16)
    return hi, lo


def _rms(x, gain):
    ms = jnp.mean(x * x, axis=-1, keepdims=True)
    return x * lax.rsqrt(ms + NORM_EPS) * gain


def _log_sigmoid(x):
    return jnp.minimum(x, 0.0) - jnp.log1p(jnp.exp(-jnp.abs(x)))


def _sigmoid(x):
    return 1.0 / (1.0 + jnp.exp(-x))


def _silu(x):
    return x * _sigmoid(x)


def _chunk_cumsum(tri_ref, x):
    tri = tri_ref[...]
    span = tri.shape[0]
    parts = []
    for r in range(x.shape[0] // span):
        hi, lo = _split2(x[r * span:(r + 1) * span])
        parts.append(_dot(tri, hi) + _dot(tri, lo))
    return jnp.concatenate(parts, axis=0) if len(parts) > 1 else parts[0]


def _mem_kv_kernel(mem_ref, gain_ref, w_ref, k_ref, v_ref):
    m = _rms(mem_ref[0], gain_ref[...]).astype(BF16)
    kv = _dot(m, w_ref[...])
    k_ref[0] = kv[:, :MEM_WIDTH].astype(BF16)
    v_ref[0] = kv[:, MEM_WIDTH:].astype(BF16)


def _mem_kv(mem, gain, w_kv):
    B = mem.shape[0]
    return pl.pallas_call(
        _mem_kv_kernel,
        out_shape=(jax.ShapeDtypeStruct((B, N_MEM, MEM_WIDTH), BF16),
                   jax.ShapeDtypeStruct((B, N_MEM, MEM_WIDTH), BF16)),
        grid=(B,),
        in_specs=[pl.BlockSpec((1, N_MEM, D_MODEL), lambda b: (b, 0, 0)),
                  pl.BlockSpec((1, D_MODEL), lambda b: (0, 0)),
                  pl.BlockSpec((D_MODEL, 2 * MEM_WIDTH), lambda b: (0, 0))],
        out_specs=(pl.BlockSpec((1, N_MEM, MEM_WIDTH), lambda b: (b, 0, 0)),
                   pl.BlockSpec((1, N_MEM, MEM_WIDTH), lambda b: (b, 0, 0))),
        compiler_params=pltpu.CompilerParams(dimension_semantics=("arbitrary",)),
        name="mem_kv",
    )(mem, gain.reshape(1, D_MODEL), w_kv.astype(BF16))


_GQ, _GK = 0, GLA_KEY_WIDTH
_GV = 2 * GLA_KEY_WIDTH
_GG = _GV + MIX_WIDTH
_GR = _GG + MIX_WIDTH
_GM = _GR + LANES
_GW = _GM + MEM_WIDTH


def _store_padded_heads(ref, x):
    zeros = jnp.zeros((x.shape[0], GLA_DV_PAD - GLA_DV), ref.dtype)
    for hd in range(GLA_HEADS):
        ref[:, hd * GLA_DV_PAD:hd * GLA_DV_PAD + GLA_DV] = x[:, hd * GLA_DV:(hd + 1) * GLA_DV].astype(ref.dtype)
        ref[:, hd * GLA_DV_PAD + GLA_DV:(hd + 1) * GLA_DV_PAD] = zeros


def _store_decay(b_ref, bmin_ref, b):
    b_ref[...] = b
    lowest = jnp.min(jnp.min(b, axis=0, keepdims=True), axis=1, keepdims=True)
    bmin_ref[0] = jnp.broadcast_to(lowest, (SUBLANES, LANES))


def _gla_proj_kernel(h_ref, gain_ref, w_ref, wg_ref, bg_ref, tri_ref,
                     q_ref, k_ref, v_ref, sg_ref, b_ref, mq_ref, bmin_ref):
    u = _rms(h_ref[...], gain_ref[...]).astype(BF16)
    q_ref[...] = (_dot(u, w_ref[:, _GQ:_GK]) * (GLA_DK ** -0.5)).astype(BF16)
    k_ref[...] = _dot(u, w_ref[:, _GK:_GV]).astype(BF16)
    _store_padded_heads(v_ref, _dot(u, w_ref[:, _GV:_GG]))
    _store_padded_heads(sg_ref, _silu(_dot(u, w_ref[:, _GG:_GR])))
    mq_ref[...] = _dot(u, w_ref[:, _GM:_GW]).astype(BF16)
    r = _dot(u, w_ref[:, _GR:_GM])
    r_hi = r.astype(BF16)
    lane = lax.broadcasted_iota(jnp.int32, r.shape, 1)
    is_lo = (lane >= GLA_GATE_RANK) & (lane < 2 * GLA_GATE_RANK)
    r_parts = jnp.where(is_lo, r - r_hi.astype(F32), r_hi.astype(F32)).astype(BF16)
    logit = _dot(r_parts, wg_ref[...]) + bg_ref[...]
    _store_decay(b_ref, bmin_ref, _chunk_cumsum(tri_ref, _log_sigmoid(logit) * (1.0 / GLA_GATE_NORMALIZER)))


_HQ, _HF, _HI, _HGG, _HM = 0, MIX_WIDTH, 2 * MIX_WIDTH, 3 * MIX_WIDTH, 4 * MIX_WIDTH
_HW = _HM + MEM_WIDTH


def _hg_proj_kernel(layer, h_ref, gain_ref, w_ref, lbp_ref, tri_ref,
                    q_ref, k_ref, v_ref, sg_ref, b_ref, mq_ref, bmin_ref):
    u = _rms(h_ref[...], gain_ref[...]).astype(BF16)
    p = lbp_ref[...]
    p = jnp.exp(p - jnp.max(p, axis=0, keepdims=True))
    p = p / jnp.sum(p, axis=0, keepdims=True)
    lb = jnp.sum(p[0:layer + 1], axis=0, keepdims=True) - p[0:1]
    q_ref[...] = _silu(_dot(u, w_ref[:, _HQ:_HF])).astype(BF16)
    z = _dot(u, w_ref[:, _HF:_HI])
    k_ref[...] = ((1.0 - lb) * _sigmoid(-z)).astype(BF16)
    v_ref[...] = _dot(u, w_ref[:, _HI:_HGG]).astype(BF16)
    sg_ref[...] = _silu(_dot(u, w_ref[:, _HGG:_HM])).astype(BF16)
    mq_ref[...] = _dot(u, w_ref[:, _HM:_HW]).astype(BF16)
    a = jnp.log(lb)
    c = jnp.log1p(-lb) + _log_sigmoid(z)
    log_f = jnp.maximum(a, c) + jnp.log1p(jnp.exp(-jnp.abs(a - c)))
    _store_decay(b_ref, bmin_ref, _chunk_cumsum(tri_ref, log_f))


def _proj_call(kernel, h, gain, w, extra, kw, vw, tm, name):
    N = h.shape[0]
    span = min(CUMSUM_SPAN, tm)
    tri = (jnp.arange(span)[:, None] >= jnp.arange(span)[None, :]) & (
        jnp.arange(span)[:, None] // CHUNK == jnp.arange(span)[None, :] // CHUNK)
    tri = tri.astype(BF16)
    row = lambda width: pl.BlockSpec((tm, width), lambda i: (i, 0))
    full = lambda a: pl.BlockSpec(a.shape, lambda i: (0,) * a.ndim)
    ins = [h, gain.reshape(1, D_MODEL), w] + list(extra) + [tri]
    return pl.pallas_call(
        kernel,
        out_shape=(jax.ShapeDtypeStruct((N, kw), BF16), jax.ShapeDtypeStruct((N, kw), BF16),
                   jax.ShapeDtypeStruct((N, vw), BF16), jax.ShapeDtypeStruct((N, vw), BF16),
                   jax.ShapeDtypeStruct((N, kw), F32), jax.ShapeDtypeStruct((N, MEM_WIDTH), BF16),
                   jax.ShapeDtypeStruct((N // tm, SUBLANES, LANES), F32)),
        grid=(N // tm,),
        in_specs=[row(D_MODEL)] + [full(a) for a in ins[1:]],
        out_specs=(row(kw), row(kw), row(vw), row(vw), row(kw), row(MEM_WIDTH),
                   pl.BlockSpec((1, SUBLANES, LANES), lambda i: (i, 0, 0))),
        compiler_params=pltpu.CompilerParams(dimension_semantics=("arbitrary",),
                                             vmem_limit_bytes=VMEM_LIMIT),
        name=name,
    )(*ins)


def _pad_heads(w, heads, dv, dv_pad):
    lead = w.shape[:-1]
    w = w.reshape(lead + (heads, dv))
    w = jnp.pad(w, [(0, 0)] * len(lead) + [(0, 0), (0, dv_pad - dv)])
    return w.reshape(lead + (heads * dv_pad,))


def _gla_proj(h, gain, w_in, w_gate_up, b_gate, tm):
    q, k, v, g, r, mq = jnp.split(
        w_in, [GLA_KEY_WIDTH, 2 * GLA_KEY_WIDTH, 2 * GLA_KEY_WIDTH + MIX_WIDTH,
               2 * GLA_KEY_WIDTH + 2 * MIX_WIDTH, 2 * GLA_KEY_WIDTH + 2 * MIX_WIDTH + GLA_GATE_RANK], axis=1)
    r3 = jnp.pad(jnp.concatenate([r, r, r], axis=1), ((0, 0), (0, LANES - 3 * GLA_GATE_RANK)))
    w = jnp.concatenate([q, k, v, g, r3, mq], axis=1).astype(BF16)
    wg_hi, wg_lo = _split2(w_gate_up)
    wg = jnp.pad(jnp.concatenate([wg_hi, wg_hi, wg_lo], axis=0), ((0, LANES - 3 * GLA_GATE_RANK), (0, 0)))
    return _proj_call(_gla_proj_kernel, h, gain, w, [wg, b_gate.reshape(1, GLA_KEY_WIDTH)],
                      GLA_KEY_WIDTH, GLA_HEADS * GLA_DV_PAD, tm, "gla_proj")


def _hg_proj(h, gain, w_in, lower_bound_params, layer, tm):
    return _proj_call(functools.partial(_hg_proj_kernel, layer), h, gain, w_in.astype(BF16),
                      [lower_bound_params], MIX_WIDTH, MIX_WIDTH, tm, "hg_proj")


def _scan_kernel(heads, dk, dv_pad, dv, n_chunks,
                 q_ref, k_ref, b_ref, bmin_ref, v_ref, sg_ref, gain_ref, o_ref, st_ref, kf_ref):
    @pl.when(pl.program_id(1) == 0)
    def _():
        st_ref[...] = jnp.zeros_like(st_ref)

    gain = gain_ref[...]
    col = lax.broadcasted_iota(jnp.int32, (CHUNK, CHUNK), 1)

    n_batch = q_ref.shape[0]

    def finish(bi, hd, rows, scores, qd, kl, eb_last):
        vc = slice(hd * dv_pad, (hd + 1) * dv_pad)
        v = v_ref[bi, rows, vc]
        st = st_ref[bi * heads + hd]
        out = _dot(scores.astype(BF16), v) + _dot_nt(qd, st.astype(BF16))
        st_ref[bi * heads + hd] = _dot_tn(v, kl) + st * eb_last
        ms = jnp.sum(out * out, axis=-1, keepdims=True) * (1.0 / dv)
        y = out * lax.rsqrt(ms + NORM_EPS) * gain * sg_ref[bi, rows, vc].astype(F32)
        o_ref[bi, rows, vc] = y.astype(BF16)

    def load(bi, hd, rows):
        kc = slice(hd * dk, (hd + 1) * dk)
        return (q_ref[bi, rows, kc].astype(F32), k_ref[bi, rows, kc].astype(F32), b_ref[bi, rows, kc])

    lowest = jnp.min(bmin_ref[...])

    def factored(n_join):
        span = n_join * CHUNK
        ri = lax.broadcasted_iota(jnp.int32, (span, span), 0)
        ci = lax.broadcasted_iota(jnp.int32, (span, span), 1)
        for c in range(n_chunks // n_join):
            rows = slice(c * span, (c + 1) * span)
            for bi in range(n_batch):
                for hd in range(heads):
                    q, k, b = load(bi, hd, rows)
                    parts = [b[:CHUNK]]
                    for i in range(1, n_join):
                        parts.append(b[i * CHUNK:(i + 1) * CHUNK] + parts[-1][CHUNK - 1:CHUNK])
                    b = jnp.concatenate(parts, axis=0) if n_join > 1 else b
                    eb = jnp.exp(b)
                    eb_last = eb[span - 1:span, :]
                    qd = (q * eb).astype(BF16)
                    kd = k * jnp.exp(-b)
                    scores = _dot_nt(qd, kd.astype(BF16))
                    finish(bi, hd, rows, jnp.where(ri >= ci, scores, 0.0), qd, (kd * eb_last).astype(BF16),
                           eb_last)

    joins = [j for j in SCAN_JOINS if n_chunks % j == 0]
    bounds = [SAFE_JOIN_LOG_DECAY / j if j > 1 else SAFE_LOG_DECAY for j in joins]
    taken = False
    for j, bound in zip(joins, bounds):
        ok = lowest >= bound
        pl.when(ok if taken is False else ok & jnp.logical_not(taken))(functools.partial(factored, j))
        taken = ok if taken is False else taken | ok
    all_safe = taken

    @pl.when(jnp.logical_not(all_safe))
    def _():
        for bi, hd in [(bi, hd) for bi in range(n_batch) for hd in range(heads)]:
            kc = slice(hd * dk, (hd + 1) * dk)

            def chunk_body(c, carry, bi=bi, hd=hd, kc=kc):
                start = pl.multiple_of(c * CHUNK, CHUNK)
                rows = pl.ds(start, CHUNK)
                q, k, b = load(bi, hd, rows)
                kf_ref[...] = k

                def col_body(j, sc):
                    base = pl.multiple_of((j >> 3) << 3, SUBLANES)
                    pick = lax.broadcasted_iota(jnp.int32, (SUBLANES, dk), 0) == (j & (SUBLANES - 1))
                    kj = jnp.sum(jnp.where(pick, kf_ref[pl.ds(base, SUBLANES), :], 0.0), axis=0, keepdims=True)
                    b8 = b_ref[bi, pl.ds(pl.multiple_of(start + base, SUBLANES), SUBLANES), kc]
                    bj = jnp.sum(jnp.where(pick, b8, 0.0), axis=0, keepdims=True)
                    rid = lax.broadcasted_iota(jnp.int32, (CHUNK, dk), 0)
                    dec = jnp.exp(jnp.where(rid >= j, b - bj, -jnp.inf))
                    colv = jnp.sum(q * kj * dec, axis=-1, keepdims=True)
                    return jnp.where(col == j, colv, sc)

                scores = lax.fori_loop(0, CHUNK, col_body, jnp.zeros((CHUNK, CHUNK), F32))
                b_last = b[CHUNK - 1:CHUNK, :]
                finish(bi, hd, rows, scores, (q * jnp.exp(b)).astype(BF16),
                       (k * jnp.exp(b_last - b)).astype(BF16), jnp.exp(b_last))
                return carry

            lax.fori_loop(0, n_chunks, chunk_body, 0)


def _scan(q, k, b, bmin, v, sg, gain, heads, dk, dv_pad, dv, tb, nb):
    B, T, kw = q.shape
    vw = v.shape[-1]
    blk = lambda w: pl.BlockSpec((nb, tb, w), lambda bi, ti: (bi, ti, 0))
    gain_row = jnp.pad(gain, (0, dv_pad - dv)).reshape(1, dv_pad)
    return pl.pallas_call(
        functools.partial(_scan_kernel, heads, dk, dv_pad, dv, tb // CHUNK),
        out_shape=jax.ShapeDtypeStruct((B, T, vw), BF16),
        grid=(B // nb, T // tb),
        in_specs=[blk(kw), blk(kw), blk(kw),
                  pl.BlockSpec((nb, 1, SUBLANES, LANES), lambda bi, ti: (bi, ti, 0, 0)),
                  blk(vw), blk(vw), pl.BlockSpec((1, dv_pad), lambda bi, ti: (0, 0))],
        out_specs=blk(vw),
        scratch_shapes=[pltpu.VMEM((nb * heads, dv_pad, dk), F32), pltpu.VMEM((CHUNK, dk), F32)],
        compiler_params=pltpu.CompilerParams(dimension_semantics=("arbitrary", "arbitrary"),
                                             vmem_limit_bytes=VMEM_LIMIT),
        name="chunk_scan",
    )(q, k, b, bmin, v, sg, gain_row)


def _out_kernel(mix_ref, mq_ref, mk_ref, mv_ref, h_ref, wmix_ref, wmem_ref, fgain_ref,
                wr_ref, br_ref, h1_ref, un_ref, route_ref):
    tm = mix_ref.shape[0]
    mk = mk_ref[0]
    mv = mv_ref[0]
    lane_w = lax.broadcasted_iota(jnp.int32, (1, MEM_WIDTH), 1)
    mq = mq_ref[...] * (MEM_HEAD_DIM ** -0.5)
    mem_o = jnp.zeros((tm, MEM_WIDTH), F32)
    for hd in range(MEM_HEADS):
        in_head = (lane_w >= hd * MEM_HEAD_DIM) & (lane_w < (hd + 1) * MEM_HEAD_DIM)
        s = _dot_nt(jnp.where(in_head, mq, jnp.zeros_like(mq)), mk)
        e = jnp.exp(s - jnp.max(s, axis=-1, keepdims=True))
        denom = jnp.sum(e, axis=-1, keepdims=True)
        mem_o = mem_o + _dot(e.astype(BF16), jnp.where(in_head, mv, jnp.zeros_like(mv))) * (1.0 / denom)
    h1 = h_ref[...] + _dot(mix_ref[...], wmix_ref[...]) + _dot(mem_o.astype(BF16), wmem_ref[...])
    h1_ref[...] = h1
    un = _rms(h1, fgain_ref[...])
    un_ref[...] = un.astype(BF16)

    both = _dot_nt(wr_ref[...], un.astype(BF16))
    lg = both[:ROUTE_ROWS] + both[ROUTE_ROWS:] + br_ref[...]
    row = lax.broadcasted_iota(jnp.int32, (ROUTE_ROWS, tm), 0)
    neg = -jnp.inf
    gl = jnp.where(row < N_GROUPS, lg, neg)
    gmax = jnp.max(gl, axis=0, keepdims=True)
    g_idx = jnp.min(jnp.where(gl == gmax, row, ROUTE_ROWS), axis=0, keepdims=True)
    g_w = 1.0 / jnp.sum(jnp.exp(gl - gmax), axis=0, keepdims=True)
    first = N_GROUPS + g_idx * EXPERTS_PER_GROUP
    el = jnp.where((row >= first) & (row < first + EXPERTS_PER_GROUP), lg, neg)
    t1 = jnp.max(el, axis=0, keepdims=True)
    i1 = jnp.min(jnp.where(el == t1, row, ROUTE_ROWS), axis=0, keepdims=True)
    el2 = jnp.where(row == i1, neg, el)
    t2 = jnp.max(el2, axis=0, keepdims=True)
    i2 = jnp.min(jnp.where(el2 == t2, row, ROUTE_ROWS), axis=0, keepdims=True)
    e2 = jnp.exp(t2 - t1)
    w1 = g_w / (1.0 + e2)
    w2 = g_w * e2 / (1.0 + e2)
    row8 = lax.broadcasted_iota(jnp.int32, (SUBLANES, tm), 0)
    route_ref[...] = (jnp.where(row8 == i1 - first, w1, 0.0) + jnp.where(row8 == i2 - first, w2, 0.0)
                      + jnp.where(row8 == ROUTE_GROUP_ROW, g_idx.astype(F32), 0.0))


def _out_call(mix, mq, mem_k, mem_v, h, w_mix, w_mem, ffn_gain, w_group, b_group, w_router, b_router, T, tm):
    N = h.shape[0]
    tiles_per_batch = T // tm
    pad = ROUTE_ROWS - N_GROUPS - N_EXPERTS
    wr = jnp.pad(jnp.concatenate([w_group, w_router], axis=1).T, ((0, pad), (0, 0)))
    br = jnp.broadcast_to(jnp.pad(jnp.concatenate([b_group, b_router]), (0, pad))[:, None], (ROUTE_ROWS, tm))
    wr_both = jnp.concatenate(_split2(wr), axis=0)
    row = lambda width: pl.BlockSpec((tm, width), lambda i: (i, 0))
    full = lambda a: pl.BlockSpec(a.shape, lambda i: (0,) * a.ndim)
    memspec = pl.BlockSpec((1, N_MEM, MEM_WIDTH), lambda i: (i // tiles_per_batch, 0, 0))
    w_mix = w_mix.astype(BF16)
    w_mem = w_mem.astype(BF16)
    fg = ffn_gain.reshape(1, D_MODEL)
    return pl.pallas_call(
        _out_kernel,
        out_shape=(jax.ShapeDtypeStruct((N, D_MODEL), F32), jax.ShapeDtypeStruct((N, D_MODEL), BF16),
                   jax.ShapeDtypeStruct((SUBLANES, N), F32)),
        grid=(N // tm,),
        in_specs=[row(mix.shape[1]), row(MEM_WIDTH), memspec, memspec, row(D_MODEL),
                  full(w_mix), full(w_mem), full(fg), full(wr_both), full(br)],
        out_specs=(row(D_MODEL), row(D_MODEL), pl.BlockSpec((SUBLANES, tm), lambda i: (0, i))),
        compiler_params=pltpu.CompilerParams(dimension_semantics=("arbitrary",),
                                             vmem_limit_bytes=VMEM_LIMIT),
        name="out_proj_route",
    )(mix, mq, mem_k, mem_v, h, w_mix, w_mem, fg, wr_both, br)


def _moe_kernel(final, n_super, un_ref, route_ref, h1_ref, utri_ref, wup_ref, wdn_ref, fgain_ref, o_ref,
                xs_ref, gs_ref, xg_ref, gg_ref, permt_ref, meta_ref):
    s = pl.program_id(0)
    k = pl.program_id(1)
    n_tiles, slots, _ = xs_ref.shape
    tm = un_ref.shape[0]
    row8 = lax.broadcasted_iota(jnp.int32, (SUBLANES, tm), 0)

    @pl.when((s == 0) & (k == 0))
    def _():
        xg_ref[...] = jnp.zeros_like(xg_ref)
        gg_ref[...] = jnp.zeros_like(gg_ref)

    def unpartition():
        y = h1_ref[...] + _dot(permt_ref[k], xs_ref[k])
        o_ref[...] = _rms(y, fgain_ref[...]) if final else y

    def partition():
        route = route_ref[...]
        g_idx = route[ROUTE_GROUP_ROW:ROUTE_GROUP_ROW + 1, :]
        member = jnp.where((row8 < N_GROUPS) & (row8.astype(F32) == g_idx), 1.0, 0.0)
        earlier = _dot(member.astype(BF16), utri_ref[...])
        rank = jnp.sum(member * earlier, axis=0, keepdims=True)
        count = jnp.sum(member, axis=1, keepdims=True)
        padded = jnp.floor((count + (MOE_ALIGN - 1.0)) * (1.0 / MOE_ALIGN)) * MOE_ALIGN
        starts = [jnp.zeros((1, 1), F32)]
        for g in range(1, N_GROUPS):
            starts.append(starts[-1] + padded[g - 1:g, :])
        seg = jnp.zeros((SUBLANES, 1), F32)
        row81 = lax.broadcasted_iota(jnp.int32, (SUBLANES, 1), 0)
        for g in range(1, N_GROUPS):
            seg = jnp.where(row81 == g, starts[g], seg)
        dest = rank + jnp.sum(member * seg, axis=0, keepdims=True)
        perm = jnp.where(lax.broadcasted_iota(jnp.int32, (slots, tm), 0) == dest.astype(jnp.int32),
                         1.0, 0.0).astype(BF16)
        d_hi = jnp.floor(dest * (1.0 / 32.0))
        d_lo = dest - 32.0 * d_hi
        digits = jnp.where(row8 == 0, d_hi, jnp.where(row8 == 1, d_lo, 0.0)).astype(BF16)
        pick = (lax.broadcasted_iota(jnp.int32, (SUBLANES, LANES), 0)
                == lax.broadcasted_iota(jnp.int32, (SUBLANES, LANES), 1)).astype(BF16)
        dig_c = _dot_tn(digits, pick)
        dest_col = (32.0 * dig_c[:, 0:1] + dig_c[:, 1:2]).astype(jnp.int32)
        permt_ref[k] = jnp.where(lax.broadcasted_iota(jnp.int32, (tm, slots), 1) == dest_col,
                                 1.0, 0.0).astype(BF16)
        xs_ref[k] = _dot(perm, un_ref[...]).astype(BF16)
        r_hi, r_lo = _split2(route)
        gs_ref[k] = _dot_nt(perm, jnp.concatenate([r_hi, r_lo], axis=0))
        for g in range(N_GROUPS):
            meta_ref[(k * N_GROUPS + g) * 2] = starts[g][0, 0].astype(jnp.int32)
            meta_ref[(k * N_GROUPS + g) * 2 + 1] = padded[g, 0].astype(jnp.int32)

    pl.when(s > 0)(unpartition)
    pl.when(s < n_super)(partition)

    def expert_block(g, r0, rows):
        xb = xg_ref[pl.ds(r0, rows), :]
        gsb = gg_ref[pl.ds(r0, rows), :]
        acc = jnp.zeros((rows, D_MODEL), F32)
        for j in range(EXPERTS_PER_GROUP):
            e = g * EXPERTS_PER_GROUP + j
            hh = _dot(xb, wup_ref[e])
            gate = gsb[:, j:j + 1] + gsb[:, SUBLANES + j:SUBLANES + j + 1]
            act = _silu(hh[:, :EXPERT_FF]) * hh[:, EXPERT_FF:] * gate
            acc = acc + _dot(act.astype(BF16), wdn_ref[e])
        xg_ref[pl.ds(r0, rows), :] = acc.astype(BF16)

    def copy_segments(g, gather):
        off = jnp.int32(0)
        for t in range(n_tiles):
            st = meta_ref[(t * N_GROUPS + g) * 2]
            ln = meta_ref[(t * N_GROUPS + g) * 2 + 1]

            def copy_rows(src0, dst0, rows, t=t):
                src = pl.ds(pl.multiple_of(src0, MOE_ALIGN), rows)
                dst = pl.ds(pl.multiple_of(dst0, MOE_ALIGN), rows)
                if gather:
                    xg_ref[dst, :] = xs_ref[t, src, :]
                    gg_ref[dst, :] = gs_ref[t, src, :]
                else:
                    xs_ref[t, src, :] = xg_ref[dst, :]

            def copy_body(i, carry, st=st, off=off, copy_rows=copy_rows):
                copy_rows(st + i * MOE_COPY, off + i * MOE_COPY, MOE_COPY)
                return carry

            n_copy = lax.shift_right_logical(ln, MOE_COPY_SHIFT)
            lax.fori_loop(0, n_copy, copy_body, 0)
            done = n_copy * MOE_COPY

            def tail_body(i, carry, st=st, off=off, done=done, copy_rows=copy_rows):
                copy_rows(st + done + i * MOE_ALIGN, off + done + i * MOE_ALIGN, MOE_ALIGN)
                return carry

            lax.fori_loop(0, lax.shift_right_logical(ln - done, MOE_ALIGN_SHIFT), tail_body, 0)
            off = off + ln
        return off

    @pl.when((s < n_super) & (k == n_tiles - 1))
    def _():
        for g in range(N_GROUPS):
            total = copy_segments(g, True)
            n_big = total // MOE_BIG
            rem = total - n_big * MOE_BIG
            n_big = n_big + (rem > MOE_BLOCK).astype(jnp.int32)

            def big_body(i, carry, g=g):
                expert_block(g, pl.multiple_of(i * MOE_BIG, MOE_BIG), MOE_BIG)
                return carry

            lax.fori_loop(0, n_big, big_body, 0)

            @pl.when((rem > 0) & (rem <= MOE_BLOCK))
            def _(g=g, n_big=n_big):
                expert_block(g, pl.multiple_of(n_big * MOE_BIG, MOE_BIG), MOE_BLOCK)

            copy_segments(g, False)


def _moe(un, route, h1, w_up, w_down, layer, final_gain, final, tm):
    N = un.shape[0]
    n_tiles = min(MOE_SUPER, N // tm)
    n_super = N // (tm * n_tiles)
    slots = tm + N_GROUPS * MOE_ALIGN
    group_rows = n_tiles * slots + MOE_BIG
    utri = (jnp.arange(tm)[:, None] < jnp.arange(tm)[None, :]).astype(BF16)
    in_tile = lambda s, k: jnp.minimum(s, n_super - 1) * n_tiles + k
    out_tile = lambda s, k: jnp.maximum(s - 1, 0) * n_tiles + jnp.where(s > 0, k, 0)
    once = lambda a: pl.BlockSpec(a.shape, lambda s, k: (0,) * a.ndim, pipeline_mode=pl.Buffered(1))
    of_layer = lambda a: pl.BlockSpec((None,) + a.shape[1:], lambda s, k: (layer,) + (0,) * (a.ndim - 1),
                                      pipeline_mode=pl.Buffered(1))
    fg = final_gain.reshape(1, D_MODEL)
    return pl.pallas_call(
        functools.partial(_moe_kernel, final, n_super),
        out_shape=jax.ShapeDtypeStruct((N, D_MODEL), F32),
        grid=(n_super + 1, n_tiles),
        in_specs=[pl.BlockSpec((tm, D_MODEL), lambda s, k: (in_tile(s, k), 0)),
                  pl.BlockSpec((SUBLANES, tm), lambda s, k: (0, in_tile(s, k))),
                  pl.BlockSpec((tm, D_MODEL), lambda s, k: (out_tile(s, k), 0)),
                  once(utri), of_layer(w_up), of_layer(w_down), once(fg)],
        out_specs=pl.BlockSpec((tm, D_MODEL), lambda s, k: (out_tile(s, k), 0)),
        scratch_shapes=[pltpu.VMEM((n_tiles, slots, D_MODEL), BF16),
                        pltpu.VMEM((n_tiles, slots, 2 * SUBLANES), F32),
                        pltpu.VMEM((group_rows, D_MODEL), BF16),
                        pltpu.VMEM((group_rows, 2 * SUBLANES), F32),
                        pltpu.VMEM((n_tiles, tm, slots), BF16),
                        pltpu.SMEM((n_tiles * N_GROUPS * 2,), jnp.int32)],
        compiler_params=pltpu.CompilerParams(dimension_semantics=("arbitrary", "arbitrary"),
                                             vmem_limit_bytes=VMEM_LIMIT),
        name="moe",
    )(un, route, h1, utri, w_up, w_down, fg)


def kernel(x, mem, mix_norm, ffn_norm, mem_norm, final_norm, gla_w_in, gla_w_gate_up, gla_b_gate, gla_out_norm,
           hg_w_in, hg_lower_bounds, hg_out_norm, w_mem_kv, w_out, w_group, b_group, w_router, b_router,
           w_up, w_down):
    B, T, _ = x.shape
    N = B * T
    depth = mix_norm.shape[0]
    tm_proj = min(1024, T)
    nb_scan = 1
    tb_scan = tm_proj
    tm_out = min(1024, T)
    tm_moe = min(512, N)
    h = x.reshape(N, D_MODEL)
    w_up_bf16 = w_up.astype(BF16)
    w_down_bf16 = w_down.astype(BF16)
    for layer in range(depth):
        j = layer // 2
        if layer % 2 == 0:
            q, k, v, sg, b, mq, bmin = _gla_proj(h, mix_norm[layer], gla_w_in[j], gla_w_gate_up[j], gla_b_gate[j],
                                                 tm_proj)
            heads, dk, dv_pad, dv, out_gain = GLA_HEADS, GLA_DK, GLA_DV_PAD, GLA_DV, gla_out_norm[j]
        else:
            q, k, v, sg, b, mq, bmin = _hg_proj(h, mix_norm[layer], hg_w_in[j], hg_lower_bounds, layer, tm_proj)
            heads, dk, dv_pad, dv, out_gain = HG_HEADS, HG_DK, HG_DV, HG_DV, hg_out_norm[j]
        r3 = lambda a: a.reshape(B, T, a.shape[-1])
        bmin = bmin.reshape(B, T // tb_scan, SUBLANES, LANES)
        mix = _scan(r3(q), r3(k), r3(b), bmin, r3(v), r3(sg), out_gain, heads, dk, dv_pad, dv, tb_scan, nb_scan)
        mix = mix.reshape(N, heads * dv_pad)
        mem_k, mem_v = _mem_kv(mem, mem_norm[layer], w_mem_kv[layer])
        w_mix = _pad_heads(w_out[layer, :MIX_WIDTH].T, heads, dv, dv_pad).T
        h1, un, gates = _out_call(mix, mq, mem_k, mem_v, h, w_mix, w_out[layer, MIX_WIDTH:], ffn_norm[layer],
                                  w_group[layer], b_group[layer], w_router[layer], b_router[layer], T, tm_out)
        h = _moe(un, gates, h1, w_up_bf16, w_down_bf16, layer, final_norm, layer == depth - 1, tm_moe)
    return h.reshape(B, T, D_MODEL)
```

```python
import functools

import jax
import jax.numpy as jnp
from jax import lax
from jax.experimental import pallas as pl
from jax.experimental.pallas import tpu as pltpu

D_MODEL = 1024
N_MEM = 256
MIX_WIDTH = 768
MEM_HEADS = 4
MEM_HEAD_DIM = 64
MEM_WIDTH = 256
CHUNK = 64
GLA_HEADS = 4
GLA_KEY_WIDTH = 512
GLA_DK = 128
GLA_DV = 192
GLA_DV_PAD = 256
GLA_GATE_RANK = 16
GLA_GATE_NORMALIZER = 16.0
HG_HEADS = 6
HG_DK = 128
HG_DV = 128
N_GROUPS = 4
EXPERTS_PER_GROUP = 4
N_EXPERTS = 16
EXPERT_FF = 256
NORM_EPS = 1e-6
LANES = 128
SUBLANES = 8
MOE_BLOCK = 128
CUMSUM_SPAN = 256
MOE_ALIGN_SHIFT = 4
MOE_ALIGN = 1 << MOE_ALIGN_SHIFT
MOE_COPY_SHIFT = 6
MOE_COPY = 1 << MOE_COPY_SHIFT
MOE_TILE = 512
MOE_SUPER = 4
MOE_BIG = MOE_SUPER * (MOE_TILE // N_GROUPS + MOE_ALIGN)
MOE_SMALL_MAX = 2
ROUTE_GROUP_ROW = 4
ROUTE_ROWS = 32
SAFE_LOG_DECAY = -60.0
SAFE_JOIN_LOG_DECAY = -70.0
SCAN_JOINS = (4, 2, 1)
VMEM_LIMIT = 56 * 1024 * 1024

F32 = jnp.float32
BF16 = jnp.bfloat16


def _dot(a, b):
    return jnp.dot(a, b, preferred_element_type=F32)


def _dot_nt(a, b):
    return lax.dot_general(a, b, (((1,), (1,)), ((), ())), preferred_element_type=F32)


def _dot_tn(a, b):
    return lax.dot_general(a, b, (((0,), (0,)), ((), ())), preferred_element_type=F32)


def _split2(x):
    hi = x.astype(BF16)
    lo = (x - hi.astype(F32)).astype(BF16)
    return hi, lo


def _rms(x, gain):
    ms = jnp.mean(x * x, axis=-1, keepdims=True)
    return x * lax.rsqrt(ms + NORM_EPS) * gain


def _log_sigmoid(x):
    return jnp.minimum(x, 0.0) - jnp.log1p(jnp.exp(-jnp.abs(x)))


def _sigmoid(x):
    return 1.0 / (1.0 + jnp.exp(-x))


def _silu(x):
    return x * _sigmoid(x)


def _chunk_cumsum(tri_ref, x):
    tri = tri_ref[...]
    span = tri.shape[0]
    parts = []
    for r in range(x.shape[0] // span):
        hi, lo = _split2(x[r * span:(r + 1) * span])
        parts.append(_dot(tri, hi) + _dot(tri, lo))
    return jnp.concatenate(parts, axis=0) if len(parts) > 1 else parts[0]


def _mem_kv_kernel(mem_ref, gain_ref, w_ref, k_ref, v_ref):
    m = _rms(mem_ref[0], gain_ref[...]).astype(BF16)
    kv = _dot(m, w_ref[...])
    k_ref[0] = kv[:, :MEM_WIDTH].astype(BF16)
    v_ref[0] = kv[:, MEM_WIDTH:].astype(BF16)


def _mem_kv(mem, gain, w_kv):
    B = mem.shape[0]
    return pl.pallas_call(
        _mem_kv_kernel,
        out_shape=(jax.ShapeDtypeStruct((B, N_MEM, MEM_WIDTH), BF16),
                   jax.ShapeDtypeStruct((B, N_MEM, MEM_WIDTH), BF16)),
        grid=(B,),
        in_specs=[pl.BlockSpec((1, N_MEM, D_MODEL), lambda b: (b, 0, 0)),
                  pl.BlockSpec((1, D_MODEL), lambda b: (0, 0)),
                  pl.BlockSpec((D_MODEL, 2 * MEM_WIDTH), lambda b: (0, 0))],
        out_specs=(pl.BlockSpec((1, N_MEM, MEM_WIDTH), lambda b: (b, 0, 0)),
                   pl.BlockSpec((1, N_MEM, MEM_WIDTH), lambda b: (b, 0, 0))),
        compiler_params=pltpu.CompilerParams(dimension_semantics=("arbitrary",)),
        name="mem_kv",
    )(mem, gain.reshape(1, D_MODEL), w_kv.astype(BF16))


_GQ, _GK = 0, GLA_KEY_WIDTH
_GV = 2 * GLA_KEY_WIDTH
_GG = _GV + MIX_WIDTH
_GR = _GG + MIX_WIDTH
_GM = _GR + LANES
_GW = _GM + MEM_WIDTH


def _store_padded_heads(ref, x):
    zeros = jnp.zeros((x.shape[0], GLA_DV_PAD - GLA_DV), ref.dtype)
    for hd in range(GLA_HEADS):
        ref[:, hd * GLA_DV_PAD:hd * GLA_DV_PAD + GLA_DV] = x[:, hd * GLA_DV:(hd + 1) * GLA_DV].astype(ref.dtype)
        ref[:, hd * GLA_DV_PAD + GLA_DV:(hd + 1) * GLA_DV_PAD] = zeros


def _store_decay(b_ref, bmin_ref, b):
    b_ref[...] = b
    lowest = jnp.min(jnp.min(b, axis=0, keepdims=True), axis=1, keepdims=True)
    bmin_ref[0] = jnp.broadcast_to(lowest, (SUBLANES, LANES))


def _gla_proj_kernel(h_ref, gain_ref, w_ref, wg_ref, bg_ref, tri_ref,
                     q_ref, k_ref, v_ref, sg_ref, b_ref, mq_ref, bmin_ref):
    u = _rms(h_ref[...], gain_ref[...]).astype(BF16)
    q_ref[...] = (_dot(u, w_ref[:, _GQ:_GK]) * (GLA_DK ** -0.5)).astype(BF16)
    k_ref[...] = _dot(u, w_ref[:, _GK:_GV]).astype(BF16)
    _store_padded_heads(v_ref, _dot(u, w_ref[:, _GV:_GG]))
    _store_padded_heads(sg_ref, _silu(_dot(u, w_ref[:, _GG:_GR])))
    mq_ref[...] = _dot(u, w_ref[:, _GM:_GW]).astype(BF16)
    r = _dot(u, w_ref[:, _GR:_GM])
    r_hi = r.astype(BF16)
    lane = lax.broadcasted_iota(jnp.int32, r.shape, 1)
    is_lo = (lane >= GLA_GATE_RANK) & (lane < 2 * GLA_GATE_RANK)
    r_parts = jnp.where(is_lo, r - r_hi.astype(F32), r_hi.astype(F32)).astype(BF16)
    logit = _dot(r_parts, wg_ref[...]) + bg_ref[...]
    _store_decay(b_ref, bmin_ref, _chunk_cumsum(tri_ref, _log_sigmoid(logit) * (1.0 / GLA_GATE_NORMALIZER)))


_HQ, _HF, _HI, _HGG, _HM = 0, MIX_WIDTH, 2 * MIX_WIDTH, 3 * MIX_WIDTH, 4 * MIX_WIDTH
_HW = _HM + MEM_WIDTH


def _hg_proj_kernel(layer, h_ref, gain_ref, w_ref, lbp_ref, tri_ref,
                    q_ref, k_ref, v_ref, sg_ref, b_ref, mq_ref, bmin_ref):
    u = _rms(h_ref[...], gain_ref[...]).astype(BF16)
    p = lbp_ref[...]
    p = jnp.exp(p - jnp.max(p, axis=0, keepdims=True))
    p = p / jnp.sum(p, axis=0, keepdims=True)
    lb = jnp.sum(p[0:layer + 1], axis=0, keepdims=True) - p[0:1]
    q_ref[...] = _silu(_dot(u, w_ref[:, _HQ:_HF])).astype(BF16)
    z = _dot(u, w_ref[:, _HF:_HI])
    k_ref[...] = ((1.0 - lb) * _sigmoid(-z)).astype(BF16)
    v_ref[...] = _dot(u, w_ref[:, _HI:_HGG]).astype(BF16)
    sg_ref[...] = _silu(_dot(u, w_ref[:, _HGG:_HM])).astype(BF16)
    mq_ref[...] = _dot(u, w_ref[:, _HM:_HW]).astype(BF16)
    a = jnp.log(lb)
    c = jnp.log1p(-lb) + _log_sigmoid(z)
    log_f = jnp.maximum(a, c) + jnp.log1p(jnp.exp(-jnp.abs(a - c)))
    _store_decay(b_ref, bmin_ref, _chunk_cumsum(tri_ref, log_f))


def _proj_call(kernel, h, gain, w, extra, kw, vw, tm, name):
    N = h.shape[0]
    span = min(CUMSUM_SPAN, tm)
    tri = (jnp.arange(span)[:, None] >= jnp.arange(span)[None, :]) & (
        jnp.arange(span)[:, None] // CHUNK == jnp.arange(span)[None, :] // CHUNK)
    tri = tri.astype(BF16)
    row = lambda width: pl.BlockSpec((tm, width), lambda i: (i, 0))
    full = lambda a: pl.BlockSpec(a.shape, lambda i: (0,) * a.ndim)
    ins = [h, gain.reshape(1, D_MODEL), w] + list(extra) + [tri]
    return pl.pallas_call(
        kernel,
        out_shape=(jax.ShapeDtypeStruct((N, kw), BF16), jax.ShapeDtypeStruct((N, kw), BF16),
                   jax.ShapeDtypeStruct((N, vw), BF16), jax.ShapeDtypeStruct((N, vw), BF16),
                   jax.ShapeDtypeStruct((N, kw), F32), jax.ShapeDtypeStruct((N, MEM_WIDTH), BF16),
                   jax.ShapeDtypeStruct((N // tm, SUBLANES, LANES), F32)),
        grid=(N // tm,),
        in_specs=[row(D_MODEL)] + [full(a) for a in ins[1:]],
        out_specs=(row(kw), row(kw), row(vw), row(vw), row(kw), row(MEM_WIDTH),
                   pl.BlockSpec((1, SUBLANES, LANES), lambda i: (i, 0, 0))),
        compiler_params=pltpu.CompilerParams(dimension_semantics=("arbitrary",),
                                             vmem_limit_bytes=VMEM_LIMIT),
        name=name,
    )(*ins)


def _pad_heads(w, heads, dv, dv_pad):
    lead = w.shape[:-1]
    w = w.reshape(lead + (heads, dv))
    w = jnp.pad(w, [(0, 0)] * len(lead) + [(0, 0), (0, dv_pad - dv)])
    return w.reshape(lead + (heads * dv_pad,))


def _gla_proj(h, gain, w_in, w_gate_up, b_gate, tm):
    q, k, v, g, r, mq = jnp.split(
        w_in, [GLA_KEY_WIDTH, 2 * GLA_KEY_WIDTH, 2 * GLA_KEY_WIDTH + MIX_WIDTH,
               2 * GLA_KEY_WIDTH + 2 * MIX_WIDTH, 2 * GLA_KEY_WIDTH + 2 * MIX_WIDTH + GLA_GATE_RANK], axis=1)
    r3 = jnp.pad(jnp.concatenate([r, r, r], axis=1), ((0, 0), (0, LANES - 3 * GLA_GATE_RANK)))
    w = jnp.concatenate([q, k, v, g, r3, mq], axis=1).astype(BF16)
    wg_hi, wg_lo = _split2(w_gate_up)
    wg = jnp.pad(jnp.concatenate([wg_hi, wg_hi, wg_lo], axis=0), ((0, LANES - 3 * GLA_GATE_RANK), (0, 0)))
    return _proj_call(_gla_proj_kernel, h, gain, w, [wg, b_gate.reshape(1, GLA_KEY_WIDTH)],
                      GLA_KEY_WIDTH, GLA_HEADS * GLA_DV_PAD, tm, "gla_proj")


def _hg_proj(h, gain, w_in, lower_bound_params, layer, tm):
    return _proj_call(functools.partial(_hg_proj_kernel, layer), h, gain, w_in.astype(BF16),
                      [lower_bound_params], MIX_WIDTH, MIX_WIDTH, tm, "hg_proj")


def _scan_kernel(heads, dk, dv_pad, dv, n_chunks,
                 q_ref, k_ref, b_ref, bmin_ref, v_ref, sg_ref, gain_ref, o_ref, st_ref, kf_ref):
    @pl.when(pl.program_id(1) == 0)
    def _():
        st_ref[...] = jnp.zeros_like(st_ref)

    gain = gain_ref[...]
    col = lax.broadcasted_iota(jnp.int32, (CHUNK, CHUNK), 1)

    n_batch = q_ref.shape[0]

    def finish(bi, hd, rows, scores, qd, kl, eb_last):
        vc = slice(hd * dv_pad, (hd + 1) * dv_pad)
        v = v_ref[bi, rows, vc]
        st = st_ref[bi * heads + hd]
        out = _dot(scores.astype(BF16), v) + _dot_nt(qd, st.astype(BF16))
        st_ref[bi * heads + hd] = _dot_tn(v, kl) + st * eb_last
        ms = jnp.sum(out * out, axis=-1, keepdims=True) * (1.0 / dv)
        y = out * lax.rsqrt(ms + NORM_EPS) * gain * sg_ref[bi, rows, vc].astype(F32)
        o_ref[bi, rows, vc] = y.astype(BF16)

    def load(bi, hd, rows):
        kc = slice(hd * dk, (hd + 1) * dk)
        return (q_ref[bi, rows, kc].astype(F32), k_ref[bi, rows, kc].astype(F32), b_ref[bi, rows, kc])

    lowest = jnp.min(bmin_ref[...])

    def factored(n_join):
        span = n_join * CHUNK
        ri = lax.broadcasted_iota(jnp.int32, (span, span), 0)
        ci = lax.broadcasted_iota(jnp.int32, (span, span), 1)
        for c in range(n_chunks // n_join):
            rows = slice(c * span, (c + 1) * span)
            for bi in range(n_batch):
                for hd in range(heads):
                    q, k, b = load(bi, hd, rows)
                    parts = [b[:CHUNK]]
                    for i in range(1, n_join):
                        parts.append(b[i * CHUNK:(i + 1) * CHUNK] + parts[-1][CHUNK - 1:CHUNK])
                    b = jnp.concatenate(parts, axis=0) if n_join > 1 else b
                    eb = jnp.exp(b)
                    eb_last = eb[span - 1:span, :]
                    qd = (q * eb).astype(BF16)
                    kd = k * jnp.exp(-b)
                    scores = _dot_nt(qd, kd.astype(BF16))
                    finish(bi, hd, rows, jnp.where(ri >= ci, scores, 0.0), qd, (kd * eb_last).astype(BF16),
                           eb_last)

    joins = [j for j in SCAN_JOINS if n_chunks % j == 0]
    bounds = [SAFE_JOIN_LOG_DECAY / j if j > 1 else SAFE_LOG_DECAY for j in joins]
    taken = False
    for j, bound in zip(joins, bounds):
        ok = lowest >= bound
        pl.when(ok if taken is False else ok & jnp.logical_not(taken))(functools.partial(factored, j))
        taken = ok if taken is False else taken | ok
    all_safe = taken

    @pl.when(jnp.logical_not(all_safe))
    def _():
        for bi, hd in [(bi, hd) for bi in range(n_batch) for hd in range(heads)]:
            kc = slice(hd * dk, (hd + 1) * dk)

            def chunk_body(c, carry, bi=bi, hd=hd, kc=kc):
                start = pl.multiple_of(c * CHUNK, CHUNK)
                rows = pl.ds(start, CHUNK)
                q, k, b = load(bi, hd, rows)
                kf_ref[...] = k

                def col_body(j, sc):
                    base = pl.multiple_of((j >> 3) << 3, SUBLANES)
                    pick = lax.broadcasted_iota(jnp.int32, (SUBLANES, dk), 0) == (j & (SUBLANES - 1))
                    kj = jnp.sum(jnp.where(pick, kf_ref[pl.ds(base, SUBLANES), :], 0.0), axis=0, keepdims=True)
                    b8 = b_ref[bi, pl.ds(pl.multiple_of(start + base, SUBLANES), SUBLANES), kc]
                    bj = jnp.sum(jnp.where(pick, b8, 0.0), axis=0, keepdims=True)
                    rid = lax.broadcasted_iota(jnp.int32, (CHUNK, dk), 0)
                    dec = jnp.exp(jnp.where(rid >= j, b - bj, -jnp.inf))
                    colv = jnp.sum(q * kj * dec, axis=-1, keepdims=True)
                    return jnp.where(col == j, colv, sc)

                scores = lax.fori_loop(0, CHUNK, col_body, jnp.zeros((CHUNK, CHUNK), F32))
                b_last = b[CHUNK - 1:CHUNK, :]
                finish(bi, hd, rows, scores, (q * jnp.exp(b)).astype(BF16),
                       (k * jnp.exp(b_last - b)).astype(BF16), jnp.exp(b_last))
                return carry

            lax.fori_loop(0, n_chunks, chunk_body, 0)


def _scan(q, k, b, bmin, v, sg, gain, heads, dk, dv_pad, dv, tb, nb):
    B, T, kw = q.shape
    vw = v.shape[-1]
    blk = lambda w: pl.BlockSpec((nb, tb, w), lambda bi, ti: (bi, ti, 0))
    gain_row = jnp.pad(gain, (0, dv_pad - dv)).reshape(1, dv_pad)
    return pl.pallas_call(
        functools.partial(_scan_kernel, heads, dk, dv_pad, dv, tb // CHUNK),
        out_shape=jax.ShapeDtypeStruct((B, T, vw), BF16),
        grid=(B // nb, T // tb),
        in_specs=[blk(kw), blk(kw), blk(kw),
                  pl.BlockSpec((nb, 1, SUBLANES, LANES), lambda bi, ti: (bi, ti, 0, 0)),
                  blk(vw), blk(vw), pl.BlockSpec((1, dv_pad), lambda bi, ti: (0, 0))],
        out_specs=blk(vw),
        scratch_shapes=[pltpu.VMEM((nb * heads, dv_pad, dk), F32), pltpu.VMEM((CHUNK, dk), F32)],
        compiler_params=pltpu.CompilerParams(dimension_semantics=("arbitrary", "arbitrary"),
                                             vmem_limit_bytes=VMEM_LIMIT),
        name="chunk_scan",
    )(q, k, b, bmin, v, sg, gain_row)


def _out_kernel(mix_ref, mq_ref, mk_ref, mv_ref, h_ref, wmix_ref, wmem_ref, fgain_ref,
                wr_ref, br_ref, h1_ref, un_ref, route_ref):
    tm = mix_ref.shape[0]
    mk = mk_ref[0]
    mv = mv_ref[0]
    lane_w = lax.broadcasted_iota(jnp.int32, (1, MEM_WIDTH), 1)
    mq = mq_ref[...] * (MEM_HEAD_DIM ** -0.5)
    mem_o = jnp.zeros((tm, MEM_WIDTH), F32)
    for hd in range(MEM_HEADS):
        in_head = (lane_w >= hd * MEM_HEAD_DIM) & (lane_w < (hd + 1) * MEM_HEAD_DIM)
        s = _dot_nt(jnp.where(in_head, mq, jnp.zeros_like(mq)), mk)
        e = jnp.exp(s - jnp.max(s, axis=-1, keepdims=True))
        denom = jnp.sum(e, axis=-1, keepdims=True)
        mem_o = mem_o + _dot(e.astype(BF16), jnp.where(in_head, mv, jnp.zeros_like(mv))) * (1.0 / denom)
    h1 = h_ref[...] + _dot(mix_ref[...], wmix_ref[...]) + _dot(mem_o.astype(BF16), wmem_ref[...])
    h1_ref[...] = h1
    un = _rms(h1, fgain_ref[...])
    un_ref[...] = un.astype(BF16)

    both = _dot_nt(wr_ref[...], un.astype(BF16))
    lg = both[:ROUTE_ROWS] + both[ROUTE_ROWS:] + br_ref[...]
    row = lax.broadcasted_iota(jnp.int32, (ROUTE_ROWS, tm), 0)
    neg = -jnp.inf
    gl = jnp.where(row < N_GROUPS, lg, neg)
    gmax = jnp.max(gl, axis=0, keepdims=True)
    g_idx = jnp.min(jnp.where(gl == gmax, row, ROUTE_ROWS), axis=0, keepdims=True)
    g_w = 1.0 / jnp.sum(jnp.exp(gl - gmax), axis=0, keepdims=True)
    first = N_GROUPS + g_idx * EXPERTS_PER_GROUP
    el = jnp.where((row >= first) & (row < first + EXPERTS_PER_GROUP), lg, neg)
    t1 = jnp.max(el, axis=0, keepdims=True)
    i1 = jnp.min(jnp.where(el == t1, row, ROUTE_ROWS), axis=0, keepdims=True)
    el2 = jnp.where(row == i1, neg, el)
    t2 = jnp.max(el2, axis=0, keepdims=True)
    i2 = jnp.min(jnp.where(el2 == t2, row, ROUTE_ROWS), axis=0, keepdims=True)
    e2 = jnp.exp(t2 - t1)
    w1 = g_w / (1.0 + e2)
    w2 = g_w * e2 / (1.0 + e2)
    row8 = lax.broadcasted_iota(jnp.int32, (SUBLANES, tm), 0)
    route_ref[...] = (jnp.where(row8 == i1 - first, w1, 0.0) + jnp.where(row8 == i2 - first, w2, 0.0)
                      + jnp.where(row8 == ROUTE_GROUP_ROW, g_idx.astype(F32), 0.0))


def _out_call(mix, mq, mem_k, mem_v, h, w_mix, w_mem, ffn_gain, w_group, b_group, w_router, b_router, T, tm):
    N = h.shape[0]
    tiles_per_batch = T // tm
    pad = ROUTE_ROWS - N_GROUPS - N_EXPERTS
    wr = jnp.pad(jnp.concatenate([w_group, w_router], axis=1).T, ((0, pad), (0, 0)))
    br = jnp.broadcast_to(jnp.pad(jnp.concatenate([b_group, b_router]), (0, pad))[:, None], (ROUTE_ROWS, tm))
    wr_both = jnp.concatenate(_split2(wr), axis=0)
    row = lambda width: pl.BlockSpec((tm, width), lambda i: (i, 0))
    full = lambda a: pl.BlockSpec(a.shape, lambda i: (0,) * a.ndim)
    memspec = pl.BlockSpec((1, N_MEM, MEM_WIDTH), lambda i: (i // tiles_per_batch, 0, 0))
    w_mix = w_mix.astype(BF16)
    w_mem = w_mem.astype(BF16)
    fg = ffn_gain.reshape(1, D_MODEL)
    return pl.pallas_call(
        _out_kernel,
        out_shape=(jax.ShapeDtypeStruct((N, D_MODEL), F32), jax.ShapeDtypeStruct((N, D_MODEL), BF16),
                   jax.ShapeDtypeStruct((SUBLANES, N), F32)),
        grid=(N // tm,),
        in_specs=[row(mix.shape[1]), row(MEM_WIDTH), memspec, memspec, row(D_MODEL),
                  full(w_mix), full(w_mem), full(fg), full(wr_both), full(br)],
        out_specs=(row(D_MODEL), row(D_MODEL), pl.BlockSpec((SUBLANES, tm), lambda i: (0, i))),
        compiler_params=pltpu.CompilerParams(dimension_semantics=("arbitrary",),
                                             vmem_limit_bytes=VMEM_LIMIT),
        name="out_proj_route",
    )(mix, mq, mem_k, mem_v, h, w_mix, w_mem, fg, wr_both, br)


def _moe_kernel(final, n_super, un_ref, route_ref, h1_ref, utri_ref, wup_ref, wdn_ref, fgain_ref, o_ref,
                xs_ref, gs_ref, xg_ref, gg_ref, permt_ref, meta_ref):
    s = pl.program_id(0)
    k = pl.program_id(1)
    n_tiles, slots, _ = xs_ref.shape
    tm = un_ref.shape[0]
    row8 = lax.broadcasted_iota(jnp.int32, (SUBLANES, tm), 0)

    @pl.when((s == 0) & (k == 0))
    def _():
        xg_ref[...] = jnp.zeros_like(xg_ref)
        gg_ref[...] = jnp.zeros_like(gg_ref)

    def unpartition():
        y = h1_ref[...] + _dot(permt_ref[k], xs_ref[k])
        o_ref[...] = _rms(y, fgain_ref[...]) if final else y

    def partition():
        route = route_ref[...]
        g_idx = route[ROUTE_GROUP_ROW:ROUTE_GROUP_ROW + 1, :]
        member = jnp.where((row8 < N_GROUPS) & (row8.astype(F32) == g_idx), 1.0, 0.0)
        earlier = _dot(member.astype(BF16), utri_ref[...])
        rank = jnp.sum(member * earlier, axis=0, keepdims=True)
        count = jnp.sum(member, axis=1, keepdims=True)
        padded = jnp.floor((count + (MOE_ALIGN - 1.0)) * (1.0 / MOE_ALIGN)) * MOE_ALIGN
        starts = [jnp.zeros((1, 1), F32)]
        for g in range(1, N_GROUPS):
            starts.append(starts[-1] + padded[g - 1:g, :])
        seg = jnp.zeros((SUBLANES, 1), F32)
        row81 = lax.broadcasted_iota(jnp.int32, (SUBLANES, 1), 0)
        for g in range(1, N_GROUPS):
            seg = jnp.where(row81 == g, starts[g], seg)
        dest = rank + jnp.sum(member * seg, axis=0, keepdims=True)
        perm = jnp.where(lax.broadcasted_iota(jnp.int32, (slots, tm), 0) == dest.astype(jnp.int32),
                         1.0, 0.0).astype(BF16)
        d_hi = jnp.floor(dest * (1.0 / 32.0))
        d_lo = dest - 32.0 * d_hi
        digits = jnp.where(row8 == 0, d_hi, jnp.where(row8 == 1, d_lo, 0.0)).astype(BF16)
        pick = (lax.broadcasted_iota(jnp.int32, (SUBLANES, LANES), 0)
                == lax.broadcasted_iota(jnp.int32, (SUBLANES, LANES), 1)).astype(BF16)
        dig_c = _dot_tn(digits, pick)
        dest_col = (32.0 * dig_c[:, 0:1] + dig_c[:, 1:2]).astype(jnp.int32)
        permt_ref[k] = jnp.where(lax.broadcasted_iota(jnp.int32, (tm, slots), 1) == dest_col,
                                 1.0, 0.0).astype(BF16)
        xs_ref[k] = _dot(perm, un_ref[...]).astype(BF16)
        r_hi, r_lo = _split2(route)
        gs_ref[k] = _dot_nt(perm, jnp.concatenate([r_hi, r_lo], axis=0))
        for g in range(N_GROUPS):
            meta_ref[(k * N_GROUPS + g) * 2] = starts[g][0, 0].astype(jnp.int32)
            meta_ref[(k * N_GROUPS + g) * 2 + 1] = padded[g, 0].astype(jnp.int32)

    pl.when(s > 0)(unpartition)
    pl.when(s < n_super)(partition)

    def expert_block(g, r0, rows):
        xb = xg_ref[pl.ds(r0, rows), :]
        gsb = gg_ref[pl.ds(r0, rows), :]
        acc = jnp.zeros((rows, D_MODEL), F32)
        for j in range(EXPERTS_PER_GROUP):
            e = g * EXPERTS_PER_GROUP + j
            hh = _dot(xb, wup_ref[e])
            gate = gsb[:, j:j + 1] + gsb[:, SUBLANES + j:SUBLANES + j + 1]
            act = _silu(hh[:, :EXPERT_FF]) * hh[:, EXPERT_FF:] * gate
            acc = acc + _dot(act.astype(BF16), wdn_ref[e])
        xg_ref[pl.ds(r0, rows), :] = acc.astype(BF16)

    def copy_segments(g, gather):
        off = jnp.int32(0)
        for t in range(n_tiles):
            st = meta_ref[(t * N_GROUPS + g) * 2]
            ln = meta_ref[(t * N_GROUPS + g) * 2 + 1]

            def copy_rows(src0, dst0, rows, t=t):
                src = pl.ds(pl.multiple_of(src0, MOE_ALIGN), rows)
                dst = pl.ds(pl.multiple_of(dst0, MOE_ALIGN), rows)
                if gather:
                    xg_ref[dst, :] = xs_ref[t, src, :]
                    gg_ref[dst, :] = gs_ref[t, src, :]
                else:
                    xs_ref[t, src, :] = xg_ref[dst, :]

            def copy_body(i, carry, st=st, off=off, copy_rows=copy_rows):
                copy_rows(st + i * MOE_COPY, off + i * MOE_COPY, MOE_COPY)
                return carry

            n_copy = lax.shift_right_logical(ln, MOE_COPY_SHIFT)
            lax.fori_loop(0, n_copy, copy_body, 0)
            done = n_copy * MOE_COPY

            def tail_body(i, carry, st=st, off=off, done=done, copy_rows=copy_rows):
                copy_rows(st + done + i * MOE_ALIGN, off + done + i * MOE_ALIGN, MOE_ALIGN)
                return carry

            lax.fori_loop(0, lax.shift_right_logical(ln - done, MOE_ALIGN_SHIFT), tail_body, 0)
            off = off + ln
        return off

    @pl.when((s < n_super) & (k == n_tiles - 1))
    def _():
        for g in range(N_GROUPS):
            total = copy_segments(g, True)
            n_big = total // MOE_BIG
            rem = total - n_big * MOE_BIG
            more = rem > MOE_SMALL_MAX * MOE_BLOCK
            n_big = n_big + more.astype(jnp.int32)
            n_small = jnp.where(more, 0, (rem + (MOE_BLOCK - 1)) // MOE_BLOCK)
            small0 = n_big * MOE_BIG

            def big_body(i, carry, g=g):
                expert_block(g, pl.multiple_of(i * MOE_BIG, MOE_ALIGN), MOE_BIG)
                return carry

            def small_body(i, carry, g=g, small0=small0):
                expert_block(g, pl.multiple_of(small0 + i * MOE_BLOCK, MOE_ALIGN), MOE_BLOCK)
                return carry

            lax.fori_loop(0, n_big, big_body, 0)
            lax.fori_loop(0, n_small, small_body, 0)

            copy_segments(g, False)


def _moe(un, route, h1, w_up, w_down, layer, final_gain, final, tm):
    N = un.shape[0]
    n_tiles = min(MOE_SUPER, N // tm)
    n_super = N // (tm * n_tiles)
    slots = tm + N_GROUPS * MOE_ALIGN
    group_rows = n_tiles * slots + MOE_BIG
    utri = (jnp.arange(tm)[:, None] < jnp.arange(tm)[None, :]).astype(BF16)
    in_tile = lambda s, k: jnp.minimum(s, n_super - 1) * n_tiles + k
    out_tile = lambda s, k: jnp.maximum(s - 1, 0) * n_tiles + jnp.where(s > 0, k, 0)
    once = lambda a: pl.BlockSpec(a.shape, lambda s, k: (0,) * a.ndim, pipeline_mode=pl.Buffered(1))
    of_layer = lambda a: pl.BlockSpec((None,) + a.shape[1:], lambda s, k: (layer,) + (0,) * (a.ndim - 1),
                                      pipeline_mode=pl.Buffered(1))
    fg = final_gain.reshape(1, D_MODEL)
    return pl.pallas_call(
        functools.partial(_moe_kernel, final, n_super),
        out_shape=jax.ShapeDtypeStruct((N, D_MODEL), F32),
        grid=(n_super + 1, n_tiles),
        in_specs=[pl.BlockSpec((tm, D_MODEL), lambda s, k: (in_tile(s, k), 0)),
                  pl.BlockSpec((SUBLANES, tm), lambda s, k: (0, in_tile(s, k))),
                  pl.BlockSpec((tm, D_MODEL), lambda s, k: (out_tile(s, k), 0)),
                  once(utri), of_layer(w_up), of_layer(w_down), once(fg)],
        out_specs=pl.BlockSpec((tm, D_MODEL), lambda s, k: (out_tile(s, k), 0)),
        scratch_shapes=[pltpu.VMEM((n_tiles, slots, D_MODEL), BF16),
                        pltpu.VMEM((n_tiles, slots, 2 * SUBLANES), F32),
                        pltpu.VMEM((group_rows, D_MODEL), BF16),
                        pltpu.VMEM((group_rows, 2 * SUBLANES), F32),
                        pltpu.VMEM((n_tiles, tm, slots), BF16),
                        pltpu.SMEM((n_tiles * N_GROUPS * 2,), jnp.int32)],
        compiler_params=pltpu.CompilerParams(dimension_semantics=("arbitrary", "arbitrary"),
                                             vmem_limit_bytes=VMEM_LIMIT),
        name="moe",
    )(un, route, h1, utri, w_up, w_down, fg)


def kernel(x, mem, mix_norm, ffn_norm, mem_norm, final_norm, gla_w_in, gla_w_gate_up, gla_b_gate, gla_out_norm,
           hg_w_in, hg_lower_bounds, hg_out_norm, w_mem_kv, w_out, w_group, b_group, w_router, b_router,
           w_up, w_down):
    B, T, _ = x.shape
    N = B * T
    depth = mix_norm.shape[0]
    tm_proj = min(1024, T)
    nb_scan = 1
    tb_scan = tm_proj
    tm_out = min(1024, T)
    tm_moe = min(MOE_TILE, N)
    h = x.reshape(N, D_MODEL)
    w_up_bf16 = w_up.astype(BF16)
    w_down_bf16 = w_down.astype(BF16)
    for layer in range(depth):
        j = layer // 2
        if layer % 2 == 0:
            q, k, v, sg, b, mq, bmin = _gla_proj(h, mix_norm[layer], gla_w_in[j], gla_w_gate_up[j], gla_b_gate[j],
                                                 tm_proj)
            heads, dk, dv_pad, dv, out_gain = GLA_HEADS, GLA_DK, GLA_DV_PAD, GLA_DV, gla_out_norm[j]
        else:
            q, k, v, sg, b, mq, bmin = _hg_proj(h, mix_norm[layer], hg_w_in[j], hg_lower_bounds, layer, tm_proj)
            heads, dk, dv_pad, dv, out_gain = HG_HEADS, HG_DK, HG_DV, HG_DV, hg_out_norm[j]
        r3 = lambda a: a.reshape(B, T, a.shape[-1])
        bmin = bmin.reshape(B, T // tb_scan, SUBLANES, LANES)
        mix = _scan(r3(q), r3(k), r3(b), bmin, r3(v), r3(sg), out_gain, heads, dk, dv_pad, dv, tb_scan, nb_scan)
        mix = mix.reshape(N, heads * dv_pad)
        mem_k, mem_v = _mem_kv(mem, mem_norm[layer], w_mem_kv[layer])
        w_mix = _pad_heads(w_out[layer, :MIX_WIDTH].T, heads, dv, dv_pad).T
        h1, un, gates = _out_call(mix, mq, mem_k, mem_v, h, w_mix, w_out[layer, MIX_WIDTH:], ffn_norm[layer],
                                  w_group[layer], b_group[layer], w_router[layer], b_router[layer], T, tm_out)
        h = _moe(un, gates, h1, w_up_bf16, w_down_bf16, layer, final_norm, layer == depth - 1, tm_moe)
    return h.reshape(B, T, D_MODEL)
```

```python
import functools

import jax
import jax.numpy as jnp
from jax import lax
from jax.experimental import pallas as pl
from jax.experimental.pallas import tpu as pltpu

D_MODEL = 1024
N_MEM = 256
MIX_WIDTH = 768
MEM_HEADS = 4
MEM_HEAD_DIM = 64
MEM_WIDTH = 256
CHUNK = 64
GLA_HEADS = 4
GLA_KEY_WIDTH = 512
GLA_DK = 128
GLA_DV = 192
GLA_DV_PAD = 256
GLA_GATE_RANK = 16
GLA_GATE_NORMALIZER = 16.0
HG_HEADS = 6
HG_DK = 128
HG_DV = 128
N_GROUPS = 4
EXPERTS_PER_GROUP = 4
N_EXPERTS = 16
EXPERT_FF = 256
NORM_EPS = 1e-6
LANES = 128
SUBLANES = 8
MOE_BLOCK = 128
CUMSUM_SPAN = 256
MOE_ALIGN_SHIFT = 4
MOE_ALIGN = 1 << MOE_ALIGN_SHIFT
MOE_COPY_SHIFT = 6
MOE_COPY = 1 << MOE_COPY_SHIFT
MOE_TILE = 512
MOE_SUPER = 4
MOE_BIG = MOE_SUPER * (MOE_TILE // N_GROUPS + MOE_ALIGN)
MOE_SMALL_MAX = 2
ROUTE_GROUP_ROW = 4
ROUTE_ROWS = 32
SAFE_LOG_DECAY = -60.0
SAFE_JOIN_LOG_DECAY = -70.0
SCAN_JOINS = (4, 2, 1)
VMEM_LIMIT = 56 * 1024 * 1024

F32 = jnp.float32
BF16 = jnp.bfloat16


def _dot(a, b):
    return jnp.dot(a, b, preferred_element_type=F32)


def _dot_nt(a, b):
    return lax.dot_general(a, b, (((1,), (1,)), ((), ())), preferred_element_type=F32)


def _dot_tn(a, b):
    return lax.dot_general(a, b, (((0,), (0,)), ((), ())), preferred_element_type=F32)


def _split2(x):
    hi = x.astype(BF16)
    lo = (x - hi.astype(F32)).astype(BF16)
    return hi, lo


def _rms(x, gain):
    ms = jnp.mean(x * x, axis=-1, keepdims=True)
    return x * lax.rsqrt(ms + NORM_EPS) * gain


def _log_sigmoid(x):
    return jnp.minimum(x, 0.0) - jnp.log1p(jnp.exp(-jnp.abs(x)))


def _sigmoid(x):
    return 1.0 / (1.0 + jnp.exp(-x))


def _silu(x):
    return x * _sigmoid(x)


def _chunk_cumsum(tri_ref, x):
    tri = tri_ref[...]
    span = tri.shape[0]
    parts = []
    for r in range(x.shape[0] // span):
        hi, lo = _split2(x[r * span:(r + 1) * span])
        parts.append(_dot(tri, hi) + _dot(tri, lo))
    return jnp.concatenate(parts, axis=0) if len(parts) > 1 else parts[0]


def _mem_kv_kernel(mem_ref, gain_ref, w_ref, k_ref, v_ref):
    m = _rms(mem_ref[0], gain_ref[...]).astype(BF16)
    kv = _dot(m, w_ref[...])
    k_ref[0] = kv[:, :MEM_WIDTH].astype(BF16)
    v_ref[0] = kv[:, MEM_WIDTH:].astype(BF16)


def _mem_kv(mem, gain, w_kv):
    B = mem.shape[0]
    return pl.pallas_call(
        _mem_kv_kernel,
        out_shape=(jax.ShapeDtypeStruct((B, N_MEM, MEM_WIDTH), BF16),
                   jax.ShapeDtypeStruct((B, N_MEM, MEM_WIDTH), BF16)),
        grid=(B,),
        in_specs=[pl.BlockSpec((1, N_MEM, D_MODEL), lambda b: (b, 0, 0)),
                  pl.BlockSpec((1, D_MODEL), lambda b: (0, 0)),
                  pl.BlockSpec((D_MODEL, 2 * MEM_WIDTH), lambda b: (0, 0))],
        out_specs=(pl.BlockSpec((1, N_MEM, MEM_WIDTH), lambda b: (b, 0, 0)),
                   pl.BlockSpec((1, N_MEM, MEM_WIDTH), lambda b: (b, 0, 0))),
        compiler_params=pltpu.CompilerParams(dimension_semantics=("arbitrary",)),
        name="mem_kv",
    )(mem, gain.reshape(1, D_MODEL), w_kv.astype(BF16))


_GQ, _GK = 0, GLA_KEY_WIDTH
_GV = 2 * GLA_KEY_WIDTH
_GG = _GV + MIX_WIDTH
_GR = _GG + MIX_WIDTH
_GM = _GR + LANES
_GW = _GM + MEM_WIDTH


def _store_padded_heads(ref, x):
    zeros = jnp.zeros((x.shape[0], GLA_DV_PAD - GLA_DV), ref.dtype)
    for hd in range(GLA_HEADS):
        ref[:, hd * GLA_DV_PAD:hd * GLA_DV_PAD + GLA_DV] = x[:, hd * GLA_DV:(hd + 1) * GLA_DV].astype(ref.dtype)
        ref[:, hd * GLA_DV_PAD + GLA_DV:(hd + 1) * GLA_DV_PAD] = zeros


def _store_decay(b_ref, bmin_ref, b):
    b_ref[...] = b
    lowest = jnp.min(jnp.min(b, axis=0, keepdims=True), axis=1, keepdims=True)
    bmin_ref[0] = jnp.broadcast_to(lowest, (SUBLANES, LANES))


def _gla_proj_kernel(h_ref, gain_ref, w_ref, wg_ref, bg_ref, tri_ref,
                     q_ref, k_ref, v_ref, sg_ref, b_ref, mq_ref, bmin_ref):
    u = _rms(h_ref[...], gain_ref[...]).astype(BF16)
    q_ref[...] = (_dot(u, w_ref[:, _GQ:_GK]) * (GLA_DK ** -0.5)).astype(BF16)
    k_ref[...] = _dot(u, w_ref[:, _GK:_GV]).astype(BF16)
    _store_padded_heads(v_ref, _dot(u, w_ref[:, _GV:_GG]))
    _store_padded_heads(sg_ref, _silu(_dot(u, w_ref[:, _GG:_GR])))
    mq_ref[...] = _dot(u, w_ref[:, _GM:_GW]).astype(BF16)
    r = _dot(u, w_ref[:, _GR:_GM])
    r_hi = r.astype(BF16)
    lane = lax.broadcasted_iota(jnp.int32, r.shape, 1)
    is_lo = (lane >= GLA_GATE_RANK) & (lane < 2 * GLA_GATE_RANK)
    r_parts = jnp.where(is_lo, r - r_hi.astype(F32), r_hi.astype(F32)).astype(BF16)
    logit = _dot(r_parts, wg_ref[...]) + bg_ref[...]
    _store_decay(b_ref, bmin_ref, _chunk_cumsum(tri_ref, _log_sigmoid(logit) * (1.0 / GLA_GATE_NORMALIZER)))


_HQ, _HF, _HI, _HGG, _HM = 0, MIX_WIDTH, 2 * MIX_WIDTH, 3 * MIX_WIDTH, 4 * MIX_WIDTH
_HW = _HM + MEM_WIDTH


def _hg_proj_kernel(layer, h_ref, gain_ref, w_ref, lbp_ref, tri_ref,
                    q_ref, k_ref, v_ref, sg_ref, b_ref, mq_ref, bmin_ref):
    u = _rms(h_ref[...], gain_ref[...]).astype(BF16)
    p = lbp_ref[...]
    p = jnp.exp(p - jnp.max(p, axis=0, keepdims=True))
    p = p / jnp.sum(p, axis=0, keepdims=True)
    lb = jnp.sum(p[0:layer + 1], axis=0, keepdims=True) - p[0:1]
    q_ref[...] = _silu(_dot(u, w_ref[:, _HQ:_HF])).astype(BF16)
    z = _dot(u, w_ref[:, _HF:_HI])
    k_ref[...] = ((1.0 - lb) * _sigmoid(-z)).astype(BF16)
    v_ref[...] = _dot(u, w_ref[:, _HI:_HGG]).astype(BF16)
    sg_ref[...] = _silu(_dot(u, w_ref[:, _HGG:_HM])).astype(BF16)
    mq_ref[...] = _dot(u, w_ref[:, _HM:_HW]).astype(BF16)
    a = jnp.log(lb)
    c = jnp.log1p(-lb) + _log_sigmoid(z)
    log_f = jnp.maximum(a, c) + jnp.log1p(jnp.exp(-jnp.abs(a - c)))
    _store_decay(b_ref, bmin_ref, _chunk_cumsum(tri_ref, log_f))


def _proj_call(kernel, h, gain, w, extra, kw, vw, tm, name):
    N = h.shape[0]
    span = min(CUMSUM_SPAN, tm)
    tri = (jnp.arange(span)[:, None] >= jnp.arange(span)[None, :]) & (
        jnp.arange(span)[:, None] // CHUNK == jnp.arange(span)[None, :] // CHUNK)
    tri = tri.astype(BF16)
    row = lambda width: pl.BlockSpec((tm, width), lambda i: (i, 0))
    full = lambda a: pl.BlockSpec(a.shape, lambda i: (0,) * a.ndim)
    ins = [h, gain.reshape(1, D_MODEL), w] + list(extra) + [tri]
    return pl.pallas_call(
        kernel,
        out_shape=(jax.ShapeDtypeStruct((N, kw), BF16), jax.ShapeDtypeStruct((N, kw), BF16),
                   jax.ShapeDtypeStruct((N, vw), BF16), jax.ShapeDtypeStruct((N, vw), BF16),
                   jax.ShapeDtypeStruct((N, kw), F32), jax.ShapeDtypeStruct((N, MEM_WIDTH), BF16),
                   jax.ShapeDtypeStruct((N // tm, SUBLANES, LANES), F32)),
        grid=(N // tm,),
        in_specs=[row(D_MODEL)] + [full(a) for a in ins[1:]],
        out_specs=(row(kw), row(kw), row(vw), row(vw), row(kw), row(MEM_WIDTH),
                   pl.BlockSpec((1, SUBLANES, LANES), lambda i: (i, 0, 0))),
        compiler_params=pltpu.CompilerParams(dimension_semantics=("arbitrary",),
                                             vmem_limit_bytes=VMEM_LIMIT),
        name=name,
    )(*ins)


def _pad_heads(w, heads, dv, dv_pad):
    lead = w.shape[:-1]
    w = w.reshape(lead + (heads, dv))
    w = jnp.pad(w, [(0, 0)] * len(lead) + [(0, 0), (0, dv_pad - dv)])
    return w.reshape(lead + (heads * dv_pad,))


def _gla_proj(h, gain, w_in, w_gate_up, b_gate, tm):
    q, k, v, g, r, mq = jnp.split(
        w_in, [GLA_KEY_WIDTH, 2 * GLA_KEY_WIDTH, 2 * GLA_KEY_WIDTH + MIX_WIDTH,
               2 * GLA_KEY_WIDTH + 2 * MIX_WIDTH, 2 * GLA_KEY_WIDTH + 2 * MIX_WIDTH + GLA_GATE_RANK], axis=1)
    r3 = jnp.pad(jnp.concatenate([r, r, r], axis=1), ((0, 0), (0, LANES - 3 * GLA_GATE_RANK)))
    w = jnp.concatenate([q, k, v, g, r3, mq], axis=1).astype(BF16)
    wg_hi, wg_lo = _split2(w_gate_up)
    wg = jnp.pad(jnp.concatenate([wg_hi, wg_hi, wg_lo], axis=0), ((0, LANES - 3 * GLA_GATE_RANK), (0, 0)))
    return _proj_call(_gla_proj_kernel, h, gain, w, [wg, b_gate.reshape(1, GLA_KEY_WIDTH)],
                      GLA_KEY_WIDTH, GLA_HEADS * GLA_DV_PAD, tm, "gla_proj")


def _hg_proj(h, gain, w_in, lower_bound_params, layer, tm):
    return _proj_call(functools.partial(_hg_proj_kernel, layer), h, gain, w_in.astype(BF16),
                      [lower_bound_params], MIX_WIDTH, MIX_WIDTH, tm, "hg_proj")


def _scan_kernel(heads, dk, dv_pad, dv, n_chunks,
                 q_ref, k_ref, b_ref, bmin_ref, v_ref, sg_ref, gain_ref, o_ref, st_ref, kf_ref):
    @pl.when(pl.program_id(1) == 0)
    def _():
        st_ref[...] = jnp.zeros_like(st_ref)

    gain = gain_ref[...]
    col = lax.broadcasted_iota(jnp.int32, (CHUNK, CHUNK), 1)

    n_batch = q_ref.shape[0]

    def finish(bi, hd, rows, scores, qd, kl, eb_last):
        vc = slice(hd * dv_pad, (hd + 1) * dv_pad)
        v = v_ref[bi, rows, vc]
        st = st_ref[bi * heads + hd]
        out = _dot(scores.astype(BF16), v) + _dot_nt(qd, st.astype(BF16))
        st_ref[bi * heads + hd] = _dot_tn(v, kl) + st * eb_last
        ms = jnp.sum(out * out, axis=-1, keepdims=True) * (1.0 / dv)
        y = out * lax.rsqrt(ms + NORM_EPS) * gain * sg_ref[bi, rows, vc].astype(F32)
        o_ref[bi, rows, vc] = y.astype(BF16)

    def load(bi, hd, rows):
        kc = slice(hd * dk, (hd + 1) * dk)
        return (q_ref[bi, rows, kc].astype(F32), k_ref[bi, rows, kc].astype(F32), b_ref[bi, rows, kc])

    lowest = jnp.min(bmin_ref[...])

    def factored(n_join):
        span = n_join * CHUNK
        ri = lax.broadcasted_iota(jnp.int32, (span, span), 0)
        ci = lax.broadcasted_iota(jnp.int32, (span, span), 1)
        for c in range(n_chunks // n_join):
            rows = slice(c * span, (c + 1) * span)
            for bi in range(n_batch):
                for hd in range(heads):
                    q, k, b = load(bi, hd, rows)
                    parts = [b[:CHUNK]]
                    for i in range(1, n_join):
                        parts.append(b[i * CHUNK:(i + 1) * CHUNK] + parts[-1][CHUNK - 1:CHUNK])
                    b = jnp.concatenate(parts, axis=0) if n_join > 1 else b
                    eb = jnp.exp(b)
                    eb_last = eb[span - 1:span, :]
                    qd = (q * eb).astype(BF16)
                    kd = k * jnp.exp(-b)
                    scores = _dot_nt(qd, kd.astype(BF16))
                    finish(bi, hd, rows, jnp.where(ri >= ci, scores, 0.0), qd, (kd * eb_last).astype(BF16),
                           eb_last)

    joins = [j for j in SCAN_JOINS if n_chunks % j == 0]
    bounds = [SAFE_JOIN_LOG_DECAY / j if j > 1 else SAFE_LOG_DECAY for j in joins]
    taken = False
    for j, bound in zip(joins, bounds):
        ok = lowest >= bound
        pl.when(ok if taken is False else ok & jnp.logical_not(taken))(functools.partial(factored, j))
        taken = ok if taken is False else taken | ok
    all_safe = taken

    @pl.when(jnp.logical_not(all_safe))
    def _():
        for bi, hd in [(bi, hd) for bi in range(n_batch) for hd in range(heads)]:
            kc = slice(hd * dk, (hd + 1) * dk)

            def chunk_body(c, carry, bi=bi, hd=hd, kc=kc):
                start = pl.multiple_of(c * CHUNK, CHUNK)
                rows = pl.ds(start, CHUNK)
                q, k, b = load(bi, hd, rows)
                kf_ref[...] = k

                def col_body(j, sc):
                    base = pl.multiple_of((j >> 3) << 3, SUBLANES)
                    pick = lax.broadcasted_iota(jnp.int32, (SUBLANES, dk), 0) == (j & (SUBLANES - 1))
                    kj = jnp.sum(jnp.where(pick, kf_ref[pl.ds(base, SUBLANES), :], 0.0), axis=0, keepdims=True)
                    b8 = b_ref[bi, pl.ds(pl.multiple_of(start + base, SUBLANES), SUBLANES), kc]
                    bj = jnp.sum(jnp.where(pick, b8, 0.0), axis=0, keepdims=True)
                    rid = lax.broadcasted_iota(jnp.int32, (CHUNK, dk), 0)
                    dec = jnp.exp(jnp.where(rid >= j, b - bj, -jnp.inf))
                    colv = jnp.sum(q * kj * dec, axis=-1, keepdims=True)
                    return jnp.where(col == j, colv, sc)

                scores = lax.fori_loop(0, CHUNK, col_body, jnp.zeros((CHUNK, CHUNK), F32))
                b_last = b[CHUNK - 1:CHUNK, :]
                finish(bi, hd, rows, scores, (q * jnp.exp(b)).astype(BF16),
                       (k * jnp.exp(b_last - b)).astype(BF16), jnp.exp(b_last))
                return carry

            lax.fori_loop(0, n_chunks, chunk_body, 0)


def _scan(q, k, b, bmin, v, sg, gain, heads, dk, dv_pad, dv, tb, nb):
    B, T, kw = q.shape
    vw = v.shape[-1]
    blk = lambda w: pl.BlockSpec((nb, tb, w), lambda bi, ti: (bi, ti, 0))
    gain_row = jnp.pad(gain, (0, dv_pad - dv)).reshape(1, dv_pad)
    return pl.pallas_call(
        functools.partial(_scan_kernel, heads, dk, dv_pad, dv, tb // CHUNK),
        out_shape=jax.ShapeDtypeStruct((B, T, vw), BF16),
        grid=(B // nb, T // tb),
        in_specs=[blk(kw), blk(kw), blk(kw),
                  pl.BlockSpec((nb, 1, SUBLANES, LANES), lambda bi, ti: (bi, ti, 0, 0)),
                  blk(vw), blk(vw), pl.BlockSpec((1, dv_pad), lambda bi, ti: (0, 0))],
        out_specs=blk(vw),
        scratch_shapes=[pltpu.VMEM((nb * heads, dv_pad, dk), F32), pltpu.VMEM((CHUNK, dk), F32)],
        compiler_params=pltpu.CompilerParams(dimension_semantics=("arbitrary", "arbitrary"),
                                             vmem_limit_bytes=VMEM_LIMIT),
        name="chunk_scan",
    )(q, k, b, bmin, v, sg, gain_row)


def _out_kernel(mix_ref, mq_ref, mk_ref, mv_ref, h_ref, wmix_ref, wmem_ref, fgain_ref,
                wr_ref, br_ref, wup_ref, wdn_ref, h1_ref, un_ref, route_ref, wup_bf_ref, wdn_bf_ref):
    wup_bf_ref[...] = wup_ref[...].astype(BF16)
    wdn_bf_ref[...] = wdn_ref[...].astype(BF16)
    tm = mix_ref.shape[0]
    mk = mk_ref[0]
    mv = mv_ref[0]
    lane_w = lax.broadcasted_iota(jnp.int32, (1, MEM_WIDTH), 1)
    mq = mq_ref[...] * (MEM_HEAD_DIM ** -0.5)
    mem_o = jnp.zeros((tm, MEM_WIDTH), F32)
    for hd in range(MEM_HEADS):
        in_head = (lane_w >= hd * MEM_HEAD_DIM) & (lane_w < (hd + 1) * MEM_HEAD_DIM)
        s = _dot_nt(jnp.where(in_head, mq, jnp.zeros_like(mq)), mk)
        e = jnp.exp(s - jnp.max(s, axis=-1, keepdims=True))
        denom = jnp.sum(e, axis=-1, keepdims=True)
        mem_o = mem_o + _dot(e.astype(BF16), jnp.where(in_head, mv, jnp.zeros_like(mv))) * (1.0 / denom)
    h1 = h_ref[...] + _dot(mix_ref[...], wmix_ref[...]) + _dot(mem_o.astype(BF16), wmem_ref[...])
    h1_ref[...] = h1
    un = _rms(h1, fgain_ref[...])
    un_ref[...] = un.astype(BF16)

    both = _dot_nt(wr_ref[...], un.astype(BF16))
    lg = both[:ROUTE_ROWS] + both[ROUTE_ROWS:] + br_ref[...]
    row = lax.broadcasted_iota(jnp.int32, (ROUTE_ROWS, tm), 0)
    neg = -jnp.inf
    gl = jnp.where(row < N_GROUPS, lg, neg)
    gmax = jnp.max(gl, axis=0, keepdims=True)
    g_idx = jnp.min(jnp.where(gl == gmax, row, ROUTE_ROWS), axis=0, keepdims=True)
    g_w = 1.0 / jnp.sum(jnp.exp(gl - gmax), axis=0, keepdims=True)
    first = N_GROUPS + g_idx * EXPERTS_PER_GROUP
    el = jnp.where((row >= first) & (row < first + EXPERTS_PER_GROUP), lg, neg)
    t1 = jnp.max(el, axis=0, keepdims=True)
    i1 = jnp.min(jnp.where(el == t1, row, ROUTE_ROWS), axis=0, keepdims=True)
    el2 = jnp.where(row == i1, neg, el)
    t2 = jnp.max(el2, axis=0, keepdims=True)
    i2 = jnp.min(jnp.where(el2 == t2, row, ROUTE_ROWS), axis=0, keepdims=True)
    e2 = jnp.exp(t2 - t1)
    w1 = g_w / (1.0 + e2)
    w2 = g_w * e2 / (1.0 + e2)
    row8 = lax.broadcasted_iota(jnp.int32, (SUBLANES, tm), 0)
    route_ref[...] = (jnp.where(row8 == i1 - first, w1, 0.0) + jnp.where(row8 == i2 - first, w2, 0.0)
                      + jnp.where(row8 == ROUTE_GROUP_ROW, g_idx.astype(F32), 0.0))


def _out_call(mix, mq, mem_k, mem_v, h, w_mix, w_mem, ffn_gain, w_group, b_group, w_router, b_router,
              w_up, w_down, layer, T, tm):
    N = h.shape[0]
    tiles_per_batch = T // tm
    n_steps = N // tm
    per_step = max(1, N_EXPERTS // n_steps)
    assert N_EXPERTS % per_step == 0 and (n_steps * per_step) % N_EXPERTS == 0
    expert_blk = lambda i: i % (N_EXPERTS // per_step)
    w_in_spec = lambda a: pl.BlockSpec((None, per_step) + a.shape[2:], lambda i: (layer, expert_blk(i), 0, 0))
    w_out_spec = lambda a: pl.BlockSpec((per_step,) + a.shape[2:], lambda i: (expert_blk(i), 0, 0))
    pad = ROUTE_ROWS - N_GROUPS - N_EXPERTS
    wr = jnp.pad(jnp.concatenate([w_group, w_router], axis=1).T, ((0, pad), (0, 0)))
    br = jnp.broadcast_to(jnp.pad(jnp.concatenate([b_group, b_router]), (0, pad))[:, None], (ROUTE_ROWS, tm))
    wr_both = jnp.concatenate(_split2(wr), axis=0)
    row = lambda width: pl.BlockSpec((tm, width), lambda i: (i, 0))
    full = lambda a: pl.BlockSpec(a.shape, lambda i: (0,) * a.ndim)
    memspec = pl.BlockSpec((1, N_MEM, MEM_WIDTH), lambda i: (i // tiles_per_batch, 0, 0))
    w_mix = w_mix.astype(BF16)
    w_mem = w_mem.astype(BF16)
    fg = ffn_gain.reshape(1, D_MODEL)
    return pl.pallas_call(
        _out_kernel,
        out_shape=(jax.ShapeDtypeStruct((N, D_MODEL), F32), jax.ShapeDtypeStruct((N, D_MODEL), BF16),
                   jax.ShapeDtypeStruct((SUBLANES, N), F32),
                   jax.ShapeDtypeStruct(w_up.shape[1:], BF16), jax.ShapeDtypeStruct(w_down.shape[1:], BF16)),
        grid=(n_steps,),
        in_specs=[row(mix.shape[1]), row(MEM_WIDTH), memspec, memspec, row(D_MODEL),
                  full(w_mix), full(w_mem), full(fg), full(wr_both), full(br),
                  w_in_spec(w_up), w_in_spec(w_down)],
        out_specs=(row(D_MODEL), row(D_MODEL), pl.BlockSpec((SUBLANES, tm), lambda i: (0, i)),
                   w_out_spec(w_up), w_out_spec(w_down)),
        compiler_params=pltpu.CompilerParams(dimension_semantics=("arbitrary",),
                                             vmem_limit_bytes=VMEM_LIMIT),
        name="out_proj_route",
    )(mix, mq, mem_k, mem_v, h, w_mix, w_mem, fg, wr_both, br, w_up, w_down)


def _moe_kernel(final, n_super, un_ref, route_ref, h1_ref, utri_ref, wup_ref, wdn_ref, fgain_ref, o_ref,
                xs_ref, gs_ref, xg_ref, gg_ref, permt_ref, meta_ref):
    s = pl.program_id(0)
    k = pl.program_id(1)
    n_tiles, slots, _ = xs_ref.shape
    tm = un_ref.shape[0]
    row8 = lax.broadcasted_iota(jnp.int32, (SUBLANES, tm), 0)

    @pl.when((s == 0) & (k == 0))
    def _():
        xg_ref[...] = jnp.zeros_like(xg_ref)
        gg_ref[...] = jnp.zeros_like(gg_ref)

    def unpartition():
        y = h1_ref[...] + _dot(permt_ref[k], xs_ref[k])
        o_ref[...] = _rms(y, fgain_ref[...]) if final else y

    def partition():
        route = route_ref[...]
        g_idx = route[ROUTE_GROUP_ROW:ROUTE_GROUP_ROW + 1, :]
        member = jnp.where((row8 < N_GROUPS) & (row8.astype(F32) == g_idx), 1.0, 0.0)
        earlier = _dot(member.astype(BF16), utri_ref[...])
        rank = jnp.sum(member * earlier, axis=0, keepdims=True)
        count = jnp.sum(member, axis=1, keepdims=True)
        padded = jnp.floor((count + (MOE_ALIGN - 1.0)) * (1.0 / MOE_ALIGN)) * MOE_ALIGN
        starts = [jnp.zeros((1, 1), F32)]
        for g in range(1, N_GROUPS):
            starts.append(starts[-1] + padded[g - 1:g, :])
        seg = jnp.zeros((SUBLANES, 1), F32)
        row81 = lax.broadcasted_iota(jnp.int32, (SUBLANES, 1), 0)
        for g in range(1, N_GROUPS):
            seg = jnp.where(row81 == g, starts[g], seg)
        dest = rank + jnp.sum(member * seg, axis=0, keepdims=True)
        perm = jnp.where(lax.broadcasted_iota(jnp.int32, (slots, tm), 0) == dest.astype(jnp.int32),
                         1.0, 0.0).astype(BF16)
        d_hi = jnp.floor(dest * (1.0 / 32.0))
        d_lo = dest - 32.0 * d_hi
        digits = jnp.where(row8 == 0, d_hi, jnp.where(row8 == 1, d_lo, 0.0)).astype(BF16)
        pick = (lax.broadcasted_iota(jnp.int32, (SUBLANES, LANES), 0)
                == lax.broadcasted_iota(jnp.int32, (SUBLANES, LANES), 1)).astype(BF16)
        dig_c = _dot_tn(digits, pick)
        dest_col = (32.0 * dig_c[:, 0:1] + dig_c[:, 1:2]).astype(jnp.int32)
        permt_ref[k] = jnp.where(lax.broadcasted_iota(jnp.int32, (tm, slots), 1) == dest_col,
                                 1.0, 0.0).astype(BF16)
        xs_ref[k] = _dot(perm, un_ref[...]).astype(BF16)
        r_hi, r_lo = _split2(route)
        gs_ref[k] = _dot_nt(perm, jnp.concatenate([r_hi, r_lo], axis=0))
        for g in range(N_GROUPS):
            meta_ref[(k * N_GROUPS + g) * 2] = starts[g][0, 0].astype(jnp.int32)
            meta_ref[(k * N_GROUPS + g) * 2 + 1] = padded[g, 0].astype(jnp.int32)

    pl.when(s > 0)(unpartition)
    pl.when(s < n_super)(partition)

    def expert_block(g, r0, rows):
        xb = xg_ref[pl.ds(r0, rows), :]
        gsb = gg_ref[pl.ds(r0, rows), :]
        acc = jnp.zeros((rows, D_MODEL), F32)
        for j in range(EXPERTS_PER_GROUP):
            e = g * EXPERTS_PER_GROUP + j
            hh = _dot(xb, wup_ref[e])
            gate = gsb[:, j:j + 1] + gsb[:, SUBLANES + j:SUBLANES + j + 1]
            act = _silu(hh[:, :EXPERT_FF]) * hh[:, EXPERT_FF:] * gate
            acc = acc + _dot(act.astype(BF16), wdn_ref[e])
        xg_ref[pl.ds(r0, rows), :] = acc.astype(BF16)

    def copy_segments(g, gather):
        off = jnp.int32(0)
        for t in range(n_tiles):
            st = meta_ref[(t * N_GROUPS + g) * 2]
            ln = meta_ref[(t * N_GROUPS + g) * 2 + 1]

            def copy_rows(src0, dst0, rows, t=t):
                src = pl.ds(pl.multiple_of(src0, MOE_ALIGN), rows)
                dst = pl.ds(pl.multiple_of(dst0, MOE_ALIGN), rows)
                if gather:
                    xg_ref[dst, :] = xs_ref[t, src, :]
                    gg_ref[dst, :] = gs_ref[t, src, :]
                else:
                    xs_ref[t, src, :] = xg_ref[dst, :]

            def copy_body(i, carry, st=st, off=off, copy_rows=copy_rows):
                copy_rows(st + i * MOE_COPY, off + i * MOE_COPY, MOE_COPY)
                return carry

            n_copy = lax.shift_right_logical(ln, MOE_COPY_SHIFT)
            lax.fori_loop(0, n_copy, copy_body, 0)
            done = n_copy * MOE_COPY

            def tail_body(i, carry, st=st, off=off, done=done, copy_rows=copy_rows):
                copy_rows(st + done + i * MOE_ALIGN, off + done + i * MOE_ALIGN, MOE_ALIGN)
                return carry

            lax.fori_loop(0, lax.shift_right_logical(ln - done, MOE_ALIGN_SHIFT), tail_body, 0)
            off = off + ln
        return off

    @pl.when((s < n_super) & (k == n_tiles - 1))
    def _():
        for g in range(N_GROUPS):
            total = copy_segments(g, True)
            n_big = total // MOE_BIG
            rem = total - n_big * MOE_BIG
            more = rem > MOE_SMALL_MAX * MOE_BLOCK
            n_big = n_big + more.astype(jnp.int32)
            n_small = jnp.where(more, 0, (rem + (MOE_BLOCK - 1)) // MOE_BLOCK)
            small0 = n_big * MOE_BIG

            def big_body(i, carry, g=g):
                expert_block(g, pl.multiple_of(i * MOE_BIG, MOE_ALIGN), MOE_BIG)
                return carry

            def small_body(i, carry, g=g, small0=small0):
                expert_block(g, pl.multiple_of(small0 + i * MOE_BLOCK, MOE_ALIGN), MOE_BLOCK)
                return carry

            lax.fori_loop(0, n_big, big_body, 0)
            lax.fori_loop(0, n_small, small_body, 0)

            copy_segments(g, False)


def _moe(un, route, h1, w_up, w_down, final_gain, final, tm):
    N = un.shape[0]
    n_tiles = min(MOE_SUPER, N // tm)
    n_super = N // (tm * n_tiles)
    slots = tm + N_GROUPS * MOE_ALIGN
    group_rows = n_tiles * slots + MOE_BIG
    utri = (jnp.arange(tm)[:, None] < jnp.arange(tm)[None, :]).astype(BF16)
    in_tile = lambda s, k: jnp.minimum(s, n_super - 1) * n_tiles + k
    out_tile = lambda s, k: jnp.maximum(s - 1, 0) * n_tiles + jnp.where(s > 0, k, 0)
    once = lambda a: pl.BlockSpec(a.shape, lambda s, k: (0,) * a.ndim, pipeline_mode=pl.Buffered(1))
    fg = final_gain.reshape(1, D_MODEL)
    return pl.pallas_call(
        functools.partial(_moe_kernel, final, n_super),
        out_shape=jax.ShapeDtypeStruct((N, D_MODEL), F32),
        grid=(n_super + 1, n_tiles),
        in_specs=[pl.BlockSpec((tm, D_MODEL), lambda s, k: (in_tile(s, k), 0)),
                  pl.BlockSpec((SUBLANES, tm), lambda s, k: (0, in_tile(s, k))),
                  pl.BlockSpec((tm, D_MODEL), lambda s, k: (out_tile(s, k), 0)),
                  once(utri), once(w_up), once(w_down), once(fg)],
        out_specs=pl.BlockSpec((tm, D_MODEL), lambda s, k: (out_tile(s, k), 0)),
        scratch_shapes=[pltpu.VMEM((n_tiles, slots, D_MODEL), BF16),
                        pltpu.VMEM((n_tiles, slots, 2 * SUBLANES), F32),
                        pltpu.VMEM((group_rows, D_MODEL), BF16),
                        pltpu.VMEM((group_rows, 2 * SUBLANES), F32),
                        pltpu.VMEM((n_tiles, tm, slots), BF16),
                        pltpu.SMEM((n_tiles * N_GROUPS * 2,), jnp.int32)],
        compiler_params=pltpu.CompilerParams(dimension_semantics=("arbitrary", "arbitrary"),
                                             vmem_limit_bytes=VMEM_LIMIT),
        name="moe",
    )(un, route, h1, utri, w_up, w_down, fg)


def kernel(x, mem, mix_norm, ffn_norm, mem_norm, final_norm, gla_w_in, gla_w_gate_up, gla_b_gate, gla_out_norm,
           hg_w_in, hg_lower_bounds, hg_out_norm, w_mem_kv, w_out, w_group, b_group, w_router, b_router,
           w_up, w_down):
    B, T, _ = x.shape
    N = B * T
    depth = mix_norm.shape[0]
    tm_proj = min(1024, T)
    nb_scan = 1
    tb_scan = tm_proj
    tm_out = min(1024, T)
    tm_moe = min(MOE_TILE, N)
    h = x.reshape(N, D_MODEL)
    for layer in range(depth):
        j = layer // 2
        if layer % 2 == 0:
            q, k, v, sg, b, mq, bmin = _gla_proj(h, mix_norm[layer], gla_w_in[j], gla_w_gate_up[j], gla_b_gate[j],
                                                 tm_proj)
            heads, dk, dv_pad, dv, out_gain = GLA_HEADS, GLA_DK, GLA_DV_PAD, GLA_DV, gla_out_norm[j]
        else:
            q, k, v, sg, b, mq, bmin = _hg_proj(h, mix_norm[layer], hg_w_in[j], hg_lower_bounds, layer, tm_proj)
            heads, dk, dv_pad, dv, out_gain = HG_HEADS, HG_DK, HG_DV, HG_DV, hg_out_norm[j]
        r3 = lambda a: a.reshape(B, T, a.shape[-1])
        bmin = bmin.reshape(B, T // tb_scan, SUBLANES, LANES)
        mix = _scan(r3(q), r3(k), r3(b), bmin, r3(v), r3(sg), out_gain, heads, dk, dv_pad, dv, tb_scan, nb_scan)
        mix = mix.reshape(N, heads * dv_pad)
        mem_k, mem_v = _mem_kv(mem, mem_norm[layer], w_mem_kv[layer])
        w_mix = _pad_heads(w_out[layer, :MIX_WIDTH].T, heads, dv, dv_pad).T
        h1, un, gates, w_up_bf16, w_down_bf16 = _out_call(
            mix, mq, mem_k, mem_v, h, w_mix, w_out[layer, MIX_WIDTH:], ffn_norm[layer], w_group[layer],
            b_group[layer], w_router[layer], b_router[layer], w_up, w_down, layer, T, tm_out)
        h = _moe(un, gates, h1, w_up_bf16, w_down_bf16, final_norm, layer == depth - 1, tm_moe)
    return h.reshape(B, T, D_MODEL)
```

```python
import functools

import jax
import jax.numpy as jnp
from jax import lax
from jax.experimental import pallas as pl
from jax.experimental.pallas import tpu as pltpu

D_MODEL = 1024
N_MEM = 256
MIX_WIDTH = 768
MEM_HEADS = 4
MEM_HEAD_DIM = 64
MEM_WIDTH = 256
CHUNK = 64
GLA_HEADS = 4
GLA_KEY_WIDTH = 512
GLA_DK = 128
GLA_DV = 192
GLA_DV_PAD = 256
GLA_GATE_RANK = 16
GLA_GATE_NORMALIZER = 16.0
HG_HEADS = 6
HG_DK = 128
HG_DV = 128
N_GROUPS = 4
EXPERTS_PER_GROUP = 4
N_EXPERTS = 16
EXPERT_FF = 256
NORM_EPS = 1e-6
LANES = 128
SUBLANES = 8
MOE_BLOCK = 128
CUMSUM_SPAN = 256
MOE_ALIGN_SHIFT = 4
MOE_ALIGN = 1 << MOE_ALIGN_SHIFT
MOE_COPY_SHIFT = 6
MOE_COPY = 1 << MOE_COPY_SHIFT
MOE_TILE = 512
MOE_SUPER = 4
MOE_BIG = MOE_SUPER * (MOE_TILE // N_GROUPS + MOE_ALIGN)
MOE_SMALL_MAX = 2
ROUTE_GROUP_ROW = 4
ROUTE_ROWS = 32
SAFE_LOG_DECAY = -60.0
SAFE_JOIN_LOG_DECAY = -70.0
SCAN_JOINS = (4, 2, 1)
VMEM_LIMIT = 56 * 1024 * 1024

F32 = jnp.float32
BF16 = jnp.bfloat16


def _dot(a, b):
    return jnp.dot(a, b, preferred_element_type=F32)


def _dot_nt(a, b):
    return lax.dot_general(a, b, (((1,), (1,)), ((), ())), preferred_element_type=F32)


def _dot_tn(a, b):
    return lax.dot_general(a, b, (((0,), (0,)), ((), ())), preferred_element_type=F32)


def _split2(x):
    hi = x.astype(BF16)
    lo = (x - hi.astype(F32)).astype(BF16)
    return hi, lo


def _rms(x, gain):
    ms = jnp.mean(x * x, axis=-1, keepdims=True)
    return x * lax.rsqrt(ms + NORM_EPS) * gain


def _log_sigmoid(x):
    return jnp.minimum(x, 0.0) - jnp.log1p(jnp.exp(-jnp.abs(x)))


def _sigmoid(x):
    return 1.0 / (1.0 + jnp.exp(-x))


def _silu(x):
    return x * _sigmoid(x)


def _chunk_cumsum(tri_ref, x):
    tri = tri_ref[...]
    span = tri.shape[0]
    parts = []
    for r in range(x.shape[0] // span):
        hi, lo = _split2(x[r * span:(r + 1) * span])
        parts.append(_dot(tri, hi) + _dot(tri, lo))
    return jnp.concatenate(parts, axis=0) if len(parts) > 1 else parts[0]


def _mem_kv_kernel(mem_ref, gain_ref, w_ref, k_ref, v_ref):
    m = _rms(mem_ref[0], gain_ref[...]).astype(BF16)
    kv = _dot(m, w_ref[...])
    k_ref[0] = kv[:, :MEM_WIDTH].astype(BF16)
    v_ref[0] = kv[:, MEM_WIDTH:].astype(BF16)


def _mem_kv(mem, gain, w_kv):
    B = mem.shape[0]
    return pl.pallas_call(
        _mem_kv_kernel,
        out_shape=(jax.ShapeDtypeStruct((B, N_MEM, MEM_WIDTH), BF16),
                   jax.ShapeDtypeStruct((B, N_MEM, MEM_WIDTH), BF16)),
        grid=(B,),
        in_specs=[pl.BlockSpec((1, N_MEM, D_MODEL), lambda b: (b, 0, 0)),
                  pl.BlockSpec((1, D_MODEL), lambda b: (0, 0)),
                  pl.BlockSpec((D_MODEL, 2 * MEM_WIDTH), lambda b: (0, 0))],
        out_specs=(pl.BlockSpec((1, N_MEM, MEM_WIDTH), lambda b: (b, 0, 0)),
                   pl.BlockSpec((1, N_MEM, MEM_WIDTH), lambda b: (b, 0, 0))),
        compiler_params=pltpu.CompilerParams(dimension_semantics=("arbitrary",)),
        name="mem_kv",
    )(mem, gain.reshape(1, D_MODEL), w_kv.astype(BF16))


_GQ, _GK = 0, GLA_KEY_WIDTH
_GV = 2 * GLA_KEY_WIDTH
_GG = _GV + MIX_WIDTH
_GR = _GG + MIX_WIDTH
_GM = _GR + LANES
_GW = _GM + MEM_WIDTH


def _store_padded_heads(ref, x):
    zeros = jnp.zeros((x.shape[0], GLA_DV_PAD - GLA_DV), ref.dtype)
    for hd in range(GLA_HEADS):
        ref[:, hd * GLA_DV_PAD:hd * GLA_DV_PAD + GLA_DV] = x[:, hd * GLA_DV:(hd + 1) * GLA_DV].astype(ref.dtype)
        ref[:, hd * GLA_DV_PAD + GLA_DV:(hd + 1) * GLA_DV_PAD] = zeros


def _store_decay(b_ref, bmin_ref, b):
    b_ref[...] = b
    lowest = jnp.min(jnp.min(b, axis=0, keepdims=True), axis=1, keepdims=True)
    bmin_ref[0] = jnp.broadcast_to(lowest, (SUBLANES, LANES))


def _round_experts(wup_ref, wdn_ref, wup_bf_ref, wdn_bf_ref):
    wup_bf_ref[...] = wup_ref[...].astype(BF16)
    wdn_bf_ref[...] = wdn_ref[...].astype(BF16)


def _gla_proj_kernel(h_ref, gain_ref, w_ref, wg_ref, bg_ref, tri_ref, wup_ref, wdn_ref,
                     q_ref, k_ref, v_ref, sg_ref, b_ref, mq_ref, bmin_ref, wup_bf_ref, wdn_bf_ref):
    _round_experts(wup_ref, wdn_ref, wup_bf_ref, wdn_bf_ref)
    u = _rms(h_ref[...], gain_ref[...]).astype(BF16)
    q_ref[...] = (_dot(u, w_ref[:, _GQ:_GK]) * (GLA_DK ** -0.5)).astype(BF16)
    k_ref[...] = _dot(u, w_ref[:, _GK:_GV]).astype(BF16)
    _store_padded_heads(v_ref, _dot(u, w_ref[:, _GV:_GG]))
    _store_padded_heads(sg_ref, _silu(_dot(u, w_ref[:, _GG:_GR])))
    mq_ref[...] = _dot(u, w_ref[:, _GM:_GW]).astype(BF16)
    r = _dot(u, w_ref[:, _GR:_GM])
    r_hi = r.astype(BF16)
    lane = lax.broadcasted_iota(jnp.int32, r.shape, 1)
    is_lo = (lane >= GLA_GATE_RANK) & (lane < 2 * GLA_GATE_RANK)
    r_parts = jnp.where(is_lo, r - r_hi.astype(F32), r_hi.astype(F32)).astype(BF16)
    logit = _dot(r_parts, wg_ref[...]) + bg_ref[...]
    _store_decay(b_ref, bmin_ref, _chunk_cumsum(tri_ref, _log_sigmoid(logit) * (1.0 / GLA_GATE_NORMALIZER)))


_HQ, _HF, _HI, _HGG, _HM = 0, MIX_WIDTH, 2 * MIX_WIDTH, 3 * MIX_WIDTH, 4 * MIX_WIDTH
_HW = _HM + MEM_WIDTH


def _hg_proj_kernel(layer, h_ref, gain_ref, w_ref, lbp_ref, tri_ref, wup_ref, wdn_ref,
                    q_ref, k_ref, v_ref, sg_ref, b_ref, mq_ref, bmin_ref, wup_bf_ref, wdn_bf_ref):
    _round_experts(wup_ref, wdn_ref, wup_bf_ref, wdn_bf_ref)
    u = _rms(h_ref[...], gain_ref[...]).astype(BF16)
    p = lbp_ref[...]
    p = jnp.exp(p - jnp.max(p, axis=0, keepdims=True))
    p = p / jnp.sum(p, axis=0, keepdims=True)
    lb = jnp.sum(p[0:layer + 1], axis=0, keepdims=True) - p[0:1]
    q_ref[...] = _silu(_dot(u, w_ref[:, _HQ:_HF])).astype(BF16)
    z = _dot(u, w_ref[:, _HF:_HI])
    k_ref[...] = ((1.0 - lb) * _sigmoid(-z)).astype(BF16)
    v_ref[...] = _dot(u, w_ref[:, _HI:_HGG]).astype(BF16)
    sg_ref[...] = _silu(_dot(u, w_ref[:, _HGG:_HM])).astype(BF16)
    mq_ref[...] = _dot(u, w_ref[:, _HM:_HW]).astype(BF16)
    a = jnp.log(lb)
    c = jnp.log1p(-lb) + _log_sigmoid(z)
    log_f = jnp.maximum(a, c) + jnp.log1p(jnp.exp(-jnp.abs(a - c)))
    _store_decay(b_ref, bmin_ref, _chunk_cumsum(tri_ref, log_f))


def _proj_call(kernel, h, gain, w, extra, w_up, w_down, layer, kw, vw, tm, name):
    N = h.shape[0]
    n_steps = N // tm
    per_step = max(1, N_EXPERTS // n_steps)
    assert N_EXPERTS % per_step == 0 and (n_steps * per_step) % N_EXPERTS == 0
    expert_blk = lambda i: i % (N_EXPERTS // per_step)
    w_in_spec = lambda a: pl.BlockSpec((None, per_step) + a.shape[2:], lambda i: (layer, expert_blk(i), 0, 0))
    w_out_spec = lambda a: pl.BlockSpec((per_step,) + a.shape[2:], lambda i: (expert_blk(i), 0, 0))
    span = min(CUMSUM_SPAN, tm)
    tri = (jnp.arange(span)[:, None] >= jnp.arange(span)[None, :]) & (
        jnp.arange(span)[:, None] // CHUNK == jnp.arange(span)[None, :] // CHUNK)
    tri = tri.astype(BF16)
    row = lambda width: pl.BlockSpec((tm, width), lambda i: (i, 0))
    full = lambda a: pl.BlockSpec(a.shape, lambda i: (0,) * a.ndim)
    ins = [h, gain.reshape(1, D_MODEL), w] + list(extra) + [tri]
    return pl.pallas_call(
        kernel,
        out_shape=(jax.ShapeDtypeStruct((N, kw), BF16), jax.ShapeDtypeStruct((N, kw), BF16),
                   jax.ShapeDtypeStruct((N, vw), BF16), jax.ShapeDtypeStruct((N, vw), BF16),
                   jax.ShapeDtypeStruct((N, kw), F32), jax.ShapeDtypeStruct((N, MEM_WIDTH), BF16),
                   jax.ShapeDtypeStruct((N // tm, SUBLANES, LANES), F32),
                   jax.ShapeDtypeStruct(w_up.shape[1:], BF16), jax.ShapeDtypeStruct(w_down.shape[1:], BF16)),
        grid=(n_steps,),
        in_specs=[row(D_MODEL)] + [full(a) for a in ins[1:]] + [w_in_spec(w_up), w_in_spec(w_down)],
        out_specs=(row(kw), row(kw), row(vw), row(vw), row(kw), row(MEM_WIDTH),
                   pl.BlockSpec((1, SUBLANES, LANES), lambda i: (i, 0, 0)),
                   w_out_spec(w_up), w_out_spec(w_down)),
        compiler_params=pltpu.CompilerParams(dimension_semantics=("arbitrary",),
                                             vmem_limit_bytes=VMEM_LIMIT),
        name=name,
    )(*ins, w_up, w_down)


def _pad_heads(w, heads, dv, dv_pad):
    lead = w.shape[:-1]
    w = w.reshape(lead + (heads, dv))
    w = jnp.pad(w, [(0, 0)] * len(lead) + [(0, 0), (0, dv_pad - dv)])
    return w.reshape(lead + (heads * dv_pad,))


def _gla_proj(h, gain, w_in, w_gate_up, b_gate, w_up, w_down, layer, tm):
    q, k, v, g, r, mq = jnp.split(
        w_in, [GLA_KEY_WIDTH, 2 * GLA_KEY_WIDTH, 2 * GLA_KEY_WIDTH + MIX_WIDTH,
               2 * GLA_KEY_WIDTH + 2 * MIX_WIDTH, 2 * GLA_KEY_WIDTH + 2 * MIX_WIDTH + GLA_GATE_RANK], axis=1)
    r3 = jnp.pad(jnp.concatenate([r, r, r], axis=1), ((0, 0), (0, LANES - 3 * GLA_GATE_RANK)))
    w = jnp.concatenate([q, k, v, g, r3, mq], axis=1).astype(BF16)
    wg_hi, wg_lo = _split2(w_gate_up)
    wg = jnp.pad(jnp.concatenate([wg_hi, wg_hi, wg_lo], axis=0), ((0, LANES - 3 * GLA_GATE_RANK), (0, 0)))
    return _proj_call(_gla_proj_kernel, h, gain, w, [wg, b_gate.reshape(1, GLA_KEY_WIDTH)], w_up, w_down, layer,
                      GLA_KEY_WIDTH, GLA_HEADS * GLA_DV_PAD, tm, "gla_proj")


def _hg_proj(h, gain, w_in, lower_bound_params, w_up, w_down, layer, tm):
    return _proj_call(functools.partial(_hg_proj_kernel, layer), h, gain, w_in.astype(BF16),
                      [lower_bound_params], w_up, w_down, layer, MIX_WIDTH, MIX_WIDTH, tm, "hg_proj")


def _scan_kernel(heads, dk, dv_pad, dv, n_chunks,
                 q_ref, k_ref, b_ref, bmin_ref, v_ref, sg_ref, gain_ref, o_ref, st_ref, kf_ref):
    @pl.when(pl.program_id(1) == 0)
    def _():
        st_ref[...] = jnp.zeros_like(st_ref)

    gain = gain_ref[...]
    col = lax.broadcasted_iota(jnp.int32, (CHUNK, CHUNK), 1)

    n_batch = q_ref.shape[0]

    def finish(bi, hd, rows, scores, qd, kl, eb_last):
        vc = slice(hd * dv_pad, (hd + 1) * dv_pad)
        v = v_ref[bi, rows, vc]
        st = st_ref[bi * heads + hd]
        out = _dot(scores.astype(BF16), v) + _dot_nt(qd, st.astype(BF16))
        st_ref[bi * heads + hd] = _dot_tn(v, kl) + st * eb_last
        ms = jnp.sum(out * out, axis=-1, keepdims=True) * (1.0 / dv)
        y = out * lax.rsqrt(ms + NORM_EPS) * gain * sg_ref[bi, rows, vc].astype(F32)
        o_ref[bi, rows, vc] = y.astype(BF16)

    def load(bi, hd, rows):
        kc = slice(hd * dk, (hd + 1) * dk)
        return (q_ref[bi, rows, kc].astype(F32), k_ref[bi, rows, kc].astype(F32), b_ref[bi, rows, kc])

    lowest = jnp.min(bmin_ref[...])

    def factored(n_join):
        span = n_join * CHUNK
        ri = lax.broadcasted_iota(jnp.int32, (span, span), 0)
        ci = lax.broadcasted_iota(jnp.int32, (span, span), 1)
        for c in range(n_chunks // n_join):
            rows = slice(c * span, (c + 1) * span)
            for bi in range(n_batch):
                for hd in range(heads):
                    q, k, b = load(bi, hd, rows)
                    parts = [b[:CHUNK]]
                    for i in range(1, n_join):
                        parts.append(b[i * CHUNK:(i + 1) * CHUNK] + parts[-1][CHUNK - 1:CHUNK])
                    b = jnp.concatenate(parts, axis=0) if n_join > 1 else b
                    eb = jnp.exp(b)
                    eb_last = eb[span - 1:span, :]
                    qd = (q * eb).astype(BF16)
                    kd = k * jnp.exp(-b)
                    scores = _dot_nt(qd, kd.astype(BF16))
                    finish(bi, hd, rows, jnp.where(ri >= ci, scores, 0.0), qd, (kd * eb_last).astype(BF16),
                           eb_last)

    joins = [j for j in SCAN_JOINS if n_chunks % j == 0]
    bounds = [SAFE_JOIN_LOG_DECAY / j if j > 1 else SAFE_LOG_DECAY for j in joins]
    taken = False
    for j, bound in zip(joins, bounds):
        ok = lowest >= bound
        pl.when(ok if taken is False else ok & jnp.logical_not(taken))(functools.partial(factored, j))
        taken = ok if taken is False else taken | ok
    all_safe = taken

    @pl.when(jnp.logical_not(all_safe))
    def _():
        for bi, hd in [(bi, hd) for bi in range(n_batch) for hd in range(heads)]:
            kc = slice(hd * dk, (hd + 1) * dk)

            def chunk_body(c, carry, bi=bi, hd=hd, kc=kc):
                start = pl.multiple_of(c * CHUNK, CHUNK)
                rows = pl.ds(start, CHUNK)
                q, k, b = load(bi, hd, rows)
                kf_ref[...] = k

                def col_body(j, sc):
                    base = pl.multiple_of((j >> 3) << 3, SUBLANES)
                    pick = lax.broadcasted_iota(jnp.int32, (SUBLANES, dk), 0) == (j & (SUBLANES - 1))
                    kj = jnp.sum(jnp.where(pick, kf_ref[pl.ds(base, SUBLANES), :], 0.0), axis=0, keepdims=True)
                    b8 = b_ref[bi, pl.ds(pl.multiple_of(start + base, SUBLANES), SUBLANES), kc]
                    bj = jnp.sum(jnp.where(pick, b8, 0.0), axis=0, keepdims=True)
                    rid = lax.broadcasted_iota(jnp.int32, (CHUNK, dk), 0)
                    dec = jnp.exp(jnp.where(rid >= j, b - bj, -jnp.inf))
                    colv = jnp.sum(q * kj * dec, axis=-1, keepdims=True)
                    return jnp.where(col == j, colv, sc)

                scores = lax.fori_loop(0, CHUNK, col_body, jnp.zeros((CHUNK, CHUNK), F32))
                b_last = b[CHUNK - 1:CHUNK, :]
                finish(bi, hd, rows, scores, (q * jnp.exp(b)).astype(BF16),
                       (k * jnp.exp(b_last - b)).astype(BF16), jnp.exp(b_last))
                return carry

            lax.fori_loop(0, n_chunks, chunk_body, 0)


def _scan(q, k, b, bmin, v, sg, gain, heads, dk, dv_pad, dv, tb, nb):
    B, T, kw = q.shape
    vw = v.shape[-1]
    blk = lambda w: pl.BlockSpec((nb, tb, w), lambda bi, ti: (bi, ti, 0))
    gain_row = jnp.pad(gain, (0, dv_pad - dv)).reshape(1, dv_pad)
    return pl.pallas_call(
        functools.partial(_scan_kernel, heads, dk, dv_pad, dv, tb // CHUNK),
        out_shape=jax.ShapeDtypeStruct((B, T, vw), BF16),
        grid=(B // nb, T // tb),
        in_specs=[blk(kw), blk(kw), blk(kw),
                  pl.BlockSpec((nb, 1, SUBLANES, LANES), lambda bi, ti: (bi, ti, 0, 0)),
                  blk(vw), blk(vw), pl.BlockSpec((1, dv_pad), lambda bi, ti: (0, 0))],
        out_specs=blk(vw),
        scratch_shapes=[pltpu.VMEM((nb * heads, dv_pad, dk), F32), pltpu.VMEM((CHUNK, dk), F32)],
        compiler_params=pltpu.CompilerParams(dimension_semantics=("arbitrary", "arbitrary"),
                                             vmem_limit_bytes=VMEM_LIMIT),
        name="chunk_scan",
    )(q, k, b, bmin, v, sg, gain_row)


def _out_kernel(mix_ref, mq_ref, mk_ref, mv_ref, h_ref, wmix_ref, wmem_ref, fgain_ref,
                wr_ref, br_ref, h1_ref, un_ref, route_ref):
    tm = mix_ref.shape[0]
    mk = mk_ref[0]
    mv = mv_ref[0]
    lane_w = lax.broadcasted_iota(jnp.int32, (1, MEM_WIDTH), 1)
    mq = mq_ref[...] * (MEM_HEAD_DIM ** -0.5)
    mem_o = jnp.zeros((tm, MEM_WIDTH), F32)
    for hd in range(MEM_HEADS):
        in_head = (lane_w >= hd * MEM_HEAD_DIM) & (lane_w < (hd + 1) * MEM_HEAD_DIM)
        s = _dot_nt(jnp.where(in_head, mq, jnp.zeros_like(mq)), mk)
        e = jnp.exp(s - jnp.max(s, axis=-1, keepdims=True))
        denom = jnp.sum(e, axis=-1, keepdims=True)
        mem_o = mem_o + _dot(e.astype(BF16), jnp.where(in_head, mv, jnp.zeros_like(mv))) * (1.0 / denom)
    h1 = h_ref[...] + _dot(mix_ref[...], wmix_ref[...]) + _dot(mem_o.astype(BF16), wmem_ref[...])
    h1_ref[...] = h1
    un = _rms(h1, fgain_ref[...])
    un_ref[...] = un.astype(BF16)

    both = _dot_nt(wr_ref[...], un.astype(BF16))
    lg = both[:ROUTE_ROWS] + both[ROUTE_ROWS:] + br_ref[...]
    row = lax.broadcasted_iota(jnp.int32, (ROUTE_ROWS, tm), 0)
    neg = -jnp.inf
    gl = jnp.where(row < N_GROUPS, lg, neg)
    gmax = jnp.max(gl, axis=0, keepdims=True)
    g_idx = jnp.min(jnp.where(gl == gmax, row, ROUTE_ROWS), axis=0, keepdims=True)
    g_w = 1.0 / jnp.sum(jnp.exp(gl - gmax), axis=0, keepdims=True)
    first = N_GROUPS + g_idx * EXPERTS_PER_GROUP
    el = jnp.where((row >= first) & (row < first + EXPERTS_PER_GROUP), lg, neg)
    t1 = jnp.max(el, axis=0, keepdims=True)
    i1 = jnp.min(jnp.where(el == t1, row, ROUTE_ROWS), axis=0, keepdims=True)
    el2 = jnp.where(row == i1, neg, el)
    t2 = jnp.max(el2, axis=0, keepdims=True)
    i2 = jnp.min(jnp.where(el2 == t2, row, ROUTE_ROWS), axis=0, keepdims=True)
    e2 = jnp.exp(t2 - t1)
    w1 = g_w / (1.0 + e2)
    w2 = g_w * e2 / (1.0 + e2)
    row8 = lax.broadcasted_iota(jnp.int32, (SUBLANES, tm), 0)
    route_ref[...] = (jnp.where(row8 == i1 - first, w1, 0.0) + jnp.where(row8 == i2 - first, w2, 0.0)
                      + jnp.where(row8 == ROUTE_GROUP_ROW, g_idx.astype(F32), 0.0))


def _out_call(mix, mq, mem_k, mem_v, h, w_mix, w_mem, ffn_gain, w_group, b_group, w_router, b_router, T, tm):
    N = h.shape[0]
    tiles_per_batch = T // tm
    pad = ROUTE_ROWS - N_GROUPS - N_EXPERTS
    wr = jnp.pad(jnp.concatenate([w_group, w_router], axis=1).T, ((0, pad), (0, 0)))
    br = jnp.broadcast_to(jnp.pad(jnp.concatenate([b_group, b_router]), (0, pad))[:, None], (ROUTE_ROWS, tm))
    wr_both = jnp.concatenate(_split2(wr), axis=0)
    row = lambda width: pl.BlockSpec((tm, width), lambda i: (i, 0))
    full = lambda a: pl.BlockSpec(a.shape, lambda i: (0,) * a.ndim)
    memspec = pl.BlockSpec((1, N_MEM, MEM_WIDTH), lambda i: (i // tiles_per_batch, 0, 0))
    w_mix = w_mix.astype(BF16)
    w_mem = w_mem.astype(BF16)
    fg = ffn_gain.reshape(1, D_MODEL)
    return pl.pallas_call(
        _out_kernel,
        out_shape=(jax.ShapeDtypeStruct((N, D_MODEL), F32), jax.ShapeDtypeStruct((N, D_MODEL), BF16),
                   jax.ShapeDtypeStruct((SUBLANES, N), F32)),
        grid=(N // tm,),
        in_specs=[row(mix.shape[1]), row(MEM_WIDTH), memspec, memspec, row(D_MODEL),
                  full(w_mix), full(w_mem), full(fg), full(wr_both), full(br)],
        out_specs=(row(D_MODEL), row(D_MODEL), pl.BlockSpec((SUBLANES, tm), lambda i: (0, i))),
        compiler_params=pltpu.CompilerParams(dimension_semantics=("arbitrary",),
                                             vmem_limit_bytes=VMEM_LIMIT),
        name="out_proj_route",
    )(mix, mq, mem_k, mem_v, h, w_mix, w_mem, fg, wr_both, br)


def _moe_kernel(final, n_super, un_ref, route_ref, h1_ref, utri_ref, wup_ref, wdn_ref, fgain_ref, o_ref,
                xs_ref, gs_ref, xg_ref, gg_ref, permt_ref, meta_ref):
    s = pl.program_id(0)
    k = pl.program_id(1)
    n_tiles, slots, _ = xs_ref.shape
    tm = un_ref.shape[0]
    row8 = lax.broadcasted_iota(jnp.int32, (SUBLANES, tm), 0)

    @pl.when((s == 0) & (k == 0))
    def _():
        xg_ref[...] = jnp.zeros_like(xg_ref)
        gg_ref[...] = jnp.zeros_like(gg_ref)

    def unpartition():
        y = h1_ref[...] + _dot(permt_ref[k], xs_ref[k])
        o_ref[...] = _rms(y, fgain_ref[...]) if final else y

    def partition():
        route = route_ref[...]
        g_idx = route[ROUTE_GROUP_ROW:ROUTE_GROUP_ROW + 1, :]
        member = jnp.where((row8 < N_GROUPS) & (row8.astype(F32) == g_idx), 1.0, 0.0)
        earlier = _dot(member.astype(BF16), utri_ref[...])
        rank = jnp.sum(member * earlier, axis=0, keepdims=True)
        count = jnp.sum(member, axis=1, keepdims=True)
        padded = jnp.floor((count + (MOE_ALIGN - 1.0)) * (1.0 / MOE_ALIGN)) * MOE_ALIGN
        starts = [jnp.zeros((1, 1), F32)]
        for g in range(1, N_GROUPS):
            starts.append(starts[-1] + padded[g - 1:g, :])
        seg = jnp.zeros((SUBLANES, 1), F32)
        row81 = lax.broadcasted_iota(jnp.int32, (SUBLANES, 1), 0)
        for g in range(1, N_GROUPS):
            seg = jnp.where(row81 == g, starts[g], seg)
        dest = rank + jnp.sum(member * seg, axis=0, keepdims=True)
        perm = jnp.where(lax.broadcasted_iota(jnp.int32, (slots, tm), 0) == dest.astype(jnp.int32),
                         1.0, 0.0).astype(BF16)
        d_hi = jnp.floor(dest * (1.0 / 32.0))
        d_lo = dest - 32.0 * d_hi
        digits = jnp.where(row8 == 0, d_hi, jnp.where(row8 == 1, d_lo, 0.0)).astype(BF16)
        pick = (lax.broadcasted_iota(jnp.int32, (SUBLANES, LANES), 0)
                == lax.broadcasted_iota(jnp.int32, (SUBLANES, LANES), 1)).astype(BF16)
        dig_c = _dot_tn(digits, pick)
        dest_col = (32.0 * dig_c[:, 0:1] + dig_c[:, 1:2]).astype(jnp.int32)
        permt_ref[k] = jnp.where(lax.broadcasted_iota(jnp.int32, (tm, slots), 1) == dest_col,
                                 1.0, 0.0).astype(BF16)
        xs_ref[k] = _dot(perm, un_ref[...]).astype(BF16)
        r_hi, r_lo = _split2(route)
        gs_ref[k] = _dot_nt(perm, jnp.concatenate([r_hi, r_lo], axis=0))
        for g in range(N_GROUPS):
            meta_ref[(k * N_GROUPS + g) * 2] = starts[g][0, 0].astype(jnp.int32)
            meta_ref[(k * N_GROUPS + g) * 2 + 1] = padded[g, 0].astype(jnp.int32)

    pl.when(s > 0)(unpartition)
    pl.when(s < n_super)(partition)

    def expert_block(g, r0, rows):
        xb = xg_ref[pl.ds(r0, rows), :]
        gsb = gg_ref[pl.ds(r0, rows), :]
        acc = jnp.zeros((rows, D_MODEL), F32)
        for j in range(EXPERTS_PER_GROUP):
            e = g * EXPERTS_PER_GROUP + j
            hh = _dot(xb, wup_ref[e])
            gate = gsb[:, j:j + 1] + gsb[:, SUBLANES + j:SUBLANES + j + 1]
            act = _silu(hh[:, :EXPERT_FF]) * hh[:, EXPERT_FF:] * gate
            acc = acc + _dot(act.astype(BF16), wdn_ref[e])
        xg_ref[pl.ds(r0, rows), :] = acc.astype(BF16)

    def copy_segments(g, gather):
        off = jnp.int32(0)
        for t in range(n_tiles):
            st = meta_ref[(t * N_GROUPS + g) * 2]
            ln = meta_ref[(t * N_GROUPS + g) * 2 + 1]

            def copy_rows(src0, dst0, rows, t=t):
                src = pl.ds(pl.multiple_of(src0, MOE_ALIGN), rows)
                dst = pl.ds(pl.multiple_of(dst0, MOE_ALIGN), rows)
                if gather:
                    xg_ref[dst, :] = xs_ref[t, src, :]
                    gg_ref[dst, :] = gs_ref[t, src, :]
                else:
                    xs_ref[t, src, :] = xg_ref[dst, :]

            def copy_body(i, carry, st=st, off=off, copy_rows=copy_rows):
                copy_rows(st + i * MOE_COPY, off + i * MOE_COPY, MOE_COPY)
                return carry

            n_copy = lax.shift_right_logical(ln, MOE_COPY_SHIFT)
            lax.fori_loop(0, n_copy, copy_body, 0)
            done = n_copy * MOE_COPY

            def tail_body(i, carry, st=st, off=off, done=done, copy_rows=copy_rows):
                copy_rows(st + done + i * MOE_ALIGN, off + done + i * MOE_ALIGN, MOE_ALIGN)
                return carry

            lax.fori_loop(0, lax.shift_right_logical(ln - done, MOE_ALIGN_SHIFT), tail_body, 0)
            off = off + ln
        return off

    @pl.when((s < n_super) & (k == n_tiles - 1))
    def _():
        for g in range(N_GROUPS):
            total = copy_segments(g, True)
            n_big = total // MOE_BIG
            rem = total - n_big * MOE_BIG
            more = rem > MOE_SMALL_MAX * MOE_BLOCK
            n_big = n_big + more.astype(jnp.int32)
            n_small = jnp.where(more, 0, (rem + (MOE_BLOCK - 1)) // MOE_BLOCK)
            small0 = n_big * MOE_BIG

            def big_body(i, carry, g=g):
                expert_block(g, pl.multiple_of(i * MOE_BIG, MOE_ALIGN), MOE_BIG)
                return carry

            def small_body(i, carry, g=g, small0=small0):
                expert_block(g, pl.multiple_of(small0 + i * MOE_BLOCK, MOE_ALIGN), MOE_BLOCK)
                return carry

            lax.fori_loop(0, n_big, big_body, 0)
            lax.fori_loop(0, n_small, small_body, 0)

            copy_segments(g, False)


def _moe(un, route, h1, w_up, w_down, final_gain, final, tm):
    N = un.shape[0]
    n_tiles = min(MOE_SUPER, N // tm)
    n_super = N // (tm * n_tiles)
    slots = tm + N_GROUPS * MOE_ALIGN
    group_rows = n_tiles * slots + MOE_BIG
    utri = (jnp.arange(tm)[:, None] < jnp.arange(tm)[None, :]).astype(BF16)
    in_tile = lambda s, k: jnp.minimum(s, n_super - 1) * n_tiles + k
    out_tile = lambda s, k: jnp.maximum(s - 1, 0) * n_tiles + jnp.where(s > 0, k, 0)
    once = lambda a: pl.BlockSpec(a.shape, lambda s, k: (0,) * a.ndim, pipeline_mode=pl.Buffered(1))
    fg = final_gain.reshape(1, D_MODEL)
    return pl.pallas_call(
        functools.partial(_moe_kernel, final, n_super),
        out_shape=jax.ShapeDtypeStruct((N, D_MODEL), F32),
        grid=(n_super + 1, n_tiles),
        in_specs=[pl.BlockSpec((tm, D_MODEL), lambda s, k: (in_tile(s, k), 0)),
                  pl.BlockSpec((SUBLANES, tm), lambda s, k: (0, in_tile(s, k))),
                  pl.BlockSpec((tm, D_MODEL), lambda s, k: (out_tile(s, k), 0)),
                  once(utri), once(w_up), once(w_down), once(fg)],
        out_specs=pl.BlockSpec((tm, D_MODEL), lambda s, k: (out_tile(s, k), 0)),
        scratch_shapes=[pltpu.VMEM((n_tiles, slots, D_MODEL), BF16),
                        pltpu.VMEM((n_tiles, slots, 2 * SUBLANES), F32),
                        pltpu.VMEM((group_rows, D_MODEL), BF16),
                        pltpu.VMEM((group_rows, 2 * SUBLANES), F32),
                        pltpu.VMEM((n_tiles, tm, slots), BF16),
                        pltpu.SMEM((n_tiles * N_GROUPS * 2,), jnp.int32)],
        compiler_params=pltpu.CompilerParams(dimension_semantics=("arbitrary", "arbitrary"),
                                             vmem_limit_bytes=VMEM_LIMIT),
        name="moe",
    )(un, route, h1, utri, w_up, w_down, fg)


def kernel(x, mem, mix_norm, ffn_norm, mem_norm, final_norm, gla_w_in, gla_w_gate_up, gla_b_gate, gla_out_norm,
           hg_w_in, hg_lower_bounds, hg_out_norm, w_mem_kv, w_out, w_group, b_group, w_router, b_router,
           w_up, w_down):
    B, T, _ = x.shape
    N = B * T
    depth = mix_norm.shape[0]
    tm_proj = min(1024, T)
    nb_scan = 1
    tb_scan = tm_proj
    tm_out = min(1024, T)
    tm_moe = min(MOE_TILE, N)
    h = x.reshape(N, D_MODEL)
    for layer in range(depth):
        j = layer // 2
        if layer % 2 == 0:
            q, k, v, sg, b, mq, bmin, w_up_bf16, w_down_bf16 = _gla_proj(
                h, mix_norm[layer], gla_w_in[j], gla_w_gate_up[j], gla_b_gate[j], w_up, w_down, layer, tm_proj)
            heads, dk, dv_pad, dv, out_gain = GLA_HEADS, GLA_DK, GLA_DV_PAD, GLA_DV, gla_out_norm[j]
        else:
            q, k, v, sg, b, mq, bmin, w_up_bf16, w_down_bf16 = _hg_proj(
                h, mix_norm[layer], hg_w_in[j], hg_lower_bounds, w_up, w_down, layer, tm_proj)
            heads, dk, dv_pad, dv, out_gain = HG_HEADS, HG_DK, HG_DV, HG_DV, hg_out_norm[j]
        r3 = lambda a: a.reshape(B, T, a.shape[-1])
        bmin = bmin.reshape(B, T // tb_scan, SUBLANES, LANES)
        mix = _scan(r3(q), r3(k), r3(b), bmin, r3(v), r3(sg), out_gain, heads, dk, dv_pad, dv, tb_scan, nb_scan)
        mix = mix.reshape(N, heads * dv_pad)
        mem_k, mem_v = _mem_kv(mem, mem_norm[layer], w_mem_kv[layer])
        w_mix = _pad_heads(w_out[layer, :MIX_WIDTH].T, heads, dv, dv_pad).T
        h1, un, gates = _out_call(mix, mq, mem_k, mem_v, h, w_mix, w_out[layer, MIX_WIDTH:], ffn_norm[layer],
                                  w_group[layer], b_group[layer], w_router[layer], b_router[layer], T, tm_out)
        h = _moe(un, gates, h1, w_up_bf16, w_down_bf16, final_norm, layer == depth - 1, tm_moe)
    return h.reshape(B, T, D_MODEL)
```

```python
import functools

import jax
import jax.numpy as jnp
from jax import lax
from jax.experimental import pallas as pl
from jax.experimental.pallas import tpu as pltpu

D_MODEL = 1024
N_MEM = 256
MIX_WIDTH = 768
MEM_HEADS = 4
MEM_HEAD_DIM = 64
MEM_WIDTH = 256
CHUNK = 64
GLA_HEADS = 4
GLA_KEY_WIDTH = 512
GLA_DK = 128
GLA_DV = 192
GLA_DV_PAD = 256
GLA_GATE_RANK = 16
GLA_GATE_NORMALIZER = 16.0
HG_HEADS = 6
HG_DK = 128
HG_DV = 128
N_GROUPS = 4
EXPERTS_PER_GROUP = 4
N_EXPERTS = 16
EXPERT_FF = 256
NORM_EPS = 1e-6
LANES = 128
SUBLANES = 8
MOE_BLOCK = 128
CUMSUM_SPAN = 256
MOE_ALIGN_SHIFT = 4
MOE_ALIGN = 1 << MOE_ALIGN_SHIFT
MOE_COPY_SHIFT = 6
MOE_COPY = 1 << MOE_COPY_SHIFT
MOE_TILE = 512
MOE_SUPER = 4
MOE_BIG = MOE_SUPER * (MOE_TILE // N_GROUPS + MOE_ALIGN)
MOE_SMALL_MAX = 2
ROUTE_GROUP_ROW = 4
ROUTE_ROWS = 32
SAFE_LOG_DECAY = -60.0
SAFE_JOIN_LOG_DECAY = -70.0
SCAN_JOINS = (4, 2, 1)
VMEM_LIMIT = 56 * 1024 * 1024

F32 = jnp.float32
BF16 = jnp.bfloat16


def _dot(a, b):
    return jnp.dot(a, b, preferred_element_type=F32)


def _dot_nt(a, b):
    return lax.dot_general(a, b, (((1,), (1,)), ((), ())), preferred_element_type=F32)


def _dot_tn(a, b):
    return lax.dot_general(a, b, (((0,), (0,)), ((), ())), preferred_element_type=F32)


def _split2(x):
    hi = x.astype(BF16)
    lo = (x - hi.astype(F32)).astype(BF16)
    return hi, lo


def _rms(x, gain):
    ms = jnp.mean(x * x, axis=-1, keepdims=True)
    return x * lax.rsqrt(ms + NORM_EPS) * gain


def _log_sigmoid(x):
    return jnp.minimum(x, 0.0) - jnp.log1p(jnp.exp(-jnp.abs(x)))


def _sigmoid(x):
    return 1.0 / (1.0 + jnp.exp(-x))


def _silu(x):
    return x * _sigmoid(x)


def _chunk_cumsum(tri_ref, x):
    tri = tri_ref[...]
    span = tri.shape[0]
    parts = []
    for r in range(x.shape[0] // span):
        hi, lo = _split2(x[r * span:(r + 1) * span])
        parts.append(_dot(tri, hi) + _dot(tri, lo))
    return jnp.concatenate(parts, axis=0) if len(parts) > 1 else parts[0]


def _mem_kv_kernel(mem_ref, gain_ref, w_ref, k_ref, v_ref):
    m = _rms(mem_ref[0], gain_ref[...]).astype(BF16)
    kv = _dot(m, w_ref[...])
    k_ref[0] = kv[:, :MEM_WIDTH].astype(BF16)
    v_ref[0] = kv[:, MEM_WIDTH:].astype(BF16)


def _mem_kv(mem, gain, w_kv):
    B = mem.shape[0]
    return pl.pallas_call(
        _mem_kv_kernel,
        out_shape=(jax.ShapeDtypeStruct((B, N_MEM, MEM_WIDTH), BF16),
                   jax.ShapeDtypeStruct((B, N_MEM, MEM_WIDTH), BF16)),
        grid=(B,),
        in_specs=[pl.BlockSpec((1, N_MEM, D_MODEL), lambda b: (b, 0, 0)),
                  pl.BlockSpec((1, D_MODEL), lambda b: (0, 0)),
                  pl.BlockSpec((D_MODEL, 2 * MEM_WIDTH), lambda b: (0, 0))],
        out_specs=(pl.BlockSpec((1, N_MEM, MEM_WIDTH), lambda b: (b, 0, 0)),
                   pl.BlockSpec((1, N_MEM, MEM_WIDTH), lambda b: (b, 0, 0))),
        compiler_params=pltpu.CompilerParams(dimension_semantics=("arbitrary",)),
        name="mem_kv",
    )(mem, gain.reshape(1, D_MODEL), w_kv.astype(BF16))


_GQ, _GK = 0, GLA_KEY_WIDTH
_GV = 2 * GLA_KEY_WIDTH
_GG = _GV + MIX_WIDTH
_GR = _GG + MIX_WIDTH
_GM = _GR + LANES
_GW = _GM + MEM_WIDTH


def _store_padded_heads(ref, x):
    zeros = jnp.zeros((x.shape[0], GLA_DV_PAD - GLA_DV), ref.dtype)
    for hd in range(GLA_HEADS):
        ref[:, hd * GLA_DV_PAD:hd * GLA_DV_PAD + GLA_DV] = x[:, hd * GLA_DV:(hd + 1) * GLA_DV].astype(ref.dtype)
        ref[:, hd * GLA_DV_PAD + GLA_DV:(hd + 1) * GLA_DV_PAD] = zeros


def _store_decay(b_ref, bmin_ref, b):
    b_ref[...] = b
    lowest = jnp.min(jnp.min(b, axis=0, keepdims=True), axis=1, keepdims=True)
    bmin_ref[0] = jnp.broadcast_to(lowest, (SUBLANES, LANES))


def _round_experts(wup_ref, wdn_ref, wup_bf_ref, wdn_bf_ref):
    wup_bf_ref[...] = wup_ref[...].astype(BF16)
    wdn_bf_ref[...] = wdn_ref[...].astype(BF16)


def _gla_proj_kernel(h_ref, gain_ref, w_ref, wg_ref, bg_ref, tri_ref, wup_ref, wdn_ref,
                     q_ref, k_ref, v_ref, sg_ref, b_ref, mq_ref, bmin_ref, wup_bf_ref, wdn_bf_ref):
    _round_experts(wup_ref, wdn_ref, wup_bf_ref, wdn_bf_ref)
    u = _rms(h_ref[...], gain_ref[...]).astype(BF16)
    q_ref[...] = (_dot(u, w_ref[:, _GQ:_GK]) * (GLA_DK ** -0.5)).astype(BF16)
    k_ref[...] = _dot(u, w_ref[:, _GK:_GV]).astype(BF16)
    _store_padded_heads(v_ref, _dot(u, w_ref[:, _GV:_GG]))
    _store_padded_heads(sg_ref, _silu(_dot(u, w_ref[:, _GG:_GR])))
    mq_ref[...] = _dot(u, w_ref[:, _GM:_GW]).astype(BF16)
    r = _dot(u, w_ref[:, _GR:_GM])
    r_hi = r.astype(BF16)
    lane = lax.broadcasted_iota(jnp.int32, r.shape, 1)
    is_lo = (lane >= GLA_GATE_RANK) & (lane < 2 * GLA_GATE_RANK)
    r_parts = jnp.where(is_lo, r - r_hi.astype(F32), r_hi.astype(F32)).astype(BF16)
    logit = _dot(r_parts, wg_ref[...]) + bg_ref[...]
    _store_decay(b_ref, bmin_ref, _chunk_cumsum(tri_ref, _log_sigmoid(logit) * (1.0 / GLA_GATE_NORMALIZER)))


_HQ, _HF, _HI, _HGG, _HM = 0, MIX_WIDTH, 2 * MIX_WIDTH, 3 * MIX_WIDTH, 4 * MIX_WIDTH
_HW = _HM + MEM_WIDTH


def _hg_proj_kernel(layer, h_ref, gain_ref, w_ref, lbp_ref, tri_ref, wup_ref, wdn_ref,
                    q_ref, k_ref, v_ref, sg_ref, b_ref, mq_ref, bmin_ref, wup_bf_ref, wdn_bf_ref):
    _round_experts(wup_ref, wdn_ref, wup_bf_ref, wdn_bf_ref)
    u = _rms(h_ref[...], gain_ref[...]).astype(BF16)
    p = lbp_ref[...]
    p = jnp.exp(p - jnp.max(p, axis=0, keepdims=True))
    p = p / jnp.sum(p, axis=0, keepdims=True)
    lb = jnp.sum(p[0:layer + 1], axis=0, keepdims=True) - p[0:1]
    q_ref[...] = _silu(_dot(u, w_ref[:, _HQ:_HF])).astype(BF16)
    z = _dot(u, w_ref[:, _HF:_HI])
    k_ref[...] = ((1.0 - lb) * _sigmoid(-z)).astype(BF16)
    v_ref[...] = _dot(u, w_ref[:, _HI:_HGG]).astype(BF16)
    sg_ref[...] = _silu(_dot(u, w_ref[:, _HGG:_HM])).astype(BF16)
    mq_ref[...] = _dot(u, w_ref[:, _HM:_HW]).astype(BF16)
    a = jnp.log(lb)
    c = jnp.log1p(-lb) + _log_sigmoid(z)
    log_f = jnp.maximum(a, c) + jnp.log1p(jnp.exp(-jnp.abs(a - c)))
    _store_decay(b_ref, bmin_ref, _chunk_cumsum(tri_ref, log_f))


def _proj_call(kernel, h, gain, w, extra, w_up, w_down, layer, kw, vw, tm, name):
    N = h.shape[0]
    n_steps = N // tm
    per_step = max(1, N_EXPERTS // n_steps)
    assert N_EXPERTS % per_step == 0 and (n_steps * per_step) % N_EXPERTS == 0
    expert_blk = lambda i: i % (N_EXPERTS // per_step)
    w_in_spec = lambda a: pl.BlockSpec((None, per_step) + a.shape[2:], lambda i: (layer, expert_blk(i), 0, 0))
    w_out_spec = lambda a: pl.BlockSpec((per_step,) + a.shape[2:], lambda i: (expert_blk(i), 0, 0))
    span = min(CUMSUM_SPAN, tm)
    tri = (jnp.arange(span)[:, None] >= jnp.arange(span)[None, :]) & (
        jnp.arange(span)[:, None] // CHUNK == jnp.arange(span)[None, :] // CHUNK)
    tri = tri.astype(BF16)
    row = lambda width: pl.BlockSpec((tm, width), lambda i: (i, 0))
    full = lambda a: pl.BlockSpec(a.shape, lambda i: (0,) * a.ndim)
    ins = [h, gain.reshape(1, D_MODEL), w] + list(extra) + [tri]
    return pl.pallas_call(
        kernel,
        out_shape=(jax.ShapeDtypeStruct((N, kw), BF16), jax.ShapeDtypeStruct((N, kw), BF16),
                   jax.ShapeDtypeStruct((N, vw), BF16), jax.ShapeDtypeStruct((N, vw), BF16),
                   jax.ShapeDtypeStruct((N, kw), F32), jax.ShapeDtypeStruct((N, MEM_WIDTH), BF16),
                   jax.ShapeDtypeStruct((N // tm, SUBLANES, LANES), F32),
                   jax.ShapeDtypeStruct(w_up.shape[1:], BF16), jax.ShapeDtypeStruct(w_down.shape[1:], BF16)),
        grid=(n_steps,),
        in_specs=[row(D_MODEL)] + [full(a) for a in ins[1:]] + [w_in_spec(w_up), w_in_spec(w_down)],
        out_specs=(row(kw), row(kw), row(vw), row(vw), row(kw), row(MEM_WIDTH),
                   pl.BlockSpec((1, SUBLANES, LANES), lambda i: (i, 0, 0)),
                   w_out_spec(w_up), w_out_spec(w_down)),
        compiler_params=pltpu.CompilerParams(dimension_semantics=("arbitrary",),
                                             vmem_limit_bytes=VMEM_LIMIT),
        name=name,
    )(*ins, w_up, w_down)


def _pad_heads(w, heads, dv, dv_pad):
    lead = w.shape[:-1]
    w = w.reshape(lead + (heads, dv))
    w = jnp.pad(w, [(0, 0)] * len(lead) + [(0, 0), (0, dv_pad - dv)])
    return w.reshape(lead + (heads * dv_pad,))


def _gla_proj(h, gain, w_in, w_gate_up, b_gate, w_up, w_down, layer, tm):
    q, k, v, g, r, mq = jnp.split(
        w_in, [GLA_KEY_WIDTH, 2 * GLA_KEY_WIDTH, 2 * GLA_KEY_WIDTH + MIX_WIDTH,
               2 * GLA_KEY_WIDTH + 2 * MIX_WIDTH, 2 * GLA_KEY_WIDTH + 2 * MIX_WIDTH + GLA_GATE_RANK], axis=1)
    r3 = jnp.pad(jnp.concatenate([r, r, r], axis=1), ((0, 0), (0, LANES - 3 * GLA_GATE_RANK)))
    w = jnp.concatenate([q, k, v, g, r3, mq], axis=1).astype(BF16)
    wg_hi, wg_lo = _split2(w_gate_up)
    wg = jnp.pad(jnp.concatenate([wg_hi, wg_hi, wg_lo], axis=0), ((0, LANES - 3 * GLA_GATE_RANK), (0, 0)))
    return _proj_call(_gla_proj_kernel, h, gain, w, [wg, b_gate.reshape(1, GLA_KEY_WIDTH)], w_up, w_down, layer,
                      GLA_KEY_WIDTH, GLA_HEADS * GLA_DV_PAD, tm, "gla_proj")


def _hg_proj(h, gain, w_in, lower_bound_params, w_up, w_down, layer, tm):
    return _proj_call(functools.partial(_hg_proj_kernel, layer), h, gain, w_in.astype(BF16),
                      [lower_bound_params], w_up, w_down, layer, MIX_WIDTH, MIX_WIDTH, tm, "hg_proj")


def _scan_kernel(heads, dk, dv_pad, dv, n_chunks,
                 q_ref, k_ref, b_ref, bmin_ref, v_ref, sg_ref, gain_ref, o_ref, st_ref, kf_ref):
    @pl.when(pl.program_id(1) == 0)
    def _():
        st_ref[...] = jnp.zeros_like(st_ref)

    gain = gain_ref[...]
    col = lax.broadcasted_iota(jnp.int32, (CHUNK, CHUNK), 1)

    n_batch = q_ref.shape[0]

    def finish(bi, hd, rows, scores, qd, kl, eb_last):
        vc = slice(hd * dv_pad, (hd + 1) * dv_pad)
        v = v_ref[bi, rows, vc]
        st = st_ref[bi * heads + hd]
        out = _dot(scores.astype(BF16), v) + _dot_nt(qd, st.astype(BF16))
        st_ref[bi * heads + hd] = _dot_tn(v, kl) + st * eb_last
        ms = jnp.sum(out * out, axis=-1, keepdims=True) * (1.0 / dv)
        y = out * lax.rsqrt(ms + NORM_EPS) * gain * sg_ref[bi, rows, vc].astype(F32)
        o_ref[bi, rows, hd * dv:(hd + 1) * dv] = y[:, :dv].astype(BF16)

    def load(bi, hd, rows):
        kc = slice(hd * dk, (hd + 1) * dk)
        return (q_ref[bi, rows, kc].astype(F32), k_ref[bi, rows, kc].astype(F32), b_ref[bi, rows, kc])

    lowest = jnp.min(bmin_ref[...])

    def factored(n_join):
        span = n_join * CHUNK
        ri = lax.broadcasted_iota(jnp.int32, (span, span), 0)
        ci = lax.broadcasted_iota(jnp.int32, (span, span), 1)
        for c in range(n_chunks // n_join):
            rows = slice(c * span, (c + 1) * span)
            for bi in range(n_batch):
                for hd in range(heads):
                    q, k, b = load(bi, hd, rows)
                    parts = [b[:CHUNK]]
                    for i in range(1, n_join):
                        parts.append(b[i * CHUNK:(i + 1) * CHUNK] + parts[-1][CHUNK - 1:CHUNK])
                    b = jnp.concatenate(parts, axis=0) if n_join > 1 else b
                    eb = jnp.exp(b)
                    eb_last = eb[span - 1:span, :]
                    qd = (q * eb).astype(BF16)
                    kd = k * jnp.exp(-b)
                    scores = _dot_nt(qd, kd.astype(BF16))
                    finish(bi, hd, rows, jnp.where(ri >= ci, scores, 0.0), qd, (kd * eb_last).astype(BF16),
                           eb_last)

    joins = [j for j in SCAN_JOINS if n_chunks % j == 0]
    bounds = [SAFE_JOIN_LOG_DECAY / j if j > 1 else SAFE_LOG_DECAY for j in joins]
    taken = False
    for j, bound in zip(joins, bounds):
        ok = lowest >= bound
        pl.when(ok if taken is False else ok & jnp.logical_not(taken))(functools.partial(factored, j))
        taken = ok if taken is False else taken | ok
    all_safe = taken

    @pl.when(jnp.logical_not(all_safe))
    def _():
        for bi, hd in [(bi, hd) for bi in range(n_batch) for hd in range(heads)]:
            kc = slice(hd * dk, (hd + 1) * dk)

            def chunk_body(c, carry, bi=bi, hd=hd, kc=kc):
                start = pl.multiple_of(c * CHUNK, CHUNK)
                rows = pl.ds(start, CHUNK)
                q, k, b = load(bi, hd, rows)
                kf_ref[...] = k

                def col_body(j, sc):
                    base = pl.multiple_of((j >> 3) << 3, SUBLANES)
                    pick = lax.broadcasted_iota(jnp.int32, (SUBLANES, dk), 0) == (j & (SUBLANES - 1))
                    kj = jnp.sum(jnp.where(pick, kf_ref[pl.ds(base, SUBLANES), :], 0.0), axis=0, keepdims=True)
                    b8 = b_ref[bi, pl.ds(pl.multiple_of(start + base, SUBLANES), SUBLANES), kc]
                    bj = jnp.sum(jnp.where(pick, b8, 0.0), axis=0, keepdims=True)
                    rid = lax.broadcasted_iota(jnp.int32, (CHUNK, dk), 0)
                    dec = jnp.exp(jnp.where(rid >= j, b - bj, -jnp.inf))
                    colv = jnp.sum(q * kj * dec, axis=-1, keepdims=True)
                    return jnp.where(col == j, colv, sc)

                scores = lax.fori_loop(0, CHUNK, col_body, jnp.zeros((CHUNK, CHUNK), F32))
                b_last = b[CHUNK - 1:CHUNK, :]
                finish(bi, hd, rows, scores, (q * jnp.exp(b)).astype(BF16),
                       (k * jnp.exp(b_last - b)).astype(BF16), jnp.exp(b_last))
                return carry

            lax.fori_loop(0, n_chunks, chunk_body, 0)


def _scan(q, k, b, bmin, v, sg, gain, heads, dk, dv_pad, dv, tb, nb):
    B, T, kw = q.shape
    vw = v.shape[-1]
    blk = lambda w: pl.BlockSpec((nb, tb, w), lambda bi, ti: (bi, ti, 0))
    gain_row = jnp.pad(gain, (0, dv_pad - dv)).reshape(1, dv_pad)
    return pl.pallas_call(
        functools.partial(_scan_kernel, heads, dk, dv_pad, dv, tb // CHUNK),
        out_shape=jax.ShapeDtypeStruct((B, T, heads * dv), BF16),
        grid=(B // nb, T // tb),
        in_specs=[blk(kw), blk(kw), blk(kw),
                  pl.BlockSpec((nb, 1, SUBLANES, LANES), lambda bi, ti: (bi, ti, 0, 0)),
                  blk(vw), blk(vw), pl.BlockSpec((1, dv_pad), lambda bi, ti: (0, 0))],
        out_specs=blk(heads * dv),
        scratch_shapes=[pltpu.VMEM((nb * heads, dv_pad, dk), F32), pltpu.VMEM((CHUNK, dk), F32)],
        compiler_params=pltpu.CompilerParams(dimension_semantics=("arbitrary", "arbitrary"),
                                             vmem_limit_bytes=VMEM_LIMIT),
        name="chunk_scan",
    )(q, k, b, bmin, v, sg, gain_row)


def _out_kernel(mix_ref, mq_ref, mk_ref, mv_ref, h_ref, wmix_ref, wmem_ref, fgain_ref,
                wr_ref, br_ref, h1_ref, un_ref, route_ref):
    tm = mix_ref.shape[0]
    mk = mk_ref[0]
    mv = mv_ref[0]
    lane_w = lax.broadcasted_iota(jnp.int32, (1, MEM_WIDTH), 1)
    mq = mq_ref[...] * (MEM_HEAD_DIM ** -0.5)
    mem_o = jnp.zeros((tm, MEM_WIDTH), F32)
    for hd in range(MEM_HEADS):
        in_head = (lane_w >= hd * MEM_HEAD_DIM) & (lane_w < (hd + 1) * MEM_HEAD_DIM)
        s = _dot_nt(jnp.where(in_head, mq, jnp.zeros_like(mq)), mk)
        e = jnp.exp(s - jnp.max(s, axis=-1, keepdims=True))
        denom = jnp.sum(e, axis=-1, keepdims=True)
        mem_o = mem_o + _dot(e.astype(BF16), jnp.where(in_head, mv, jnp.zeros_like(mv))) * (1.0 / denom)
    h1 = h_ref[...] + _dot(mix_ref[...], wmix_ref[...]) + _dot(mem_o.astype(BF16), wmem_ref[...])
    h1_ref[...] = h1
    un = _rms(h1, fgain_ref[...])
    un_ref[...] = un.astype(BF16)

    both = _dot_nt(wr_ref[...], un.astype(BF16))
    lg = both[:ROUTE_ROWS] + both[ROUTE_ROWS:] + br_ref[...]
    row = lax.broadcasted_iota(jnp.int32, (ROUTE_ROWS, tm), 0)
    neg = -jnp.inf
    gl = jnp.where(row < N_GROUPS, lg, neg)
    gmax = jnp.max(gl, axis=0, keepdims=True)
    g_idx = jnp.min(jnp.where(gl == gmax, row, ROUTE_ROWS), axis=0, keepdims=True)
    g_w = 1.0 / jnp.sum(jnp.exp(gl - gmax), axis=0, keepdims=True)
    first = N_GROUPS + g_idx * EXPERTS_PER_GROUP
    el = jnp.where((row >= first) & (row < first + EXPERTS_PER_GROUP), lg, neg)
    t1 = jnp.max(el, axis=0, keepdims=True)
    i1 = jnp.min(jnp.where(el == t1, row, ROUTE_ROWS), axis=0, keepdims=True)
    el2 = jnp.where(row == i1, neg, el)
    t2 = jnp.max(el2, axis=0, keepdims=True)
    i2 = jnp.min(jnp.where(el2 == t2, row, ROUTE_ROWS), axis=0, keepdims=True)
    e2 = jnp.exp(t2 - t1)
    w1 = g_w / (1.0 + e2)
    w2 = g_w * e2 / (1.0 + e2)
    row8 = lax.broadcasted_iota(jnp.int32, (SUBLANES, tm), 0)
    route_ref[...] = (jnp.where(row8 == i1 - first, w1, 0.0) + jnp.where(row8 == i2 - first, w2, 0.0)
                      + jnp.where(row8 == ROUTE_GROUP_ROW, g_idx.astype(F32), 0.0))


def _out_call(mix, mq, mem_k, mem_v, h, w_mix, w_mem, ffn_gain, w_group, b_group, w_router, b_router, T, tm):
    N = h.shape[0]
    tiles_per_batch = T // tm
    pad = ROUTE_ROWS - N_GROUPS - N_EXPERTS
    wr = jnp.pad(jnp.concatenate([w_group, w_router], axis=1).T, ((0, pad), (0, 0)))
    br = jnp.broadcast_to(jnp.pad(jnp.concatenate([b_group, b_router]), (0, pad))[:, None], (ROUTE_ROWS, tm))
    wr_both = jnp.concatenate(_split2(wr), axis=0)
    row = lambda width: pl.BlockSpec((tm, width), lambda i: (i, 0))
    full = lambda a: pl.BlockSpec(a.shape, lambda i: (0,) * a.ndim)
    memspec = pl.BlockSpec((1, N_MEM, MEM_WIDTH), lambda i: (i // tiles_per_batch, 0, 0))
    w_mix = w_mix.astype(BF16)
    w_mem = w_mem.astype(BF16)
    fg = ffn_gain.reshape(1, D_MODEL)
    return pl.pallas_call(
        _out_kernel,
        out_shape=(jax.ShapeDtypeStruct((N, D_MODEL), F32), jax.ShapeDtypeStruct((N, D_MODEL), BF16),
                   jax.ShapeDtypeStruct((SUBLANES, N), F32)),
        grid=(N // tm,),
        in_specs=[row(mix.shape[1]), row(MEM_WIDTH), memspec, memspec, row(D_MODEL),
                  full(w_mix), full(w_mem), full(fg), full(wr_both), full(br)],
        out_specs=(row(D_MODEL), row(D_MODEL), pl.BlockSpec((SUBLANES, tm), lambda i: (0, i))),
        compiler_params=pltpu.CompilerParams(dimension_semantics=("arbitrary",),
                                             vmem_limit_bytes=VMEM_LIMIT),
        name="out_proj_route",
    )(mix, mq, mem_k, mem_v, h, w_mix, w_mem, fg, wr_both, br)


def _moe_kernel(final, n_super, un_ref, route_ref, h1_ref, utri_ref, wup_ref, wdn_ref, fgain_ref, o_ref,
                xs_ref, gs_ref, xg_ref, gg_ref, permt_ref, meta_ref):
    s = pl.program_id(0)
    k = pl.program_id(1)
    n_tiles, slots, _ = xs_ref.shape
    tm = un_ref.shape[0]
    row8 = lax.broadcasted_iota(jnp.int32, (SUBLANES, tm), 0)

    @pl.when((s == 0) & (k == 0))
    def _():
        xg_ref[...] = jnp.zeros_like(xg_ref)
        gg_ref[...] = jnp.zeros_like(gg_ref)

    def unpartition():
        y = h1_ref[...] + _dot(permt_ref[k], xs_ref[k])
        o_ref[...] = _rms(y, fgain_ref[...]) if final else y

    def partition():
        route = route_ref[...]
        g_idx = route[ROUTE_GROUP_ROW:ROUTE_GROUP_ROW + 1, :]
        member = jnp.where((row8 < N_GROUPS) & (row8.astype(F32) == g_idx), 1.0, 0.0)
        earlier = _dot(member.astype(BF16), utri_ref[...])
        rank = jnp.sum(member * earlier, axis=0, keepdims=True)
        count = jnp.sum(member, axis=1, keepdims=True)
        padded = jnp.floor((count + (MOE_ALIGN - 1.0)) * (1.0 / MOE_ALIGN)) * MOE_ALIGN
        starts = [jnp.zeros((1, 1), F32)]
        for g in range(1, N_GROUPS):
            starts.append(starts[-1] + padded[g - 1:g, :])
        seg = jnp.zeros((SUBLANES, 1), F32)
        row81 = lax.broadcasted_iota(jnp.int32, (SUBLANES, 1), 0)
        for g in range(1, N_GROUPS):
            seg = jnp.where(row81 == g, starts[g], seg)
        dest = rank + jnp.sum(member * seg, axis=0, keepdims=True)
        perm = jnp.where(lax.broadcasted_iota(jnp.int32, (slots, tm), 0) == dest.astype(jnp.int32),
                         1.0, 0.0).astype(BF16)
        d_hi = jnp.floor(dest * (1.0 / 32.0))
        d_lo = dest - 32.0 * d_hi
        digits = jnp.where(row8 == 0, d_hi, jnp.where(row8 == 1, d_lo, 0.0)).astype(BF16)
        pick = (lax.broadcasted_iota(jnp.int32, (SUBLANES, LANES), 0)
                == lax.broadcasted_iota(jnp.int32, (SUBLANES, LANES), 1)).astype(BF16)
        dig_c = _dot_tn(digits, pick)
        dest_col = (32.0 * dig_c[:, 0:1] + dig_c[:, 1:2]).astype(jnp.int32)
        permt_ref[k] = jnp.where(lax.broadcasted_iota(jnp.int32, (tm, slots), 1) == dest_col,
                                 1.0, 0.0).astype(BF16)
        xs_ref[k] = _dot(perm, un_ref[...]).astype(BF16)
        r_hi, r_lo = _split2(route)
        gs_ref[k] = _dot_nt(perm, jnp.concatenate([r_hi, r_lo], axis=0))
        for g in range(N_GROUPS):
            meta_ref[(k * N_GROUPS + g) * 2] = starts[g][0, 0].astype(jnp.int32)
            meta_ref[(k * N_GROUPS + g) * 2 + 1] = padded[g, 0].astype(jnp.int32)

    pl.when(s > 0)(unpartition)
    pl.when(s < n_super)(partition)

    def expert_block(g, r0, rows):
        xb = xg_ref[pl.ds(r0, rows), :]
        gsb = gg_ref[pl.ds(r0, rows), :]
        acc = jnp.zeros((rows, D_MODEL), F32)
        for j in range(EXPERTS_PER_GROUP):
            e = g * EXPERTS_PER_GROUP + j
            hh = _dot(xb, wup_ref[e])
            gate = gsb[:, j:j + 1] + gsb[:, SUBLANES + j:SUBLANES + j + 1]
            act = _silu(hh[:, :EXPERT_FF]) * hh[:, EXPERT_FF:] * gate
            acc = acc + _dot(act.astype(BF16), wdn_ref[e])
        xg_ref[pl.ds(r0, rows), :] = acc.astype(BF16)

    def copy_segments(g, gather):
        off = jnp.int32(0)
        for t in range(n_tiles):
            st = meta_ref[(t * N_GROUPS + g) * 2]
            ln = meta_ref[(t * N_GROUPS + g) * 2 + 1]

            def copy_rows(src0, dst0, rows, t=t):
                src = pl.ds(pl.multiple_of(src0, MOE_ALIGN), rows)
                dst = pl.ds(pl.multiple_of(dst0, MOE_ALIGN), rows)
                if gather:
                    xg_ref[dst, :] = xs_ref[t, src, :]
                    gg_ref[dst, :] = gs_ref[t, src, :]
                else:
                    xs_ref[t, src, :] = xg_ref[dst, :]

            def copy_body(i, carry, st=st, off=off, copy_rows=copy_rows):
                copy_rows(st + i * MOE_COPY, off + i * MOE_COPY, MOE_COPY)
                return carry

            n_copy = lax.shift_right_logical(ln, MOE_COPY_SHIFT)
            lax.fori_loop(0, n_copy, copy_body, 0)
            done = n_copy * MOE_COPY

            def tail_body(i, carry, st=st, off=off, done=done, copy_rows=copy_rows):
                copy_rows(st + done + i * MOE_ALIGN, off + done + i * MOE_ALIGN, MOE_ALIGN)
                return carry

            lax.fori_loop(0, lax.shift_right_logical(ln - done, MOE_ALIGN_SHIFT), tail_body, 0)
            off = off + ln
        return off

    @pl.when((s < n_super) & (k == n_tiles - 1))
    def _():
        for g in range(N_GROUPS):
            total = copy_segments(g, True)
            n_big = total // MOE_BIG
            rem = total - n_big * MOE_BIG
            more = rem > MOE_SMALL_MAX * MOE_BLOCK
            n_big = n_big + more.astype(jnp.int32)
            n_small = jnp.where(more, 0, (rem + (MOE_BLOCK - 1)) // MOE_BLOCK)
            small0 = n_big * MOE_BIG

            def big_body(i, carry, g=g):
                expert_block(g, pl.multiple_of(i * MOE_BIG, MOE_ALIGN), MOE_BIG)
                return carry

            def small_body(i, carry, g=g, small0=small0):
                expert_block(g, pl.multiple_of(small0 + i * MOE_BLOCK, MOE_ALIGN), MOE_BLOCK)
                return carry

            lax.fori_loop(0, n_big, big_body, 0)
            lax.fori_loop(0, n_small, small_body, 0)

            copy_segments(g, False)


def _moe(un, route, h1, w_up, w_down, final_gain, final, tm):
    N = un.shape[0]
    n_tiles = min(MOE_SUPER, N // tm)
    n_super = N // (tm * n_tiles)
    slots = tm + N_GROUPS * MOE_ALIGN
    group_rows = n_tiles * slots + MOE_BIG
    utri = (jnp.arange(tm)[:, None] < jnp.arange(tm)[None, :]).astype(BF16)
    in_tile = lambda s, k: jnp.minimum(s, n_super - 1) * n_tiles + k
    out_tile = lambda s, k: jnp.maximum(s - 1, 0) * n_tiles + jnp.where(s > 0, k, 0)
    once = lambda a: pl.BlockSpec(a.shape, lambda s, k: (0,) * a.ndim, pipeline_mode=pl.Buffered(1))
    fg = final_gain.reshape(1, D_MODEL)
    return pl.pallas_call(
        functools.partial(_moe_kernel, final, n_super),
        out_shape=jax.ShapeDtypeStruct((N, D_MODEL), F32),
        grid=(n_super + 1, n_tiles),
        in_specs=[pl.BlockSpec((tm, D_MODEL), lambda s, k: (in_tile(s, k), 0)),
                  pl.BlockSpec((SUBLANES, tm), lambda s, k: (0, in_tile(s, k))),
                  pl.BlockSpec((tm, D_MODEL), lambda s, k: (out_tile(s, k), 0)),
                  once(utri), once(w_up), once(w_down), once(fg)],
        out_specs=pl.BlockSpec((tm, D_MODEL), lambda s, k: (out_tile(s, k), 0)),
        scratch_shapes=[pltpu.VMEM((n_tiles, slots, D_MODEL), BF16),
                        pltpu.VMEM((n_tiles, slots, 2 * SUBLANES), F32),
                        pltpu.VMEM((group_rows, D_MODEL), BF16),
                        pltpu.VMEM((group_rows, 2 * SUBLANES), F32),
                        pltpu.VMEM((n_tiles, tm, slots), BF16),
                        pltpu.SMEM((n_tiles * N_GROUPS * 2,), jnp.int32)],
        compiler_params=pltpu.CompilerParams(dimension_semantics=("arbitrary", "arbitrary"),
                                             vmem_limit_bytes=VMEM_LIMIT),
        name="moe",
    )(un, route, h1, utri, w_up, w_down, fg)


def kernel(x, mem, mix_norm, ffn_norm, mem_norm, final_norm, gla_w_in, gla_w_gate_up, gla_b_gate, gla_out_norm,
           hg_w_in, hg_lower_bounds, hg_out_norm, w_mem_kv, w_out, w_group, b_group, w_router, b_router,
           w_up, w_down):
    B, T, _ = x.shape
    N = B * T
    depth = mix_norm.shape[0]
    tm_proj = min(1024, T)
    nb_scan = 1
    tb_scan = tm_proj
    tm_out = min(1024, T)
    tm_moe = min(MOE_TILE, N)
    h = x.reshape(N, D_MODEL)
    for layer in range(depth):
        j = layer // 2
        if layer % 2 == 0:
            q, k, v, sg, b, mq, bmin, w_up_bf16, w_down_bf16 = _gla_proj(
                h, mix_norm[layer], gla_w_in[j], gla_w_gate_up[j], gla_b_gate[j], w_up, w_down, layer, tm_proj)
            heads, dk, dv_pad, dv, out_gain = GLA_HEADS, GLA_DK, GLA_DV_PAD, GLA_DV, gla_out_norm[j]
        else:
            q, k, v, sg, b, mq, bmin, w_up_bf16, w_down_bf16 = _hg_proj(
                h, mix_norm[layer], hg_w_in[j], hg_lower_bounds, w_up, w_down, layer, tm_proj)
            heads, dk, dv_pad, dv, out_gain = HG_HEADS, HG_DK, HG_DV, HG_DV, hg_out_norm[j]
        r3 = lambda a: a.reshape(B, T, a.shape[-1])
        bmin = bmin.reshape(B, T // tb_scan, SUBLANES, LANES)
        mix = _scan(r3(q), r3(k), r3(b), bmin, r3(v), r3(sg), out_gain, heads, dk, dv_pad, dv, tb_scan, nb_scan)
        mix = mix.reshape(N, MIX_WIDTH)
        mem_k, mem_v = _mem_kv(mem, mem_norm[layer], w_mem_kv[layer])
        h1, un, gates = _out_call(mix, mq, mem_k, mem_v, h, w_out[layer, :MIX_WIDTH], w_out[layer, MIX_WIDTH:],
                                  ffn_norm[layer],
                                  w_group[layer], b_group[layer], w_router[layer], b_router[layer], T, tm_out)
        h = _moe(un, gates, h1, w_up_bf16, w_down_bf16, final_norm, layer == depth - 1, tm_moe)
    return h.reshape(B, T, D_MODEL)
```

```python
import functools

import jax
import jax.numpy as jnp
from jax import lax
from jax.experimental import pallas as pl
from jax.experimental.pallas import tpu as pltpu

D_MODEL = 1024
N_MEM = 256
MIX_WIDTH = 768
MEM_HEADS = 4
MEM_HEAD_DIM = 64
MEM_WIDTH = 256
CHUNK = 64
GLA_HEADS = 4
GLA_KEY_WIDTH = 512
GLA_DK = 128
GLA_DV = 192
GLA_DV_PAD = 256
GLA_GATE_RANK = 16
GLA_GATE_NORMALIZER = 16.0
HG_HEADS = 6
HG_DK = 128
HG_DV = 128
N_GROUPS = 4
EXPERTS_PER_GROUP = 4
N_EXPERTS = 16
EXPERT_FF = 256
NORM_EPS = 1e-6
LANES = 128
SUBLANES = 8
MOE_BLOCK = 128
CUMSUM_SPAN = 256
MOE_ALIGN_SHIFT = 4
MOE_ALIGN = 1 << MOE_ALIGN_SHIFT
MOE_COPY_SHIFT = 6
MOE_COPY = 1 << MOE_COPY_SHIFT
MOE_TILE = 512
MOE_SUPER = 4
MOE_BIG = MOE_SUPER * (MOE_TILE // N_GROUPS + MOE_ALIGN)
MOE_SMALL_MAX = 2
ROUTE_GROUP_ROW = 4
ROUTE_ROWS = 32
SAFE_LOG_DECAY = -60.0
SAFE_JOIN_LOG_DECAY = -70.0
SCAN_JOINS = (4, 2, 1)
VMEM_LIMIT = 56 * 1024 * 1024

F32 = jnp.float32
BF16 = jnp.bfloat16


def _dot(a, b):
    return jnp.dot(a, b, preferred_element_type=F32)


def _dot_nt(a, b):
    return lax.dot_general(a, b, (((1,), (1,)), ((), ())), preferred_element_type=F32)


def _dot_tn(a, b):
    return lax.dot_general(a, b, (((0,), (0,)), ((), ())), preferred_element_type=F32)


def _split2(x):
    hi = x.astype(BF16)
    lo = (x - hi.astype(F32)).astype(BF16)
    return hi, lo


def _rms(x, gain):
    ms = jnp.mean(x * x, axis=-1, keepdims=True)
    return x * lax.rsqrt(ms + NORM_EPS) * gain


def _log_sigmoid(x):
    return jnp.minimum(x, 0.0) - jnp.log(1.0 + jnp.exp(-jnp.abs(x)))


def _sigmoid(x):
    return 1.0 / (1.0 + jnp.exp(-x))


def _silu(x):
    return x * _sigmoid(x)


def _chunk_cumsum(tri_ref, x):
    tri = tri_ref[...]
    span = tri.shape[0]
    parts = []
    for r in range(x.shape[0] // span):
        hi, lo = _split2(x[r * span:(r + 1) * span])
        parts.append(_dot(tri, hi) + _dot(tri, lo))
    return jnp.concatenate(parts, axis=0) if len(parts) > 1 else parts[0]


_GQ, _GK = 0, GLA_KEY_WIDTH
_GV = 2 * GLA_KEY_WIDTH
_GG = _GV + MIX_WIDTH
_GR = _GG + MIX_WIDTH
_GM = _GR + LANES
_GW = _GM + MEM_WIDTH


def _store_padded_heads(ref, x):
    zeros = jnp.zeros((x.shape[0], GLA_DV_PAD - GLA_DV), ref.dtype)
    for hd in range(GLA_HEADS):
        ref[:, hd * GLA_DV_PAD:hd * GLA_DV_PAD + GLA_DV] = x[:, hd * GLA_DV:(hd + 1) * GLA_DV].astype(ref.dtype)
        ref[:, hd * GLA_DV_PAD + GLA_DV:(hd + 1) * GLA_DV_PAD] = zeros


def _store_decay(b_ref, bmin_ref, b):
    b_ref[...] = b
    lowest = jnp.min(jnp.min(b, axis=0, keepdims=True), axis=1, keepdims=True)
    bmin_ref[0] = jnp.broadcast_to(lowest, (SUBLANES, LANES))


def _round_experts(wup_ref, wdn_ref, wup_bf_ref, wdn_bf_ref):
    wup_bf_ref[...] = wup_ref[...].astype(BF16)
    wdn_bf_ref[...] = wdn_ref[...].astype(BF16)


def _gla_proj_kernel(h_ref, gain_ref, w_ref, wg_ref, bg_ref, tri_ref, wup_ref, wdn_ref,
                     q_ref, k_ref, v_ref, sg_ref, b_ref, mq_ref, bmin_ref, wup_bf_ref, wdn_bf_ref):
    _round_experts(wup_ref, wdn_ref, wup_bf_ref, wdn_bf_ref)
    u = _rms(h_ref[...], gain_ref[...]).astype(BF16)
    q_ref[...] = (_dot(u, w_ref[:, _GQ:_GK]) * (GLA_DK ** -0.5)).astype(BF16)
    k_ref[...] = _dot(u, w_ref[:, _GK:_GV]).astype(BF16)
    _store_padded_heads(v_ref, _dot(u, w_ref[:, _GV:_GG]))
    _store_padded_heads(sg_ref, _silu(_dot(u, w_ref[:, _GG:_GR])))
    mq_ref[...] = _dot(u, w_ref[:, _GM:_GW]).astype(BF16)
    r = _dot(u, w_ref[:, _GR:_GM])
    r_hi = r.astype(BF16)
    lane = lax.broadcasted_iota(jnp.int32, r.shape, 1)
    is_lo = (lane >= GLA_GATE_RANK) & (lane < 2 * GLA_GATE_RANK)
    r_parts = jnp.where(is_lo, r - r_hi.astype(F32), r_hi.astype(F32)).astype(BF16)
    logit = _dot(r_parts, wg_ref[...]) + bg_ref[...]
    _store_decay(b_ref, bmin_ref, _chunk_cumsum(tri_ref, _log_sigmoid(logit) * (1.0 / GLA_GATE_NORMALIZER)))


_HQ, _HF, _HI, _HGG, _HM = 0, MIX_WIDTH, 2 * MIX_WIDTH, 3 * MIX_WIDTH, 4 * MIX_WIDTH
_HW = _HM + MEM_WIDTH


def _hg_proj_kernel(layer, h_ref, gain_ref, w_ref, lbp_ref, tri_ref, wup_ref, wdn_ref,
                    q_ref, k_ref, v_ref, sg_ref, b_ref, mq_ref, bmin_ref, wup_bf_ref, wdn_bf_ref):
    _round_experts(wup_ref, wdn_ref, wup_bf_ref, wdn_bf_ref)
    u = _rms(h_ref[...], gain_ref[...]).astype(BF16)
    p = lbp_ref[...]
    p = jnp.exp(p - jnp.max(p, axis=0, keepdims=True))
    p = p / jnp.sum(p, axis=0, keepdims=True)
    lb = jnp.sum(p[0:layer + 1], axis=0, keepdims=True) - p[0:1]
    q_ref[...] = _silu(_dot(u, w_ref[:, _HQ:_HF])).astype(BF16)
    z = _dot(u, w_ref[:, _HF:_HI])
    e = jnp.exp(-jnp.abs(z))
    k_ref[...] = ((1.0 - lb) * (jnp.where(z >= 0.0, e, 1.0) / (1.0 + e))).astype(BF16)
    v_ref[...] = _dot(u, w_ref[:, _HI:_HGG]).astype(BF16)
    sg_ref[...] = _silu(_dot(u, w_ref[:, _HGG:_HM])).astype(BF16)
    mq_ref[...] = _dot(u, w_ref[:, _HM:_HW]).astype(BF16)
    a = jnp.log(lb)
    c = jnp.log1p(-lb) + (jnp.minimum(z, 0.0) - jnp.log(1.0 + e))
    log_f = jnp.maximum(a, c) + jnp.log(1.0 + jnp.exp(-jnp.abs(a - c)))
    _store_decay(b_ref, bmin_ref, _chunk_cumsum(tri_ref, log_f))


def _proj_call(kernel, h, gain, w, extra, w_up, w_down, layer, kw, vw, tm, name):
    N = h.shape[0]
    n_steps = N // tm
    per_step = max(1, N_EXPERTS // n_steps)
    assert N_EXPERTS % per_step == 0 and (n_steps * per_step) % N_EXPERTS == 0
    expert_blk = lambda i: i % (N_EXPERTS // per_step)
    w_in_spec = lambda a: pl.BlockSpec((None, per_step) + a.shape[2:], lambda i: (layer, expert_blk(i), 0, 0))
    w_out_spec = lambda a: pl.BlockSpec((per_step,) + a.shape[2:], lambda i: (expert_blk(i), 0, 0))
    span = min(CUMSUM_SPAN, tm)
    tri = (jnp.arange(span)[:, None] >= jnp.arange(span)[None, :]) & (
        jnp.arange(span)[:, None] // CHUNK == jnp.arange(span)[None, :] // CHUNK)
    tri = tri.astype(BF16)
    row = lambda width: pl.BlockSpec((tm, width), lambda i: (i, 0))
    full = lambda a: pl.BlockSpec(a.shape, lambda i: (0,) * a.ndim)
    ins = [h, gain.reshape(1, D_MODEL), w] + list(extra) + [tri]
    return pl.pallas_call(
        kernel,
        out_shape=(jax.ShapeDtypeStruct((N, kw), BF16), jax.ShapeDtypeStruct((N, kw), BF16),
                   jax.ShapeDtypeStruct((N, vw), BF16), jax.ShapeDtypeStruct((N, vw), BF16),
                   jax.ShapeDtypeStruct((N, kw), F32), jax.ShapeDtypeStruct((N, MEM_WIDTH), BF16),
                   jax.ShapeDtypeStruct((N // tm, SUBLANES, LANES), F32),
                   jax.ShapeDtypeStruct(w_up.shape[1:], BF16), jax.ShapeDtypeStruct(w_down.shape[1:], BF16)),
        grid=(n_steps,),
        in_specs=[row(D_MODEL)] + [full(a) for a in ins[1:]] + [w_in_spec(w_up), w_in_spec(w_down)],
        out_specs=(row(kw), row(kw), row(vw), row(vw), row(kw), row(MEM_WIDTH),
                   pl.BlockSpec((1, SUBLANES, LANES), lambda i: (i, 0, 0)),
                   w_out_spec(w_up), w_out_spec(w_down)),
        compiler_params=pltpu.CompilerParams(dimension_semantics=("arbitrary",),
                                             vmem_limit_bytes=VMEM_LIMIT),
        name=name,
    )(*ins, w_up, w_down)


def _gla_proj(h, gain, w_in, w_gate_up, b_gate, w_up, w_down, layer, tm):
    q, k, v, g, r, mq = jnp.split(
        w_in, [GLA_KEY_WIDTH, 2 * GLA_KEY_WIDTH, 2 * GLA_KEY_WIDTH + MIX_WIDTH,
               2 * GLA_KEY_WIDTH + 2 * MIX_WIDTH, 2 * GLA_KEY_WIDTH + 2 * MIX_WIDTH + GLA_GATE_RANK], axis=1)
    r3 = jnp.pad(jnp.concatenate([r, r, r], axis=1), ((0, 0), (0, LANES - 3 * GLA_GATE_RANK)))
    w = jnp.concatenate([q, k, v, g, r3, mq], axis=1).astype(BF16)
    wg_hi, wg_lo = _split2(w_gate_up)
    wg = jnp.pad(jnp.concatenate([wg_hi, wg_hi, wg_lo], axis=0), ((0, LANES - 3 * GLA_GATE_RANK), (0, 0)))
    return _proj_call(_gla_proj_kernel, h, gain, w, [wg, b_gate.reshape(1, GLA_KEY_WIDTH)], w_up, w_down, layer,
                      GLA_KEY_WIDTH, GLA_HEADS * GLA_DV_PAD, tm, "gla_proj")


def _hg_proj(h, gain, w_in, lower_bound_params, w_up, w_down, layer, tm):
    return _proj_call(functools.partial(_hg_proj_kernel, layer), h, gain, w_in.astype(BF16),
                      [lower_bound_params], w_up, w_down, layer, MIX_WIDTH, MIX_WIDTH, tm, "hg_proj")


def _scan_kernel(heads, dk, dv_pad, dv, n_chunks,
                 q_ref, k_ref, b_ref, bmin_ref, v_ref, sg_ref, gain_ref, o_ref, st_ref, kf_ref):
    @pl.when(pl.program_id(1) == 0)
    def _():
        st_ref[...] = jnp.zeros_like(st_ref)

    gain = gain_ref[...]
    col = lax.broadcasted_iota(jnp.int32, (CHUNK, CHUNK), 1)

    n_batch = q_ref.shape[0]

    def finish(bi, hd, rows, scores, qd, kl, eb_last):
        vc = slice(hd * dv_pad, (hd + 1) * dv_pad)
        v = v_ref[bi, rows, vc]
        st = st_ref[bi * heads + hd]
        out = _dot(scores.astype(BF16), v) + _dot_nt(qd, st.astype(BF16))
        st_ref[bi * heads + hd] = _dot_tn(v, kl) + st * eb_last
        ms = jnp.sum(out * out, axis=-1, keepdims=True) * (1.0 / dv)
        y = out * lax.rsqrt(ms + NORM_EPS) * gain * sg_ref[bi, rows, vc].astype(F32)
        o_ref[bi, rows, hd * dv:(hd + 1) * dv] = y[:, :dv].astype(BF16)

    def load(bi, hd, rows):
        kc = slice(hd * dk, (hd + 1) * dk)
        return (q_ref[bi, rows, kc].astype(F32), k_ref[bi, rows, kc].astype(F32), b_ref[bi, rows, kc])

    lowest = jnp.min(bmin_ref[...])

    def factored(n_join):
        span = n_join * CHUNK
        ri = lax.broadcasted_iota(jnp.int32, (span, span), 0)
        ci = lax.broadcasted_iota(jnp.int32, (span, span), 1)
        for c in range(n_chunks // n_join):
            rows = slice(c * span, (c + 1) * span)
            for bi in range(n_batch):
                for hd in range(heads):
                    q, k, b = load(bi, hd, rows)
                    parts = [b[:CHUNK]]
                    for i in range(1, n_join):
                        parts.append(b[i * CHUNK:(i + 1) * CHUNK] + parts[-1][CHUNK - 1:CHUNK])
                    b = jnp.concatenate(parts, axis=0) if n_join > 1 else b
                    eb = jnp.exp(b)
                    eb_last = eb[span - 1:span, :]
                    qd = (q * eb).astype(BF16)
                    kd = k * jnp.exp(-b)
                    scores = _dot_nt(qd, kd.astype(BF16))
                    finish(bi, hd, rows, jnp.where(ri >= ci, scores, 0.0), qd, (kd * eb_last).astype(BF16),
                           eb_last)

    joins = [j for j in SCAN_JOINS if n_chunks % j == 0]
    bounds = [SAFE_JOIN_LOG_DECAY / j if j > 1 else SAFE_LOG_DECAY for j in joins]
    taken = False
    for j, bound in zip(joins, bounds):
        ok = lowest >= bound
        pl.when(ok if taken is False else ok & jnp.logical_not(taken))(functools.partial(factored, j))
        taken = ok if taken is False else taken | ok
    all_safe = taken

    @pl.when(jnp.logical_not(all_safe))
    def _():
        for bi, hd in [(bi, hd) for bi in range(n_batch) for hd in range(heads)]:
            kc = slice(hd * dk, (hd + 1) * dk)

            def chunk_body(c, carry, bi=bi, hd=hd, kc=kc):
                start = pl.multiple_of(c * CHUNK, CHUNK)
                rows = pl.ds(start, CHUNK)
                q, k, b = load(bi, hd, rows)
                kf_ref[...] = k

                def col_body(j, sc):
                    base = pl.multiple_of((j >> 3) << 3, SUBLANES)
                    pick = lax.broadcasted_iota(jnp.int32, (SUBLANES, dk), 0) == (j & (SUBLANES - 1))
                    kj = jnp.sum(jnp.where(pick, kf_ref[pl.ds(base, SUBLANES), :], 0.0), axis=0, keepdims=True)
                    b8 = b_ref[bi, pl.ds(pl.multiple_of(start + base, SUBLANES), SUBLANES), kc]
                    bj = jnp.sum(jnp.where(pick, b8, 0.0), axis=0, keepdims=True)
                    rid = lax.broadcasted_iota(jnp.int32, (CHUNK, dk), 0)
                    dec = jnp.exp(jnp.where(rid >= j, b - bj, -jnp.inf))
                    colv = jnp.sum(q * kj * dec, axis=-1, keepdims=True)
                    return jnp.where(col == j, colv, sc)

                scores = lax.fori_loop(0, CHUNK, col_body, jnp.zeros((CHUNK, CHUNK), F32))
                b_last = b[CHUNK - 1:CHUNK, :]
                finish(bi, hd, rows, scores, (q * jnp.exp(b)).astype(BF16),
                       (k * jnp.exp(b_last - b)).astype(BF16), jnp.exp(b_last))
                return carry

            lax.fori_loop(0, n_chunks, chunk_body, 0)


def _scan(q, k, b, bmin, v, sg, gain, heads, dk, dv_pad, dv, tb, nb):
    B, T, kw = q.shape
    vw = v.shape[-1]
    blk = lambda w: pl.BlockSpec((nb, tb, w), lambda bi, ti: (bi, ti, 0))
    gain_row = jnp.pad(gain, (0, dv_pad - dv)).reshape(1, dv_pad)
    return pl.pallas_call(
        functools.partial(_scan_kernel, heads, dk, dv_pad, dv, tb // CHUNK),
        out_shape=jax.ShapeDtypeStruct((B, T, heads * dv), BF16),
        grid=(B // nb, T // tb),
        in_specs=[blk(kw), blk(kw), blk(kw),
                  pl.BlockSpec((nb, 1, SUBLANES, LANES), lambda bi, ti: (bi, ti, 0, 0)),
                  blk(vw), blk(vw), pl.BlockSpec((1, dv_pad), lambda bi, ti: (0, 0))],
        out_specs=blk(heads * dv),
        scratch_shapes=[pltpu.VMEM((nb * heads, dv_pad, dk), F32), pltpu.VMEM((CHUNK, dk), F32)],
        compiler_params=pltpu.CompilerParams(dimension_semantics=("arbitrary", "arbitrary"),
                                             vmem_limit_bytes=VMEM_LIMIT),
        name="chunk_scan",
    )(q, k, b, bmin, v, sg, gain_row)


def _out_kernel(tiles_per_batch, mix_ref, mq_ref, mem_ref, mgain_ref, wkv_ref, h_ref, wmix_ref, wmem_ref,
                fgain_ref, wr_ref, br_ref, h1_ref, un_ref, route_ref, mk_ref, mv_ref):
    tm = mix_ref.shape[0]

    @pl.when(pl.program_id(0) % tiles_per_batch == 0)
    def _():
        kv = _dot(_rms(mem_ref[0], mgain_ref[...]).astype(BF16), wkv_ref[...])
        mk_ref[...] = kv[:, :MEM_WIDTH].astype(BF16)
        mv_ref[...] = kv[:, MEM_WIDTH:].astype(BF16)

    mk = mk_ref[...]
    mv = mv_ref[...]
    lane_w = lax.broadcasted_iota(jnp.int32, (1, MEM_WIDTH), 1)
    mq = mq_ref[...] * (MEM_HEAD_DIM ** -0.5)
    mem_o = jnp.zeros((tm, MEM_WIDTH), F32)
    for hd in range(MEM_HEADS):
        in_head = (lane_w >= hd * MEM_HEAD_DIM) & (lane_w < (hd + 1) * MEM_HEAD_DIM)
        s = _dot_nt(jnp.where(in_head, mq, jnp.zeros_like(mq)), mk)
        e = jnp.exp(s - jnp.max(s, axis=-1, keepdims=True))
        denom = jnp.sum(e, axis=-1, keepdims=True)
        mem_o = mem_o + _dot(e.astype(BF16), jnp.where(in_head, mv, jnp.zeros_like(mv))) * (1.0 / denom)
    h1 = h_ref[...] + _dot(mix_ref[...], wmix_ref[...]) + _dot(mem_o.astype(BF16), wmem_ref[...])
    h1_ref[...] = h1
    un = _rms(h1, fgain_ref[...])
    un_ref[...] = un.astype(BF16)

    both = _dot_nt(wr_ref[...], un.astype(BF16))
    lg = both[:ROUTE_ROWS] + both[ROUTE_ROWS:] + br_ref[...]
    row = lax.broadcasted_iota(jnp.int32, (ROUTE_ROWS, tm), 0)
    neg = -jnp.inf
    gl = jnp.where(row < N_GROUPS, lg, neg)
    gmax = jnp.max(gl, axis=0, keepdims=True)
    g_idx = jnp.min(jnp.where(gl == gmax, row, ROUTE_ROWS), axis=0, keepdims=True)
    g_w = 1.0 / jnp.sum(jnp.exp(gl - gmax), axis=0, keepdims=True)
    first = N_GROUPS + g_idx * EXPERTS_PER_GROUP
    el = jnp.where((row >= first) & (row < first + EXPERTS_PER_GROUP), lg, neg)
    t1 = jnp.max(el, axis=0, keepdims=True)
    i1 = jnp.min(jnp.where(el == t1, row, ROUTE_ROWS), axis=0, keepdims=True)
    el2 = jnp.where(row == i1, neg, el)
    t2 = jnp.max(el2, axis=0, keepdims=True)
    i2 = jnp.min(jnp.where(el2 == t2, row, ROUTE_ROWS), axis=0, keepdims=True)
    e2 = jnp.exp(t2 - t1)
    w1 = g_w / (1.0 + e2)
    w2 = g_w * e2 / (1.0 + e2)
    row8 = lax.broadcasted_iota(jnp.int32, (SUBLANES, tm), 0)
    route_ref[...] = (jnp.where(row8 == i1 - first, w1, 0.0) + jnp.where(row8 == i2 - first, w2, 0.0)
                      + jnp.where(row8 == ROUTE_GROUP_ROW, g_idx.astype(F32), 0.0))


def _out_call(mix, mq, mem, mem_gain, w_kv, h, w_mix, w_mem, ffn_gain, w_group, b_group, w_router, b_router,
              T, tm):
    N = h.shape[0]
    tiles_per_batch = T // tm
    pad = ROUTE_ROWS - N_GROUPS - N_EXPERTS
    wr = jnp.pad(jnp.concatenate([w_group, w_router], axis=1).T, ((0, pad), (0, 0)))
    br = jnp.broadcast_to(jnp.pad(jnp.concatenate([b_group, b_router]), (0, pad))[:, None], (ROUTE_ROWS, tm))
    wr_both = jnp.concatenate(_split2(wr), axis=0)
    row = lambda width: pl.BlockSpec((tm, width), lambda i: (i, 0))
    full = lambda a: pl.BlockSpec(a.shape, lambda i: (0,) * a.ndim)
    memspec = pl.BlockSpec((1, N_MEM, D_MODEL), lambda i: (i // tiles_per_batch, 0, 0))
    w_mix = w_mix.astype(BF16)
    w_mem = w_mem.astype(BF16)
    w_kv = w_kv.astype(BF16)
    fg = ffn_gain.reshape(1, D_MODEL)
    mg = mem_gain.reshape(1, D_MODEL)
    return pl.pallas_call(
        functools.partial(_out_kernel, tiles_per_batch),
        out_shape=(jax.ShapeDtypeStruct((N, D_MODEL), F32), jax.ShapeDtypeStruct((N, D_MODEL), BF16),
                   jax.ShapeDtypeStruct((SUBLANES, N), F32)),
        grid=(N // tm,),
        in_specs=[row(mix.shape[1]), row(MEM_WIDTH), memspec, full(mg), full(w_kv), row(D_MODEL),
                  full(w_mix), full(w_mem), full(fg), full(wr_both), full(br)],
        out_specs=(row(D_MODEL), row(D_MODEL), pl.BlockSpec((SUBLANES, tm), lambda i: (0, i))),
        scratch_shapes=[pltpu.VMEM((N_MEM, MEM_WIDTH), BF16), pltpu.VMEM((N_MEM, MEM_WIDTH), BF16)],
        compiler_params=pltpu.CompilerParams(dimension_semantics=("arbitrary",),
                                             vmem_limit_bytes=VMEM_LIMIT),
        name="out_proj_route",
    )(mix, mq, mem, mg, w_kv, h, w_mix, w_mem, fg, wr_both, br)


def _moe_kernel(final, n_super, un_ref, route_ref, h1_ref, utri_ref, wup_ref, wdn_ref, fgain_ref, o_ref,
                xs_ref, gs_ref, xg_ref, gg_ref, permt_ref, meta_ref):
    s = pl.program_id(0)
    k = pl.program_id(1)
    n_tiles, slots, _ = xs_ref.shape
    tm = un_ref.shape[0]
    row8 = lax.broadcasted_iota(jnp.int32, (SUBLANES, tm), 0)

    @pl.when((s == 0) & (k == 0))
    def _():
        xg_ref[...] = jnp.zeros_like(xg_ref)
        gg_ref[...] = jnp.zeros_like(gg_ref)

    def unpartition():
        y = h1_ref[...] + _dot(permt_ref[k], xs_ref[k])
        o_ref[...] = _rms(y, fgain_ref[...]) if final else y

    def partition():
        route = route_ref[...]
        g_idx = route[ROUTE_GROUP_ROW:ROUTE_GROUP_ROW + 1, :]
        member = jnp.where((row8 < N_GROUPS) & (row8.astype(F32) == g_idx), 1.0, 0.0)
        earlier = _dot(member.astype(BF16), utri_ref[...])
        rank = jnp.sum(member * earlier, axis=0, keepdims=True)
        count = jnp.sum(member, axis=1, keepdims=True)
        padded = jnp.floor((count + (MOE_ALIGN - 1.0)) * (1.0 / MOE_ALIGN)) * MOE_ALIGN
        starts = [jnp.zeros((1, 1), F32)]
        for g in range(1, N_GROUPS):
            starts.append(starts[-1] + padded[g - 1:g, :])
        seg = jnp.zeros((SUBLANES, 1), F32)
        row81 = lax.broadcasted_iota(jnp.int32, (SUBLANES, 1), 0)
        for g in range(1, N_GROUPS):
            seg = jnp.where(row81 == g, starts[g], seg)
        dest = rank + jnp.sum(member * seg, axis=0, keepdims=True)
        perm = jnp.where(lax.broadcasted_iota(jnp.int32, (slots, tm), 0) == dest.astype(jnp.int32),
                         1.0, 0.0).astype(BF16)
        d_hi = jnp.floor(dest * (1.0 / 32.0))
        d_lo = dest - 32.0 * d_hi
        digits = jnp.where(row8 == 0, d_hi, jnp.where(row8 == 1, d_lo, 0.0)).astype(BF16)
        pick = (lax.broadcasted_iota(jnp.int32, (SUBLANES, LANES), 0)
                == lax.broadcasted_iota(jnp.int32, (SUBLANES, LANES), 1)).astype(BF16)
        dig_c = _dot_tn(digits, pick)
        dest_col = (32.0 * dig_c[:, 0:1] + dig_c[:, 1:2]).astype(jnp.int32)
        permt_ref[k] = jnp.where(lax.broadcasted_iota(jnp.int32, (tm, slots), 1) == dest_col,
                                 1.0, 0.0).astype(BF16)
        xs_ref[k] = _dot(perm, un_ref[...]).astype(BF16)
        r_hi, r_lo = _split2(route)
        gs_ref[k] = _dot_nt(perm, jnp.concatenate([r_hi, r_lo], axis=0))
        for g in range(N_GROUPS):
            meta_ref[(k * N_GROUPS + g) * 2] = starts[g][0, 0].astype(jnp.int32)
            meta_ref[(k * N_GROUPS + g) * 2 + 1] = padded[g, 0].astype(jnp.int32)

    pl.when(s > 0)(unpartition)
    pl.when(s < n_super)(partition)

    def expert_block(g, r0, rows):
        xb = xg_ref[pl.ds(r0, rows), :]
        gsb = gg_ref[pl.ds(r0, rows), :]
        acc = jnp.zeros((rows, D_MODEL), F32)
        for j in range(EXPERTS_PER_GROUP):
            e = g * EXPERTS_PER_GROUP + j
            hh = _dot(xb, wup_ref[e])
            gate = gsb[:, j:j + 1] + gsb[:, SUBLANES + j:SUBLANES + j + 1]
            act = _silu(hh[:, :EXPERT_FF]) * hh[:, EXPERT_FF:] * gate
            acc = acc + _dot(act.astype(BF16), wdn_ref[e])
        xg_ref[pl.ds(r0, rows), :] = acc.astype(BF16)

    def copy_segments(g, gather):
        off = jnp.int32(0)
        for t in range(n_tiles):
            st = meta_ref[(t * N_GROUPS + g) * 2]
            ln = meta_ref[(t * N_GROUPS + g) * 2 + 1]

            def copy_rows(src0, dst0, rows, t=t):
                src = pl.ds(pl.multiple_of(src0, MOE_ALIGN), rows)
                dst = pl.ds(pl.multiple_of(dst0, MOE_ALIGN), rows)
                if gather:
                    xg_ref[dst, :] = xs_ref[t, src, :]
                    gg_ref[dst, :] = gs_ref[t, src, :]
                else:
                    xs_ref[t, src, :] = xg_ref[dst, :]

            def copy_body(i, carry, st=st, off=off, copy_rows=copy_rows):
                copy_rows(st + i * MOE_COPY, off + i * MOE_COPY, MOE_COPY)
                return carry

            n_copy = lax.shift_right_logical(ln, MOE_COPY_SHIFT)
            lax.fori_loop(0, n_copy, copy_body, 0)
            done = n_copy * MOE_COPY

            def tail_body(i, carry, st=st, off=off, done=done, copy_rows=copy_rows):
                copy_rows(st + done + i * MOE_ALIGN, off + done + i * MOE_ALIGN, MOE_ALIGN)
                return carry

            lax.fori_loop(0, lax.shift_right_logical(ln - done, MOE_ALIGN_SHIFT), tail_body, 0)
            off = off + ln
        return off

    @pl.when((s < n_super) & (k == n_tiles - 1))
    def _():
        for g in range(N_GROUPS):
            total = copy_segments(g, True)
            n_big = total // MOE_BIG
            rem = total - n_big * MOE_BIG
            more = rem > MOE_SMALL_MAX * MOE_BLOCK
            n_big = n_big + more.astype(jnp.int32)
            n_small = jnp.where(more, 0, (rem + (MOE_BLOCK - 1)) // MOE_BLOCK)
            small0 = n_big * MOE_BIG

            def big_body(i, carry, g=g):
                expert_block(g, pl.multiple_of(i * MOE_BIG, MOE_ALIGN), MOE_BIG)
                return carry

            def small_body(i, carry, g=g, small0=small0):
                expert_block(g, pl.multiple_of(small0 + i * MOE_BLOCK, MOE_ALIGN), MOE_BLOCK)
                return carry

            lax.fori_loop(0, n_big, big_body, 0)
            lax.fori_loop(0, n_small, small_body, 0)

            copy_segments(g, False)


def _moe(un, route, h1, w_up, w_down, final_gain, final, tm):
    N = un.shape[0]
    n_tiles = min(MOE_SUPER, N // tm)
    n_super = N // (tm * n_tiles)
    slots = tm + N_GROUPS * MOE_ALIGN
    group_rows = n_tiles * slots + MOE_BIG
    utri = (jnp.arange(tm)[:, None] < jnp.arange(tm)[None, :]).astype(BF16)
    in_tile = lambda s, k: jnp.minimum(s, n_super - 1) * n_tiles + k
    out_tile = lambda s, k: jnp.maximum(s - 1, 0) * n_tiles + jnp.where(s > 0, k, 0)
    once = lambda a: pl.BlockSpec(a.shape, lambda s, k: (0,) * a.ndim, pipeline_mode=pl.Buffered(1))
    fg = final_gain.reshape(1, D_MODEL)
    return pl.pallas_call(
        functools.partial(_moe_kernel, final, n_super),
        out_shape=jax.ShapeDtypeStruct((N, D_MODEL), F32),
        grid=(n_super + 1, n_tiles),
        in_specs=[pl.BlockSpec((tm, D_MODEL), lambda s, k: (in_tile(s, k), 0)),
                  pl.BlockSpec((SUBLANES, tm), lambda s, k: (0, in_tile(s, k))),
                  pl.BlockSpec((tm, D_MODEL), lambda s, k: (out_tile(s, k), 0)),
                  once(utri), once(w_up), once(w_down), once(fg)],
        out_specs=pl.BlockSpec((tm, D_MODEL), lambda s, k: (out_tile(s, k), 0)),
        scratch_shapes=[pltpu.VMEM((n_tiles, slots, D_MODEL), BF16),
                        pltpu.VMEM((n_tiles, slots, 2 * SUBLANES), F32),
                        pltpu.VMEM((group_rows, D_MODEL), BF16),
                        pltpu.VMEM((group_rows, 2 * SUBLANES), F32),
                        pltpu.VMEM((n_tiles, tm, slots), BF16),
                        pltpu.SMEM((n_tiles * N_GROUPS * 2,), jnp.int32)],
        compiler_params=pltpu.CompilerParams(dimension_semantics=("arbitrary", "arbitrary"),
                                             vmem_limit_bytes=VMEM_LIMIT),
        name="moe",
    )(un, route, h1, utri, w_up, w_down, fg)


def kernel(x, mem, mix_norm, ffn_norm, mem_norm, final_norm, gla_w_in, gla_w_gate_up, gla_b_gate, gla_out_norm,
           hg_w_in, hg_lower_bounds, hg_out_norm, w_mem_kv, w_out, w_group, b_group, w_router, b_router,
           w_up, w_down):
    B, T, _ = x.shape
    N = B * T
    depth = mix_norm.shape[0]
    tm_proj = min(1024, T)
    nb_scan = 1
    tb_scan = tm_proj
    tm_out = min(1024, T)
    tm_moe = min(MOE_TILE, N)
    h = x.reshape(N, D_MODEL)
    for layer in range(depth):
        j = layer // 2
        if layer % 2 == 0:
            q, k, v, sg, b, mq, bmin, w_up_bf16, w_down_bf16 = _gla_proj(
                h, mix_norm[layer], gla_w_in[j], gla_w_gate_up[j], gla_b_gate[j], w_up, w_down, layer, tm_proj)
            heads, dk, dv_pad, dv, out_gain = GLA_HEADS, GLA_DK, GLA_DV_PAD, GLA_DV, gla_out_norm[j]
        else:
            q, k, v, sg, b, mq, bmin, w_up_bf16, w_down_bf16 = _hg_proj(
                h, mix_norm[layer], hg_w_in[j], hg_lower_bounds, w_up, w_down, layer, tm_proj)
            heads, dk, dv_pad, dv, out_gain = HG_HEADS, HG_DK, HG_DV, HG_DV, hg_out_norm[j]
        r3 = lambda a: a.reshape(B, T, a.shape[-1])
        bmin = bmin.reshape(B, T // tb_scan, SUBLANES, LANES)
        mix = _scan(r3(q), r3(k), r3(b), bmin, r3(v), r3(sg), out_gain, heads, dk, dv_pad, dv, tb_scan, nb_scan)
        mix = mix.reshape(N, MIX_WIDTH)
        h1, un, gates = _out_call(mix, mq, mem, mem_norm[layer], w_mem_kv[layer], h,
                                  w_out[layer, :MIX_WIDTH], w_out[layer, MIX_WIDTH:], ffn_norm[layer],
                                  w_group[layer], b_group[layer], w_router[layer], b_router[layer], T, tm_out)
        h = _moe(un, gates, h1, w_up_bf16, w_down_bf16, final_norm, layer == depth - 1, tm_moe)
    return h.reshape(B, T, D_MODEL)
```

```python
import functools

import jax
import jax.numpy as jnp
from jax import lax
from jax.experimental import pallas as pl
from jax.experimental.pallas import tpu as pltpu

D_MODEL = 1024
N_MEM = 256
MIX_WIDTH = 768
MEM_HEADS = 4
MEM_HEAD_DIM = 64
MEM_WIDTH = 256
CHUNK = 64
GLA_HEADS = 4
GLA_KEY_WIDTH = 512
GLA_DK = 128
GLA_DV = 192
GLA_DV_PAD = 256
GLA_GATE_RANK = 16
GLA_GATE_NORMALIZER = 16.0
HG_HEADS = 6
HG_DK = 128
HG_DV = 128
N_GROUPS = 4
EXPERTS_PER_GROUP = 4
N_EXPERTS = 16
EXPERT_FF = 256
NORM_EPS = 1e-6
LANES = 128
SUBLANES = 8
MOE_BLOCK = 128
CUMSUM_SPAN = 256
MOE_ALIGN_SHIFT = 4
MOE_ALIGN = 1 << MOE_ALIGN_SHIFT
MOE_COPY_SHIFT = 6
MOE_COPY = 1 << MOE_COPY_SHIFT
MOE_TILE = 512
MOE_SUPER = 4
MOE_BIG = MOE_SUPER * (MOE_TILE // N_GROUPS + MOE_ALIGN)
MOE_SMALL_MAX = 2
ROUTE_GROUP_ROW = 4
ROUTE_ROWS = 32
SAFE_EXPONENT = 80.0
SCAN_JOINS = (4, 2, 1)
V7X_VMEM_BYTES = 64 * 1024 * 1024
VMEM_LIMIT = V7X_VMEM_BYTES - 8 * 1024 * 1024

F32 = jnp.float32
BF16 = jnp.bfloat16


def _dot(a, b):
    return jnp.dot(a, b, preferred_element_type=F32)


def _dot_nt(a, b):
    return lax.dot_general(a, b, (((1,), (1,)), ((), ())), preferred_element_type=F32)


def _dot_tn(a, b):
    return lax.dot_general(a, b, (((0,), (0,)), ((), ())), preferred_element_type=F32)


def _split2(x):
    hi = x.astype(BF16)
    lo = (x - hi.astype(F32)).astype(BF16)
    return hi, lo


def _rms(x, gain):
    ms = jnp.mean(x * x, axis=-1, keepdims=True)
    return x * lax.rsqrt(ms + NORM_EPS) * gain


def _log_sigmoid(x):
    return jnp.minimum(x, 0.0) - jnp.log(1.0 + jnp.exp(-jnp.abs(x)))


def _sigmoid(x):
    return 1.0 / (1.0 + jnp.exp(-x))


def _silu(x):
    return x * _sigmoid(x)


def _chunk_cumsum(tri_ref, x):
    tri = tri_ref[...]
    span = tri.shape[0]
    parts = []
    for r in range(x.shape[0] // span):
        hi, lo = _split2(x[r * span:(r + 1) * span])
        parts.append(_dot(tri, hi) + _dot(tri, lo))
    return jnp.concatenate(parts, axis=0) if len(parts) > 1 else parts[0]


_GQ, _GK = 0, GLA_KEY_WIDTH
_GV = 2 * GLA_KEY_WIDTH
_GG = _GV + MIX_WIDTH
_GR = _GG + MIX_WIDTH
_GM = _GR + LANES
_GW = _GM + MEM_WIDTH


def _store_padded_heads(ref, x):
    zeros = jnp.zeros((x.shape[0], GLA_DV_PAD - GLA_DV), ref.dtype)
    for hd in range(GLA_HEADS):
        ref[:, hd * GLA_DV_PAD:hd * GLA_DV_PAD + GLA_DV] = x[:, hd * GLA_DV:(hd + 1) * GLA_DV].astype(ref.dtype)
        ref[:, hd * GLA_DV_PAD + GLA_DV:(hd + 1) * GLA_DV_PAD] = zeros


def _store_decay(b_ref, bmin_ref, b, q, k):
    b_ref[...] = b

    def tile_max(x):
        return jnp.max(jnp.max(x, axis=0, keepdims=True), axis=1, keepdims=True)

    lowest = -tile_max(-b)
    magnitude = jnp.maximum(tile_max(jnp.abs(q)), 1.0) * tile_max(jnp.abs(k)) * (2.0 * LANES)
    log_mag = jnp.log(jnp.maximum(magnitude, 1.0))
    sub = lax.broadcasted_iota(jnp.int32, (SUBLANES, LANES), 0)
    bmin_ref[0] = jnp.where(sub == 0, lowest, log_mag)


def _round_experts(wup_ref, wdn_ref, wup_bf_ref, wdn_bf_ref):
    wup_bf_ref[...] = wup_ref[...].astype(BF16)
    wdn_bf_ref[...] = wdn_ref[...].astype(BF16)


def _gla_proj_kernel(h_ref, gain_ref, w_ref, wg_ref, bg_ref, tri_ref, wup_ref, wdn_ref,
                     q_ref, k_ref, v_ref, sg_ref, b_ref, mq_ref, bmin_ref, wup_bf_ref, wdn_bf_ref):
    _round_experts(wup_ref, wdn_ref, wup_bf_ref, wdn_bf_ref)
    u = _rms(h_ref[...], gain_ref[...]).astype(BF16)
    q = _dot(u, w_ref[:, _GQ:_GK]) * (GLA_DK ** -0.5)
    k = _dot(u, w_ref[:, _GK:_GV])
    q_ref[...] = q.astype(BF16)
    k_ref[...] = k.astype(BF16)
    _store_padded_heads(v_ref, _dot(u, w_ref[:, _GV:_GG]))
    _store_padded_heads(sg_ref, _silu(_dot(u, w_ref[:, _GG:_GR])))
    mq_ref[...] = _dot(u, w_ref[:, _GM:_GW]).astype(BF16)
    r = _dot(u, w_ref[:, _GR:_GM])
    r_hi = r.astype(BF16)
    lane = lax.broadcasted_iota(jnp.int32, r.shape, 1)
    is_lo = (lane >= GLA_GATE_RANK) & (lane < 2 * GLA_GATE_RANK)
    r_parts = jnp.where(is_lo, r - r_hi.astype(F32), r_hi.astype(F32)).astype(BF16)
    logit = _dot(r_parts, wg_ref[...]) + bg_ref[...]
    _store_decay(b_ref, bmin_ref, _chunk_cumsum(tri_ref, _log_sigmoid(logit) * (1.0 / GLA_GATE_NORMALIZER)),
                 q, k)


_HQ, _HF, _HI, _HGG, _HM = 0, MIX_WIDTH, 2 * MIX_WIDTH, 3 * MIX_WIDTH, 4 * MIX_WIDTH
_HW = _HM + MEM_WIDTH


def _hg_proj_kernel(layer, h_ref, gain_ref, w_ref, lbp_ref, tri_ref, wup_ref, wdn_ref,
                    q_ref, k_ref, v_ref, sg_ref, b_ref, mq_ref, bmin_ref, wup_bf_ref, wdn_bf_ref):
    _round_experts(wup_ref, wdn_ref, wup_bf_ref, wdn_bf_ref)
    u = _rms(h_ref[...], gain_ref[...]).astype(BF16)
    p = lbp_ref[...]
    p = jnp.exp(p - jnp.max(p, axis=0, keepdims=True))
    p = p / jnp.sum(p, axis=0, keepdims=True)
    lb = jnp.sum(p[0:layer + 1], axis=0, keepdims=True) - p[0:1]
    q = _silu(_dot(u, w_ref[:, _HQ:_HF]))
    z = _dot(u, w_ref[:, _HF:_HI])
    e = jnp.exp(-jnp.abs(z))
    k = (1.0 - lb) * (jnp.where(z >= 0.0, e, 1.0) / (1.0 + e))
    q_ref[...] = q.astype(BF16)
    k_ref[...] = k.astype(BF16)
    v_ref[...] = _dot(u, w_ref[:, _HI:_HGG]).astype(BF16)
    sg_ref[...] = _silu(_dot(u, w_ref[:, _HGG:_HM])).astype(BF16)
    mq_ref[...] = _dot(u, w_ref[:, _HM:_HW]).astype(BF16)
    a = jnp.log(lb)
    c = jnp.log1p(-lb) + (jnp.minimum(z, 0.0) - jnp.log(1.0 + e))
    log_f = jnp.maximum(a, c) + jnp.log(1.0 + jnp.exp(-jnp.abs(a - c)))
    _store_decay(b_ref, bmin_ref, _chunk_cumsum(tri_ref, log_f), q, k)


def _mixer_kernel(proj_kernel, n_in, heads, dk, dv_pad, dv, tiles_per_batch, *refs):
    ins, gain_ref = refs[:n_in], refs[n_in]
    mix_ref, mq_ref, wup_bf_ref, wdn_bf_ref = refs[n_in + 1:n_in + 5]
    q_s, k_s, v_s, sg_s, b_s, bmin_s, st_ref, kf_ref = refs[n_in + 5:]
    proj_kernel(*ins, q_s.at[0], k_s.at[0], v_s.at[0], sg_s.at[0], b_s.at[0], mq_ref, bmin_s.at[0],
                wup_bf_ref, wdn_bf_ref)
    first_of_row = pl.program_id(0) % tiles_per_batch == 0
    _scan_kernel(heads, dk, dv_pad, dv, q_s.shape[1] // CHUNK, first_of_row,
                 q_s, k_s, b_s, bmin_s, v_s, sg_s, gain_ref, mix_ref, st_ref, kf_ref)


def _proj_call(kernel, h, gain, w, extra, w_up, w_down, layer, out_gain, heads, dk, dv_pad, dv, T, tm, name):
    N = h.shape[0]
    kw, vw = heads * dk, heads * dv_pad
    n_steps = N // tm
    per_step = max(1, N_EXPERTS // n_steps)
    assert N_EXPERTS % per_step == 0 and (n_steps * per_step) % N_EXPERTS == 0
    expert_blk = lambda i: i % (N_EXPERTS // per_step)
    w_in_spec = lambda a: pl.BlockSpec((None, per_step) + a.shape[2:], lambda i: (layer, expert_blk(i), 0, 0))
    w_out_spec = lambda a: pl.BlockSpec((per_step,) + a.shape[2:], lambda i: (expert_blk(i), 0, 0))
    span = min(CUMSUM_SPAN, tm)
    tri = (jnp.arange(span)[:, None] >= jnp.arange(span)[None, :]) & (
        jnp.arange(span)[:, None] // CHUNK == jnp.arange(span)[None, :] // CHUNK)
    tri = tri.astype(BF16)
    row = lambda width: pl.BlockSpec((tm, width), lambda i: (i, 0))
    full = lambda a: pl.BlockSpec(a.shape, lambda i: (0,) * a.ndim)
    ins = [h, gain.reshape(1, D_MODEL), w] + list(extra) + [tri]
    gain_row = jnp.pad(out_gain, (0, dv_pad - dv)).reshape(1, dv_pad)
    mix, mq, w_up_bf16, w_down_bf16 = pl.pallas_call(
        functools.partial(_mixer_kernel, kernel, len(ins) + 2, heads, dk, dv_pad, dv, T // tm),
        out_shape=(jax.ShapeDtypeStruct((n_steps, tm, heads * dv), BF16),
                   jax.ShapeDtypeStruct((N, MEM_WIDTH), BF16),
                   jax.ShapeDtypeStruct(w_up.shape[1:], BF16), jax.ShapeDtypeStruct(w_down.shape[1:], BF16)),
        grid=(n_steps,),
        in_specs=([row(D_MODEL)] + [full(a) for a in ins[1:]] + [w_in_spec(w_up), w_in_spec(w_down)]
                  + [full(gain_row)]),
        out_specs=(pl.BlockSpec((1, tm, heads * dv), lambda i: (i, 0, 0)), row(MEM_WIDTH),
                   w_out_spec(w_up), w_out_spec(w_down)),
        scratch_shapes=[pltpu.VMEM((1, tm, kw), BF16), pltpu.VMEM((1, tm, kw), BF16),
                        pltpu.VMEM((1, tm, vw), BF16), pltpu.VMEM((1, tm, vw), BF16),
                        pltpu.VMEM((1, tm, kw), F32), pltpu.VMEM((1, 1, SUBLANES, LANES), F32),
                        pltpu.VMEM((heads, dv_pad, dk), F32), pltpu.VMEM((CHUNK, dk), F32)],
        compiler_params=pltpu.CompilerParams(dimension_semantics=("arbitrary",),
                                             vmem_limit_bytes=VMEM_LIMIT),
        name=name,
    )(*ins, w_up, w_down, gain_row)
    return mix.reshape(N, heads * dv), mq, w_up_bf16, w_down_bf16


def _gla_mixer(h, gain, w_in, w_gate_up, b_gate, out_gain, w_up, w_down, layer, T, tm):
    q, k, v, g, r, mq = jnp.split(
        w_in, [GLA_KEY_WIDTH, 2 * GLA_KEY_WIDTH, 2 * GLA_KEY_WIDTH + MIX_WIDTH,
               2 * GLA_KEY_WIDTH + 2 * MIX_WIDTH, 2 * GLA_KEY_WIDTH + 2 * MIX_WIDTH + GLA_GATE_RANK], axis=1)
    r3 = jnp.pad(jnp.concatenate([r, r, r], axis=1), ((0, 0), (0, LANES - 3 * GLA_GATE_RANK)))
    w = jnp.concatenate([q, k, v, g, r3, mq], axis=1).astype(BF16)
    wg_hi, wg_lo = _split2(w_gate_up)
    wg = jnp.pad(jnp.concatenate([wg_hi, wg_hi, wg_lo], axis=0), ((0, LANES - 3 * GLA_GATE_RANK), (0, 0)))
    return _proj_call(_gla_proj_kernel, h, gain, w, [wg, b_gate.reshape(1, GLA_KEY_WIDTH)], w_up, w_down, layer,
                      out_gain, GLA_HEADS, GLA_DK, GLA_DV_PAD, GLA_DV, T, tm, "gla_mixer")


def _hg_mixer(h, gain, w_in, lower_bound_params, out_gain, w_up, w_down, layer, T, tm):
    return _proj_call(functools.partial(_hg_proj_kernel, layer), h, gain, w_in.astype(BF16),
                      [lower_bound_params], w_up, w_down, layer, out_gain, HG_HEADS, HG_DK, HG_DV, HG_DV, T, tm,
                      "hg_mixer")


def _scan_kernel(heads, dk, dv_pad, dv, n_chunks, first_block,
                 q_ref, k_ref, b_ref, bmin_ref, v_ref, sg_ref, gain_ref, o_ref, st_ref, kf_ref):
    @pl.when(first_block)
    def _():
        st_ref[...] = jnp.zeros_like(st_ref)

    gain = gain_ref[...]
    col = lax.broadcasted_iota(jnp.int32, (CHUNK, CHUNK), 1)

    n_batch = q_ref.shape[0]

    def finish(bi, hd, rows, scores, qd, kl, eb_last):
        vc = slice(hd * dv_pad, (hd + 1) * dv_pad)
        v = v_ref[bi, rows, vc]
        st = st_ref[bi * heads + hd]
        out = _dot(scores.astype(BF16), v) + _dot_nt(qd, st.astype(BF16))
        st_ref[bi * heads + hd] = _dot_tn(v, kl) + st * eb_last
        ms = jnp.sum(out * out, axis=-1, keepdims=True) * (1.0 / dv)
        y = out * lax.rsqrt(ms + NORM_EPS) * gain * sg_ref[bi, rows, vc].astype(F32)
        o_ref[bi, rows, hd * dv:(hd + 1) * dv] = y[:, :dv].astype(BF16)

    def load(bi, hd, rows):
        kc = slice(hd * dk, (hd + 1) * dk)
        return (q_ref[bi, rows, kc].astype(F32), k_ref[bi, rows, kc].astype(F32), b_ref[bi, rows, kc])

    summary = bmin_ref[...]
    lowest = jnp.min(summary[:, :, 0:1, :])
    log_mag = jnp.max(summary[:, :, 1:2, :])

    def factored(n_join):
        span = n_join * CHUNK
        ri = lax.broadcasted_iota(jnp.int32, (span, span), 0)
        ci = lax.broadcasted_iota(jnp.int32, (span, span), 1)
        for c in range(n_chunks // n_join):
            rows = slice(c * span, (c + 1) * span)
            for bi in range(n_batch):
                for hd in range(heads):
                    q, k, b = load(bi, hd, rows)
                    parts = [b[:CHUNK]]
                    for i in range(1, n_join):
                        parts.append(b[i * CHUNK:(i + 1) * CHUNK] + parts[-1][CHUNK - 1:CHUNK])
                    b = jnp.concatenate(parts, axis=0) if n_join > 1 else b
                    eb = jnp.exp(b)
                    eb_last = eb[span - 1:span, :]
                    qd = (q * eb).astype(BF16)
                    kd = k * jnp.exp(-b)
                    scores = _dot_nt(qd, kd.astype(BF16))
                    finish(bi, hd, rows, jnp.where(ri >= ci, scores, 0.0), qd, (kd * eb_last).astype(BF16),
                           eb_last)

    joins = [j for j in SCAN_JOINS if n_chunks % j == 0]
    taken = False
    for j in joins:
        ok = j * lowest - log_mag >= -SAFE_EXPONENT
        pl.when(ok if taken is False else ok & jnp.logical_not(taken))(functools.partial(factored, j))
        taken = ok if taken is False else taken | ok
    all_safe = taken

    @pl.when(jnp.logical_not(all_safe))
    def _():
        for bi, hd in [(bi, hd) for bi in range(n_batch) for hd in range(heads)]:
            kc = slice(hd * dk, (hd + 1) * dk)

            def chunk_body(c, carry, bi=bi, hd=hd, kc=kc):
                start = pl.multiple_of(c * CHUNK, CHUNK)
                rows = pl.ds(start, CHUNK)
                q, k, b = load(bi, hd, rows)
                kf_ref[...] = k

                def col_body(j, sc):
                    base = pl.multiple_of((j >> 3) << 3, SUBLANES)
                    pick = lax.broadcasted_iota(jnp.int32, (SUBLANES, dk), 0) == (j & (SUBLANES - 1))
                    kj = jnp.sum(jnp.where(pick, kf_ref[pl.ds(base, SUBLANES), :], 0.0), axis=0, keepdims=True)
                    b8 = b_ref[bi, pl.ds(pl.multiple_of(start + base, SUBLANES), SUBLANES), kc]
                    bj = jnp.sum(jnp.where(pick, b8, 0.0), axis=0, keepdims=True)
                    rid = lax.broadcasted_iota(jnp.int32, (CHUNK, dk), 0)
                    dec = jnp.exp(jnp.where(rid >= j, b - bj, -jnp.inf))
                    colv = jnp.sum(q * kj * dec, axis=-1, keepdims=True)
                    return jnp.where(col == j, colv, sc)

                scores = lax.fori_loop(0, CHUNK, col_body, jnp.zeros((CHUNK, CHUNK), F32))
                b_last = b[CHUNK - 1:CHUNK, :]
                finish(bi, hd, rows, scores, (q * jnp.exp(b)).astype(BF16),
                       (k * jnp.exp(b_last - b)).astype(BF16), jnp.exp(b_last))
                return carry

            lax.fori_loop(0, n_chunks, chunk_body, 0)


def _out_kernel(tiles_per_batch, mix_ref, mq_ref, mem_ref, mgain_ref, wkv_ref, h_ref, wmix_ref, wmem_ref,
                fgain_ref, wr_ref, br_ref, h1_ref, un_ref, route_ref, mk_ref, mv_ref):
    tm = mix_ref.shape[0]

    @pl.when(pl.program_id(0) % tiles_per_batch == 0)
    def _():
        kv = _dot(_rms(mem_ref[0], mgain_ref[...]).astype(BF16), wkv_ref[...])
        mk_ref[...] = kv[:, :MEM_WIDTH].astype(BF16)
        mv_ref[...] = kv[:, MEM_WIDTH:].astype(BF16)

    mk = mk_ref[...]
    mv = mv_ref[...]
    lane_w = lax.broadcasted_iota(jnp.int32, (1, MEM_WIDTH), 1)
    mq = mq_ref[...] * (MEM_HEAD_DIM ** -0.5)
    mem_o = jnp.zeros((tm, MEM_WIDTH), F32)
    for hd in range(MEM_HEADS):
        in_head = (lane_w >= hd * MEM_HEAD_DIM) & (lane_w < (hd + 1) * MEM_HEAD_DIM)
        s = _dot_nt(jnp.where(in_head, mq, jnp.zeros_like(mq)), mk)
        e = jnp.exp(s - jnp.max(s, axis=-1, keepdims=True))
        denom = jnp.sum(e, axis=-1, keepdims=True)
        mem_o = mem_o + _dot(e.astype(BF16), jnp.where(in_head, mv, jnp.zeros_like(mv))) * (1.0 / denom)
    h1 = h_ref[...] + _dot(mix_ref[...], wmix_ref[...]) + _dot(mem_o.astype(BF16), wmem_ref[...])
    h1_ref[...] = h1
    un = _rms(h1, fgain_ref[...])
    un_ref[...] = un.astype(BF16)

    both = _dot_nt(wr_ref[...], un.astype(BF16))
    lg = both[:ROUTE_ROWS] + both[ROUTE_ROWS:] + br_ref[...]
    row = lax.broadcasted_iota(jnp.int32, (ROUTE_ROWS, tm), 0)
    neg = -jnp.inf
    gl = jnp.where(row < N_GROUPS, lg, neg)
    gmax = jnp.max(gl, axis=0, keepdims=True)
    g_idx = jnp.min(jnp.where(gl == gmax, row, ROUTE_ROWS), axis=0, keepdims=True)
    g_w = 1.0 / jnp.sum(jnp.exp(gl - gmax), axis=0, keepdims=True)
    first = N_GROUPS + g_idx * EXPERTS_PER_GROUP
    el = jnp.where((row >= first) & (row < first + EXPERTS_PER_GROUP), lg, neg)
    t1 = jnp.max(el, axis=0, keepdims=True)
    i1 = jnp.min(jnp.where(el == t1, row, ROUTE_ROWS), axis=0, keepdims=True)
    el2 = jnp.where(row == i1, neg, el)
    t2 = jnp.max(el2, axis=0, keepdims=True)
    i2 = jnp.min(jnp.where(el2 == t2, row, ROUTE_ROWS), axis=0, keepdims=True)
    e2 = jnp.exp(t2 - t1)
    w1 = g_w / (1.0 + e2)
    w2 = g_w * e2 / (1.0 + e2)
    row8 = lax.broadcasted_iota(jnp.int32, (SUBLANES, tm), 0)
    route_ref[...] = (jnp.where(row8 == i1 - first, w1, 0.0) + jnp.where(row8 == i2 - first, w2, 0.0)
                      + jnp.where(row8 == ROUTE_GROUP_ROW, g_idx.astype(F32), 0.0))


def _out_call(mix, mq, mem, mem_gain, w_kv, h, w_mix, w_mem, ffn_gain, w_group, b_group, w_router, b_router,
              T, tm):
    N = h.shape[0]
    tiles_per_batch = T // tm
    pad = ROUTE_ROWS - N_GROUPS - N_EXPERTS
    wr = jnp.pad(jnp.concatenate([w_group, w_router], axis=1).T, ((0, pad), (0, 0)))
    br = jnp.broadcast_to(jnp.pad(jnp.concatenate([b_group, b_router]), (0, pad))[:, None], (ROUTE_ROWS, tm))
    wr_both = jnp.concatenate(_split2(wr), axis=0)
    row = lambda width: pl.BlockSpec((tm, width), lambda i: (i, 0))
    full = lambda a: pl.BlockSpec(a.shape, lambda i: (0,) * a.ndim)
    memspec = pl.BlockSpec((1, N_MEM, D_MODEL), lambda i: (i // tiles_per_batch, 0, 0))
    w_mix = w_mix.astype(BF16)
    w_mem = w_mem.astype(BF16)
    w_kv = w_kv.astype(BF16)
    fg = ffn_gain.reshape(1, D_MODEL)
    mg = mem_gain.reshape(1, D_MODEL)
    return pl.pallas_call(
        functools.partial(_out_kernel, tiles_per_batch),
        out_shape=(jax.ShapeDtypeStruct((N, D_MODEL), F32), jax.ShapeDtypeStruct((N, D_MODEL), BF16),
                   jax.ShapeDtypeStruct((SUBLANES, N), F32)),
        grid=(N // tm,),
        in_specs=[row(mix.shape[1]), row(MEM_WIDTH), memspec, full(mg), full(w_kv), row(D_MODEL),
                  full(w_mix), full(w_mem), full(fg), full(wr_both), full(br)],
        out_specs=(row(D_MODEL), row(D_MODEL), pl.BlockSpec((SUBLANES, tm), lambda i: (0, i))),
        scratch_shapes=[pltpu.VMEM((N_MEM, MEM_WIDTH), BF16), pltpu.VMEM((N_MEM, MEM_WIDTH), BF16)],
        compiler_params=pltpu.CompilerParams(dimension_semantics=("arbitrary",),
                                             vmem_limit_bytes=VMEM_LIMIT),
        name="out_proj_route",
    )(mix, mq, mem, mg, w_kv, h, w_mix, w_mem, fg, wr_both, br)


def _moe_kernel(final, n_super, un_ref, route_ref, h1_ref, utri_ref, wup_ref, wdn_ref, fgain_ref, o_ref,
                xs_ref, gs_ref, xg_ref, gg_ref, permt_ref, meta_ref):
    s = pl.program_id(0)
    k = pl.program_id(1)
    n_tiles, slots, _ = xs_ref.shape
    tm = un_ref.shape[0]
    row8 = lax.broadcasted_iota(jnp.int32, (SUBLANES, tm), 0)

    @pl.when((s == 0) & (k == 0))
    def _():
        xg_ref[...] = jnp.zeros_like(xg_ref)
        gg_ref[...] = jnp.zeros_like(gg_ref)

    def unpartition():
        y = h1_ref[...] + _dot(permt_ref[k], xs_ref[k])
        o_ref[...] = _rms(y, fgain_ref[...]) if final else y

    def partition():
        route = route_ref[...]
        g_idx = route[ROUTE_GROUP_ROW:ROUTE_GROUP_ROW + 1, :]
        member = jnp.where((row8 < N_GROUPS) & (row8.astype(F32) == g_idx), 1.0, 0.0)
        earlier = _dot(member.astype(BF16), utri_ref[...])
        rank = jnp.sum(member * earlier, axis=0, keepdims=True)
        count = jnp.sum(member, axis=1, keepdims=True)
        padded = jnp.floor((count + (MOE_ALIGN - 1.0)) * (1.0 / MOE_ALIGN)) * MOE_ALIGN
        starts = [jnp.zeros((1, 1), F32)]
        for g in range(1, N_GROUPS):
            starts.append(starts[-1] + padded[g - 1:g, :])
        seg = jnp.zeros((SUBLANES, 1), F32)
        row81 = lax.broadcasted_iota(jnp.int32, (SUBLANES, 1), 0)
        for g in range(1, N_GROUPS):
            seg = jnp.where(row81 == g, starts[g], seg)
        dest = rank + jnp.sum(member * seg, axis=0, keepdims=True)
        perm = jnp.where(lax.broadcasted_iota(jnp.int32, (slots, tm), 0) == dest.astype(jnp.int32),
                         1.0, 0.0).astype(BF16)
        d_hi = jnp.floor(dest * (1.0 / 32.0))
        d_lo = dest - 32.0 * d_hi
        digits = jnp.where(row8 == 0, d_hi, jnp.where(row8 == 1, d_lo, 0.0)).astype(BF16)
        pick = (lax.broadcasted_iota(jnp.int32, (SUBLANES, LANES), 0)
                == lax.broadcasted_iota(jnp.int32, (SUBLANES, LANES), 1)).astype(BF16)
        dig_c = _dot_tn(digits, pick)
        dest_col = (32.0 * dig_c[:, 0:1] + dig_c[:, 1:2]).astype(jnp.int32)
        permt_ref[k] = jnp.where(lax.broadcasted_iota(jnp.int32, (tm, slots), 1) == dest_col,
                                 1.0, 0.0).astype(BF16)
        xs_ref[k] = _dot(perm, un_ref[...]).astype(BF16)
        r_hi, r_lo = _split2(route)
        gs_ref[k] = _dot_nt(perm, jnp.concatenate([r_hi, r_lo], axis=0))
        for g in range(N_GROUPS):
            meta_ref[(k * N_GROUPS + g) * 2] = starts[g][0, 0].astype(jnp.int32)
            meta_ref[(k * N_GROUPS + g) * 2 + 1] = padded[g, 0].astype(jnp.int32)

    pl.when(s > 0)(unpartition)
    pl.when(s < n_super)(partition)

    def expert_block(g, r0, rows):
        xb = xg_ref[pl.ds(r0, rows), :]
        gsb = gg_ref[pl.ds(r0, rows), :]
        acc = jnp.zeros((rows, D_MODEL), F32)
        for j in range(EXPERTS_PER_GROUP):
            e = g * EXPERTS_PER_GROUP + j
            hh = _dot(xb, wup_ref[e])
            gate = gsb[:, j:j + 1] + gsb[:, SUBLANES + j:SUBLANES + j + 1]
            act = _silu(hh[:, :EXPERT_FF]) * hh[:, EXPERT_FF:] * gate
            acc = acc + _dot(act.astype(BF16), wdn_ref[e])
        xg_ref[pl.ds(r0, rows), :] = acc.astype(BF16)

    def copy_segments(g, gather):
        off = jnp.int32(0)
        for t in range(n_tiles):
            st = meta_ref[(t * N_GROUPS + g) * 2]
            ln = meta_ref[(t * N_GROUPS + g) * 2 + 1]

            def copy_rows(src0, dst0, rows, t=t):
                src = pl.ds(pl.multiple_of(src0, MOE_ALIGN), rows)
                dst = pl.ds(pl.multiple_of(dst0, MOE_ALIGN), rows)
                if gather:
                    xg_ref[dst, :] = xs_ref[t, src, :]
                    gg_ref[dst, :] = gs_ref[t, src, :]
                else:
                    xs_ref[t, src, :] = xg_ref[dst, :]

            def copy_body(i, carry, st=st, off=off, copy_rows=copy_rows):
                copy_rows(st + i * MOE_COPY, off + i * MOE_COPY, MOE_COPY)
                return carry

            n_copy = lax.shift_right_logical(ln, MOE_COPY_SHIFT)
            lax.fori_loop(0, n_copy, copy_body, 0)
            done = n_copy * MOE_COPY

            def tail_body(i, carry, st=st, off=off, done=done, copy_rows=copy_rows):
                copy_rows(st + done + i * MOE_ALIGN, off + done + i * MOE_ALIGN, MOE_ALIGN)
                return carry

            lax.fori_loop(0, lax.shift_right_logical(ln - done, MOE_ALIGN_SHIFT), tail_body, 0)
            off = off + ln
        return off

    @pl.when((s < n_super) & (k == n_tiles - 1))
    def _():
        for g in range(N_GROUPS):
            total = copy_segments(g, True)
            n_big = total // MOE_BIG
            rem = total - n_big * MOE_BIG
            more = rem > MOE_SMALL_MAX * MOE_BLOCK
            n_big = n_big + more.astype(jnp.int32)
            n_small = jnp.where(more, 0, (rem + (MOE_BLOCK - 1)) // MOE_BLOCK)
            small0 = n_big * MOE_BIG

            def big_body(i, carry, g=g):
                expert_block(g, pl.multiple_of(i * MOE_BIG, MOE_ALIGN), MOE_BIG)
                return carry

            def small_body(i, carry, g=g, small0=small0):
                expert_block(g, pl.multiple_of(small0 + i * MOE_BLOCK, MOE_ALIGN), MOE_BLOCK)
                return carry

            lax.fori_loop(0, n_big, big_body, 0)
            lax.fori_loop(0, n_small, small_body, 0)

            copy_segments(g, False)


def _moe(un, route, h1, w_up, w_down, final_gain, final, tm):
    N = un.shape[0]
    n_tiles = min(MOE_SUPER, N // tm)
    n_super = N // (tm * n_tiles)
    slots = tm + N_GROUPS * MOE_ALIGN
    group_rows = n_tiles * slots + MOE_BIG
    utri = (jnp.arange(tm)[:, None] < jnp.arange(tm)[None, :]).astype(BF16)
    in_tile = lambda s, k: jnp.minimum(s, n_super - 1) * n_tiles + k
    out_tile = lambda s, k: jnp.maximum(s - 1, 0) * n_tiles + jnp.where(s > 0, k, 0)
    once = lambda a: pl.BlockSpec(a.shape, lambda s, k: (0,) * a.ndim, pipeline_mode=pl.Buffered(1))
    fg = final_gain.reshape(1, D_MODEL)
    return pl.pallas_call(
        functools.partial(_moe_kernel, final, n_super),
        out_shape=jax.ShapeDtypeStruct((N, D_MODEL), F32),
        grid=(n_super + 1, n_tiles),
        in_specs=[pl.BlockSpec((tm, D_MODEL), lambda s, k: (in_tile(s, k), 0)),
                  pl.BlockSpec((SUBLANES, tm), lambda s, k: (0, in_tile(s, k))),
                  pl.BlockSpec((tm, D_MODEL), lambda s, k: (out_tile(s, k), 0)),
                  once(utri), once(w_up), once(w_down), once(fg)],
        out_specs=pl.BlockSpec((tm, D_MODEL), lambda s, k: (out_tile(s, k), 0)),
        scratch_shapes=[pltpu.VMEM((n_tiles, slots, D_MODEL), BF16),
                        pltpu.VMEM((n_tiles, slots, 2 * SUBLANES), F32),
                        pltpu.VMEM((group_rows, D_MODEL), BF16),
                        pltpu.VMEM((group_rows, 2 * SUBLANES), F32),
                        pltpu.VMEM((n_tiles, tm, slots), BF16),
                        pltpu.SMEM((n_tiles * N_GROUPS * 2,), jnp.int32)],
        compiler_params=pltpu.CompilerParams(dimension_semantics=("arbitrary", "arbitrary"),
                                             vmem_limit_bytes=VMEM_LIMIT),
        name="moe",
    )(un, route, h1, utri, w_up, w_down, fg)


def kernel(x, mem, mix_norm, ffn_norm, mem_norm, final_norm, gla_w_in, gla_w_gate_up, gla_b_gate, gla_out_norm,
           hg_w_in, hg_lower_bounds, hg_out_norm, w_mem_kv, w_out, w_group, b_group, w_router, b_router,
           w_up, w_down):
    B, T, _ = x.shape
    N = B * T
    depth = mix_norm.shape[0]
    tm_mix = min(1024, T)
    tm_out = min(1024, T)
    tm_moe = min(MOE_TILE, N)
    h = x.reshape(N, D_MODEL)
    for layer in range(depth):
        j = layer // 2
        if layer % 2 == 0:
            mix, mq, w_up_bf16, w_down_bf16 = _gla_mixer(
                h, mix_norm[layer], gla_w_in[j], gla_w_gate_up[j], gla_b_gate[j], gla_out_norm[j],
                w_up, w_down, layer, T, tm_mix)
        else:
            mix, mq, w_up_bf16, w_down_bf16 = _hg_mixer(
                h, mix_norm[layer], hg_w_in[j], hg_lower_bounds, hg_out_norm[j], w_up, w_down, layer, T, tm_mix)
        h1, un, gates = _out_call(mix, mq, mem, mem_norm[layer], w_mem_kv[layer], h,
                                  w_out[layer, :MIX_WIDTH], w_out[layer, MIX_WIDTH:], ffn_norm[layer],
                                  w_group[layer], b_group[layer], w_router[layer], b_router[layer], T, tm_out)
        h = _moe(un, gates, h1, w_up_bf16, w_down_bf16, final_norm, layer == depth - 1, tm_moe)
    return h.reshape(B, T, D_MODEL)
```

```python
import functools

import jax
import jax.numpy as jnp
from jax import lax
from jax.experimental import pallas as pl
from jax.experimental.pallas import tpu as pltpu

D_MODEL = 1024
N_MEM = 256
MIX_WIDTH = 768
MEM_HEADS = 4
MEM_HEAD_DIM = 64
MEM_WIDTH = 256
CHUNK = 64
GLA_HEADS = 4
GLA_KEY_WIDTH = 512
GLA_DK = 128
GLA_DV = 192
GLA_DV_PAD = 256
GLA_GATE_RANK = 16
GLA_GATE_NORMALIZER = 16.0
HG_HEADS = 6
HG_DK = 128
HG_DV = 128
N_GROUPS = 4
EXPERTS_PER_GROUP = 4
N_EXPERTS = 16
EXPERT_FF = 256
NORM_EPS = 1e-6
LANES = 128
SUBLANES = 8
MOE_BLOCK = 128
CUMSUM_SPAN = 256
MOE_ALIGN_SHIFT = 4
MOE_ALIGN = 1 << MOE_ALIGN_SHIFT
MOE_COPY_SHIFT = 6
MOE_COPY = 1 << MOE_COPY_SHIFT
MOE_TILE = 512
MOE_SUPER = 4
MOE_BIG = MOE_SUPER * (MOE_TILE // N_GROUPS + MOE_ALIGN)
MOE_SMALL_MAX = 2
ROUTE_GROUP_ROW = 4
ROUTE_DEST_ROW, ROUTE_START_ROW, ROUTE_LEN_ROW = 5, 6, 7
ROUTE_ROWS = 32
SAFE_EXPONENT = 80.0
SCAN_JOINS = (4, 2, 1)
V7X_VMEM_BYTES = 64 * 1024 * 1024
VMEM_LIMIT = V7X_VMEM_BYTES - 8 * 1024 * 1024

F32 = jnp.float32
BF16 = jnp.bfloat16


def _dot(a, b):
    return jnp.dot(a, b, preferred_element_type=F32)


def _dot_nt(a, b):
    return lax.dot_general(a, b, (((1,), (1,)), ((), ())), preferred_element_type=F32)


def _dot_tn(a, b):
    return lax.dot_general(a, b, (((0,), (0,)), ((), ())), preferred_element_type=F32)


def _split2(x):
    hi = x.astype(BF16)
    lo = (x - hi.astype(F32)).astype(BF16)
    return hi, lo


def _rms(x, gain):
    ms = jnp.mean(x * x, axis=-1, keepdims=True)
    return x * lax.rsqrt(ms + NORM_EPS) * gain


def _log_sigmoid(x):
    return jnp.minimum(x, 0.0) - jnp.log(1.0 + jnp.exp(-jnp.abs(x)))


def _sigmoid(x):
    return 1.0 / (1.0 + jnp.exp(-x))


def _silu(x):
    return x * _sigmoid(x)


def _chunk_cumsum(tri_ref, x):
    tri = tri_ref[...]
    span = tri.shape[0]
    parts = []
    for r in range(x.shape[0] // span):
        hi, lo = _split2(x[r * span:(r + 1) * span])
        parts.append(_dot(tri, hi) + _dot(tri, lo))
    return jnp.concatenate(parts, axis=0) if len(parts) > 1 else parts[0]


_GQ, _GK = 0, GLA_KEY_WIDTH
_GV = 2 * GLA_KEY_WIDTH
_GG = _GV + MIX_WIDTH
_GR = _GG + MIX_WIDTH
_GM = _GR + LANES
_GW = _GM + MEM_WIDTH


def _store_padded_heads(ref, x):
    zeros = jnp.zeros((x.shape[0], GLA_DV_PAD - GLA_DV), ref.dtype)
    for hd in range(GLA_HEADS):
        ref[:, hd * GLA_DV_PAD:hd * GLA_DV_PAD + GLA_DV] = x[:, hd * GLA_DV:(hd + 1) * GLA_DV].astype(ref.dtype)
        ref[:, hd * GLA_DV_PAD + GLA_DV:(hd + 1) * GLA_DV_PAD] = zeros


def _store_decay(b_ref, bmin_ref, b, q, k):
    b_ref[...] = b

    def tile_max(x):
        return jnp.max(jnp.max(x, axis=0, keepdims=True), axis=1, keepdims=True)

    lowest = -tile_max(-b)
    magnitude = jnp.maximum(tile_max(jnp.abs(q)), 1.0) * tile_max(jnp.abs(k)) * (2.0 * LANES)
    log_mag = jnp.log(jnp.maximum(magnitude, 1.0))
    sub = lax.broadcasted_iota(jnp.int32, (SUBLANES, LANES), 0)
    bmin_ref[0] = jnp.where(sub == 0, lowest, log_mag)


def _round_experts(wup_ref, wdn_ref, wup_bf_ref, wdn_bf_ref):
    wup_bf_ref[...] = wup_ref[...].astype(BF16)
    wdn_bf_ref[...] = wdn_ref[...].astype(BF16)


def _gla_proj_kernel(h_ref, gain_ref, w_ref, wg_ref, bg_ref, tri_ref, wup_ref, wdn_ref,
                     q_ref, k_ref, v_ref, sg_ref, b_ref, mq_ref, bmin_ref, wup_bf_ref, wdn_bf_ref):
    _round_experts(wup_ref, wdn_ref, wup_bf_ref, wdn_bf_ref)
    u = _rms(h_ref[...], gain_ref[...]).astype(BF16)
    q = _dot(u, w_ref[:, _GQ:_GK]) * (GLA_DK ** -0.5)
    k = _dot(u, w_ref[:, _GK:_GV])
    q_ref[...] = q.astype(BF16)
    k_ref[...] = k.astype(BF16)
    _store_padded_heads(v_ref, _dot(u, w_ref[:, _GV:_GG]))
    _store_padded_heads(sg_ref, _silu(_dot(u, w_ref[:, _GG:_GR])))
    mq_ref[...] = _dot(u, w_ref[:, _GM:_GW]).astype(BF16)
    r = _dot(u, w_ref[:, _GR:_GM])
    r_hi = r.astype(BF16)
    lane = lax.broadcasted_iota(jnp.int32, r.shape, 1)
    is_lo = (lane >= GLA_GATE_RANK) & (lane < 2 * GLA_GATE_RANK)
    r_parts = jnp.where(is_lo, r - r_hi.astype(F32), r_hi.astype(F32)).astype(BF16)
    logit = _dot(r_parts, wg_ref[...]) + bg_ref[...]
    _store_decay(b_ref, bmin_ref, _chunk_cumsum(tri_ref, _log_sigmoid(logit) * (1.0 / GLA_GATE_NORMALIZER)),
                 q, k)


_HQ, _HF, _HI, _HGG, _HM = 0, MIX_WIDTH, 2 * MIX_WIDTH, 3 * MIX_WIDTH, 4 * MIX_WIDTH
_HW = _HM + MEM_WIDTH


def _hg_proj_kernel(layer, h_ref, gain_ref, w_ref, lbp_ref, tri_ref, wup_ref, wdn_ref,
                    q_ref, k_ref, v_ref, sg_ref, b_ref, mq_ref, bmin_ref, wup_bf_ref, wdn_bf_ref):
    _round_experts(wup_ref, wdn_ref, wup_bf_ref, wdn_bf_ref)
    u = _rms(h_ref[...], gain_ref[...]).astype(BF16)
    p = lbp_ref[...]
    p = jnp.exp(p - jnp.max(p, axis=0, keepdims=True))
    p = p / jnp.sum(p, axis=0, keepdims=True)
    lb = jnp.sum(p[0:layer + 1], axis=0, keepdims=True) - p[0:1]
    q = _silu(_dot(u, w_ref[:, _HQ:_HF]))
    z = _dot(u, w_ref[:, _HF:_HI])
    e = jnp.exp(-jnp.abs(z))
    k = (1.0 - lb) * (jnp.where(z >= 0.0, e, 1.0) / (1.0 + e))
    q_ref[...] = q.astype(BF16)
    k_ref[...] = k.astype(BF16)
    v_ref[...] = _dot(u, w_ref[:, _HI:_HGG]).astype(BF16)
    sg_ref[...] = _silu(_dot(u, w_ref[:, _HGG:_HM])).astype(BF16)
    mq_ref[...] = _dot(u, w_ref[:, _HM:_HW]).astype(BF16)
    a = jnp.log(lb)
    c = jnp.log1p(-lb) + (jnp.minimum(z, 0.0) - jnp.log(1.0 + e))
    log_f = jnp.maximum(a, c) + jnp.log(1.0 + jnp.exp(-jnp.abs(a - c)))
    _store_decay(b_ref, bmin_ref, _chunk_cumsum(tri_ref, log_f), q, k)


def _proj_call(kernel, h, gain, w, extra, w_up, w_down, layer, kw, vw, tm, name):
    N = h.shape[0]
    n_steps = N // tm
    per_step = max(1, N_EXPERTS // n_steps)
    assert N_EXPERTS % per_step == 0 and (n_steps * per_step) % N_EXPERTS == 0
    expert_blk = lambda i: i % (N_EXPERTS // per_step)
    w_in_spec = lambda a: pl.BlockSpec((None, per_step) + a.shape[2:], lambda i: (layer, expert_blk(i), 0, 0))
    w_out_spec = lambda a: pl.BlockSpec((per_step,) + a.shape[2:], lambda i: (expert_blk(i), 0, 0))
    span = min(CUMSUM_SPAN, tm)
    tri = (jnp.arange(span)[:, None] >= jnp.arange(span)[None, :]) & (
        jnp.arange(span)[:, None] // CHUNK == jnp.arange(span)[None, :] // CHUNK)
    tri = tri.astype(BF16)
    row = lambda width: pl.BlockSpec((tm, width), lambda i: (i, 0))
    full = lambda a: pl.BlockSpec(a.shape, lambda i: (0,) * a.ndim)
    ins = [h, gain.reshape(1, D_MODEL), w] + list(extra) + [tri]
    return pl.pallas_call(
        kernel,
        out_shape=(jax.ShapeDtypeStruct((N, kw), BF16), jax.ShapeDtypeStruct((N, kw), BF16),
                   jax.ShapeDtypeStruct((N, vw), BF16), jax.ShapeDtypeStruct((N, vw), BF16),
                   jax.ShapeDtypeStruct((N, kw), F32), jax.ShapeDtypeStruct((N, MEM_WIDTH), BF16),
                   jax.ShapeDtypeStruct((N // tm, SUBLANES, LANES), F32),
                   jax.ShapeDtypeStruct(w_up.shape[1:], BF16), jax.ShapeDtypeStruct(w_down.shape[1:], BF16)),
        grid=(n_steps,),
        in_specs=[row(D_MODEL)] + [full(a) for a in ins[1:]] + [w_in_spec(w_up), w_in_spec(w_down)],
        out_specs=(row(kw), row(kw), row(vw), row(vw), row(kw), row(MEM_WIDTH),
                   pl.BlockSpec((1, SUBLANES, LANES), lambda i: (i, 0, 0)),
                   w_out_spec(w_up), w_out_spec(w_down)),
        compiler_params=pltpu.CompilerParams(dimension_semantics=("parallel",),
                                             vmem_limit_bytes=VMEM_LIMIT),
        name=name,
    )(*ins, w_up, w_down)


def _gla_proj(h, gain, w_in, w_gate_up, b_gate, w_up, w_down, layer, tm):
    q, k, v, g, r, mq = jnp.split(
        w_in, [GLA_KEY_WIDTH, 2 * GLA_KEY_WIDTH, 2 * GLA_KEY_WIDTH + MIX_WIDTH,
               2 * GLA_KEY_WIDTH + 2 * MIX_WIDTH, 2 * GLA_KEY_WIDTH + 2 * MIX_WIDTH + GLA_GATE_RANK], axis=1)
    r3 = jnp.pad(jnp.concatenate([r, r, r], axis=1), ((0, 0), (0, LANES - 3 * GLA_GATE_RANK)))
    w = jnp.concatenate([q, k, v, g, r3, mq], axis=1).astype(BF16)
    wg_hi, wg_lo = _split2(w_gate_up)
    wg = jnp.pad(jnp.concatenate([wg_hi, wg_hi, wg_lo], axis=0), ((0, LANES - 3 * GLA_GATE_RANK), (0, 0)))
    return _proj_call(_gla_proj_kernel, h, gain, w, [wg, b_gate.reshape(1, GLA_KEY_WIDTH)], w_up, w_down, layer,
                      GLA_KEY_WIDTH, GLA_HEADS * GLA_DV_PAD, tm, "gla_proj")


def _hg_proj(h, gain, w_in, lower_bound_params, w_up, w_down, layer, tm):
    return _proj_call(functools.partial(_hg_proj_kernel, layer), h, gain, w_in.astype(BF16),
                      [lower_bound_params], w_up, w_down, layer, MIX_WIDTH, MIX_WIDTH, tm, "hg_proj")


def _scan_kernel(heads, dk, dv_pad, dv, n_chunks,
                 q_ref, k_ref, b_ref, bmin_ref, v_ref, sg_ref, gain_ref, o_ref, st_ref, kf_ref):
    @pl.when(pl.program_id(1) == 0)
    def _():
        st_ref[...] = jnp.zeros_like(st_ref)

    gain = gain_ref[...]
    col = lax.broadcasted_iota(jnp.int32, (CHUNK, CHUNK), 1)

    n_batch = q_ref.shape[0]

    def finish(bi, hd, rows, scores, qd, kl, eb_last):
        vc = slice(hd * dv_pad, (hd + 1) * dv_pad)
        v = v_ref[bi, rows, vc]
        st = st_ref[bi * heads + hd]
        out = _dot(scores.astype(BF16), v) + _dot_nt(qd, st.astype(BF16))
        st_ref[bi * heads + hd] = _dot_tn(v, kl) + st * eb_last
        ms = jnp.sum(out * out, axis=-1, keepdims=True) * (1.0 / dv)
        y = out * lax.rsqrt(ms + NORM_EPS) * gain * sg_ref[bi, rows, vc].astype(F32)
        o_ref[bi, rows, hd * dv:(hd + 1) * dv] = y[:, :dv].astype(BF16)

    def load(bi, hd, rows):
        kc = slice(hd * dk, (hd + 1) * dk)
        return (q_ref[bi, rows, kc].astype(F32), k_ref[bi, rows, kc].astype(F32), b_ref[bi, rows, kc])

    summary = bmin_ref[...]
    lowest = jnp.min(summary[:, :, 0:1, :])
    log_mag = jnp.max(summary[:, :, 1:2, :])

    def factored(n_join):
        span = n_join * CHUNK
        ri = lax.broadcasted_iota(jnp.int32, (span, span), 0)
        ci = lax.broadcasted_iota(jnp.int32, (span, span), 1)
        for c in range(n_chunks // n_join):
            rows = slice(c * span, (c + 1) * span)
            for bi in range(n_batch):
                for hd in range(heads):
                    q, k, b = load(bi, hd, rows)
                    parts = [b[:CHUNK]]
                    for i in range(1, n_join):
                        parts.append(b[i * CHUNK:(i + 1) * CHUNK] + parts[-1][CHUNK - 1:CHUNK])
                    b = jnp.concatenate(parts, axis=0) if n_join > 1 else b
                    eb = jnp.exp(b)
                    eb_last = eb[span - 1:span, :]
                    qd = (q * eb).astype(BF16)
                    kd = k * jnp.exp(-b)
                    scores = _dot_nt(qd, kd.astype(BF16))
                    finish(bi, hd, rows, jnp.where(ri >= ci, scores, 0.0), qd, (kd * eb_last).astype(BF16),
                           eb_last)

    joins = [j for j in SCAN_JOINS if n_chunks % j == 0]
    taken = False
    for j in joins:
        ok = j * lowest - log_mag >= -SAFE_EXPONENT
        pl.when(ok if taken is False else ok & jnp.logical_not(taken))(functools.partial(factored, j))
        taken = ok if taken is False else taken | ok
    all_safe = taken

    @pl.when(jnp.logical_not(all_safe))
    def _():
        for bi, hd in [(bi, hd) for bi in range(n_batch) for hd in range(heads)]:
            kc = slice(hd * dk, (hd + 1) * dk)

            def chunk_body(c, carry, bi=bi, hd=hd, kc=kc):
                start = pl.multiple_of(c * CHUNK, CHUNK)
                rows = pl.ds(start, CHUNK)
                q, k, b = load(bi, hd, rows)
                kf_ref[...] = k

                def col_body(j, sc):
                    base = pl.multiple_of((j >> 3) << 3, SUBLANES)
                    pick = lax.broadcasted_iota(jnp.int32, (SUBLANES, dk), 0) == (j & (SUBLANES - 1))
                    kj = jnp.sum(jnp.where(pick, kf_ref[pl.ds(base, SUBLANES), :], 0.0), axis=0, keepdims=True)
                    b8 = b_ref[bi, pl.ds(pl.multiple_of(start + base, SUBLANES), SUBLANES), kc]
                    bj = jnp.sum(jnp.where(pick, b8, 0.0), axis=0, keepdims=True)
                    rid = lax.broadcasted_iota(jnp.int32, (CHUNK, dk), 0)
                    dec = jnp.exp(jnp.where(rid >= j, b - bj, -jnp.inf))
                    colv = jnp.sum(q * kj * dec, axis=-1, keepdims=True)
                    return jnp.where(col == j, colv, sc)

                scores = lax.fori_loop(0, CHUNK, col_body, jnp.zeros((CHUNK, CHUNK), F32))
                b_last = b[CHUNK - 1:CHUNK, :]
                finish(bi, hd, rows, scores, (q * jnp.exp(b)).astype(BF16),
                       (k * jnp.exp(b_last - b)).astype(BF16), jnp.exp(b_last))
                return carry

            lax.fori_loop(0, n_chunks, chunk_body, 0)


def _scan(q, k, b, bmin, v, sg, gain, heads, dk, dv_pad, dv, tb):
    B, T, kw = q.shape
    vw = v.shape[-1]
    nb = 1
    blk = lambda w: pl.BlockSpec((nb, tb, w), lambda bi, ti: (bi, ti, 0))
    gain_row = jnp.pad(gain, (0, dv_pad - dv)).reshape(1, dv_pad)
    return pl.pallas_call(
        functools.partial(_scan_kernel, heads, dk, dv_pad, dv, tb // CHUNK),
        out_shape=jax.ShapeDtypeStruct((B, T, heads * dv), BF16),
        grid=(B // nb, T // tb),
        in_specs=[blk(kw), blk(kw), blk(kw),
                  pl.BlockSpec((nb, 1, SUBLANES, LANES), lambda bi, ti: (bi, ti, 0, 0)),
                  blk(vw), blk(vw), pl.BlockSpec((1, dv_pad), lambda bi, ti: (0, 0))],
        out_specs=blk(heads * dv),
        scratch_shapes=[pltpu.VMEM((nb * heads, dv_pad, dk), F32), pltpu.VMEM((CHUNK, dk), F32)],
        compiler_params=pltpu.CompilerParams(dimension_semantics=("arbitrary", "arbitrary"),
                                             vmem_limit_bytes=VMEM_LIMIT),
        name="chunk_scan",
    )(q, k, b, bmin, v, sg, gain_row)


def _out_kernel(tiles_per_batch, mix_ref, mq_ref, mem_ref, mgain_ref, wkv_ref, h_ref, wmix_ref, wmem_ref,
                fgain_ref, wr_ref, br_ref, utri_ref, h1_ref, un_ref, route_ref, mk_ref, mv_ref):
    tm = mix_ref.shape[0]
    moe_tile = utri_ref.shape[0]

    @pl.when(pl.program_id(0) % tiles_per_batch == 0)
    def _():
        kv = _dot(_rms(mem_ref[0], mgain_ref[...]).astype(BF16), wkv_ref[...])
        mk_ref[...] = kv[:, :MEM_WIDTH].astype(BF16)
        mv_ref[...] = kv[:, MEM_WIDTH:].astype(BF16)

    mk = mk_ref[...]
    mv = mv_ref[...]
    lane_w = lax.broadcasted_iota(jnp.int32, (1, MEM_WIDTH), 1)
    mq = mq_ref[...] * (MEM_HEAD_DIM ** -0.5)
    mem_o = jnp.zeros((tm, MEM_WIDTH), F32)
    for hd in range(MEM_HEADS):
        in_head = (lane_w >= hd * MEM_HEAD_DIM) & (lane_w < (hd + 1) * MEM_HEAD_DIM)
        s = _dot_nt(jnp.where(in_head, mq, jnp.zeros_like(mq)), mk)
        e = jnp.exp(s - jnp.max(s, axis=-1, keepdims=True))
        denom = jnp.sum(e, axis=-1, keepdims=True)
        mem_o = mem_o + _dot(e.astype(BF16), jnp.where(in_head, mv, jnp.zeros_like(mv))) * (1.0 / denom)
    h1 = h_ref[...] + _dot(mix_ref[...], wmix_ref[...]) + _dot(mem_o.astype(BF16), wmem_ref[...])
    h1_ref[...] = h1
    un = _rms(h1, fgain_ref[...])
    un_ref[...] = un.astype(BF16)

    both = _dot_nt(wr_ref[...], un.astype(BF16))
    lg = both[:ROUTE_ROWS] + both[ROUTE_ROWS:] + br_ref[...]
    row = lax.broadcasted_iota(jnp.int32, (ROUTE_ROWS, tm), 0)
    neg = -jnp.inf
    gl = jnp.where(row < N_GROUPS, lg, neg)
    gmax = jnp.max(gl, axis=0, keepdims=True)
    g_idx = jnp.min(jnp.where(gl == gmax, row, ROUTE_ROWS), axis=0, keepdims=True)
    g_w = 1.0 / jnp.sum(jnp.exp(gl - gmax), axis=0, keepdims=True)
    first = N_GROUPS + g_idx * EXPERTS_PER_GROUP
    el = jnp.where((row >= first) & (row < first + EXPERTS_PER_GROUP), lg, neg)
    t1 = jnp.max(el, axis=0, keepdims=True)
    i1 = jnp.min(jnp.where(el == t1, row, ROUTE_ROWS), axis=0, keepdims=True)
    el2 = jnp.where(row == i1, neg, el)
    t2 = jnp.max(el2, axis=0, keepdims=True)
    i2 = jnp.min(jnp.where(el2 == t2, row, ROUTE_ROWS), axis=0, keepdims=True)
    e2 = jnp.exp(t2 - t1)
    w1 = g_w / (1.0 + e2)
    w2 = g_w * e2 / (1.0 + e2)
    row8 = lax.broadcasted_iota(jnp.int32, (SUBLANES, tm), 0)
    route = (jnp.where(row8 == i1 - first, w1, 0.0) + jnp.where(row8 == i2 - first, w2, 0.0)
             + jnp.where(row8 == ROUTE_GROUP_ROW, g_idx.astype(F32), 0.0))

    g_f = g_idx.astype(F32)
    r8 = lax.broadcasted_iota(jnp.int32, (SUBLANES, moe_tile), 0)
    row81 = lax.broadcasted_iota(jnp.int32, (SUBLANES, 1), 0)
    diag = (lax.broadcasted_iota(jnp.int32, (SUBLANES, LANES), 0)
            == lax.broadcasted_iota(jnp.int32, (SUBLANES, LANES), 1))
    for t in range(tm // moe_tile):
        cols = slice(t * moe_tile, (t + 1) * moe_tile)
        member = jnp.where((r8 < N_GROUPS) & (r8.astype(F32) == g_f[:, cols]), 1.0, 0.0)
        earlier = _dot(member.astype(BF16), utri_ref[...])
        rank = jnp.sum(member * earlier, axis=0, keepdims=True)
        count = jnp.sum(member, axis=1, keepdims=True)
        padded = jnp.floor((count + (MOE_ALIGN - 1.0)) * (1.0 / MOE_ALIGN)) * MOE_ALIGN
        start = jnp.zeros((1, 1), F32)
        seg = jnp.zeros((SUBLANES, 1), F32)
        for g in range(N_GROUPS):
            seg = jnp.where(row81 == g, start, seg)
            start = start + padded[g:g + 1, :]
        dest = rank + jnp.sum(member * seg, axis=0, keepdims=True)
        route_ref[:, cols] = route[:, cols] + jnp.where(r8 == ROUTE_DEST_ROW, dest, 0.0)
        head = slice(t * moe_tile, t * moe_tile + LANES)
        route_ref[ROUTE_START_ROW:ROUTE_START_ROW + 1, head] = jnp.sum(
            jnp.where(diag, seg, 0.0), axis=0, keepdims=True)
        route_ref[ROUTE_LEN_ROW:ROUTE_LEN_ROW + 1, head] = jnp.sum(
            jnp.where(diag & (row81 < N_GROUPS), padded, 0.0), axis=0, keepdims=True)


def _out_call(mix, mq, mem, mem_gain, w_kv, h, w_mix, w_mem, ffn_gain, w_group, b_group, w_router, b_router,
              T, tm, moe_tile):
    N = h.shape[0]
    tiles_per_batch = T // tm
    assert tm % moe_tile == 0
    utri = (jnp.arange(moe_tile)[:, None] < jnp.arange(moe_tile)[None, :]).astype(BF16)
    pad = ROUTE_ROWS - N_GROUPS - N_EXPERTS
    wr = jnp.pad(jnp.concatenate([w_group, w_router], axis=1).T, ((0, pad), (0, 0)))
    br = jnp.broadcast_to(jnp.pad(jnp.concatenate([b_group, b_router]), (0, pad))[:, None], (ROUTE_ROWS, tm))
    wr_both = jnp.concatenate(_split2(wr), axis=0)
    row = lambda width: pl.BlockSpec((tm, width), lambda i: (i, 0))
    full = lambda a: pl.BlockSpec(a.shape, lambda i: (0,) * a.ndim)
    memspec = pl.BlockSpec((1, N_MEM, D_MODEL), lambda i: (i // tiles_per_batch, 0, 0))
    w_mix = w_mix.astype(BF16)
    w_mem = w_mem.astype(BF16)
    w_kv = w_kv.astype(BF16)
    fg = ffn_gain.reshape(1, D_MODEL)
    mg = mem_gain.reshape(1, D_MODEL)
    return pl.pallas_call(
        functools.partial(_out_kernel, tiles_per_batch),
        out_shape=(jax.ShapeDtypeStruct((N, D_MODEL), F32), jax.ShapeDtypeStruct((N, D_MODEL), BF16),
                   jax.ShapeDtypeStruct((SUBLANES, N), F32)),
        grid=(N // tm,),
        in_specs=[row(mix.shape[1]), row(MEM_WIDTH), memspec, full(mg), full(w_kv), row(D_MODEL),
                  full(w_mix), full(w_mem), full(fg), full(wr_both), full(br), full(utri)],
        out_specs=(row(D_MODEL), row(D_MODEL), pl.BlockSpec((SUBLANES, tm), lambda i: (0, i))),
        scratch_shapes=[pltpu.VMEM((N_MEM, MEM_WIDTH), BF16), pltpu.VMEM((N_MEM, MEM_WIDTH), BF16)],
        compiler_params=pltpu.CompilerParams(dimension_semantics=("arbitrary",),
                                             vmem_limit_bytes=VMEM_LIMIT),
        name="out_proj_route",
    )(mix, mq, mem, mg, w_kv, h, w_mix, w_mem, fg, wr_both, br, utri)


def _moe_kernel(final, n_super, un_ref, route_ref, h1_ref, wup_ref, wdn_ref, fgain_ref, o_ref,
                xs_ref, gs_ref, xg_ref, gg_ref, permt_ref, meta_ref):
    s = pl.program_id(0)
    k = pl.program_id(1)
    n_tiles, slots, _ = xs_ref.shape
    tm = un_ref.shape[0]
    row8 = lax.broadcasted_iota(jnp.int32, (SUBLANES, tm), 0)

    @pl.when((s == 0) & (k == 0))
    def _():
        xg_ref[...] = jnp.zeros_like(xg_ref)
        gg_ref[...] = jnp.zeros_like(gg_ref)

    def unpartition():
        y = h1_ref[...] + _dot(permt_ref[k], xs_ref[k])
        o_ref[...] = _rms(y, fgain_ref[...]) if final else y

    def partition():
        route = route_ref[...]
        dest = route[ROUTE_DEST_ROW:ROUTE_DEST_ROW + 1, :]
        perm = jnp.where(lax.broadcasted_iota(jnp.int32, (slots, tm), 0) == dest.astype(jnp.int32),
                         1.0, 0.0).astype(BF16)
        d_hi = jnp.floor(dest * (1.0 / 32.0))
        d_lo = dest - 32.0 * d_hi
        digits = jnp.where(row8 == 0, d_hi, jnp.where(row8 == 1, d_lo, 0.0)).astype(BF16)
        pick = (lax.broadcasted_iota(jnp.int32, (SUBLANES, LANES), 0)
                == lax.broadcasted_iota(jnp.int32, (SUBLANES, LANES), 1)).astype(BF16)
        dig_c = _dot_tn(digits, pick)
        dest_col = (32.0 * dig_c[:, 0:1] + dig_c[:, 1:2]).astype(jnp.int32)
        permt_ref[k] = jnp.where(lax.broadcasted_iota(jnp.int32, (tm, slots), 1) == dest_col,
                                 1.0, 0.0).astype(BF16)
        xs_ref[k] = _dot(perm, un_ref[...]).astype(BF16)
        r_hi, r_lo = _split2(route)
        gs_ref[k] = _dot_nt(perm, jnp.concatenate([r_hi, r_lo], axis=0))
        for g in range(N_GROUPS):
            in_lane = lax.broadcasted_iota(jnp.int32, (1, LANES), 1) == g
            start_g = jnp.sum(jnp.where(in_lane, route[ROUTE_START_ROW:ROUTE_START_ROW + 1, :LANES], 0.0))
            len_g = jnp.sum(jnp.where(in_lane, route[ROUTE_LEN_ROW:ROUTE_LEN_ROW + 1, :LANES], 0.0))
            meta_ref[(k * N_GROUPS + g) * 2] = start_g.astype(jnp.int32)
            meta_ref[(k * N_GROUPS + g) * 2 + 1] = len_g.astype(jnp.int32)

    pl.when(s > 0)(unpartition)
    pl.when(s < n_super)(partition)

    def expert_block(g, r0, rows):
        xb = xg_ref[pl.ds(r0, rows), :]
        gsb = gg_ref[pl.ds(r0, rows), :]
        acc = jnp.zeros((rows, D_MODEL), F32)
        for j in range(EXPERTS_PER_GROUP):
            e = g * EXPERTS_PER_GROUP + j
            hh = _dot(xb, wup_ref[e])
            gate = gsb[:, j:j + 1] + gsb[:, SUBLANES + j:SUBLANES + j + 1]
            act = _silu(hh[:, :EXPERT_FF]) * hh[:, EXPERT_FF:] * gate
            acc = acc + _dot(act.astype(BF16), wdn_ref[e])
        xg_ref[pl.ds(r0, rows), :] = acc.astype(BF16)

    def copy_segments(g, gather):
        off = jnp.int32(0)
        for t in range(n_tiles):
            st = meta_ref[(t * N_GROUPS + g) * 2]
            ln = meta_ref[(t * N_GROUPS + g) * 2 + 1]

            def copy_rows(src0, dst0, rows, t=t):
                src = pl.ds(pl.multiple_of(src0, MOE_ALIGN), rows)
                dst = pl.ds(pl.multiple_of(dst0, MOE_ALIGN), rows)
                if gather:
                    xg_ref[dst, :] = xs_ref[t, src, :]
                    gg_ref[dst, :] = gs_ref[t, src, :]
                else:
                    xs_ref[t, src, :] = xg_ref[dst, :]

            def copy_body(i, carry, st=st, off=off, copy_rows=copy_rows):
                copy_rows(st + i * MOE_COPY, off + i * MOE_COPY, MOE_COPY)
                return carry

            n_copy = lax.shift_right_logical(ln, MOE_COPY_SHIFT)
            lax.fori_loop(0, n_copy, copy_body, 0)
            done = n_copy * MOE_COPY

            def tail_body(i, carry, st=st, off=off, done=done, copy_rows=copy_rows):
                copy_rows(st + done + i * MOE_ALIGN, off + done + i * MOE_ALIGN, MOE_ALIGN)
                return carry

            lax.fori_loop(0, lax.shift_right_logical(ln - done, MOE_ALIGN_SHIFT), tail_body, 0)
            off = off + ln
        return off

    @pl.when((s < n_super) & (k == n_tiles - 1))
    def _():
        for g in range(N_GROUPS):
            total = copy_segments(g, True)
            n_big = total // MOE_BIG
            rem = total - n_big * MOE_BIG
            more = rem > MOE_SMALL_MAX * MOE_BLOCK
            n_big = n_big + more.astype(jnp.int32)
            n_small = jnp.where(more, 0, (rem + (MOE_BLOCK - 1)) // MOE_BLOCK)
            small0 = n_big * MOE_BIG

            def big_body(i, carry, g=g):
                expert_block(g, pl.multiple_of(i * MOE_BIG, MOE_ALIGN), MOE_BIG)
                return carry

            def small_body(i, carry, g=g, small0=small0):
                expert_block(g, pl.multiple_of(small0 + i * MOE_BLOCK, MOE_ALIGN), MOE_BLOCK)
                return carry

            lax.fori_loop(0, n_big, big_body, 0)
            lax.fori_loop(0, n_small, small_body, 0)

            copy_segments(g, False)


def _moe(un, route, h1, w_up, w_down, final_gain, final, tm):
    N = un.shape[0]
    n_tiles = min(MOE_SUPER, N // tm)
    n_super = N // (tm * n_tiles)
    slots = tm + N_GROUPS * MOE_ALIGN
    group_rows = n_tiles * slots + MOE_BIG
    in_tile = lambda s, k: jnp.minimum(s, n_super - 1) * n_tiles + k
    out_tile = lambda s, k: jnp.maximum(s - 1, 0) * n_tiles + jnp.where(s > 0, k, 0)
    once = lambda a: pl.BlockSpec(a.shape, lambda s, k: (0,) * a.ndim, pipeline_mode=pl.Buffered(1))
    fg = final_gain.reshape(1, D_MODEL)
    return pl.pallas_call(
        functools.partial(_moe_kernel, final, n_super),
        out_shape=jax.ShapeDtypeStruct((N, D_MODEL), F32),
        grid=(n_super + 1, n_tiles),
        in_specs=[pl.BlockSpec((tm, D_MODEL), lambda s, k: (in_tile(s, k), 0)),
                  pl.BlockSpec((SUBLANES, tm), lambda s, k: (0, in_tile(s, k))),
                  pl.BlockSpec((tm, D_MODEL), lambda s, k: (out_tile(s, k), 0)),
                  once(w_up), once(w_down), once(fg)],
        out_specs=pl.BlockSpec((tm, D_MODEL), lambda s, k: (out_tile(s, k), 0)),
        scratch_shapes=[pltpu.VMEM((n_tiles, slots, D_MODEL), BF16),
                        pltpu.VMEM((n_tiles, slots, 2 * SUBLANES), F32),
                        pltpu.VMEM((group_rows, D_MODEL), BF16),
                        pltpu.VMEM((group_rows, 2 * SUBLANES), F32),
                        pltpu.VMEM((n_tiles, tm, slots), BF16),
                        pltpu.SMEM((n_tiles * N_GROUPS * 2,), jnp.int32)],
        compiler_params=pltpu.CompilerParams(dimension_semantics=("arbitrary", "arbitrary"),
                                             vmem_limit_bytes=VMEM_LIMIT),
        name="moe",
    )(un, route, h1, w_up, w_down, fg)


def kernel(x, mem, mix_norm, ffn_norm, mem_norm, final_norm, gla_w_in, gla_w_gate_up, gla_b_gate, gla_out_norm,
           hg_w_in, hg_lower_bounds, hg_out_norm, w_mem_kv, w_out, w_group, b_group, w_router, b_router,
           w_up, w_down):
    B, T, _ = x.shape
    N = B * T
    depth = mix_norm.shape[0]
    tm_proj = min(1024, T)
    tb_scan = tm_proj
    tm_out = min(1024, T)
    tm_moe = min(MOE_TILE, N)
    h = x.reshape(N, D_MODEL)
    for layer in range(depth):
        j = layer // 2
        if layer % 2 == 0:
            q, k, v, sg, b, mq, bmin, w_up_bf16, w_down_bf16 = _gla_proj(
                h, mix_norm[layer], gla_w_in[j], gla_w_gate_up[j], gla_b_gate[j], w_up, w_down, layer, tm_proj)
            heads, dk, dv_pad, dv, out_gain = GLA_HEADS, GLA_DK, GLA_DV_PAD, GLA_DV, gla_out_norm[j]
        else:
            q, k, v, sg, b, mq, bmin, w_up_bf16, w_down_bf16 = _hg_proj(
                h, mix_norm[layer], hg_w_in[j], hg_lower_bounds, w_up, w_down, layer, tm_proj)
            heads, dk, dv_pad, dv, out_gain = HG_HEADS, HG_DK, HG_DV, HG_DV, hg_out_norm[j]
        r3 = lambda a: a.reshape(B, T, a.shape[-1])
        bmin = bmin.reshape(B, T // tb_scan, SUBLANES, LANES)
        mix = _scan(r3(q), r3(k), r3(b), bmin, r3(v), r3(sg), out_gain, heads, dk, dv_pad, dv, tb_scan)
        mix = mix.reshape(N, MIX_WIDTH)
        h1, un, gates = _out_call(mix, mq, mem, mem_norm[layer], w_mem_kv[layer], h,
                                  w_out[layer, :MIX_WIDTH], w_out[layer, MIX_WIDTH:], ffn_norm[layer],
                                  w_group[layer], b_group[layer], w_router[layer], b_router[layer], T, tm_out,
                                  tm_moe)
        h = _moe(un, gates, h1, w_up_bf16, w_down_bf16, final_norm, layer == depth - 1, tm_moe)
    return h.reshape(B, T, D_MODEL)
```
